```python
import math
import jax, jax.numpy as jnp
from jax import lax
import numpy as np

D_MODEL = 2048
BATCH = 8
SEQ = 4096
DEPTH = 2

N_A_LAYERS = DEPTH // 2
N_B_LAYERS = DEPTH - N_A_LAYERS
HEAD_DIM = 128
N_HEADS = D_MODEL // HEAD_DIM
DILATED_GROUPS = ((128, 1), (512, 4), (2048, 16))
N_GROUPS = len(DILATED_GROUPS)
ATTN_BLOCK = 128
CONV_WIDTH = 31
D_FF = 4 * D_MODEL
PLE_DIM = 256
ROPE_THETA = 10000.0
LN_EPS = 1e-5
DEEPNORM_ALPHA = (2 * DEPTH) ** 0.25
DEEPNORM_BETA = (8 * DEPTH) ** -0.25

kernel_name = "yoco_conformer_dilated_hybrid"


def layer_norm(x, g, b):
    xf = x.astype(jnp.float32)
    mu = jnp.mean(xf, axis=-1, keepdims=True)
    var = jnp.mean(jnp.square(xf - mu), axis=-1, keepdims=True)
    y = (xf - mu) * lax.rsqrt(var + LN_EPS) * g.astype(jnp.float32) + b.astype(jnp.float32)
    return y.astype(x.dtype)


def rotary(x, positions):
    half = HEAD_DIM // 2
    inv_freq = ROPE_THETA ** (-jnp.arange(half, dtype=jnp.float32) * (2.0 / HEAD_DIM))
    ang = positions.astype(jnp.float32)[..., None] * inv_freq
    cos = jnp.cos(ang)[:, :, None, :]
    sin = jnp.sin(ang)[:, :, None, :]
    xf = x.astype(jnp.float32)
    x1, x2 = xf[..., :half], xf[..., half:]
    return jnp.concatenate([x1 * cos - x2 * sin, x2 * cos + x1 * sin], axis=-1).astype(x.dtype)


def conformer_conv(x, w_in, b_in, dw, dw_b, ln_g, ln_b, w_out):
    u = x @ w_in + b_in
    a, g = jnp.split(u, 2, axis=-1)
    u = a * jax.nn.sigmoid(g)
    u = lax.conv_general_dilated(
        u, dw.astype(u.dtype), window_strides=(1,), padding=[(CONV_WIDTH - 1, 0)],
        dimension_numbers=("NWC", "WIO", "NWC"), feature_group_count=D_MODEL) + dw_b
    u = jax.nn.silu(layer_norm(u, ln_g, ln_b))
    return u @ w_out


def to_strided_blocks(t, dil):
    b_, s_pad, h, hd = t.shape
    length = s_pad // dil
    t = t.reshape(b_, length, dil, h, hd).transpose(0, 2, 3, 1, 4)
    return t.reshape(b_, dil, h, length // ATTN_BLOCK, ATTN_BLOCK, hd)


def with_previous_block(t):
    prev = jnp.pad(t, ((0, 0), (0, 0), (0, 0), (1, 0), (0, 0), (0, 0)))[:, :, :, :-1]
    return jnp.concatenate([prev, t], axis=4)


def dilated_window_attention(q, k, v, window, dil):
    n_back = window // dil
    b_, s, h, hd = q.shape
    span = dil * ATTN_BLOCK
    s_pad = -(-s // span) * span
    pad = ((0, 0), (0, s_pad - s), (0, 0), (0, 0))
    qb = to_strided_blocks(jnp.pad(q, pad), dil)
    kk = with_previous_block(to_strided_blocks(jnp.pad(k, pad), dil))
    vv = with_previous_block(to_strided_blocks(jnp.pad(v, pad), dil))
    nb = qb.shape[3]
    sc = jnp.einsum("brhnqc,brhnkc->brhnqk", qb, kk,
                    preferred_element_type=jnp.float32) * (HEAD_DIM ** -0.5)
    qi = jnp.arange(ATTN_BLOCK)[:, None]
    kj = jnp.arange(2 * ATTN_BLOCK)[None, :]
    delta = qi + ATTN_BLOCK - kj
    key_sub = jnp.arange(nb)[:, None, None] * ATTN_BLOCK + kj - ATTN_BLOCK
    valid = (delta >= 0) & (delta <= n_back) & (key_sub >= 0)
    sc = jnp.where(valid, sc, -jnp.inf)
    m = jnp.max(sc, axis=-1, keepdims=True)
    pexp = jnp.exp(sc - m)
    l = jnp.sum(pexp, axis=-1, keepdims=True)
    o = jnp.einsum("brhnqk,brhnkc->brhnqc", pexp, vv.astype(jnp.float32)) / l
    lse = (m + jnp.log(l))[..., 0]
    o = o.transpose(0, 3, 4, 1, 2, 5).reshape(b_, s_pad, h, hd)[:, :s]
    lse = lse.transpose(0, 3, 4, 1, 2).reshape(b_, s_pad, h)[:, :s]
    return o, lse


def shared_kv(x, g, b, w_kv, positions):
    b_, s, _ = x.shape
    kv = (layer_norm(x, g, b) @ w_kv).reshape(b_, s, 2, N_HEADS, HEAD_DIM)
    return rotary(kv[:, :, 0], positions), kv[:, :, 1]


def dilated_mixer(x, w_q, k, v, w_o, positions):
    b_, s, _ = x.shape
    q = (x @ w_q).reshape(b_, s, N_GROUPS, N_HEADS, HEAD_DIM)
    outs, lses = [], []
    for g, (window, dil) in enumerate(DILATED_GROUPS):
        o, lse = dilated_window_attention(rotary(q[:, :, g], positions), k, v, window, dil)
        outs.append(o)
        lses.append(lse)
    wts = jax.nn.softmax(jnp.stack(lses), axis=0)
    o = jnp.sum(wts[..., None] * jnp.stack(outs), axis=0)
    return o.reshape(b_, s, D_MODEL).astype(x.dtype) @ w_o


def sq_relu_mlp(x, w_up, w_down):
    return jnp.square(jax.nn.relu(x @ w_up)) @ w_down


def per_layer_embedding(x, p_i, w_proj, w_gate):
    return (p_i @ w_proj) * jax.nn.sigmoid(x @ w_gate)


def _fwd_setup_inputs(seed: int = 0) -> dict:
    key = jax.random.key(seed)
    ks = jax.random.split(key, 24)
    f32 = jnp.float32
    D = D_MODEL

    def w(k, shape, fan_in, scale=1.0):
        return jax.random.normal(k, shape, f32) * (fan_in ** -0.5) * scale

    def gain(k, shape):
        return 1.0 + 0.02 * jax.random.normal(k, shape, f32)

    def bias(k, shape):
        return 0.02 * jax.random.normal(k, shape, f32)

    x = jax.random.normal(ks[0], (BATCH, SEQ, D), f32)
    p = jax.random.normal(ks[1], (DEPTH, BATCH, SEQ, PLE_DIM), f32)
    offsets = jax.random.randint(ks[2], (BATCH, 1), 0, 1024, dtype=jnp.int32)
    positions = (jnp.arange(SEQ, dtype=jnp.int32)[None, :] + offsets).astype(jnp.int32)

    w_k = w(ks[10], (D, D), D)
    w_v = w(ks[11], (D, D), D, DEEPNORM_BETA)
    w_kv = jnp.concatenate([w_k, w_v], axis=-1)

    return {
        "x": x,
        "p": p,
        "positions": positions,
        "conv_w_in": w(ks[3], (N_A_LAYERS, D, 2 * D), D),
        "conv_b_in": bias(ks[4], (N_A_LAYERS, 2 * D)),
        "conv_dw": w(ks[5], (N_A_LAYERS, CONV_WIDTH, 1, D), CONV_WIDTH),
        "conv_dw_b": bias(ks[6], (N_A_LAYERS, D)),
        "conv_ln_g": gain(ks[7], (N_A_LAYERS, D)),
        "conv_ln_b": bias(ks[8], (N_A_LAYERS, D)),
        "conv_w_out": w(ks[9], (N_A_LAYERS, D, D), D, DEEPNORM_BETA),
        "kv_ln_g": gain(ks[12], (D,)),
        "kv_ln_b": bias(ks[13], (D,)),
        "w_kv": w_kv,
        "attn_w_q": w(ks[14], (N_B_LAYERS, D, N_GROUPS * D), D),
        "attn_w_o": w(ks[15], (N_B_LAYERS, D, D), D, DEEPNORM_BETA),
        "ln1_g": gain(ks[16], (DEPTH, D)),
        "ln1_b": bias(ks[17], (DEPTH, D)),
        "mlp_up": w(ks[18], (DEPTH, D, D_FF), D),
        "mlp_down": w(ks[19], (DEPTH, D_FF, D), D_FF, DEEPNORM_BETA),
        "ln2_g": gain(ks[20], (DEPTH, D)),
        "ln2_b": bias(ks[21], (DEPTH, D)),
        "ple_proj": w(ks[22], (DEPTH, PLE_DIM, D), PLE_DIM),
        "ple_gate": w(ks[23], (DEPTH, D, D), D),
    }


def _fwd_reference(x, p, positions, conv_w_in, conv_b_in, conv_dw, conv_dw_b, conv_ln_g, conv_ln_b,
              conv_w_out, kv_ln_g, kv_ln_b, w_kv, attn_w_q, attn_w_o, ln1_g, ln1_b, mlp_up, mlp_down,
              ln2_g, ln2_b, ple_proj, ple_gate):
    k = v = None
    for i in range(DEPTH):
        if i < N_A_LAYERS:
            mix = conformer_conv(x, conv_w_in[i], conv_b_in[i], conv_dw[i], conv_dw_b[i],
                                 conv_ln_g[i], conv_ln_b[i], conv_w_out[i])
        else:
            j = i - N_A_LAYERS
            mix = dilated_mixer(x, attn_w_q[j], k, v, attn_w_o[j], positions)
        x = layer_norm(DEEPNORM_ALPHA * x + mix, ln1_g[i], ln1_b[i])
        x = layer_norm(DEEPNORM_ALPHA * x + sq_relu_mlp(x, mlp_up[i], mlp_down[i]), ln2_g[i], ln2_b[i])
        x = x + per_layer_embedding(x, p[i], ple_proj[i], ple_gate[i])
        if i == N_A_LAYERS - 1:
            k, v = shared_kv(x, kv_ln_g, kv_ln_b, w_kv, positions)
    return x


import jax as _jax
import jax.numpy as _jnp

TWIN_FORMAT = 'train_step'
FWD_PARAMS = ['x', 'p', 'positions', 'conv_w_in', 'conv_b_in', 'conv_dw', 'conv_dw_b', 'conv_ln_g', 'conv_ln_b', 'conv_w_out', 'kv_ln_g', 'kv_ln_b', 'w_kv', 'attn_w_q', 'attn_w_o', 'ln1_g', 'ln1_b', 'mlp_up', 'mlp_down', 'ln2_g', 'ln2_b', 'ple_proj', 'ple_gate']
TWIN_WEIGHTS = ['conv_w_in', 'conv_b_in', 'conv_dw', 'conv_dw_b', 'conv_ln_g', 'conv_ln_b', 'conv_w_out', 'kv_ln_g', 'kv_ln_b', 'w_kv', 'attn_w_q', 'attn_w_o', 'ln1_g', 'ln1_b', 'mlp_up', 'mlp_down', 'ln2_g', 'ln2_b', 'ple_proj', 'ple_gate']
TWIN_DIFF_INPUT = 'x'
TWIN_INPUTS = ['x', 'p', 'positions', 'conv_w_in', 'conv_b_in', 'conv_dw', 'conv_dw_b', 'conv_ln_g', 'conv_ln_b', 'conv_w_out', 'kv_ln_g', 'kv_ln_b', 'w_kv', 'attn_w_q', 'attn_w_o', 'ln1_g', 'ln1_b', 'mlp_up', 'mlp_down', 'ln2_g', 'ln2_b', 'ple_proj', 'ple_gate', 'loss_target', 'm_conv_w_in', 'm_conv_b_in', 'm_conv_dw', 'm_conv_dw_b', 'm_conv_ln_g', 'm_conv_ln_b', 'm_conv_w_out', 'm_kv_ln_g', 'm_kv_ln_b', 'm_w_kv', 'm_attn_w_q', 'm_attn_w_o', 'm_ln1_g', 'm_ln1_b', 'm_mlp_up', 'm_mlp_down', 'm_ln2_g', 'm_ln2_b', 'm_ple_proj', 'm_ple_gate', 'v_conv_w_in', 'v_conv_b_in', 'v_conv_dw', 'v_conv_dw_b', 'v_conv_ln_g', 'v_conv_ln_b', 'v_conv_w_out', 'v_kv_ln_g', 'v_kv_ln_b', 'v_w_kv', 'v_attn_w_q', 'v_attn_w_o', 'v_ln1_g', 'v_ln1_b', 'v_mlp_up', 'v_mlp_down', 'v_ln2_g', 'v_ln2_b', 'v_ple_proj', 'v_ple_gate']
TWIN_OUTPUTS = ['loss', 'grad_x', 'grad_conv_w_in', 'grad_conv_b_in', 'grad_conv_dw', 'grad_conv_dw_b', 'grad_conv_ln_g', 'grad_conv_ln_b', 'grad_conv_w_out', 'grad_kv_ln_g', 'grad_kv_ln_b', 'grad_w_kv', 'grad_attn_w_q', 'grad_attn_w_o', 'grad_ln1_g', 'grad_ln1_b', 'grad_mlp_up', 'grad_mlp_down', 'grad_ln2_g', 'grad_ln2_b', 'grad_ple_proj', 'grad_ple_gate', 'delta_conv_w_in', 'delta_conv_b_in', 'delta_conv_dw', 'delta_conv_dw_b', 'delta_conv_ln_g', 'delta_conv_ln_b', 'delta_conv_w_out', 'delta_kv_ln_g', 'delta_kv_ln_b', 'delta_w_kv', 'delta_attn_w_q', 'delta_attn_w_o', 'delta_ln1_g', 'delta_ln1_b', 'delta_mlp_up', 'delta_mlp_down', 'delta_ln2_g', 'delta_ln2_b', 'delta_ple_proj', 'delta_ple_gate', 'new_m_conv_w_in', 'new_m_conv_b_in', 'new_m_conv_dw', 'new_m_conv_dw_b', 'new_m_conv_ln_g', 'new_m_conv_ln_b', 'new_m_conv_w_out', 'new_m_kv_ln_g', 'new_m_kv_ln_b', 'new_m_w_kv', 'new_m_attn_w_q', 'new_m_attn_w_o', 'new_m_ln1_g', 'new_m_ln1_b', 'new_m_mlp_up', 'new_m_mlp_down', 'new_m_ln2_g', 'new_m_ln2_b', 'new_m_ple_proj', 'new_m_ple_gate', 'new_v_conv_w_in', 'new_v_conv_b_in', 'new_v_conv_dw', 'new_v_conv_dw_b', 'new_v_conv_ln_g', 'new_v_conv_ln_b', 'new_v_conv_w_out', 'new_v_kv_ln_g', 'new_v_kv_ln_b', 'new_v_w_kv', 'new_v_attn_w_q', 'new_v_attn_w_o', 'new_v_ln1_g', 'new_v_ln1_b', 'new_v_mlp_up', 'new_v_mlp_down', 'new_v_ln2_g', 'new_v_ln2_b', 'new_v_ple_proj', 'new_v_ple_gate']
TWIN_LEAF_KINDS = {'loss': 'loss', 'grad_x': 'grad_x', 'grad_conv_w_in': 'grad_w', 'grad_conv_b_in': 'grad_w', 'grad_conv_dw': 'grad_w', 'grad_conv_dw_b': 'grad_w', 'grad_conv_ln_g': 'grad_w', 'grad_conv_ln_b': 'grad_w', 'grad_conv_w_out': 'grad_w', 'grad_kv_ln_g': 'grad_w', 'grad_kv_ln_b': 'grad_w', 'grad_w_kv': 'grad_w', 'grad_attn_w_q': 'grad_w', 'grad_attn_w_o': 'grad_w', 'grad_ln1_g': 'grad_w', 'grad_ln1_b': 'grad_w', 'grad_mlp_up': 'grad_w', 'grad_mlp_down': 'grad_w', 'grad_ln2_g': 'grad_w', 'grad_ln2_b': 'grad_w', 'grad_ple_proj': 'grad_w', 'grad_ple_gate': 'grad_w', 'delta_conv_w_in': 'delta_w', 'delta_conv_b_in': 'delta_w', 'delta_conv_dw': 'delta_w', 'delta_conv_dw_b': 'delta_w', 'delta_conv_ln_g': 'delta_w', 'delta_conv_ln_b': 'delta_w', 'delta_conv_w_out': 'delta_w', 'delta_kv_ln_g': 'delta_w', 'delta_kv_ln_b': 'delta_w', 'delta_w_kv': 'delta_w', 'delta_attn_w_q': 'delta_w', 'delta_attn_w_o': 'delta_w', 'delta_ln1_g': 'delta_w', 'delta_ln1_b': 'delta_w', 'delta_mlp_up': 'delta_w', 'delta_mlp_down': 'delta_w', 'delta_ln2_g': 'delta_w', 'delta_ln2_b': 'delta_w', 'delta_ple_proj': 'delta_w', 'delta_ple_gate': 'delta_w', 'new_m_conv_w_in': 'new_m', 'new_m_conv_b_in': 'new_m', 'new_m_conv_dw': 'new_m', 'new_m_conv_dw_b': 'new_m', 'new_m_conv_ln_g': 'new_m', 'new_m_conv_ln_b': 'new_m', 'new_m_conv_w_out': 'new_m', 'new_m_kv_ln_g': 'new_m', 'new_m_kv_ln_b': 'new_m', 'new_m_w_kv': 'new_m', 'new_m_attn_w_q': 'new_m', 'new_m_attn_w_o': 'new_m', 'new_m_ln1_g': 'new_m', 'new_m_ln1_b': 'new_m', 'new_m_mlp_up': 'new_m', 'new_m_mlp_down': 'new_m', 'new_m_ln2_g': 'new_m', 'new_m_ln2_b': 'new_m', 'new_m_ple_proj': 'new_m', 'new_m_ple_gate': 'new_m', 'new_v_conv_w_in': 'new_v', 'new_v_conv_b_in': 'new_v', 'new_v_conv_dw': 'new_v', 'new_v_conv_dw_b': 'new_v', 'new_v_conv_ln_g': 'new_v', 'new_v_conv_ln_b': 'new_v', 'new_v_conv_w_out': 'new_v', 'new_v_kv_ln_g': 'new_v', 'new_v_kv_ln_b': 'new_v', 'new_v_w_kv': 'new_v', 'new_v_attn_w_q': 'new_v', 'new_v_attn_w_o': 'new_v', 'new_v_ln1_g': 'new_v', 'new_v_ln1_b': 'new_v', 'new_v_mlp_up': 'new_v', 'new_v_mlp_down': 'new_v', 'new_v_ln2_g': 'new_v', 'new_v_ln2_b': 'new_v', 'new_v_ple_proj': 'new_v', 'new_v_ple_gate': 'new_v'}


def _forward(args):
    return _fwd_reference(*[args[k] for k in FWD_PARAMS])


def _output_shape():
    def fwd():
        inp = _fwd_setup_inputs(0)
        return _fwd_reference(*[inp[k] for k in FWD_PARAMS])
    out = _jax.eval_shape(fwd)
    return out.shape, out.dtype

N_MICROBATCH = 1
ADAM_LR = 0.001
ADAM_B1 = 0.9
ADAM_B2 = 0.999
ADAM_EPS = 1e-08
ADAM_WD = 0.01
ADAM_STEP = 10
PER_EXAMPLE_BATCH_AXIS = {'x': 0, 'p': 1, 'positions': 0, 'loss_target': 0}
SHARED_INPUTS = []
_WEIGHT_DTYPES = {'conv_w_in': _jnp.float32, 'conv_b_in': _jnp.float32, 'conv_dw': _jnp.float32, 'conv_dw_b': _jnp.float32, 'conv_ln_g': _jnp.float32, 'conv_ln_b': _jnp.float32, 'conv_w_out': _jnp.float32, 'kv_ln_g': _jnp.float32, 'kv_ln_b': _jnp.float32, 'w_kv': _jnp.float32, 'attn_w_q': _jnp.float32, 'attn_w_o': _jnp.float32, 'ln1_g': _jnp.float32, 'ln1_b': _jnp.float32, 'mlp_up': _jnp.float32, 'mlp_down': _jnp.float32, 'ln2_g': _jnp.float32, 'ln2_b': _jnp.float32, 'ple_proj': _jnp.float32, 'ple_gate': _jnp.float32}
MOMENT_SCALE = {'conv_w_in': 1.642212e-02, 'conv_b_in': 1.781090e-01, 'conv_dw': 3.065479e-02, 'conv_dw_b': 4.285877e-01, 'conv_ln_g': 1.651686e-01, 'conv_ln_b': 2.449792e-01, 'conv_w_out': 1.873426e-01, 'kv_ln_g': 3.429599e-02, 'kv_ln_b': 2.490290e-01, 'w_kv': 5.327828e-02, 'attn_w_q': 1.874516e-03, 'attn_w_o': 7.450130e-02, 'ln1_g': 4.697157e-01, 'ln1_b': 1.528289e+00, 'mlp_up': 3.059698e-02, 'mlp_down': 5.007290e-01, 'ln2_g': 1.170993e+01, 'ln2_b': 2.982607e+00, 'ple_proj': 1.823444e-01, 'ple_gate': 2.468670e-01}


def _to_microbatches(a, axis):
    t = _jnp.moveaxis(a, axis, 0)
    t = t.reshape((N_MICROBATCH, t.shape[0] // N_MICROBATCH) + t.shape[1:])
    return _jnp.moveaxis(t, 1, axis + 1)


def setup_inputs(seed: int = 0) -> dict:
    inp = _fwd_setup_inputs(seed)
    key = _jax.random.fold_in(_jax.random.key(seed), 7919)
    shape, _ = _output_shape()
    out = dict(inp)
    out["loss_target"] = _jax.random.normal(_jax.random.fold_in(key, 0), shape, _jnp.float32)
    for i, name in enumerate(TWIN_WEIGHTS):
        w = inp[name].astype(_jnp.float32)
        if MOMENT_SCALE is None:
            s = _jnp.sqrt(_jnp.mean(_jnp.square(w)) + 1e-30)
        else:
            s = MOMENT_SCALE[name]
        km, kv = _jax.random.split(_jax.random.fold_in(key, i + 1))
        out[name] = w
        out["m_" + name] = s * _jax.random.normal(km, w.shape, _jnp.float32)
        out["v_" + name] = (s * s) * _jax.random.uniform(kv, w.shape, _jnp.float32, 0.5, 1.5)
    if N_MICROBATCH > 1:
        for name, axis in PER_EXAMPLE_BATCH_AXIS.items():
            out[name] = _to_microbatches(out[name], axis)
    return {'x': out['x'], 'p': out['p'], 'positions': out['positions'], 'conv_w_in': out['conv_w_in'], 'conv_b_in': out['conv_b_in'], 'conv_dw': out['conv_dw'], 'conv_dw_b': out['conv_dw_b'], 'conv_ln_g': out['conv_ln_g'], 'conv_ln_b': out['conv_ln_b'], 'conv_w_out': out['conv_w_out'], 'kv_ln_g': out['kv_ln_g'], 'kv_ln_b': out['kv_ln_b'], 'w_kv': out['w_kv'], 'attn_w_q': out['attn_w_q'], 'attn_w_o': out['attn_w_o'], 'ln1_g': out['ln1_g'], 'ln1_b': out['ln1_b'], 'mlp_up': out['mlp_up'], 'mlp_down': out['mlp_down'], 'ln2_g': out['ln2_g'], 'ln2_b': out['ln2_b'], 'ple_proj': out['ple_proj'], 'ple_gate': out['ple_gate'], 'loss_target': out['loss_target'], 'm_conv_w_in': out['m_conv_w_in'], 'm_conv_b_in': out['m_conv_b_in'], 'm_conv_dw': out['m_conv_dw'], 'm_conv_dw_b': out['m_conv_dw_b'], 'm_conv_ln_g': out['m_conv_ln_g'], 'm_conv_ln_b': out['m_conv_ln_b'], 'm_conv_w_out': out['m_conv_w_out'], 'm_kv_ln_g': out['m_kv_ln_g'], 'm_kv_ln_b': out['m_kv_ln_b'], 'm_w_kv': out['m_w_kv'], 'm_attn_w_q': out['m_attn_w_q'], 'm_attn_w_o': out['m_attn_w_o'], 'm_ln1_g': out['m_ln1_g'], 'm_ln1_b': out['m_ln1_b'], 'm_mlp_up': out['m_mlp_up'], 'm_mlp_down': out['m_mlp_down'], 'm_ln2_g': out['m_ln2_g'], 'm_ln2_b': out['m_ln2_b'], 'm_ple_proj': out['m_ple_proj'], 'm_ple_gate': out['m_ple_gate'], 'v_conv_w_in': out['v_conv_w_in'], 'v_conv_b_in': out['v_conv_b_in'], 'v_conv_dw': out['v_conv_dw'], 'v_conv_dw_b': out['v_conv_dw_b'], 'v_conv_ln_g': out['v_conv_ln_g'], 'v_conv_ln_b': out['v_conv_ln_b'], 'v_conv_w_out': out['v_conv_w_out'], 'v_kv_ln_g': out['v_kv_ln_g'], 'v_kv_ln_b': out['v_kv_ln_b'], 'v_w_kv': out['v_w_kv'], 'v_attn_w_q': out['v_attn_w_q'], 'v_attn_w_o': out['v_attn_w_o'], 'v_ln1_g': out['v_ln1_g'], 'v_ln1_b': out['v_ln1_b'], 'v_mlp_up': out['v_mlp_up'], 'v_mlp_down': out['v_mlp_down'], 'v_ln2_g': out['v_ln2_g'], 'v_ln2_b': out['v_ln2_b'], 'v_ple_proj': out['v_ple_proj'], 'v_ple_gate': out['v_ple_gate']}


def _loss(weights, diff, rest, loss_target):
    with _jax.named_scope("forward"):
        args = {**rest, TWIN_DIFF_INPUT: diff, **{k: w.astype(_WEIGHT_DTYPES[k]) for k, w in weights.items()}}
        y = _forward(args)
    with _jax.named_scope("loss_head"):
        err = _jnp.square(y.astype(_jnp.float32) - loss_target)
        return 0.5 * _jnp.sum(_jnp.mean(err, axis=-1)) if err.ndim else 0.5 * err


def _adamw(w, g, m, v):
    m = ADAM_B1 * m + (1.0 - ADAM_B1) * g
    v = ADAM_B2 * v + (1.0 - ADAM_B2) * _jnp.square(g)
    m_hat = m / (1.0 - ADAM_B1 ** ADAM_STEP)
    v_hat = v / (1.0 - ADAM_B2 ** ADAM_STEP)
    delta = -ADAM_LR * (m_hat / (_jnp.sqrt(v_hat) + ADAM_EPS) + ADAM_WD * w)
    return delta, m, v


def reference(x, p, positions, conv_w_in, conv_b_in, conv_dw, conv_dw_b, conv_ln_g, conv_ln_b, conv_w_out, kv_ln_g, kv_ln_b, w_kv, attn_w_q, attn_w_o, ln1_g, ln1_b, mlp_up, mlp_down, ln2_g, ln2_b, ple_proj, ple_gate, loss_target, m_conv_w_in, m_conv_b_in, m_conv_dw, m_conv_dw_b, m_conv_ln_g, m_conv_ln_b, m_conv_w_out, m_kv_ln_g, m_kv_ln_b, m_w_kv, m_attn_w_q, m_attn_w_o, m_ln1_g, m_ln1_b, m_mlp_up, m_mlp_down, m_ln2_g, m_ln2_b, m_ple_proj, m_ple_gate, v_conv_w_in, v_conv_b_in, v_conv_dw, v_conv_dw_b, v_conv_ln_g, v_conv_ln_b, v_conv_w_out, v_kv_ln_g, v_kv_ln_b, v_w_kv, v_attn_w_q, v_attn_w_o, v_ln1_g, v_ln1_b, v_mlp_up, v_mlp_down, v_ln2_g, v_ln2_b, v_ple_proj, v_ple_gate):
    given = dict(x=x, p=p, positions=positions, conv_w_in=conv_w_in, conv_b_in=conv_b_in, conv_dw=conv_dw, conv_dw_b=conv_dw_b, conv_ln_g=conv_ln_g, conv_ln_b=conv_ln_b, conv_w_out=conv_w_out, kv_ln_g=kv_ln_g, kv_ln_b=kv_ln_b, w_kv=w_kv, attn_w_q=attn_w_q, attn_w_o=attn_w_o, ln1_g=ln1_g, ln1_b=ln1_b, mlp_up=mlp_up, mlp_down=mlp_down, ln2_g=ln2_g, ln2_b=ln2_b, ple_proj=ple_proj, ple_gate=ple_gate, loss_target=loss_target, m_conv_w_in=m_conv_w_in, m_conv_b_in=m_conv_b_in, m_conv_dw=m_conv_dw, m_conv_dw_b=m_conv_dw_b, m_conv_ln_g=m_conv_ln_g, m_conv_ln_b=m_conv_ln_b, m_conv_w_out=m_conv_w_out, m_kv_ln_g=m_kv_ln_g, m_kv_ln_b=m_kv_ln_b, m_w_kv=m_w_kv, m_attn_w_q=m_attn_w_q, m_attn_w_o=m_attn_w_o, m_ln1_g=m_ln1_g, m_ln1_b=m_ln1_b, m_mlp_up=m_mlp_up, m_mlp_down=m_mlp_down, m_ln2_g=m_ln2_g, m_ln2_b=m_ln2_b, m_ple_proj=m_ple_proj, m_ple_gate=m_ple_gate, v_conv_w_in=v_conv_w_in, v_conv_b_in=v_conv_b_in, v_conv_dw=v_conv_dw, v_conv_dw_b=v_conv_dw_b, v_conv_ln_g=v_conv_ln_g, v_conv_ln_b=v_conv_ln_b, v_conv_w_out=v_conv_w_out, v_kv_ln_g=v_kv_ln_g, v_kv_ln_b=v_kv_ln_b, v_w_kv=v_w_kv, v_attn_w_q=v_attn_w_q, v_attn_w_o=v_attn_w_o, v_ln1_g=v_ln1_g, v_ln1_b=v_ln1_b, v_mlp_up=v_mlp_up, v_mlp_down=v_mlp_down, v_ln2_g=v_ln2_g, v_ln2_b=v_ln2_b, v_ple_proj=v_ple_proj, v_ple_gate=v_ple_gate)
    weights = {n: given[n] for n in TWIN_WEIGHTS}
    shared = {n: given[n] for n in SHARED_INPUTS}
    per_example = {n: given[n] for n in ['x', 'p', 'positions']}
    grad_fn = _jax.value_and_grad(_loss, argnums=(0, 1))

    def one_microbatch(ex, loss_target):
        ex = dict(ex)
        diff = ex.pop(TWIN_DIFF_INPUT)
        return grad_fn(weights, diff, {**shared, **ex}, loss_target)

    if N_MICROBATCH == 1:
        loss, (grad_w, grad_x) = one_microbatch(per_example, given["loss_target"])
    else:
        def body(carry, xs):
            loss_sum, grad_sum = carry
            l_k, (gw_k, gx_k) = one_microbatch(xs[0], xs[1])
            with _jax.named_scope("update"):
                return (loss_sum + l_k, _jax.tree.map(_jnp.add, grad_sum, gw_k)), gx_k

        init = (_jnp.zeros((), _jnp.float32), _jax.tree.map(_jnp.zeros_like, weights))
        (loss, grad_w), grad_x = _jax.lax.scan(body, init, (per_example, given["loss_target"]))
    with _jax.named_scope("update"):
        delta_w, new_m, new_v = {}, {}, {}
        for n in TWIN_WEIGHTS:
            delta_w[n], new_m[n], new_v[n] = _adamw(weights[n], grad_w[n], given["m_" + n], given["v_" + n])
    return (loss, grad_x, *[grad_w[n] for n in TWIN_WEIGHTS], *[delta_w[n] for n in TWIN_WEIGHTS],
            *[new_m[n] for n in TWIN_WEIGHTS], *[new_v[n] for n in TWIN_WEIGHTS])
```

```python
import functools

import numpy as np
import jax
import jax.numpy as jnp
from jax import lax
from jax.experimental import pallas as pl
from jax.experimental.pallas import tpu as pltpu

F32, BF16 = jnp.float32, jnp.bfloat16

N_DEV = 8
HEAD_DIM = 128
ATTN_BLOCK = 128
GROUP_DILATIONS = (1, 4, 16)
N_GROUPS = len(GROUP_DILATIONS)
CONV_WIDTH = 31
CONV_HALO = 32
CONV_ROWS = 64
ROPE_THETA = 10000.0
LN_EPS = 1e-5
DEPTH = 2
ALPHA = (2 * DEPTH) ** 0.25
ADAM_LR, ADAM_B1, ADAM_B2, ADAM_EPS, ADAM_WD, ADAM_STEP = 0.001, 0.9, 0.999, 1e-08, 0.01, 10
NEG = -1e30
V7X_VMEM_LIMIT = 56 * 2 ** 20
LANE = 128
GRAD_DTYPE = BF16
SMALL_ROWS = 48
PACK_ROWS = 40

MESH = pl.DeviceIdType.MESH
ANY = pl.BlockSpec(memory_space=pl.ANY)


def _params(*sem):
    return pltpu.CompilerParams(dimension_semantics=sem or None, vmem_limit_bytes=V7X_VMEM_LIMIT)


def _sigmoid(x):
    return 1.0 / (1.0 + jnp.exp(-x))


def _divisor(n, most):
    best = None
    for t in range(LANE, min(n, most) + 1, LANE):
        if n % t == 0:
            best = t
    assert best is not None, (n, most)
    return best


class W:
    def __init__(self, arr, blocked):
        self.arr, self.blocked = arr, blocked
        if blocked:
            self.nb, self.k, self.nblk = arr.shape
            self.n = self.nb * self.nblk
        else:
            self.k, self.n = arr.shape
            self.nblk = self.n

    def spec(self, t_rows, t_cols, idx):
        if not self.blocked:
            return pl.BlockSpec((t_rows, t_cols), idx)
        per = self.nblk // t_cols

        def index(*g):
            rt, ct = idx(*g)
            return (ct // per, rt, ct % per)

        return pl.BlockSpec((None, t_rows, t_cols), index)


def _matmul(name, grid, operands, specs, dims, tile, extras, outs, out_specs, epilogue):
    steps = grid[2]
    n_ex, n_out = len(extras), len(outs)

    def body(*refs):
        a_ref, b_ref = refs[0], refs[1]
        ex_refs = refs[2:2 + n_ex]
        out_refs = refs[2 + n_ex:2 + n_ex + n_out]
        part = lax.dot_general(a_ref[...].astype(BF16), b_ref[...].astype(BF16), (dims, ((), ())),
                               preferred_element_type=F32)

        def finish(acc):
            res = epilogue(acc, *[r[...] for r in ex_refs]) if epilogue else (acc,) * n_out
            for r, v in zip(out_refs, res):
                r[...] = v.astype(r.dtype)

        if steps == 1:
            finish(part)
        else:
            acc_ref = refs[2 + n_ex + n_out]
            c = pl.program_id(2)

            @pl.when(c == 0)
            def _():
                acc_ref[...] = part

            @pl.when(c > 0)
            def _():
                acc_ref[...] += part

            @pl.when(c == steps - 1)
            def _():
                finish(acc_ref[...])

    return pl.pallas_call(
        body, name=name, grid=grid,
        in_specs=list(specs) + [s for _, s in extras],
        out_specs=out_specs, out_shape=outs,
        scratch_shapes=[pltpu.VMEM(tile, F32)] if steps > 1 else [],
        compiler_params=_params("parallel", "parallel", "arbitrary"),
    )(*operands, *[a for a, _ in extras])


def _extra_specs(extras, tm, tn):
    out = []
    for arr, kind in extras:
        if kind == "tile":
            out.append((arr, pl.BlockSpec((tm, tn), lambda i, j, c: (i, j))))
        else:
            out.append((arr, pl.BlockSpec((1, tn), lambda i, j, c: (0, j))))
    return out


def _mm_nn(name, a, w, out_dtypes, epilogue=None, extras=()):
    m, k = a.shape
    assert k == w.k
    tm = 1024 if a.dtype == BF16 else 512
    tn = _divisor(w.nblk, 512) if w.nblk % 512 == 0 else w.nblk
    tk = min(k, 2048)
    grid = (m // tm, w.n // tn, k // tk)
    specs = [pl.BlockSpec((tm, tk), lambda i, j, c: (i, c)), w.spec(tk, tn, lambda i, j, c: (c, j))]
    outs = [jax.ShapeDtypeStruct((m, w.n), d) for d in out_dtypes]
    out_specs = [pl.BlockSpec((tm, tn), lambda i, j, c: (i, j)) for _ in outs]
    return _matmul(name, grid, (a, w.arr), specs, ((1,), (0,)), (tm, tn), _extra_specs(extras, tm, tn), outs, out_specs, epilogue)


def _mm_nt(name, dy, w, out_dtypes, epilogue=None, extras=()):
    m, n = dy.shape
    assert n == w.n
    tm = 1024
    to = _divisor(w.k, 512)
    tc = w.nblk if w.blocked else min(n, 2048)
    grid = (m // tm, w.k // to, n // tc)
    specs = [pl.BlockSpec((tm, tc), lambda i, j, c: (i, c)), w.spec(to, tc, lambda i, j, c: (j, c))]
    outs = [jax.ShapeDtypeStruct((m, w.k), d) for d in out_dtypes]
    out_specs = [pl.BlockSpec((tm, to), lambda i, j, c: (i, j)) for _ in outs]
    return _matmul(name, grid, (dy, w.arr), specs, ((1,), (1,)), (tm, to), _extra_specs(extras, tm, to), outs, out_specs, epilogue)


def _mm_tn(name, a, dy, like):
    m, k = a.shape
    n = dy.shape[1]
    assert (k, n) == (like.k, like.n)
    tk = _divisor(k, 1024)
    tn = _divisor(like.nblk, 1024)
    tc = 512
    grid = (k // tk, n // tn, m // tc)
    specs = [pl.BlockSpec((tc, tk), lambda i, j, c: (c, i)), pl.BlockSpec((tc, tn), lambda i, j, c: (c, j))]
    if like.blocked:
        out = W(jax.ShapeDtypeStruct((like.nb, k, like.nblk), GRAD_DTYPE), True)
    else:
        out = W(jax.ShapeDtypeStruct((k, n), GRAD_DTYPE), False)
    out_specs = [out.spec(tk, tn, lambda i, j, c: (i, j))]
    (g,) = _matmul(name, grid, (a, dy), specs, ((0,), (0,)), (tk, tn), [], [out.arr], out_specs, None)
    return g if like.blocked else g.reshape(N_DEV, k // N_DEV, n)


def _rows(arr, blk=0, width=None):
    return ("rows", arr, blk, width or arr.shape[1])


def _full(arr):
    return ("full", arr)


def _rowwise(name, fn, ins, outs, reds=(), ts=256):
    s = next(i[1].shape[0] for i in ins if i[0] == "rows")
    n_in, n_out = len(ins), len(outs)
    in_specs = []
    for i in ins:
        if i[0] == "rows":
            in_specs.append(pl.BlockSpec((ts, i[3]), functools.partial(lambda t, blk: (t, blk), blk=i[2])))
        else:
            in_specs.append(pl.BlockSpec(i[1].shape, functools.partial(lambda t, nd: (0,) * nd, nd=i[1].ndim)))
    out_shape = [jax.ShapeDtypeStruct((s, w), d) for w, d in outs] + [jax.ShapeDtypeStruct(r, F32) for r in reds]
    out_specs = [pl.BlockSpec((ts, w), lambda t: (t, 0)) for w, _ in outs] + [pl.BlockSpec(r, lambda t: (0, 0)) for r in reds]

    def body(*refs):
        red_refs = refs[n_in + n_out:]
        if red_refs:
            @pl.when(pl.program_id(0) == 0)
            def _():
                for r in red_refs:
                    r[...] = jnp.zeros(r.shape, F32)
        fn(refs[:n_in], refs[n_in:n_in + n_out], red_refs)

    return pl.pallas_call(
        body, name=name, grid=(s // ts,), in_specs=in_specs, out_specs=out_specs, out_shape=out_shape,
        compiler_params=_params("arbitrary" if reds else "parallel"),
    )(*[i[1] for i in ins])


def _ln_stats(x):
    mu = jnp.mean(x, axis=-1, keepdims=True)
    xc = x - mu
    var = jnp.mean(xc * xc, axis=-1, keepdims=True)
    return xc * lax.rsqrt(var + LN_EPS), lax.rsqrt(var + LN_EPS)


def _layer_norm(name, x, g, b, out_dtypes):
    def fn(i, o, r):
        xhat, _ = _ln_stats(i[0][...])
        y = xhat * i[1][...] + i[2][...]
        for ref in o:
            ref[...] = y.astype(ref.dtype)

    return _rowwise(name, fn, [_rows(x), _full(g), _full(b)], [(x.shape[1], d) for d in out_dtypes])


def _ln_bwd_tile(x, g, dy):
    xhat, rstd = _ln_stats(x)
    dyg = dy * g
    m1 = jnp.mean(dyg, axis=-1, keepdims=True)
    m2 = jnp.mean(dyg * xhat, axis=-1, keepdims=True)
    dx = rstd * (dyg - m1 - xhat * m2)
    return dx, jnp.sum(dy * xhat, axis=0, keepdims=True), jnp.sum(dy, axis=0, keepdims=True)


def _layer_norm_bwd(name, x, g, dy, extra=None):
    d = x.shape[1]

    def fn(i, o, r):
        dx, dg, db = _ln_bwd_tile(i[0][...], i[1][...], i[2][...])
        if extra is not None:
            dx = dx + i[3][...]
        o[0][...] = dx
        o[1][...] = dx.astype(BF16)
        r[0][...] += dg
        r[1][...] += db

    ins = [_rows(x), _full(g), _rows(dy)] + ([_rows(extra)] if extra is not None else [])
    return _rowwise(name, fn, ins, [(d, F32), (d, BF16)], [(1, d), (1, d)])


def _conv_fwd(name, glu, dw, dw_b, ts=512):
    s, c = glu.shape
    tc = dw.shape[2]
    per = ts // CONV_HALO
    back = CONV_HALO - (CONV_WIDTH - 1)

    def body(cur_ref, prev_ref, w_ref, b_ref, out_ref, buf):
        i = pl.program_id(1)
        buf[pl.ds(0, CONV_HALO), :] = jnp.where(i > 0, prev_ref[...], 0.0)
        buf[pl.ds(CONV_HALO, ts), :] = cur_ref[...]
        for r0 in range(0, ts, CONV_ROWS):
            acc = jnp.broadcast_to(b_ref[...], (CONV_ROWS, tc))
            for j in range(CONV_WIDTH):
                acc = acc + w_ref[j:j + 1, :] * buf[pl.ds(r0 + back + j, CONV_ROWS), :]
            out_ref[pl.ds(r0, CONV_ROWS), :] = acc

    return pl.pallas_call(
        body, name=name, grid=(c // tc, s // ts),
        in_specs=[pl.BlockSpec((ts, tc), lambda j, i: (i, j)),
                  pl.BlockSpec((CONV_HALO, tc), lambda j, i: (jnp.maximum(i * per - 1, 0), j)),
                  pl.BlockSpec((None, CONV_WIDTH, tc), lambda j, i: (j, 0, 0)),
                  pl.BlockSpec((1, tc), lambda j, i: (0, j))],
        out_specs=pl.BlockSpec((ts, tc), lambda j, i: (i, j)),
        out_shape=jax.ShapeDtypeStruct((s, c), F32),
        scratch_shapes=[pltpu.VMEM((ts + CONV_HALO, tc), F32)],
        compiler_params=_params("parallel", "parallel"),
    )(glu, glu, dw, dw_b)


def _conv_bwd(name, glu, dc, dw, ts=512):
    s, c = glu.shape
    tc = dw.shape[2]
    per = ts // CONV_HALO
    back = CONV_HALO - (CONV_WIDTH - 1)
    last = s // ts - 1

    def body(g_ref, gprev_ref, dc_ref, dcnext_ref, w_ref, dglu_ref, ddw_ref, ddb_ref, gbuf, dbuf):
        i = pl.program_id(1)

        @pl.when(i == 0)
        def _():
            ddw_ref[...] = jnp.zeros(ddw_ref.shape, F32)
            ddb_ref[...] = jnp.zeros(ddb_ref.shape, F32)

        gbuf[pl.ds(0, CONV_HALO), :] = jnp.where(i > 0, gprev_ref[...], 0.0)
        gbuf[pl.ds(CONV_HALO, ts), :] = g_ref[...]
        dbuf[pl.ds(0, ts), :] = dc_ref[...]
        dbuf[pl.ds(ts, CONV_HALO), :] = jnp.where(i < last, dcnext_ref[...], 0.0)
        taps = [jnp.zeros((1, tc), F32)] * CONV_WIDTH
        for r0 in range(0, ts, CONV_ROWS):
            d_here = dbuf[pl.ds(r0, CONV_ROWS), :]
            acc = jnp.zeros((CONV_ROWS, tc), F32)
            for j in range(CONV_WIDTH):
                acc = acc + w_ref[j:j + 1, :] * dbuf[pl.ds(r0 + (CONV_WIDTH - 1) - j, CONV_ROWS), :]
                taps[j] = taps[j] + jnp.sum(d_here * gbuf[pl.ds(r0 + back + j, CONV_ROWS), :], axis=0, keepdims=True)
            dglu_ref[pl.ds(r0, CONV_ROWS), :] = acc
        for j in range(CONV_WIDTH):
            ddw_ref[j:j + 1, :] += taps[j]
        ddb_ref[...] += jnp.sum(dc_ref[...], axis=0, keepdims=True)

    return pl.pallas_call(
        body, name=name, grid=(c // tc, s // ts),
        in_specs=[pl.BlockSpec((ts, tc), lambda j, i: (i, j)),
                  pl.BlockSpec((CONV_HALO, tc), lambda j, i: (jnp.maximum(i * per - 1, 0), j)),
                  pl.BlockSpec((ts, tc), lambda j, i: (i, j)),
                  pl.BlockSpec((CONV_HALO, tc), lambda j, i: (jnp.minimum((i + 1) * per, (last + 1) * per - 1), j)),
                  pl.BlockSpec((None, CONV_WIDTH, tc), lambda j, i: (j, 0, 0))],
        out_specs=[pl.BlockSpec((ts, tc), lambda j, i: (i, j)),
                   pl.BlockSpec((CONV_WIDTH, tc), lambda j, i: (0, j)),
                   pl.BlockSpec((1, tc), lambda j, i: (0, j))],
        out_shape=[jax.ShapeDtypeStruct((s, c), F32), jax.ShapeDtypeStruct((CONV_WIDTH, c), F32),
                   jax.ShapeDtypeStruct((1, c), F32)],
        scratch_shapes=[pltpu.VMEM((ts + CONV_HALO, tc), F32), pltpu.VMEM((ts + CONV_HALO, tc), F32)],
        compiler_params=_params("parallel", "arbitrary"),
    )(glu, glu, dc, dc, dw)


def _rope_tables(positions):
    half = HEAD_DIM // 2
    inv = (np.float32(ROPE_THETA) ** (-np.arange(half, dtype=np.float32) * np.float32(2.0 / HEAD_DIM))).astype(np.float32)
    inv_freq = jnp.asarray(np.concatenate([inv, inv])[None, :])
    sign = jnp.asarray(np.concatenate([-np.ones(half, np.float32), np.ones(half, np.float32)])[None, :])

    def fn(i, o, r):
        ang = i[0][...].astype(F32) * i[1][...]
        o[0][...] = jnp.cos(ang)
        o[1][...] = jnp.sin(ang) * i[2][...]

    return _rowwise("rope_tables", fn, [_rows(positions), _full(inv_freq), _full(sign)], [(HEAD_DIM, F32), (HEAD_DIM, F32)], ts=512)


def _rot(x, cos, sin):
    return x * cos + pltpu.roll(x, HEAD_DIM // 2, 1) * sin


def _masks(n):
    row = lax.broadcasted_iota(jnp.int32, (ATTN_BLOCK, ATTN_BLOCK), 0)
    col = lax.broadcasted_iota(jnp.int32, (ATTN_BLOCK, ATTN_BLOCK), 1)
    return col <= row, jnp.logical_and(col >= row, n > 0)


_NT = (((1,), (1,)), ((), ()))
_TN = (((0,), (0,)), ((), ()))
_NN = (((1,), (0,)), ((), ()))


def _attn_fwd(name, qr, kr, vb, g, dil):
    s, d = kr.shape
    nh = d // HEAD_DIM
    ln = s // dil
    nb = ln // ATTN_BLOCK
    scale = HEAD_DIM ** -0.5

    def body(q_ref, kc_ref, kp_ref, vc_ref, vp_ref, o_ref, l_ref):
        mask_c, mask_p = _masks(pl.program_id(1))
        for h in range(nh):
            hs = slice(h * HEAD_DIM, (h + 1) * HEAD_DIM)
            q = q_ref[:, hs]
            s_c = jnp.where(mask_c, lax.dot_general(q, kc_ref[:, hs], _NT, preferred_element_type=F32) * scale, NEG)
            s_p = jnp.where(mask_p, lax.dot_general(q, kp_ref[:, hs], _NT, preferred_element_type=F32) * scale, NEG)
            m = jnp.maximum(jnp.max(s_c, axis=1, keepdims=True), jnp.max(s_p, axis=1, keepdims=True))
            p_c = jnp.exp(s_c - m)
            p_p = jnp.exp(s_p - m)
            l = jnp.sum(p_c, axis=1, keepdims=True) + jnp.sum(p_p, axis=1, keepdims=True)
            o = (lax.dot_general(p_c.astype(BF16), vc_ref[:, hs], _NN, preferred_element_type=F32)
                 + lax.dot_general(p_p.astype(BF16), vp_ref[:, hs], _NN, preferred_element_type=F32))
            o_ref[:, hs] = o / l
            l_ref[:, hs] = jnp.broadcast_to(m + jnp.log(l), (ATTN_BLOCK, HEAD_DIM))

    cur = pl.BlockSpec((ATTN_BLOCK, d), lambda r, n: (n, r))
    prev = pl.BlockSpec((ATTN_BLOCK, d), lambda r, n: (jnp.maximum(n - 1, 0), r))
    o, l = pl.pallas_call(
        body, name=name, grid=(dil, nb),
        in_specs=[pl.BlockSpec((ATTN_BLOCK, d), lambda r, n: (n, r * N_GROUPS + g)), cur, prev, cur, prev],
        out_specs=[cur, cur],
        out_shape=[jax.ShapeDtypeStruct((ln, dil * d), F32)] * 2,
        compiler_params=_params("parallel", "parallel"),
    )(qr.reshape(ln, dil * N_GROUPS * d), kr.reshape(ln, dil * d), kr.reshape(ln, dil * d),
      vb.reshape(ln, dil * d), vb.reshape(ln, dil * d))
    return o.reshape(s, d), l.reshape(s, d)


def _attn_dq(name, qr, kr, vb, do, lse, dsum, g, dil):
    s, d = kr.shape
    nh = d // HEAD_DIM
    ln = s // dil
    nb = ln // ATTN_BLOCK
    scale = HEAD_DIM ** -0.5

    def body(q_ref, kc_ref, kp_ref, vc_ref, vp_ref, do_ref, l_ref, d_ref, dq_ref):
        mask_c, mask_p = _masks(pl.program_id(1))
        for h in range(nh):
            hs = slice(h * HEAD_DIM, (h + 1) * HEAD_DIM)
            q, dout = q_ref[:, hs], do_ref[:, hs]
            lrow = l_ref[:, h * HEAD_DIM:h * HEAD_DIM + 1]
            drow = d_ref[:, h * HEAD_DIM:h * HEAD_DIM + 1]
            dq = jnp.zeros((ATTN_BLOCK, HEAD_DIM), F32)
            for mask, k_ref, v_ref in ((mask_c, kc_ref, vc_ref), (mask_p, kp_ref, vp_ref)):
                sc = lax.dot_general(q, k_ref[:, hs], _NT, preferred_element_type=F32) * scale
                p = jnp.where(mask, jnp.exp(jnp.where(mask, sc, NEG) - lrow), 0.0)
                dp = lax.dot_general(dout, v_ref[:, hs], _NT, preferred_element_type=F32)
                ds = p * (dp - drow)
                dq = dq + lax.dot_general(ds.astype(BF16), k_ref[:, hs], _NN, preferred_element_type=F32)
            dq_ref[:, hs] = dq * scale

    cur = pl.BlockSpec((ATTN_BLOCK, d), lambda r, n: (n, r))
    prev = pl.BlockSpec((ATTN_BLOCK, d), lambda r, n: (jnp.maximum(n - 1, 0), r))
    view = lambda t: t.reshape(ln, dil * d)
    dq = pl.pallas_call(
        body, name=name, grid=(dil, nb),
        in_specs=[pl.BlockSpec((ATTN_BLOCK, d), lambda r, n: (n, r * N_GROUPS + g)), cur, prev, cur, prev, cur, cur, cur],
        out_specs=cur,
        out_shape=jax.ShapeDtypeStruct((ln, dil * d), F32),
        compiler_params=_params("parallel", "parallel"),
    )(qr.reshape(ln, dil * N_GROUPS * d), view(kr), view(kr), view(vb), view(vb), view(do), view(lse), view(dsum))
    return dq.reshape(s, d)


def _attn_dkv(name, qr, kr, vb, do, lse, dsum, g, dil):
    s, d = kr.shape
    nh = d // HEAD_DIM
    ln = s // dil
    nb = ln // ATTN_BLOCK
    scale = HEAD_DIM ** -0.5

    def body(k_ref, v_ref, qc_ref, qn_ref, doc_ref, don_ref, lc_ref, lnx_ref, dc_ref, dn_ref, dk_ref, dv_ref):
        n = pl.program_id(1)
        row = lax.broadcasted_iota(jnp.int32, (ATTN_BLOCK, ATTN_BLOCK), 0)
        col = lax.broadcasted_iota(jnp.int32, (ATTN_BLOCK, ATTN_BLOCK), 1)
        masks = (col <= row, jnp.logical_and(col >= row, n < nb - 1))
        for h in range(nh):
            hs = slice(h * HEAD_DIM, (h + 1) * HEAD_DIM)
            k, v = k_ref[:, hs], v_ref[:, hs]
            dk = jnp.zeros((ATTN_BLOCK, HEAD_DIM), F32)
            dv = jnp.zeros((ATTN_BLOCK, HEAD_DIM), F32)
            for mask, q_ref, do_ref, l_ref, d_ref in ((masks[0], qc_ref, doc_ref, lc_ref, dc_ref),
                                                      (masks[1], qn_ref, don_ref, lnx_ref, dn_ref)):
                q, dout = q_ref[:, hs], do_ref[:, hs]
                lrow = l_ref[:, h * HEAD_DIM:h * HEAD_DIM + 1]
                drow = d_ref[:, h * HEAD_DIM:h * HEAD_DIM + 1]
                sc = lax.dot_general(q, k, _NT, preferred_element_type=F32) * scale
                p = jnp.where(mask, jnp.exp(jnp.where(mask, sc, NEG) - lrow), 0.0)
                dp = lax.dot_general(dout, v, _NT, preferred_element_type=F32)
                ds = p * (dp - drow)
                dv = dv + lax.dot_general(p.astype(BF16), dout, _TN, preferred_element_type=F32)
                dk = dk + lax.dot_general(ds.astype(BF16), q, _TN, preferred_element_type=F32)
            dk_ref[:, hs] = dk * scale
            dv_ref[:, hs] = dv

    cur = pl.BlockSpec((ATTN_BLOCK, d), lambda r, n: (n, r))
    nxt = pl.BlockSpec((ATTN_BLOCK, d), lambda r, n: (jnp.minimum(n + 1, nb - 1), r))
    qcur = pl.BlockSpec((ATTN_BLOCK, d), lambda r, n: (n, r * N_GROUPS + g))
    qnxt = pl.BlockSpec((ATTN_BLOCK, d), lambda r, n: (jnp.minimum(n + 1, nb - 1), r * N_GROUPS + g))
    view = lambda t: t.reshape(ln, dil * d)
    qv = qr.reshape(ln, dil * N_GROUPS * d)
    dk, dv = pl.pallas_call(
        body, name=name, grid=(dil, nb),
        in_specs=[cur, cur, qcur, qnxt, cur, nxt, cur, nxt, cur, nxt],
        out_specs=[cur, cur],
        out_shape=[jax.ShapeDtypeStruct((ln, dil * d), F32)] * 2,
        compiler_params=_params("parallel", "parallel"),
    )(view(kr), view(vb), qv, qv, view(do), view(do), view(lse), view(lse), view(dsum), view(dsum))
    return dk.reshape(s, d), dv.reshape(s, d)


def _exchange(name, gathers, scatters):
    n_g, n_s = len(gathers), len(scatters)
    other_chips = (4, 2, 6)

    def body(*refs):
        g_in, s_in = refs[:n_g], refs[n_g:n_g + n_s]
        g_out, s_out = refs[n_g + n_s:2 * n_g + n_s], refs[2 * n_g + n_s:2 * (n_g + n_s)]
        g_send, g_recv, s_send, s_recv, local_sem = refs[2 * (n_g + n_s):]
        x, y, c = lax.axis_index("x"), lax.axis_index("y"), lax.axis_index("c")

        def peer(k):
            return (1 - x if k & 4 else x, 1 - y if k & 2 else y, 1 - c if k & 1 else c)

        def number(p):
            return 4 * p[0] + 2 * p[1] + p[2]

        me = number((x, y, c))
        local = [pltpu.make_async_copy(g_in[t], g_out[t].at[me], local_sem.at[t]) for t in range(n_g)]
        local += [pltpu.make_async_copy(s_in[t].at[me], s_out[t].at[me], local_sem.at[n_g + t]) for t in range(n_s)]
        for cp in local:
            cp.start()

        def scatter(t, k):
            p = peer(k)
            return pltpu.make_async_remote_copy(
                src_ref=s_in[t].at[number(p)], dst_ref=s_out[t].at[me], send_sem=s_send.at[t * 7 + k - 1],
                recv_sem=s_recv.at[t * 7 + k - 1], device_id=p, device_id_type=MESH)

        def landed(t, k):
            p = peer(k)
            return pltpu.make_async_remote_copy(
                src_ref=s_in[t].at[me], dst_ref=s_out[t].at[number(p)], send_sem=s_send.at[t * 7 + k - 1],
                recv_sem=s_recv.at[t * 7 + k - 1], device_id=p, device_id_type=MESH)

        def gather(t, pair, block, to, src=None):
            slot = g_out[t].at[number(block)]
            return pltpu.make_async_remote_copy(
                src_ref=slot if src is None else src, dst_ref=slot, send_sem=g_send.at[t * 7 + pair],
                recv_sem=g_recv.at[t * 7 + pair], device_id=to, device_id_type=MESH)

        mine, sibling = peer(0), peer(1)
        sends = []
        for t in range(n_g):
            sends.append(gather(t, 0, mine, sibling, src=g_in[t]))
            sends += [gather(t, 1 + j, mine, peer(k), src=g_in[t]) for j, k in enumerate(other_chips)]
        for t in range(n_s):
            sends += [scatter(t, k) for k in range(1, N_DEV)]
        for cp in sends:
            cp.start()
        for j, k in enumerate(other_chips):
            for t in range(n_g):
                gather(t, 1 + j, peer(k), mine).wait_recv()
                passed = gather(t, 4 + j, peer(k), sibling)
                passed.start()
                sends.append(passed)
        for t in range(n_g):
            gather(t, 0, sibling, mine).wait_recv()
            for j, k in enumerate(other_chips):
                gather(t, 4 + j, peer(k ^ 1), mine).wait_recv()
        for t in range(n_s):
            for k in range(1, N_DEV):
                landed(t, k).wait_recv()
        for cp in sends:
            cp.wait_send()
        for cp in local:
            cp.wait()

    outs = [jax.ShapeDtypeStruct((N_DEV,) + a.shape, a.dtype) for a in gathers]
    outs += [jax.ShapeDtypeStruct(a.shape, a.dtype) for a in scatters]
    res = pl.pallas_call(
        body, name=name, in_specs=[ANY] * (n_g + n_s), out_specs=[ANY] * (n_g + n_s), out_shape=outs,
        scratch_shapes=[pltpu.SemaphoreType.DMA((max(n_g, 1) * 7,)), pltpu.SemaphoreType.DMA((max(n_g, 1) * 7,)),
                        pltpu.SemaphoreType.DMA((max(n_s, 1) * 7,)), pltpu.SemaphoreType.DMA((max(n_s, 1) * 7,)),
                        pltpu.SemaphoreType.DMA((n_g + n_s,))],
    )(*gathers, *scatters)
    return res[:n_g], res[n_g:]


def _adamw_tile(w, g, m, v):
    m = ADAM_B1 * m + (1.0 - ADAM_B1) * g
    v = ADAM_B2 * v + (1.0 - ADAM_B2) * (g * g)
    m_hat = m / (1.0 - ADAM_B1 ** ADAM_STEP)
    v_hat = v / (1.0 - ADAM_B2 ** ADAM_STEP)
    delta = -ADAM_LR * (m_hat / (jnp.sqrt(v_hat) + ADAM_EPS) + ADAM_WD * w)
    return delta, m, v


def _adamw_big(name, parts, w, m, v):
    layers, r, c = w.shape
    assert len(parts) == layers
    tr = 16
    while tr * 2 <= min(r, (256 * 1024) // c) and r % (tr * 2) == 0:
        tr *= 2

    def body(*refs):
        part_refs = refs[:layers]
        w_ref, m_ref, v_ref, g_out, d_out, m_out, v_out = refs[layers:]
        layer = pl.program_id(0)
        for ly in range(layers):
            @pl.when(layer == ly)
            def _(ly=ly):
                g = part_refs[ly][0].astype(F32)
                for j in range(1, N_DEV):
                    g = g + part_refs[ly][j].astype(F32)
                delta, mn, vn = _adamw_tile(w_ref[...], g, m_ref[...], v_ref[...])
                g_out[...] = g
                d_out[...] = delta
                m_out[...] = mn
                v_out[...] = vn

    own = pl.BlockSpec((None, tr, c), lambda ly, i: (ly, i, 0))
    part_specs = [pl.BlockSpec((N_DEV, tr, c), functools.partial(lambda ly, i, which: (0, jnp.where(ly == which, i, 0), 0), which=t))
                  for t in range(layers)]
    return pl.pallas_call(
        body, name=name, grid=(layers, r // tr), in_specs=part_specs + [own] * 3, out_specs=[own] * 4,
        out_shape=[jax.ShapeDtypeStruct(w.shape, F32)] * 4,
        compiler_params=_params("parallel", "parallel"),
    )(*parts, w, m, v)


def _sum_slots(name, slots):
    _, r, c = slots.shape

    def body(s_ref, o_ref):
        g = s_ref[0]
        for j in range(1, N_DEV):
            g = g + s_ref[j]
        o_ref[...] = g

    return pl.pallas_call(body, name=name, out_shape=jax.ShapeDtypeStruct((r, c), F32),
                          compiler_params=_params())(slots)


def _adamw_small(name, w, g, m, v):
    def body(w_ref, g_ref, m_ref, v_ref, d_out, m_out, v_out):
        delta, mn, vn = _adamw_tile(w_ref[...], g_ref[...], m_ref[...], v_ref[...])
        d_out[...] = delta
        m_out[...] = mn
        v_out[...] = vn

    return pl.pallas_call(body, name=name, out_shape=[jax.ShapeDtypeStruct(w.shape, F32)] * 3,
                          compiler_params=_params())(w, g, m, v)


def _mlp_ple_fwd(tag, z1, p_i, ln1_g, ln1_b, ln2_g, ln2_b, w_up, w_down, w_proj, w_gate):
    h1, h1b = _layer_norm(f"ln1_{tag}", z1, ln1_g, ln1_b, (F32, BF16))
    up, act = _mm_nn(f"mlp_up_{tag}", h1b, w_up, (F32, BF16),
                     epilogue=lambda acc: (acc, jnp.square(jnp.maximum(acc, 0.0))))
    (z2,) = _mm_nn(f"mlp_down_{tag}", act, w_down, (F32,), epilogue=lambda acc, h: (ALPHA * h + acc,), extras=[(h1, "tile")])
    h2, h2b = _layer_norm(f"ln2_{tag}", z2, ln2_g, ln2_b, (F32, BF16))
    (pe,) = _mm_nn(f"ple_proj_{tag}", p_i, w_proj, (F32,))

    def gate(acc, h, e):
        out = h + e * _sigmoid(acc)
        return acc, out, out

    gp, out, outb = _mm_nn(f"ple_gate_{tag}", h2b, w_gate, (F32, F32, BF16), epilogue=gate, extras=[(h2, "tile"), (pe, "tile")])
    saved = dict(z1=z1, h1b=h1b, up=up, act=act, z2=z2, h2b=h2b, pe=pe, gp=gp, p=p_i)
    return out, outb, saved


def _mlp_ple_bwd(tag, d_out, sv, ln1_g, ln2_g, w_up, w_down, w_proj, w_gate):
    d = d_out.shape[1]

    def fn(i, o, r):
        dy, pe, gp = i[0][...], i[1][...], i[2][...]
        sg = _sigmoid(gp)
        o[0][...] = (dy * sg).astype(BF16)
        o[1][...] = (dy * pe * sg * (1.0 - sg)).astype(BF16)

    d_pe, d_gp = _rowwise(f"ple_bwd_{tag}", fn, [_rows(d_out), _rows(sv["pe"]), _rows(sv["gp"])], [(d, BF16), (d, BF16)])
    g_proj = _mm_tn(f"g_ple_proj_{tag}", sv["p"], d_pe, w_proj)
    g_gate = _mm_tn(f"g_ple_gate_{tag}", sv["h2b"], d_gp, w_gate)
    (d_h2,) = _mm_nt(f"d_ple_gate_{tag}", d_gp, w_gate, (F32,), epilogue=lambda acc, dy: (dy + acc,), extras=[(d_out, "tile")])
    d_z2, d_z2b, g_ln2_g, g_ln2_b = _layer_norm_bwd(f"ln2_bwd_{tag}", sv["z2"], ln2_g, d_h2)
    g_down = _mm_tn(f"g_mlp_down_{tag}", sv["act"], d_z2b, w_down)
    (d_up,) = _mm_nt(f"d_mlp_down_{tag}", d_z2b, w_down, (BF16,),
                     epilogue=lambda acc, u: (acc * (2.0 * jnp.maximum(u, 0.0)),), extras=[(sv["up"], "tile")])
    g_up = _mm_tn(f"g_mlp_up_{tag}", sv["h1b"], d_up, w_up)
    (d_h1,) = _mm_nt(f"d_mlp_up_{tag}", d_up, w_up, (F32,), epilogue=lambda acc, dz: (ALPHA * dz + acc,), extras=[(d_z2, "tile")])
    d_z1, d_z1b, g_ln1_g, g_ln1_b = _layer_norm_bwd(f"ln1_bwd_{tag}", sv["z1"], ln1_g, d_h1)
    grads = dict(ple_proj=g_proj, ple_gate=g_gate, mlp_down=g_down, mlp_up=g_up,
                 ln1_g=g_ln1_g, ln1_b=g_ln1_b, ln2_g=g_ln2_g, ln2_b=g_ln2_b)
    return d_z1, d_z1b, grads


def _local_step(x, p, positions, target, wt, small):
    s, d = x.shape
    nh = d // HEAD_DIM

    (u,) = _mm_nn("conv_in", x, wt["conv_w_in"], (F32,), epilogue=lambda acc, b: (acc + b,), extras=[(small["conv_b_in"], "row")])

    def glu_fn(i, o, r):
        o[0][...] = i[0][...] * _sigmoid(i[1][...])

    (glu,) = _rowwise("glu", glu_fn, [_rows(u, 0, d), _rows(u, 1, d)], [(d, F32)])
    c = _conv_fwd("dwconv", glu, small["conv_dw"], small["conv_dw_b"])

    def ln_silu(i, o, r):
        xhat, _ = _ln_stats(i[0][...])
        n = xhat * i[1][...] + i[2][...]
        o[0][...] = (n * _sigmoid(n)).astype(BF16)

    (sb,) = _rowwise("conv_ln_silu", ln_silu, [_rows(c), _full(small["conv_ln_g"]), _full(small["conv_ln_b"])], [(d, BF16)])
    (z1,) = _mm_nn("conv_out", sb, wt["conv_w_out"], (F32,), epilogue=lambda acc, xt: (ALPHA * xt + acc,), extras=[(x, "tile")])
    x1, x1b, sv0 = _mlp_ple_fwd("0", z1, p[0], small["ln1_g"][0:1], small["ln1_b"][0:1], small["ln2_g"][0:1], small["ln2_b"][0:1],
                                wt["mlp_up0"], wt["mlp_down0"], wt["ple_proj0"], wt["ple_gate0"])

    (kvn,) = _layer_norm("kv_ln", x1, small["kv_ln_g"], small["kv_ln_b"], (BF16,))
    (kv,) = _mm_nn("kv_proj", kvn, wt["w_kv"], (F32,))
    (q,) = _mm_nn("q_proj", x1b, wt["attn_w_q"], (F32,))
    cos, sin = _rope_tables(positions)

    def rot_kv(i, o, r):
        cs, sn = i[2][...], i[3][...]
        for h in range(nh):
            hs = slice(h * HEAD_DIM, (h + 1) * HEAD_DIM)
            o[0][:, hs] = _rot(i[0][:, hs], cs, sn).astype(BF16)
        o[1][...] = i[1][...].astype(BF16)

    kr, vb = _rowwise("rotary_kv", rot_kv, [_rows(kv, 0, d), _rows(kv, 1, d), _rows(cos), _rows(sin)], [(d, BF16), (d, BF16)])

    def rot_q(i, o, r):
        cs, sn = i[1][...], i[2][...]
        for h in range(N_GROUPS * nh):
            hs = slice(h * HEAD_DIM, (h + 1) * HEAD_DIM)
            o[0][:, hs] = _rot(i[0][:, hs], cs, sn).astype(BF16)

    (qr,) = _rowwise("rotary_q", rot_q, [_rows(q), _rows(cos), _rows(sin)], [(N_GROUPS * d, BF16)])

    og, lg = zip(*[_attn_fwd(f"attn_fwd_{g}", qr, kr, vb, g, dil) for g, dil in enumerate(GROUP_DILATIONS)])

    def merge(i, o, r):
        ls = [i[N_GROUPS + g][...] for g in range(N_GROUPS)]
        top = functools.reduce(jnp.maximum, ls)
        es = [jnp.exp(l - top) for l in ls]
        den = functools.reduce(lambda a, b: a + b, es)
        out = functools.reduce(lambda a, b: a + b, [e * i[g][...] for g, e in enumerate(es)]) / den
        o[0][...] = out
        o[1][...] = out.astype(BF16)
        o[2][...] = top + jnp.log(den)

    o, ob, lse = _rowwise("attn_merge", merge, [_rows(t) for t in og + lg], [(d, F32), (d, BF16), (d, F32)], ts=128)
    (z1b,) = _mm_nn("attn_out", ob, wt["attn_w_o"], (F32,), epilogue=lambda acc, xt: (ALPHA * xt + acc,), extras=[(x1, "tile")])
    y, _, sv1 = _mlp_ple_fwd("1", z1b, p[1], small["ln1_g"][1:2], small["ln1_b"][1:2], small["ln2_g"][1:2], small["ln2_b"][1:2],
                             wt["mlp_up1"], wt["mlp_down1"], wt["ple_proj1"], wt["ple_gate1"])

    def loss_fn(i, o, r):
        diff = i[0][...] - i[1][...]
        o[0][...] = diff * (1.0 / d)
        r[0][...] += jnp.broadcast_to(jnp.sum(diff * diff), (1, LANE))

    d_y, sq = _rowwise("loss", loss_fn, [_rows(y), _rows(target)], [(d, F32)], [(1, LANE)])
    loss = 0.5 * sq[0, 0] / d

    d_z1, d_z1b, g1 = _mlp_ple_bwd("1", d_y, sv1, small["ln1_g"][1:2], small["ln2_g"][1:2],
                                   wt["mlp_up1"], wt["mlp_down1"], wt["ple_proj1"], wt["ple_gate1"])
    g_wo = _mm_tn("g_attn_out", ob, d_z1b, wt["attn_w_o"])
    d_o, d_ob = _mm_nt("d_attn_out", d_z1b, wt["attn_w_o"], (F32, BF16))

    def dsum_fn(i, o, r):
        for h in range(nh):
            hs = slice(h * HEAD_DIM, (h + 1) * HEAD_DIM)
            o[0][:, hs] = jnp.broadcast_to(jnp.sum(i[0][:, hs] * i[1][:, hs], axis=1, keepdims=True), (i[0].shape[0], HEAD_DIM))

    (dsum,) = _rowwise("attn_dsum", dsum_fn, [_rows(d_o), _rows(o)], [(d, F32)])
    dqs, dks, dvs = [], [], []
    for g, dil in enumerate(GROUP_DILATIONS):
        dqs.append(_attn_dq(f"attn_dq_{g}", qr, kr, vb, d_ob, lse, dsum, g, dil))
        dk, dv = _attn_dkv(f"attn_dkv_{g}", qr, kr, vb, d_ob, lse, dsum, g, dil)
        dks.append(dk)
        dvs.append(dv)

    def unrot(x_, cs, sn):
        return x_ * cs - pltpu.roll(x_, HEAD_DIM // 2, 1) * sn

    def unrot_q(i, o, r):
        cs, sn = i[N_GROUPS][...], i[N_GROUPS + 1][...]
        for g in range(N_GROUPS):
            for h in range(nh):
                hs = slice(h * HEAD_DIM, (h + 1) * HEAD_DIM)
                o[0][:, g * d + h * HEAD_DIM:g * d + (h + 1) * HEAD_DIM] = unrot(i[g][:, hs], cs, sn).astype(BF16)

    (d_q,) = _rowwise("rotary_q_bwd", unrot_q, [_rows(t) for t in dqs] + [_rows(cos), _rows(sin)], [(N_GROUPS * d, BF16)])

    def unrot_kv(i, o, r):
        cs, sn = i[2 * N_GROUPS][...], i[2 * N_GROUPS + 1][...]
        for h in range(nh):
            hs = slice(h * HEAD_DIM, (h + 1) * HEAD_DIM)
            dk = functools.reduce(lambda a, b: a + b, [i[g][:, hs] for g in range(N_GROUPS)])
            o[0][:, hs] = unrot(dk, cs, sn).astype(BF16)
            dv = functools.reduce(lambda a, b: a + b, [i[N_GROUPS + g][:, hs] for g in range(N_GROUPS)])
            o[0][:, d + h * HEAD_DIM:d + (h + 1) * HEAD_DIM] = dv.astype(BF16)

    (d_kv,) = _rowwise("rotary_kv_bwd", unrot_kv, [_rows(t) for t in dks + dvs] + [_rows(cos), _rows(sin)], [(2 * d, BF16)], ts=128)
    g_wq = _mm_tn("g_q_proj", x1b, d_q, wt["attn_w_q"])
    g_wkv = _mm_tn("g_kv_proj", kvn, d_kv, wt["w_kv"])
    (d_x1a,) = _mm_nt("d_q_proj", d_q, wt["attn_w_q"], (F32,), epilogue=lambda acc, dz: (ALPHA * dz + acc,), extras=[(d_z1, "tile")])
    (d_kvn,) = _mm_nt("d_kv_proj", d_kv, wt["w_kv"], (F32,))
    d_x1, _, g_kv_ln_g, g_kv_ln_b = _layer_norm_bwd("kv_ln_bwd", x1, small["kv_ln_g"], d_kvn, extra=d_x1a)

    d_z1, d_z1b, g0 = _mlp_ple_bwd("0", d_x1, sv0, small["ln1_g"][0:1], small["ln2_g"][0:1],
                                   wt["mlp_up0"], wt["mlp_down0"], wt["ple_proj0"], wt["ple_gate0"])
    g_wout = _mm_tn("g_conv_out", sb, d_z1b, wt["conv_w_out"])
    (d_s,) = _mm_nt("d_conv_out", d_z1b, wt["conv_w_out"], (F32,))

    def ln_silu_bwd(i, o, r):
        cx, gn, bn, ds_ = i[0][...], i[1][...], i[2][...], i[3][...]
        xhat, _ = _ln_stats(cx)
        n = xhat * gn + bn
        sg = _sigmoid(n)
        dn = ds_ * (sg * (1.0 + n * (1.0 - sg)))
        dx, dg, db = _ln_bwd_tile(cx, gn, dn)
        o[0][...] = dx
        r[0][...] += dg
        r[1][...] += db

    d_c, g_cln_g, g_cln_b = _rowwise("conv_ln_silu_bwd", ln_silu_bwd,
                                     [_rows(c), _full(small["conv_ln_g"]), _full(small["conv_ln_b"]), _rows(d_s)],
                                     [(d, F32)], [(1, d), (1, d)])
    d_glu, g_dw, g_dwb = _conv_bwd("dwconv_bwd", glu, d_c, small["conv_dw"])

    def glu_bwd(i, o, r):
        a, gt, dg_ = i[0][...], i[1][...], i[2][...]
        sg = _sigmoid(gt)
        da = dg_ * sg
        dgate = dg_ * a * sg * (1.0 - sg)
        o[0][:, 0:d] = da.astype(BF16)
        o[0][:, d:2 * d] = dgate.astype(BF16)
        r[0][:, 0:d] += jnp.sum(da, axis=0, keepdims=True)
        r[0][:, d:2 * d] += jnp.sum(dgate, axis=0, keepdims=True)

    d_u, g_bin = _rowwise("glu_bwd", glu_bwd, [_rows(u, 0, d), _rows(u, 1, d), _rows(d_glu)], [(2 * d, BF16)], [(1, 2 * d)])
    g_win = _mm_tn("g_conv_in", x, d_u, wt["conv_w_in"])
    (grad_x,) = _mm_nt("d_conv_in", d_u, wt["conv_w_in"], (F32,), epilogue=lambda acc, dz: (ALPHA * dz + acc,), extras=[(d_z1, "tile")])

    big = dict(conv_w_in=[g_win], conv_w_out=[g_wout], w_kv=[g_wkv], attn_w_q=[g_wq], attn_w_o=[g_wo],
               mlp_up=[g0["mlp_up"], g1["mlp_up"]], mlp_down=[g0["mlp_down"], g1["mlp_down"]],
               ple_proj=[g0["ple_proj"], g1["ple_proj"]], ple_gate=[g0["ple_gate"], g1["ple_gate"]])
    rows = [g_bin.reshape(2, d), g_dw, g_dwb, g_cln_g, g_cln_b, g_kv_ln_g, g_kv_ln_b]
    rows += [jnp.concatenate([g0[n], g1[n]], axis=0) for n in ("ln1_g", "ln1_b", "ln2_g", "ln2_b")]
    return loss, grad_x, big, rows


BIG = ("conv_w_in", "conv_w_out", "w_kv", "attn_w_q", "attn_w_o", "mlp_up", "mlp_down", "ple_proj", "ple_gate")
COLUMN_SHARDED = ("conv_w_in", "w_kv", "attn_w_q", "mlp_up", "ple_proj")
WEIGHTS = ("conv_w_in", "conv_b_in", "conv_dw", "conv_dw_b", "conv_ln_g", "conv_ln_b", "conv_w_out", "kv_ln_g", "kv_ln_b",
           "w_kv", "attn_w_q", "attn_w_o", "ln1_g", "ln1_b", "mlp_up", "mlp_down", "ln2_g", "ln2_b", "ple_proj", "ple_gate")


def kernel(x, p, positions, conv_w_in, conv_b_in, conv_dw, conv_dw_b, conv_ln_g, conv_ln_b, conv_w_out, kv_ln_g, kv_ln_b, w_kv, attn_w_q, attn_w_o, ln1_g, ln1_b, mlp_up, mlp_down, ln2_g, ln2_b, ple_proj, ple_gate, loss_target, m_conv_w_in, m_conv_b_in, m_conv_dw, m_conv_dw_b, m_conv_ln_g, m_conv_ln_b, m_conv_w_out, m_kv_ln_g, m_kv_ln_b, m_w_kv, m_attn_w_q, m_attn_w_o, m_ln1_g, m_ln1_b, m_mlp_up, m_mlp_down, m_ln2_g, m_ln2_b, m_ple_proj, m_ple_gate, v_conv_w_in, v_conv_b_in, v_conv_dw, v_conv_dw_b, v_conv_ln_g, v_conv_ln_b, v_conv_w_out, v_kv_ln_g, v_kv_ln_b, v_w_kv, v_attn_w_q, v_attn_w_o, v_ln1_g, v_ln1_b, v_mlp_up, v_mlp_down, v_ln2_g, v_ln2_b, v_ple_proj, v_ple_gate):
    given = dict(locals())
    wts = {n: given[n] for n in WEIGHTS}
    moms = {n: given["m_" + n] for n in WEIGHTS}
    vels = {n: given["v_" + n] for n in WEIGHTS}
    s, d = x.shape[1], x.shape[2]
    shard = d // N_DEV
    me = 4 * lax.axis_index("x") + 2 * lax.axis_index("y") + lax.axis_index("c")

    def layers_of(a):
        return a.reshape((-1,) + a.shape[-2:])

    def pack(t):
        rows = [t["conv_b_in"].reshape(2, shard), t["conv_dw"].reshape(CONV_WIDTH, shard), t["conv_dw_b"], t["conv_ln_g"], t["conv_ln_b"]]
        rows.append(jnp.zeros((PACK_ROWS - 2 - CONV_WIDTH - 3, shard), F32))
        return jnp.concatenate(rows, axis=0)

    pieces = [(n, ly, layers_of(wts[n])[ly].astype(BF16)) for n in BIG for ly in range(layers_of(wts[n]).shape[0])]
    gathered, _ = _exchange("gather_weights", [a for _, _, a in pieces] + [pack(wts)], [])
    wt = {}
    for (n, ly, _), full in zip(pieces, gathered):
        key = n + str(ly) if layers_of(wts[n]).shape[0] > 1 else n
        wt[key] = W(full, True) if n in COLUMN_SHARDED else W(full.reshape(-1, full.shape[-1]), False)
    packed = gathered[-1]
    small = dict(conv_b_in=packed[:, 0:2].reshape(1, 2 * d), conv_dw=packed[:, 2:2 + CONV_WIDTH],
                 conv_dw_b=packed[:, 2 + CONV_WIDTH].reshape(1, d), conv_ln_g=packed[:, 3 + CONV_WIDTH].reshape(1, d),
                 conv_ln_b=packed[:, 4 + CONV_WIDTH].reshape(1, d), kv_ln_g=kv_ln_g.reshape(1, d), kv_ln_b=kv_ln_b.reshape(1, d),
                 ln1_g=ln1_g, ln1_b=ln1_b, ln2_g=ln2_g, ln2_b=ln2_b)

    loss, grad_x, big, rows = _local_step(x[0], p[:, 0], positions.reshape(s, 1), loss_target[0], wt, small)
    loss = lax.psum(loss, ("x", "y", "c"))

    rows.append(jnp.zeros((SMALL_ROWS - sum(r.shape[0] for r in rows), d), F32))
    names = [(n, ly) for n in BIG for ly in range(len(big[n]))]
    (all_rows,), parts = _exchange("scatter_grads", [jnp.concatenate(rows, axis=0)], [big[n][ly] for n, ly in names])
    parts = dict(zip(names, parts))

    out = {}
    for n in BIG:
        w3 = layers_of(wts[n])
        res = _adamw_big("adamw_" + n, [parts[(n, ly)] for ly in range(w3.shape[0])], w3, layers_of(moms[n]), layers_of(vels[n]))
        out[n] = [r.reshape(wts[n].shape) for r in res]
    tot = _sum_slots("sum_small_grads", all_rows)
    mine = lax.dynamic_slice_in_dim(tot, me * shard, shard, axis=1)
    b_in = lax.dynamic_slice_in_dim(tot[0:2].reshape(1, 2 * d), me * 2 * shard, 2 * shard, axis=1)
    g_small = dict(conv_b_in=b_in, conv_dw=mine[2:2 + CONV_WIDTH].reshape(conv_dw.shape), conv_dw_b=mine[2 + CONV_WIDTH:3 + CONV_WIDTH],
                   conv_ln_g=mine[3 + CONV_WIDTH:4 + CONV_WIDTH], conv_ln_b=mine[4 + CONV_WIDTH:5 + CONV_WIDTH],
                   kv_ln_g=tot[5 + CONV_WIDTH], kv_ln_b=tot[6 + CONV_WIDTH])
    base = 7 + CONV_WIDTH
    for j, n in enumerate(("ln1_g", "ln1_b", "ln2_g", "ln2_b")):
        g_small[n] = tot[base + 2 * j:base + 2 * j + 2]
    order = [n for n in WEIGHTS if n not in BIG]

    def flat(t):
        parts_ = []
        for n in order:
            a = t[n].reshape(-1, shard)
            parts_ += [a, jnp.zeros((-a.shape[0] % 8, shard), F32)]
        return jnp.concatenate(parts_, axis=0)

    d_s, m_s, v_s = _adamw_small("adamw_small", flat(wts), flat(g_small), flat(moms), flat(vels))
    at = 0
    for n in order:
        nrow = wts[n].size // shard
        out[n] = [g_small[n].reshape(wts[n].shape)] + [t[at:at + nrow].reshape(wts[n].shape) for t in (d_s, m_s, v_s)]
        at += nrow + (-nrow % 8)
    return (loss, grad_x[None], *[out[n][0] for n in WEIGHTS], *[out[n][1] for n in WEIGHTS],
            *[out[n][2] for n in WEIGHTS], *[out[n][3] for n in WEIGHTS])
```

```python
import functools

import numpy as np
import jax
import jax.numpy as jnp
from jax import lax
from jax.experimental import pallas as pl
from jax.experimental.pallas import tpu as pltpu

F32, BF16 = jnp.float32, jnp.bfloat16

N_DEV = 8
HEAD_DIM = 128
ATTN_BLOCK = 128
GROUP_DILATIONS = (1, 4, 16)
N_GROUPS = len(GROUP_DILATIONS)
CONV_WIDTH = 31
CONV_HALO = 32
CONV_ROWS = 64
ROPE_THETA = 10000.0
LN_EPS = 1e-5
DEPTH = 2
ALPHA = (2 * DEPTH) ** 0.25
ADAM_LR, ADAM_B1, ADAM_B2, ADAM_EPS, ADAM_WD, ADAM_STEP = 0.001, 0.9, 0.999, 1e-08, 0.01, 10
NEG = -1e30
V7X_VMEM_LIMIT = 56 * 2 ** 20
LANE = 128
SUBLANES = 8
GRAD_DTYPE = BF16

MESH = pl.DeviceIdType.MESH
ANY = pl.BlockSpec(memory_space=pl.ANY)


def _params(*sem):
    return pltpu.CompilerParams(dimension_semantics=sem or None, vmem_limit_bytes=V7X_VMEM_LIMIT)


def _sigmoid(x):
    return 1.0 / (1.0 + jnp.exp(-x))


def _divisor(n, most):
    best = None
    for t in range(LANE, min(n, most) + 1, LANE):
        if n % t == 0:
            best = t
    assert best is not None, (n, most)
    return best


def _stack_rows(parts):
    out, offsets, at = [], [], 0
    for a in parts:
        pad = -a.shape[0] % SUBLANES
        offsets.append(at)
        out.append(a)
        if pad:
            out.append(jnp.zeros((pad, a.shape[1]), a.dtype))
        at += a.shape[0] + pad
    return jnp.concatenate(out, axis=0), offsets


class Exchange:
    OTHER_CHIPS = (4, 2, 6)

    def __init__(self, gathers=(), scatters=(), keys=()):
        self.gathers, self.scatters, self.keys = list(gathers), list(scatters), list(keys)
        self.n_g, self.n_s = len(self.gathers), len(self.scatters)
        self.n = self.n_g + self.n_s
        self.operands = self.gathers + self.scatters
        self.gathered = self.parts = None

    def out_shape(self):
        outs = [jax.ShapeDtypeStruct((N_DEV,) + a.shape, a.dtype) for a in self.gathers]
        return outs + [jax.ShapeDtypeStruct(a.shape, a.dtype) for a in self.scatters]

    def scratch(self):
        dma = pltpu.SemaphoreType.DMA
        return [dma((max(self.n_g, 1) * 7,)), dma((max(self.n_g, 1) * 7,)), dma((max(self.n_s, 1) * 7,)),
                dma((max(self.n_s, 1) * 7,)), dma((self.n,))]

    def take(self, results):
        self.gathered, self.parts = list(results[:self.n_g]), list(results[self.n_g:])

    def _copies(self, ins, outs, sems):
        n_g, n_s = self.n_g, self.n_s
        g_in, s_in, g_out, s_out = ins[:n_g], ins[n_g:], outs[:n_g], outs[n_g:]
        g_send, g_recv, s_send, s_recv, local_sem = sems
        x, y, c = lax.axis_index("x"), lax.axis_index("y"), lax.axis_index("c")

        def peer(k):
            return (1 - x if k & 4 else x, 1 - y if k & 2 else y, 1 - c if k & 1 else c)

        def number(p):
            return 4 * p[0] + 2 * p[1] + p[2]

        me = number((x, y, c))

        def local():
            cps = [pltpu.make_async_copy(g_in[t], g_out[t].at[me], local_sem.at[t]) for t in range(n_g)]
            return cps + [pltpu.make_async_copy(s_in[t].at[me], s_out[t].at[me], local_sem.at[n_g + t]) for t in range(n_s)]

        def scatter(t, k):
            p = peer(k)
            return pltpu.make_async_remote_copy(
                src_ref=s_in[t].at[number(p)], dst_ref=s_out[t].at[me], send_sem=s_send.at[t * 7 + k - 1],
                recv_sem=s_recv.at[t * 7 + k - 1], device_id=p, device_id_type=MESH)

        def landed(t, k):
            p = peer(k)
            return pltpu.make_async_remote_copy(
                src_ref=s_in[t].at[me], dst_ref=s_out[t].at[number(p)], send_sem=s_send.at[t * 7 + k - 1],
                recv_sem=s_recv.at[t * 7 + k - 1], device_id=p, device_id_type=MESH)

        def gather(t, pair, block, to, src=None):
            slot = g_out[t].at[number(block)]
            return pltpu.make_async_remote_copy(
                src_ref=slot if src is None else src, dst_ref=slot, send_sem=g_send.at[t * 7 + pair],
                recv_sem=g_recv.at[t * 7 + pair], device_id=to, device_id_type=MESH)

        def first_sends():
            cps = []
            for t in range(n_g):
                cps.append(gather(t, 0, peer(0), peer(1), src=g_in[t]))
                cps += [gather(t, 1 + j, peer(0), peer(k), src=g_in[t]) for j, k in enumerate(self.OTHER_CHIPS)]
            for t in range(n_s):
                cps += [scatter(t, k) for k in range(1, N_DEV)]
            return cps

        return peer, local, landed, gather, first_sends

    def start(self, ins, outs, sems):
        _, local, _, _, first_sends = self._copies(ins, outs, sems)
        for cp in local() + first_sends():
            cp.start()

    def finish(self, ins, outs, sems):
        peer, local, landed, gather, first_sends = self._copies(ins, outs, sems)
        mine, sibling = peer(0), peer(1)
        passed = []
        for j, k in enumerate(self.OTHER_CHIPS):
            for t in range(self.n_g):
                gather(t, 1 + j, peer(k), mine).wait_recv()
                passed.append(gather(t, 4 + j, peer(k), sibling))
                passed[-1].start()
        for t in range(self.n_g):
            gather(t, 0, sibling, mine).wait_recv()
            for j, k in enumerate(self.OTHER_CHIPS):
                gather(t, 4 + j, peer(k ^ 1), mine).wait_recv()
        for t in range(self.n_s):
            for k in range(1, N_DEV):
                landed(t, k).wait_recv()
        for cp in first_sends() + passed:
            cp.wait_send()
        for cp in local():
            cp.wait()


def _exchange_alone(name, ex):
    def body(*refs):
        ins, outs, sems = refs[:ex.n], refs[ex.n:2 * ex.n], refs[2 * ex.n:]
        ex.start(ins, outs, sems)
        ex.finish(ins, outs, sems)

    ex.take(pl.pallas_call(body, name=name, in_specs=[ANY] * ex.n, out_specs=[ANY] * ex.n, out_shape=ex.out_shape(),
                           scratch_shapes=ex.scratch())(*ex.operands))


def _call(name, body, args, *, grid, in_specs, out_specs, out_shape, scratch_shapes=(), sem=(), carry=None):
    if carry is None:
        return pl.pallas_call(body, name=name, grid=grid, in_specs=in_specs, out_specs=out_specs, out_shape=out_shape,
                              scratch_shapes=list(scratch_shapes), compiler_params=_params(*sem))(*args)
    ex = carry
    n_in, n_out, n_scr = len(args), len(out_shape), len(scratch_shapes)

    def carried(*refs):
        ins, cin = refs[:n_in], refs[n_in:n_in + ex.n]
        at = n_in + ex.n
        outs, cout = refs[at:at + n_out], refs[at + n_out:at + n_out + ex.n]
        at += n_out + ex.n
        scr, sems = refs[at:at + n_scr], refs[at + n_scr:]
        ids = [pl.program_id(a) for a in range(len(grid))]
        first = functools.reduce(jnp.logical_and, [i == 0 for i in ids])
        last = functools.reduce(jnp.logical_and, [i == g - 1 for i, g in zip(ids, grid)])

        @pl.when(first)
        def _():
            ex.start(cin, cout, sems)

        body(*ins, *outs, *scr)

        @pl.when(last)
        def _():
            ex.finish(cin, cout, sems)

    res = pl.pallas_call(
        carried, name=name, grid=grid, in_specs=list(in_specs) + [ANY] * ex.n, out_specs=list(out_specs) + [ANY] * ex.n,
        out_shape=list(out_shape) + ex.out_shape(), scratch_shapes=list(scratch_shapes) + ex.scratch(),
        compiler_params=_params(*("arbitrary",) * len(grid)),
    )(*args, *ex.operands)
    ex.take(res[n_out:])
    return res[:n_out]


class W:
    def __init__(self, arr, blocked):
        self.arr, self.blocked = arr, blocked
        if blocked:
            self.nb, self.k, self.nblk = arr.shape
            self.n = self.nb * self.nblk
        else:
            self.k, self.n = arr.shape
            self.nblk = self.n

    def spec(self, t_rows, t_cols, idx):
        if not self.blocked:
            return pl.BlockSpec((t_rows, t_cols), idx)
        per = self.nblk // t_cols

        def index(*g):
            rt, ct = idx(*g)
            return (ct // per, rt, ct % per)

        return pl.BlockSpec((None, t_rows, t_cols), index)


def _matmul(name, grid, operands, specs, dims, tile, extras, outs, out_specs, epilogue, carry=None):
    steps = grid[2]
    n_ex, n_out = len(extras), len(outs)

    def body(*refs):
        a_ref, b_ref = refs[0], refs[1]
        ex_refs = refs[2:2 + n_ex]
        out_refs = refs[2 + n_ex:2 + n_ex + n_out]
        part = lax.dot_general(a_ref[...].astype(BF16), b_ref[...].astype(BF16), (dims, ((), ())),
                               preferred_element_type=F32)

        def finish(acc):
            res = epilogue(acc, *[r[...] for r in ex_refs]) if epilogue else (acc,) * n_out
            for r, v in zip(out_refs, res):
                r[...] = v.astype(r.dtype)

        if steps == 1:
            finish(part)
        else:
            acc_ref = refs[2 + n_ex + n_out]
            c = pl.program_id(2)

            @pl.when(c == 0)
            def _():
                acc_ref[...] = part

            @pl.when(c > 0)
            def _():
                acc_ref[...] += part

            @pl.when(c == steps - 1)
            def _():
                finish(acc_ref[...])

    return _call(name, body, list(operands) + [a for a, _ in extras], grid=grid,
                 in_specs=list(specs) + [s for _, s in extras], out_specs=out_specs, out_shape=outs,
                 scratch_shapes=[pltpu.VMEM(tile, F32)] if steps > 1 else [],
                 sem=("parallel", "parallel", "arbitrary"), carry=carry)


def _extra_specs(extras, tm, tn):
    out = []
    for arr, kind in extras:
        if kind == "tile":
            out.append((arr, pl.BlockSpec((tm, tn), lambda i, j, c: (i, j))))
        else:
            out.append((arr, pl.BlockSpec((1, tn), lambda i, j, c: (0, j))))
    return out


def _mm_nn(name, a, w, out_dtypes, epilogue=None, extras=(), carry=None):
    m, k = a.shape
    assert k == w.k
    tm = 1024 if a.dtype == BF16 else 512
    tn = _divisor(w.nblk, 512) if w.nblk % 512 == 0 else w.nblk
    tk = min(k, 2048)
    grid = (m // tm, w.n // tn, k // tk)
    specs = [pl.BlockSpec((tm, tk), lambda i, j, c: (i, c)), w.spec(tk, tn, lambda i, j, c: (c, j))]
    outs = [jax.ShapeDtypeStruct((m, w.n), d) for d in out_dtypes]
    out_specs = [pl.BlockSpec((tm, tn), lambda i, j, c: (i, j)) for _ in outs]
    return _matmul(name, grid, (a, w.arr), specs, ((1,), (0,)), (tm, tn), _extra_specs(extras, tm, tn), outs, out_specs,
                   epilogue, carry)


def _mm_nt(name, dy, w, out_dtypes, epilogue=None, extras=(), carry=None):
    m, n = dy.shape
    assert n == w.n
    tm = 1024
    to = _divisor(w.k, 512)
    tc = w.nblk if w.blocked else min(n, 2048)
    grid = (m // tm, w.k // to, n // tc)
    specs = [pl.BlockSpec((tm, tc), lambda i, j, c: (i, c)), w.spec(to, tc, lambda i, j, c: (j, c))]
    outs = [jax.ShapeDtypeStruct((m, w.k), d) for d in out_dtypes]
    out_specs = [pl.BlockSpec((tm, to), lambda i, j, c: (i, j)) for _ in outs]
    return _matmul(name, grid, (dy, w.arr), specs, ((1,), (1,)), (tm, to), _extra_specs(extras, tm, to), outs, out_specs,
                   epilogue, carry)


def _mm_tn(name, a, dy, like, carry=None):
    m, k = a.shape
    n = dy.shape[1]
    assert (k, n) == (like.k, like.n)
    tk = _divisor(k, 1024)
    tn = _divisor(like.nblk, 1024)
    tc = 512
    grid = (k // tk, n // tn, m // tc)
    specs = [pl.BlockSpec((tc, tk), lambda i, j, c: (c, i)), pl.BlockSpec((tc, tn), lambda i, j, c: (c, j))]
    if like.blocked:
        out = W(jax.ShapeDtypeStruct((like.nb, k, like.nblk), GRAD_DTYPE), True)
    else:
        out = W(jax.ShapeDtypeStruct((k, n), GRAD_DTYPE), False)
    out_specs = [out.spec(tk, tn, lambda i, j, c: (i, j))]
    (g,) = _matmul(name, grid, (a, dy), specs, ((0,), (0,)), (tk, tn), [], [out.arr], out_specs, None, carry)
    return g if like.blocked else g.reshape(N_DEV, k // N_DEV, n)


def _rows(arr, blk=0, width=None):
    return ("rows", arr, blk, width or arr.shape[1])


def _full(arr):
    return ("full", arr)


def _rowwise(name, fn, ins, outs, reds=(), ts=256, carry=None):
    s = next(i[1].shape[0] for i in ins if i[0] == "rows")
    n_in, n_out = len(ins), len(outs)
    in_specs = []
    for i in ins:
        if i[0] == "rows":
            in_specs.append(pl.BlockSpec((ts, i[3]), functools.partial(lambda t, blk: (t, blk), blk=i[2])))
        else:
            in_specs.append(pl.BlockSpec(i[1].shape, functools.partial(lambda t, nd: (0,) * nd, nd=i[1].ndim)))
    out_shape = [jax.ShapeDtypeStruct((s, w), d) for w, d in outs] + [jax.ShapeDtypeStruct(r, F32) for r in reds]
    out_specs = [pl.BlockSpec((ts, w), lambda t: (t, 0)) for w, _ in outs] + [pl.BlockSpec(r, lambda t: (0, 0)) for r in reds]

    def body(*refs):
        red_refs = refs[n_in + n_out:]
        if red_refs:
            @pl.when(pl.program_id(0) == 0)
            def _():
                for r in red_refs:
                    r[...] = jnp.zeros(r.shape, F32)
        fn(refs[:n_in], refs[n_in:n_in + n_out], red_refs)

    return _call(name, body, [i[1] for i in ins], grid=(s // ts,), in_specs=in_specs, out_specs=out_specs,
                 out_shape=out_shape, sem=("arbitrary" if reds else "parallel",), carry=carry)


def _ln_stats(x):
    mu = jnp.mean(x, axis=-1, keepdims=True)
    xc = x - mu
    var = jnp.mean(xc * xc, axis=-1, keepdims=True)
    return xc * lax.rsqrt(var + LN_EPS), lax.rsqrt(var + LN_EPS)


def _layer_norm(name, x, g, b, out_dtypes):
    def fn(i, o, r):
        xhat, _ = _ln_stats(i[0][...])
        y = xhat * i[1][...] + i[2][...]
        for ref in o:
            ref[...] = y.astype(ref.dtype)

    return _rowwise(name, fn, [_rows(x), _full(g), _full(b)], [(x.shape[1], d) for d in out_dtypes])


def _ln_bwd_tile(x, g, dy):
    xhat, rstd = _ln_stats(x)
    dyg = dy * g
    m1 = jnp.mean(dyg, axis=-1, keepdims=True)
    m2 = jnp.mean(dyg * xhat, axis=-1, keepdims=True)
    dx = rstd * (dyg - m1 - xhat * m2)
    return dx, jnp.sum(dy * xhat, axis=0, keepdims=True), jnp.sum(dy, axis=0, keepdims=True)


def _layer_norm_bwd(name, x, g, dy, extra=None):
    d = x.shape[1]

    def fn(i, o, r):
        dx, dg, db = _ln_bwd_tile(i[0][...], i[1][...], i[2][...])
        if extra is not None:
            dx = dx + i[3][...]
        o[0][...] = dx
        o[1][...] = dx.astype(BF16)
        r[0][...] += dg
        r[1][...] += db

    ins = [_rows(x), _full(g), _rows(dy)] + ([_rows(extra)] if extra is not None else [])
    return _rowwise(name, fn, ins, [(d, F32), (d, BF16)], [(1, d), (1, d)])


def _conv_fwd(name, glu, dw, dw_b, ts=512, carry=None):
    s, c = glu.shape
    tc = dw.shape[2]
    per = ts // CONV_HALO
    back = CONV_HALO - (CONV_WIDTH - 1)

    def body(cur_ref, prev_ref, w_ref, b_ref, out_ref, buf):
        i = pl.program_id(1)
        buf[pl.ds(0, CONV_HALO), :] = jnp.where(i > 0, prev_ref[...], 0.0)
        buf[pl.ds(CONV_HALO, ts), :] = cur_ref[...]
        for r0 in range(0, ts, CONV_ROWS):
            acc = jnp.broadcast_to(b_ref[...], (CONV_ROWS, tc))
            for j in range(CONV_WIDTH):
                acc = acc + w_ref[j:j + 1, :] * buf[pl.ds(r0 + back + j, CONV_ROWS), :]
            out_ref[pl.ds(r0, CONV_ROWS), :] = acc

    (out,) = _call(
        name, body, [glu, glu, dw, dw_b], grid=(c // tc, s // ts),
        in_specs=[pl.BlockSpec((ts, tc), lambda j, i: (i, j)),
                  pl.BlockSpec((CONV_HALO, tc), lambda j, i: (jnp.maximum(i * per - 1, 0), j)),
                  pl.BlockSpec((None, CONV_WIDTH, tc), lambda j, i: (j, 0, 0)),
                  pl.BlockSpec((1, tc), lambda j, i: (0, j))],
        out_specs=[pl.BlockSpec((ts, tc), lambda j, i: (i, j))],
        out_shape=[jax.ShapeDtypeStruct((s, c), F32)],
        scratch_shapes=[pltpu.VMEM((ts + CONV_HALO, tc), F32)],
        sem=("parallel", "parallel"), carry=carry)
    return out


def _conv_bwd(name, glu, dc, dw, ts=512, carry=None):
    s, c = glu.shape
    tc = dw.shape[2]
    per = ts // CONV_HALO
    back = CONV_HALO - (CONV_WIDTH - 1)
    last = s // ts - 1

    def body(g_ref, gprev_ref, dc_ref, dcnext_ref, w_ref, dglu_ref, ddw_ref, ddb_ref, gbuf, dbuf):
        i = pl.program_id(1)

        @pl.when(i == 0)
        def _():
            ddw_ref[...] = jnp.zeros(ddw_ref.shape, F32)
            ddb_ref[...] = jnp.zeros(ddb_ref.shape, F32)

        gbuf[pl.ds(0, CONV_HALO), :] = jnp.where(i > 0, gprev_ref[...], 0.0)
        gbuf[pl.ds(CONV_HALO, ts), :] = g_ref[...]
        dbuf[pl.ds(0, ts), :] = dc_ref[...]
        dbuf[pl.ds(ts, CONV_HALO), :] = jnp.where(i < last, dcnext_ref[...], 0.0)
        taps = [jnp.zeros((1, tc), F32)] * CONV_WIDTH
        for r0 in range(0, ts, CONV_ROWS):
            d_here = dbuf[pl.ds(r0, CONV_ROWS), :]
            acc = jnp.zeros((CONV_ROWS, tc), F32)
            for j in range(CONV_WIDTH):
                acc = acc + w_ref[j:j + 1, :] * dbuf[pl.ds(r0 + (CONV_WIDTH - 1) - j, CONV_ROWS), :]
                taps[j] = taps[j] + jnp.sum(d_here * gbuf[pl.ds(r0 + back + j, CONV_ROWS), :], axis=0, keepdims=True)
            dglu_ref[pl.ds(r0, CONV_ROWS), :] = acc
        for j in range(CONV_WIDTH):
            ddw_ref[j:j + 1, :] += taps[j]
        ddb_ref[...] += jnp.sum(dc_ref[...], axis=0, keepdims=True)

    return _call(
        name, body, [glu, glu, dc, dc, dw], grid=(c // tc, s // ts),
        in_specs=[pl.BlockSpec((ts, tc), lambda j, i: (i, j)),
                  pl.BlockSpec((CONV_HALO, tc), lambda j, i: (jnp.maximum(i * per - 1, 0), j)),
                  pl.BlockSpec((ts, tc), lambda j, i: (i, j)),
                  pl.BlockSpec((CONV_HALO, tc), lambda j, i: (jnp.minimum((i + 1) * per, (last + 1) * per - 1), j)),
                  pl.BlockSpec((None, CONV_WIDTH, tc), lambda j, i: (j, 0, 0))],
        out_specs=[pl.BlockSpec((ts, tc), lambda j, i: (i, j)),
                   pl.BlockSpec((CONV_WIDTH, tc), lambda j, i: (0, j)),
                   pl.BlockSpec((1, tc), lambda j, i: (0, j))],
        out_shape=[jax.ShapeDtypeStruct((s, c), F32), jax.ShapeDtypeStruct((CONV_WIDTH, c), F32),
                   jax.ShapeDtypeStruct((1, c), F32)],
        scratch_shapes=[pltpu.VMEM((ts + CONV_HALO, tc), F32), pltpu.VMEM((ts + CONV_HALO, tc), F32)],
        sem=("parallel", "arbitrary"), carry=carry)


def _rope_tables(positions):
    half = HEAD_DIM // 2
    inv = (np.float32(ROPE_THETA) ** (-np.arange(half, dtype=np.float32) * np.float32(2.0 / HEAD_DIM))).astype(np.float32)
    inv_freq = jnp.asarray(np.concatenate([inv, inv])[None, :])
    sign = jnp.asarray(np.concatenate([-np.ones(half, np.float32), np.ones(half, np.float32)])[None, :])

    def fn(i, o, r):
        ang = i[0][...].astype(F32) * i[1][...]
        o[0][...] = jnp.cos(ang)
        o[1][...] = jnp.sin(ang) * i[2][...]

    return _rowwise("rope_tables", fn, [_rows(positions), _full(inv_freq), _full(sign)], [(HEAD_DIM, F32), (HEAD_DIM, F32)], ts=512)


def _rot(x, cos, sin):
    return x * cos + pltpu.roll(x, HEAD_DIM // 2, 1) * sin


def _masks(n):
    row = lax.broadcasted_iota(jnp.int32, (ATTN_BLOCK, ATTN_BLOCK), 0)
    col = lax.broadcasted_iota(jnp.int32, (ATTN_BLOCK, ATTN_BLOCK), 1)
    return col <= row, jnp.logical_and(col >= row, n > 0)


_NT = (((1,), (1,)), ((), ()))
_TN = (((0,), (0,)), ((), ()))
_NN = (((1,), (0,)), ((), ()))


def _attn_fwd(name, qr, kr, vb, g, dil, carry=None):
    s, d = kr.shape
    nh = d // HEAD_DIM
    ln = s // dil
    nb = ln // ATTN_BLOCK
    scale = HEAD_DIM ** -0.5

    def body(q_ref, kc_ref, kp_ref, vc_ref, vp_ref, o_ref, l_ref):
        mask_c, mask_p = _masks(pl.program_id(1))
        for h in range(nh):
            hs = slice(h * HEAD_DIM, (h + 1) * HEAD_DIM)
            q = q_ref[:, hs]
            s_c = jnp.where(mask_c, lax.dot_general(q, kc_ref[:, hs], _NT, preferred_element_type=F32) * scale, NEG)
            s_p = jnp.where(mask_p, lax.dot_general(q, kp_ref[:, hs], _NT, preferred_element_type=F32) * scale, NEG)
            m = jnp.maximum(jnp.max(s_c, axis=1, keepdims=True), jnp.max(s_p, axis=1, keepdims=True))
            p_c = jnp.exp(s_c - m)
            p_p = jnp.exp(s_p - m)
            l = jnp.sum(p_c, axis=1, keepdims=True) + jnp.sum(p_p, axis=1, keepdims=True)
            o = (lax.dot_general(p_c.astype(BF16), vc_ref[:, hs], _NN, preferred_element_type=F32)
                 + lax.dot_general(p_p.astype(BF16), vp_ref[:, hs], _NN, preferred_element_type=F32))
            o_ref[:, hs] = o / l
            l_ref[:, hs] = jnp.broadcast_to(m + jnp.log(l), (ATTN_BLOCK, HEAD_DIM))

    cur = pl.BlockSpec((ATTN_BLOCK, d), lambda r, n: (n, r))
    prev = pl.BlockSpec((ATTN_BLOCK, d), lambda r, n: (jnp.maximum(n - 1, 0), r))
    view = lambda t: t.reshape(ln, dil * d)
    o, l = _call(
        name, body, [qr.reshape(ln, dil * N_GROUPS * d), view(kr), view(kr), view(vb), view(vb)], grid=(dil, nb),
        in_specs=[pl.BlockSpec((ATTN_BLOCK, d), lambda r, n: (n, r * N_GROUPS + g)), cur, prev, cur, prev],
        out_specs=[cur, cur], out_shape=[jax.ShapeDtypeStruct((ln, dil * d), F32)] * 2,
        sem=("parallel", "parallel"), carry=carry)
    return o.reshape(s, d), l.reshape(s, d)


def _attn_dq(name, qr, kr, vb, do, lse, dsum, g, dil, carry=None):
    s, d = kr.shape
    nh = d // HEAD_DIM
    ln = s // dil
    nb = ln // ATTN_BLOCK
    scale = HEAD_DIM ** -0.5

    def body(q_ref, kc_ref, kp_ref, vc_ref, vp_ref, do_ref, l_ref, d_ref, dq_ref):
        mask_c, mask_p = _masks(pl.program_id(1))
        for h in range(nh):
            hs = slice(h * HEAD_DIM, (h + 1) * HEAD_DIM)
            q, dout = q_ref[:, hs], do_ref[:, hs]
            lrow = l_ref[:, h * HEAD_DIM:h * HEAD_DIM + 1]
            drow = d_ref[:, h * HEAD_DIM:h * HEAD_DIM + 1]
            dq = jnp.zeros((ATTN_BLOCK, HEAD_DIM), F32)
            for mask, k_ref, v_ref in ((mask_c, kc_ref, vc_ref), (mask_p, kp_ref, vp_ref)):
                sc = lax.dot_general(q, k_ref[:, hs], _NT, preferred_element_type=F32) * scale
                p = jnp.where(mask, jnp.exp(jnp.where(mask, sc, NEG) - lrow), 0.0)
                dp = lax.dot_general(dout, v_ref[:, hs], _NT, preferred_element_type=F32)
                ds = p * (dp - drow)
                dq = dq + lax.dot_general(ds.astype(BF16), k_ref[:, hs], _NN, preferred_element_type=F32)
            dq_ref[:, hs] = dq * scale

    cur = pl.BlockSpec((ATTN_BLOCK, d), lambda r, n: (n, r))
    prev = pl.BlockSpec((ATTN_BLOCK, d), lambda r, n: (jnp.maximum(n - 1, 0), r))
    view = lambda t: t.reshape(ln, dil * d)
    (dq,) = _call(
        name, body, [qr.reshape(ln, dil * N_GROUPS * d), view(kr), view(kr), view(vb), view(vb), view(do), view(lse), view(dsum)],
        grid=(dil, nb),
        in_specs=[pl.BlockSpec((ATTN_BLOCK, d), lambda r, n: (n, r * N_GROUPS + g)), cur, prev, cur, prev, cur, cur, cur],
        out_specs=[cur], out_shape=[jax.ShapeDtypeStruct((ln, dil * d), F32)],
        sem=("parallel", "parallel"), carry=carry)
    return dq.reshape(s, d)


def _attn_dkv(name, qr, kr, vb, do, lse, dsum, g, dil, carry=None):
    s, d = kr.shape
    nh = d // HEAD_DIM
    ln = s // dil
    nb = ln // ATTN_BLOCK
    scale = HEAD_DIM ** -0.5

    def body(k_ref, v_ref, qc_ref, qn_ref, doc_ref, don_ref, lc_ref, lnx_ref, dc_ref, dn_ref, dk_ref, dv_ref):
        n = pl.program_id(1)
        row = lax.broadcasted_iota(jnp.int32, (ATTN_BLOCK, ATTN_BLOCK), 0)
        col = lax.broadcasted_iota(jnp.int32, (ATTN_BLOCK, ATTN_BLOCK), 1)
        masks = (col <= row, jnp.logical_and(col >= row, n < nb - 1))
        for h in range(nh):
            hs = slice(h * HEAD_DIM, (h + 1) * HEAD_DIM)
            k, v = k_ref[:, hs], v_ref[:, hs]
            dk = jnp.zeros((ATTN_BLOCK, HEAD_DIM), F32)
            dv = jnp.zeros((ATTN_BLOCK, HEAD_DIM), F32)
            for mask, q_ref, do_ref, l_ref, d_ref in ((masks[0], qc_ref, doc_ref, lc_ref, dc_ref),
                                                      (masks[1], qn_ref, don_ref, lnx_ref, dn_ref)):
                q, dout = q_ref[:, hs], do_ref[:, hs]
                lrow = l_ref[:, h * HEAD_DIM:h * HEAD_DIM + 1]
                drow = d_ref[:, h * HEAD_DIM:h * HEAD_DIM + 1]
                sc = lax.dot_general(q, k, _NT, preferred_element_type=F32) * scale
                p = jnp.where(mask, jnp.exp(jnp.where(mask, sc, NEG) - lrow), 0.0)
                dp = lax.dot_general(dout, v, _NT, preferred_element_type=F32)
                ds = p * (dp - drow)
                dv = dv + lax.dot_general(p.astype(BF16), dout, _TN, preferred_element_type=F32)
                dk = dk + lax.dot_general(ds.astype(BF16), q, _TN, preferred_element_type=F32)
            dk_ref[:, hs] = dk * scale
            dv_ref[:, hs] = dv

    cur = pl.BlockSpec((ATTN_BLOCK, d), lambda r, n: (n, r))
    nxt = pl.BlockSpec((ATTN_BLOCK, d), lambda r, n: (jnp.minimum(n + 1, nb - 1), r))
    qcur = pl.BlockSpec((ATTN_BLOCK, d), lambda r, n: (n, r * N_GROUPS + g))
    qnxt = pl.BlockSpec((ATTN_BLOCK, d), lambda r, n: (jnp.minimum(n + 1, nb - 1), r * N_GROUPS + g))
    view = lambda t: t.reshape(ln, dil * d)
    qv = qr.reshape(ln, dil * N_GROUPS * d)
    dk, dv = _call(
        name, body, [view(kr), view(vb), qv, qv, view(do), view(do), view(lse), view(lse), view(dsum), view(dsum)],
        grid=(dil, nb), in_specs=[cur, cur, qcur, qnxt, cur, nxt, cur, nxt, cur, nxt],
        out_specs=[cur, cur], out_shape=[jax.ShapeDtypeStruct((ln, dil * d), F32)] * 2,
        sem=("parallel", "parallel"), carry=carry)
    return dk.reshape(s, d), dv.reshape(s, d)


def _adamw_tile(w, g, m, v):
    m = ADAM_B1 * m + (1.0 - ADAM_B1) * g
    v = ADAM_B2 * v + (1.0 - ADAM_B2) * (g * g)
    m_hat = m / (1.0 - ADAM_B1 ** ADAM_STEP)
    v_hat = v / (1.0 - ADAM_B2 ** ADAM_STEP)
    delta = -ADAM_LR * (m_hat / (jnp.sqrt(v_hat) + ADAM_EPS) + ADAM_WD * w)
    return delta, m, v


def _adamw_big(name, parts, w, m, v):
    layers, r, c = w.shape
    assert len(parts) == layers
    tr = 16
    while tr * 2 <= min(r, (256 * 1024) // c) and r % (tr * 2) == 0:
        tr *= 2

    def body(*refs):
        part_refs = refs[:layers]
        w_ref, m_ref, v_ref, g_out, d_out, m_out, v_out = refs[layers:]
        layer = pl.program_id(0)
        for ly in range(layers):
            @pl.when(layer == ly)
            def _(ly=ly):
                g = part_refs[ly][0].astype(F32)
                for j in range(1, N_DEV):
                    g = g + part_refs[ly][j].astype(F32)
                delta, mn, vn = _adamw_tile(w_ref[...], g, m_ref[...], v_ref[...])
                g_out[...] = g
                d_out[...] = delta
                m_out[...] = mn
                v_out[...] = vn

    own = pl.BlockSpec((None, tr, c), lambda ly, i: (ly, i, 0))
    part_specs = [pl.BlockSpec((N_DEV, tr, c), functools.partial(lambda ly, i, which: (0, jnp.where(ly == which, i, 0), 0), which=t))
                  for t in range(layers)]
    return _call(name, body, list(parts) + [w, m, v], grid=(layers, r // tr), in_specs=part_specs + [own] * 3,
                 out_specs=[own] * 4, out_shape=[jax.ShapeDtypeStruct(w.shape, F32)] * 4, sem=("parallel", "parallel"))


def _sum_slots(name, slots):
    _, r, c = slots.shape

    def body(s_ref, o_ref):
        g = s_ref[0]
        for j in range(1, N_DEV):
            g = g + s_ref[j]
        o_ref[...] = g

    return pl.pallas_call(body, name=name, out_shape=jax.ShapeDtypeStruct((r, c), F32),
                          compiler_params=_params())(slots)


def _adamw_small(name, w, g, m, v):
    def body(w_ref, g_ref, m_ref, v_ref, d_out, m_out, v_out):
        delta, mn, vn = _adamw_tile(w_ref[...], g_ref[...], m_ref[...], v_ref[...])
        d_out[...] = delta
        m_out[...] = mn
        v_out[...] = vn

    return pl.pallas_call(body, name=name, out_shape=[jax.ShapeDtypeStruct(w.shape, F32)] * 3,
                          compiler_params=_params())(w, g, m, v)


class Weights:
    def __init__(self, shards):
        self.shards, self.full, self.parts = shards, {}, {}

    def gather(self, *keys):
        return Exchange(gathers=[self.shards[k] for k in keys], keys=keys)

    def landed(self, ex):
        for key, full in zip(ex.keys, ex.gathered):
            if key.rstrip("01") in COLUMN_SHARDED:
                self.full[key] = W(full, True)
            else:
                self.full[key] = W(full.reshape(-1, full.shape[-1]), False)

    def scatter(self, grads, gathers=()):
        return Exchange(gathers=gathers, scatters=list(grads.values()), keys=list(grads))

    def received(self, ex):
        self.parts.update(zip(ex.keys, ex.parts))

    def __getitem__(self, key):
        return self.full[key]


def _mlp_ple_fwd(tag, z1, p_i, ln1_g, ln1_b, ln2_g, ln2_b, wt, carries):
    h1, h1b = _layer_norm(f"ln1_{tag}", z1, ln1_g, ln1_b, (F32, BF16))
    up, act = _mm_nn(f"mlp_up_{tag}", h1b, wt["mlp_up" + tag], (F32, BF16),
                     epilogue=lambda acc: (acc, jnp.square(jnp.maximum(acc, 0.0))), carry=carries.get("mlp_up"))
    if "mlp_up" in carries:
        wt.landed(carries["mlp_up"])
    (z2,) = _mm_nn(f"mlp_down_{tag}", act, wt["mlp_down" + tag], (F32,), epilogue=lambda acc, h: (ALPHA * h + acc,),
                   extras=[(h1, "tile")], carry=carries.get("mlp_down"))
    if "mlp_down" in carries:
        wt.landed(carries["mlp_down"])
    h2, h2b = _layer_norm(f"ln2_{tag}", z2, ln2_g, ln2_b, (F32, BF16))
    (pe,) = _mm_nn(f"ple_proj_{tag}", p_i, wt["ple_proj" + tag], (F32,))

    def gate(acc, h, e):
        out = h + e * _sigmoid(acc)
        return acc, out, out

    gp, out, outb = _mm_nn(f"ple_gate_{tag}", h2b, wt["ple_gate" + tag], (F32, F32, BF16), epilogue=gate,
                           extras=[(h2, "tile"), (pe, "tile")])
    saved = dict(z1=z1, h1b=h1b, up=up, act=act, z2=z2, h2b=h2b, pe=pe, gp=gp, p=p_i)
    return out, outb, saved


def _mlp_ple_bwd(tag, d_out, sv, ln1_g, ln2_g, wt):
    d = d_out.shape[1]

    def fn(i, o, r):
        dy, pe, gp = i[0][...], i[1][...], i[2][...]
        sg = _sigmoid(gp)
        o[0][...] = (dy * sg).astype(BF16)
        o[1][...] = (dy * pe * sg * (1.0 - sg)).astype(BF16)

    d_pe, d_gp = _rowwise(f"ple_bwd_{tag}", fn, [_rows(d_out), _rows(sv["pe"]), _rows(sv["gp"])], [(d, BF16), (d, BF16)])
    g_proj = _mm_tn(f"g_ple_proj_{tag}", sv["p"], d_pe, wt["ple_proj" + tag])
    g_gate = _mm_tn(f"g_ple_gate_{tag}", sv["h2b"], d_gp, wt["ple_gate" + tag])
    (d_h2,) = _mm_nt(f"d_ple_gate_{tag}", d_gp, wt["ple_gate" + tag], (F32,), epilogue=lambda acc, dy: (dy + acc,),
                     extras=[(d_out, "tile")])
    d_z2, d_z2b, g_ln2_g, g_ln2_b = _layer_norm_bwd(f"ln2_bwd_{tag}", sv["z2"], ln2_g, d_h2)
    ex = wt.scatter({"ple_proj" + tag: g_proj, "ple_gate" + tag: g_gate})
    g_down = _mm_tn(f"g_mlp_down_{tag}", sv["act"], d_z2b, wt["mlp_down" + tag], carry=ex)
    wt.received(ex)
    (d_up,) = _mm_nt(f"d_mlp_down_{tag}", d_z2b, wt["mlp_down" + tag], (BF16,),
                     epilogue=lambda acc, u: (acc * (2.0 * jnp.maximum(u, 0.0)),), extras=[(sv["up"], "tile")])
    ex = wt.scatter({"mlp_down" + tag: g_down})
    g_up = _mm_tn(f"g_mlp_up_{tag}", sv["h1b"], d_up, wt["mlp_up" + tag], carry=ex)
    wt.received(ex)
    ex = wt.scatter({"mlp_up" + tag: g_up})
    (d_h1,) = _mm_nt(f"d_mlp_up_{tag}", d_up, wt["mlp_up" + tag], (F32,), epilogue=lambda acc, dz: (ALPHA * dz + acc,),
                     extras=[(d_z2, "tile")], carry=ex)
    wt.received(ex)
    d_z1, d_z1b, g_ln1_g, g_ln1_b = _layer_norm_bwd(f"ln1_bwd_{tag}", sv["z1"], ln1_g, d_h1)
    return d_z1, d_z1b, dict(ln1_g=g_ln1_g, ln1_b=g_ln1_b, ln2_g=g_ln2_g, ln2_b=g_ln2_b)


def _local_step(x, p, positions, target, wt, small):
    s, d = x.shape
    nh = d // HEAD_DIM

    ex = wt.gather("mlp_up0")
    (u,) = _mm_nn("conv_in", x, wt["conv_w_in"], (F32,), epilogue=lambda acc, b: (acc + b,), extras=[(small["conv_b_in"], "row")],
                  carry=ex)
    wt.landed(ex)

    def glu_fn(i, o, r):
        o[0][...] = i[0][...] * _sigmoid(i[1][...])

    (glu,) = _rowwise("glu", glu_fn, [_rows(u, 0, d), _rows(u, 1, d)], [(d, F32)])
    ex = wt.gather("mlp_down0")
    c = _conv_fwd("dwconv", glu, small["conv_dw"], small["conv_dw_b"], carry=ex)
    wt.landed(ex)

    def ln_silu(i, o, r):
        xhat, _ = _ln_stats(i[0][...])
        n = xhat * i[1][...] + i[2][...]
        o[0][...] = (n * _sigmoid(n)).astype(BF16)

    (sb,) = _rowwise("conv_ln_silu", ln_silu, [_rows(c), _full(small["conv_ln_g"]), _full(small["conv_ln_b"])], [(d, BF16)])
    ex = wt.gather("ple_proj0", "ple_gate0")
    (z1,) = _mm_nn("conv_out", sb, wt["conv_w_out"], (F32,), epilogue=lambda acc, xt: (ALPHA * xt + acc,), extras=[(x, "tile")],
                   carry=ex)
    wt.landed(ex)
    x1, x1b, sv0 = _mlp_ple_fwd("0", z1, p[0], small["ln1_g"][0:1], small["ln1_b"][0:1], small["ln2_g"][0:1], small["ln2_b"][0:1], wt,
                                dict(mlp_up=wt.gather("w_kv", "attn_w_q"), mlp_down=wt.gather("attn_w_o", "mlp_up1")))

    (kvn,) = _layer_norm("kv_ln", x1, small["kv_ln_g"], small["kv_ln_b"], (BF16,))
    (kv,) = _mm_nn("kv_proj", kvn, wt["w_kv"], (F32,))
    ex = wt.gather("mlp_down1")
    (q,) = _mm_nn("q_proj", x1b, wt["attn_w_q"], (F32,), carry=ex)
    wt.landed(ex)
    cos, sin = _rope_tables(positions)

    def rot_kv(i, o, r):
        cs, sn = i[2][...], i[3][...]
        for h in range(nh):
            hs = slice(h * HEAD_DIM, (h + 1) * HEAD_DIM)
            o[0][:, hs] = _rot(i[0][:, hs], cs, sn).astype(BF16)
        o[1][...] = i[1][...].astype(BF16)

    kr, vb = _rowwise("rotary_kv", rot_kv, [_rows(kv, 0, d), _rows(kv, 1, d), _rows(cos), _rows(sin)], [(d, BF16), (d, BF16)])

    def rot_q(i, o, r):
        cs, sn = i[1][...], i[2][...]
        for h in range(N_GROUPS * nh):
            hs = slice(h * HEAD_DIM, (h + 1) * HEAD_DIM)
            o[0][:, hs] = _rot(i[0][:, hs], cs, sn).astype(BF16)

    (qr,) = _rowwise("rotary_q", rot_q, [_rows(q), _rows(cos), _rows(sin)], [(N_GROUPS * d, BF16)])

    ex = wt.gather("ple_proj1", "ple_gate1")
    og, lg = zip(*[_attn_fwd(f"attn_fwd_{g}", qr, kr, vb, g, dil, carry=ex if g == 0 else None)
                   for g, dil in enumerate(GROUP_DILATIONS)])
    wt.landed(ex)

    def merge(i, o, r):
        ls = [i[N_GROUPS + g][...] for g in range(N_GROUPS)]
        top = functools.reduce(jnp.maximum, ls)
        es = [jnp.exp(l - top) for l in ls]
        den = functools.reduce(lambda a, b: a + b, es)
        out = functools.reduce(lambda a, b: a + b, [e * i[g][...] for g, e in enumerate(es)]) / den
        o[0][...] = out
        o[1][...] = out.astype(BF16)
        o[2][...] = top + jnp.log(den)

    o, ob, lse = _rowwise("attn_merge", merge, [_rows(t) for t in og + lg], [(d, F32), (d, BF16), (d, F32)], ts=128)
    (z1b,) = _mm_nn("attn_out", ob, wt["attn_w_o"], (F32,), epilogue=lambda acc, xt: (ALPHA * xt + acc,), extras=[(x1, "tile")])
    y, _, sv1 = _mlp_ple_fwd("1", z1b, p[1], small["ln1_g"][1:2], small["ln1_b"][1:2], small["ln2_g"][1:2], small["ln2_b"][1:2], wt, {})

    def loss_fn(i, o, r):
        diff = i[0][...] - i[1][...]
        o[0][...] = diff * (1.0 / d)
        r[0][...] += jnp.broadcast_to(jnp.sum(diff * diff), (1, LANE))

    d_y, sq = _rowwise("loss", loss_fn, [_rows(y), _rows(target)], [(d, F32)], [(1, LANE)])
    loss = 0.5 * sq[0, 0] / d

    d_z1, d_z1b, g1 = _mlp_ple_bwd("1", d_y, sv1, small["ln1_g"][1:2], small["ln2_g"][1:2], wt)
    g_wo = _mm_tn("g_attn_out", ob, d_z1b, wt["attn_w_o"])
    d_o, d_ob = _mm_nt("d_attn_out", d_z1b, wt["attn_w_o"], (F32, BF16))

    def dsum_fn(i, o, r):
        for h in range(nh):
            hs = slice(h * HEAD_DIM, (h + 1) * HEAD_DIM)
            o[0][:, hs] = jnp.broadcast_to(jnp.sum(i[0][:, hs] * i[1][:, hs], axis=1, keepdims=True), (i[0].shape[0], HEAD_DIM))

    (dsum,) = _rowwise("attn_dsum", dsum_fn, [_rows(d_o), _rows(o)], [(d, F32)])
    dqs, dks, dvs = [], [], []
    ex = wt.scatter({"attn_w_o": g_wo})
    for g, dil in enumerate(GROUP_DILATIONS):
        dqs.append(_attn_dq(f"attn_dq_{g}", qr, kr, vb, d_ob, lse, dsum, g, dil, carry=ex if g == 0 else None))
        dk, dv = _attn_dkv(f"attn_dkv_{g}", qr, kr, vb, d_ob, lse, dsum, g, dil)
        dks.append(dk)
        dvs.append(dv)
    wt.received(ex)

    def unrot(x_, cs, sn):
        return x_ * cs - pltpu.roll(x_, HEAD_DIM // 2, 1) * sn

    def unrot_q(i, o, r):
        cs, sn = i[N_GROUPS][...], i[N_GROUPS + 1][...]
        for g in range(N_GROUPS):
            for h in range(nh):
                hs = slice(h * HEAD_DIM, (h + 1) * HEAD_DIM)
                o[0][:, g * d + h * HEAD_DIM:g * d + (h + 1) * HEAD_DIM] = unrot(i[g][:, hs], cs, sn).astype(BF16)

    (d_q,) = _rowwise("rotary_q_bwd", unrot_q, [_rows(t) for t in dqs] + [_rows(cos), _rows(sin)], [(N_GROUPS * d, BF16)])

    def unrot_kv(i, o, r):
        cs, sn = i[2 * N_GROUPS][...], i[2 * N_GROUPS + 1][...]
        for h in range(nh):
            hs = slice(h * HEAD_DIM, (h + 1) * HEAD_DIM)
            dk = functools.reduce(lambda a, b: a + b, [i[g][:, hs] for g in range(N_GROUPS)])
            o[0][:, hs] = unrot(dk, cs, sn).astype(BF16)
            dv = functools.reduce(lambda a, b: a + b, [i[N_GROUPS + g][:, hs] for g in range(N_GROUPS)])
            o[0][:, d + h * HEAD_DIM:d + (h + 1) * HEAD_DIM] = dv.astype(BF16)

    (d_kv,) = _rowwise("rotary_kv_bwd", unrot_kv, [_rows(t) for t in dks + dvs] + [_rows(cos), _rows(sin)], [(2 * d, BF16)], ts=128)
    g_wq = _mm_tn("g_q_proj", x1b, d_q, wt["attn_w_q"])
    ex = wt.scatter({"attn_w_q": g_wq})
    g_wkv = _mm_tn("g_kv_proj", kvn, d_kv, wt["w_kv"], carry=ex)
    wt.received(ex)
    ex = wt.scatter({"w_kv": g_wkv})
    (d_x1a,) = _mm_nt("d_q_proj", d_q, wt["attn_w_q"], (F32,), epilogue=lambda acc, dz: (ALPHA * dz + acc,), extras=[(d_z1, "tile")],
                      carry=ex)
    wt.received(ex)
    (d_kvn,) = _mm_nt("d_kv_proj", d_kv, wt["w_kv"], (F32,))
    d_x1, _, g_kv_ln_g, g_kv_ln_b = _layer_norm_bwd("kv_ln_bwd", x1, small["kv_ln_g"], d_kvn, extra=d_x1a)

    d_z1, d_z1b, g0 = _mlp_ple_bwd("0", d_x1, sv0, small["ln1_g"][0:1], small["ln2_g"][0:1], wt)
    g_wout = _mm_tn("g_conv_out", sb, d_z1b, wt["conv_w_out"])
    (d_s,) = _mm_nt("d_conv_out", d_z1b, wt["conv_w_out"], (F32,))

    def ln_silu_bwd(i, o, r):
        cx, gn, bn, ds_ = i[0][...], i[1][...], i[2][...], i[3][...]
        xhat, _ = _ln_stats(cx)
        n = xhat * gn + bn
        sg = _sigmoid(n)
        dn = ds_ * (sg * (1.0 + n * (1.0 - sg)))
        dx, dg, db = _ln_bwd_tile(cx, gn, dn)
        o[0][...] = dx
        r[0][...] += dg
        r[1][...] += db

    d_c, g_cln_g, g_cln_b = _rowwise("conv_ln_silu_bwd", ln_silu_bwd,
                                     [_rows(c), _full(small["conv_ln_g"]), _full(small["conv_ln_b"]), _rows(d_s)],
                                     [(d, F32)], [(1, d), (1, d)])
    ex = wt.scatter({"conv_w_out": g_wout})
    d_glu, g_dw, g_dwb = _conv_bwd("dwconv_bwd", glu, d_c, small["conv_dw"], carry=ex)
    wt.received(ex)

    def glu_bwd(i, o, r):
        a, gt, dg_ = i[0][...], i[1][...], i[2][...]
        sg = _sigmoid(gt)
        da = dg_ * sg
        dgate = dg_ * a * sg * (1.0 - sg)
        o[0][:, 0:d] = da.astype(BF16)
        o[0][:, d:2 * d] = dgate.astype(BF16)
        r[0][:, 0:d] += jnp.sum(da, axis=0, keepdims=True)
        r[0][:, d:2 * d] += jnp.sum(dgate, axis=0, keepdims=True)

    d_u, g_bin = _rowwise("glu_bwd", glu_bwd, [_rows(u, 0, d), _rows(u, 1, d), _rows(d_glu)], [(2 * d, BF16)], [(1, 2 * d)])
    g_win = _mm_tn("g_conv_in", x, d_u, wt["conv_w_in"])
    rows = [g_bin.reshape(2, d), g_dw, g_dwb, g_cln_g, g_cln_b, g_kv_ln_g, g_kv_ln_b]
    rows += [jnp.concatenate([g0[n], g1[n]], axis=0) for n in ("ln1_g", "ln1_b", "ln2_g", "ln2_b")]
    rows, offsets = _stack_rows(rows)
    ex = wt.scatter({"conv_w_in": g_win}, gathers=[rows])
    (grad_x,) = _mm_nt("d_conv_in", d_u, wt["conv_w_in"], (F32,), epilogue=lambda acc, dz: (ALPHA * dz + acc,), extras=[(d_z1, "tile")],
                       carry=ex)
    wt.received(ex)
    return loss, grad_x, ex.gathered[0], offsets


BIG = ("conv_w_in", "conv_w_out", "w_kv", "attn_w_q", "attn_w_o", "mlp_up", "mlp_down", "ple_proj", "ple_gate")
COLUMN_SHARDED = ("conv_w_in", "w_kv", "attn_w_q", "mlp_up", "ple_proj")
WEIGHTS = ("conv_w_in", "conv_b_in", "conv_dw", "conv_dw_b", "conv_ln_g", "conv_ln_b", "conv_w_out", "kv_ln_g", "kv_ln_b",
           "w_kv", "attn_w_q", "attn_w_o", "ln1_g", "ln1_b", "mlp_up", "mlp_down", "ln2_g", "ln2_b", "ple_proj", "ple_gate")


def kernel(x, p, positions, conv_w_in, conv_b_in, conv_dw, conv_dw_b, conv_ln_g, conv_ln_b, conv_w_out, kv_ln_g, kv_ln_b, w_kv, attn_w_q, attn_w_o, ln1_g, ln1_b, mlp_up, mlp_down, ln2_g, ln2_b, ple_proj, ple_gate, loss_target, m_conv_w_in, m_conv_b_in, m_conv_dw, m_conv_dw_b, m_conv_ln_g, m_conv_ln_b, m_conv_w_out, m_kv_ln_g, m_kv_ln_b, m_w_kv, m_attn_w_q, m_attn_w_o, m_ln1_g, m_ln1_b, m_mlp_up, m_mlp_down, m_ln2_g, m_ln2_b, m_ple_proj, m_ple_gate, v_conv_w_in, v_conv_b_in, v_conv_dw, v_conv_dw_b, v_conv_ln_g, v_conv_ln_b, v_conv_w_out, v_kv_ln_g, v_kv_ln_b, v_w_kv, v_attn_w_q, v_attn_w_o, v_ln1_g, v_ln1_b, v_mlp_up, v_mlp_down, v_ln2_g, v_ln2_b, v_ple_proj, v_ple_gate):
    given = dict(locals())
    wts = {n: given[n] for n in WEIGHTS}
    moms = {n: given["m_" + n] for n in WEIGHTS}
    vels = {n: given["v_" + n] for n in WEIGHTS}
    s, d = x.shape[1], x.shape[2]
    shard = d // N_DEV
    me = 4 * lax.axis_index("x") + 2 * lax.axis_index("y") + lax.axis_index("c")

    def layers_of(a):
        return a.reshape((-1,) + a.shape[-2:])

    shards = {}
    for n in BIG:
        w3 = layers_of(wts[n])
        for ly in range(w3.shape[0]):
            shards[n + str(ly) if w3.shape[0] > 1 else n] = w3[ly].astype(BF16)
    wt = Weights(shards)
    pack, at = _stack_rows([wts["conv_b_in"].reshape(2, shard), wts["conv_dw"].reshape(CONV_WIDTH, shard),
                            wts["conv_dw_b"], wts["conv_ln_g"], wts["conv_ln_b"]])
    ex = Exchange(gathers=[shards["conv_w_in"], shards["conv_w_out"], pack], keys=["conv_w_in", "conv_w_out"])
    _exchange_alone("gather_first", ex)
    wt.landed(ex)
    packed = ex.gathered[2]
    small = dict(conv_b_in=packed[:, at[0]:at[0] + 2].reshape(1, 2 * d), conv_dw=packed[:, at[1]:at[1] + CONV_WIDTH],
                 conv_dw_b=packed[:, at[2]].reshape(1, d), conv_ln_g=packed[:, at[3]].reshape(1, d),
                 conv_ln_b=packed[:, at[4]].reshape(1, d), kv_ln_g=kv_ln_g.reshape(1, d), kv_ln_b=kv_ln_b.reshape(1, d),
                 ln1_g=ln1_g, ln1_b=ln1_b, ln2_g=ln2_g, ln2_b=ln2_b)

    loss, grad_x, all_rows, at = _local_step(x[0], p[:, 0], positions.reshape(s, 1), loss_target[0], wt, small)
    loss = lax.psum(loss, ("x", "y", "c"))

    out = {}
    for n in BIG:
        w3 = layers_of(wts[n])
        keys = [n + str(ly) if w3.shape[0] > 1 else n for ly in range(w3.shape[0])]
        res = _adamw_big("adamw_" + n, [wt.parts[k] for k in keys], w3, layers_of(moms[n]), layers_of(vels[n]))
        out[n] = [r.reshape(wts[n].shape) for r in res]
    tot = _sum_slots("sum_small_grads", all_rows)
    mine = lax.dynamic_slice_in_dim(tot, me * shard, shard, axis=1)
    b_in = lax.dynamic_slice_in_dim(tot[at[0]:at[0] + 2].reshape(1, 2 * d), me * 2 * shard, 2 * shard, axis=1)
    g_small = dict(conv_b_in=b_in, conv_dw=mine[at[1]:at[1] + CONV_WIDTH].reshape(conv_dw.shape), conv_dw_b=mine[at[2]:at[2] + 1],
                   conv_ln_g=mine[at[3]:at[3] + 1], conv_ln_b=mine[at[4]:at[4] + 1], kv_ln_g=tot[at[5]], kv_ln_b=tot[at[6]])
    for j, n in enumerate(("ln1_g", "ln1_b", "ln2_g", "ln2_b")):
        g_small[n] = tot[at[7 + j]:at[7 + j] + DEPTH]
    order = [n for n in WEIGHTS if n not in BIG]

    def flat(t):
        return _stack_rows([t[n].reshape(-1, shard) for n in order])

    (w_s, at), (g_s, _), (m_s, _), (v_s, _) = flat(wts), flat(g_small), flat(moms), flat(vels)
    d_s, m_s, v_s = _adamw_small("adamw_small", w_s, g_s, m_s, v_s)
    for n, a in zip(order, at):
        nrow = wts[n].size // shard
        out[n] = [g_small[n].reshape(wts[n].shape)] + [t[a:a + nrow].reshape(wts[n].shape) for t in (d_s, m_s, v_s)]
    return (loss, grad_x[None], *[out[n][0] for n in WEIGHTS], *[out[n][1] for n in WEIGHTS],
            *[out[n][2] for n in WEIGHTS], *[out[n][3] for n in WEIGHTS])
```

```python
import functools

import numpy as np
import jax
import jax.numpy as jnp
from jax import lax
from jax.experimental import pallas as pl
from jax.experimental.pallas import tpu as pltpu

F32, BF16 = jnp.float32, jnp.bfloat16

N_DEV = 8
HEAD_DIM = 128
ATTN_BLOCK = 128
GROUP_DILATIONS = (1, 4, 16)
N_GROUPS = len(GROUP_DILATIONS)
CONV_WIDTH = 31
CONV_HALO = 32
CONV_ROWS = 64
ROPE_THETA = 10000.0
LN_EPS = 1e-5
DEPTH = 2
ALPHA = (2 * DEPTH) ** 0.25
ADAM_LR, ADAM_B1, ADAM_B2, ADAM_EPS, ADAM_WD, ADAM_STEP = 0.001, 0.9, 0.999, 1e-08, 0.01, 10
NEG = -1e30
V7X_VMEM_LIMIT = 56 * 2 ** 20
LANE = 128
SUBLANES = 8
GRAD_DTYPE = BF16

MESH = pl.DeviceIdType.MESH
ANY = pl.BlockSpec(memory_space=pl.ANY)


def _params(*sem):
    return pltpu.CompilerParams(dimension_semantics=sem or None, vmem_limit_bytes=V7X_VMEM_LIMIT)


def _sigmoid(x):
    return 1.0 / (1.0 + jnp.exp(-x))


def _divisor(n, most):
    best = None
    for t in range(LANE, min(n, most) + 1, LANE):
        if n % t == 0:
            best = t
    assert best is not None, (n, most)
    return best


def _stack_rows(parts):
    out, offsets, at = [], [], 0
    for a in parts:
        pad = -a.shape[0] % SUBLANES
        offsets.append(at)
        out.append(a)
        if pad:
            out.append(jnp.zeros((pad, a.shape[1]), a.dtype))
        at += a.shape[0] + pad
    return jnp.concatenate(out, axis=0), offsets


class Exchange:
    OTHER_CHIPS = (4, 2, 6)

    def __init__(self, gathers=(), scatters=(), keys=(), chunk=(0, 1)):
        self.gathers, self.g_cols = [a for a, _ in gathers], [c for _, c in gathers]
        self.scatters, self.s_cols = [a for a, _ in scatters], [c for _, c in scatters]
        self.keys, self.chunk = list(keys), chunk
        self.n_g, self.n_s = len(self.gathers), len(self.scatters)
        self.n = self.n_g + self.n_s
        self.operands = self.gathers + self.scatters
        self.gathered = self.parts = None

    def rows(self, t):
        a = self.scatters[t]
        per = (a.shape[0] if self.s_cols[t] else a.shape[1]) // self.chunk[1]
        return self.chunk[0] * per, per

    def out_shape(self):
        outs = []
        for a, cols in zip(self.gathers, self.g_cols):
            outs.append(jax.ShapeDtypeStruct((a.shape[0], N_DEV * a.shape[1]) if cols else (N_DEV,) + a.shape, a.dtype))
        for t, (a, cols) in enumerate(zip(self.scatters, self.s_cols)):
            outs.append(jax.ShapeDtypeStruct((N_DEV, self.rows(t)[1], a.shape[1] // N_DEV if cols else a.shape[2]), a.dtype))
        return outs

    def scratch(self):
        dma = pltpu.SemaphoreType.DMA
        return [dma((max(self.n_g, 1) * 7,)), dma((max(self.n_g, 1) * 7,)), dma((max(self.n_s, 1) * 7,)),
                dma((max(self.n_s, 1) * 7,)), dma((self.n,))]

    def take(self, results):
        self.gathered, self.parts = list(results[:self.n_g]), list(results[self.n_g:])

    def _copies(self, ins, outs, sems):
        n_g, n_s = self.n_g, self.n_s
        g_in, s_in, g_out, s_out = ins[:n_g], ins[n_g:], outs[:n_g], outs[n_g:]
        g_send, g_recv, s_send, s_recv, local_sem = sems
        x, y, c = lax.axis_index("x"), lax.axis_index("y"), lax.axis_index("c")

        def peer(k):
            return (1 - x if k & 4 else x, 1 - y if k & 2 else y, 1 - c if k & 1 else c)

        def number(p):
            return 4 * p[0] + 2 * p[1] + p[2]

        me = number((x, y, c))

        def slot(t, j):
            first, count = self.rows(t)
            if self.s_cols[t]:
                width = self.scatters[t].shape[1] // N_DEV
                return s_in[t].at[pl.ds(first, count), pl.ds(pl.multiple_of(j * width, LANE), width)]
            return s_in[t].at[j, pl.ds(first, count)]

        def place(t, j):
            if self.g_cols[t]:
                width = self.gathers[t].shape[1]
                return g_out[t].at[:, pl.ds(pl.multiple_of(j * width, LANE), width)]
            return g_out[t].at[j]

        def local():
            cps = [pltpu.make_async_copy(g_in[t], place(t, me), local_sem.at[t]) for t in range(n_g)]
            return cps + [pltpu.make_async_copy(slot(t, me), s_out[t].at[me], local_sem.at[n_g + t]) for t in range(n_s)]

        def scatter(t, k):
            p = peer(k)
            return pltpu.make_async_remote_copy(
                src_ref=slot(t, number(p)), dst_ref=s_out[t].at[me], send_sem=s_send.at[t * 7 + k - 1],
                recv_sem=s_recv.at[t * 7 + k - 1], device_id=p, device_id_type=MESH)

        def landed(t, k):
            p = peer(k)
            return pltpu.make_async_remote_copy(
                src_ref=slot(t, me), dst_ref=s_out[t].at[number(p)], send_sem=s_send.at[t * 7 + k - 1],
                recv_sem=s_recv.at[t * 7 + k - 1], device_id=p, device_id_type=MESH)

        def gather(t, pair, block, to, src=None):
            slot = place(t, number(block))
            return pltpu.make_async_remote_copy(
                src_ref=slot if src is None else src, dst_ref=slot, send_sem=g_send.at[t * 7 + pair],
                recv_sem=g_recv.at[t * 7 + pair], device_id=to, device_id_type=MESH)

        def first_sends():
            cps = []
            for t in range(n_g):
                cps.append(gather(t, 0, peer(0), peer(1), src=g_in[t]))
                cps += [gather(t, 1 + j, peer(0), peer(k), src=g_in[t]) for j, k in enumerate(self.OTHER_CHIPS)]
            for t in range(n_s):
                cps += [scatter(t, k) for k in range(1, N_DEV)]
            return cps

        return peer, local, landed, gather, first_sends

    def start(self, ins, outs, sems):
        _, local, _, _, first_sends = self._copies(ins, outs, sems)
        for cp in local() + first_sends():
            cp.start()

    def finish(self, ins, outs, sems):
        peer, local, landed, gather, first_sends = self._copies(ins, outs, sems)
        mine, sibling = peer(0), peer(1)
        passed = []
        for j, k in enumerate(self.OTHER_CHIPS):
            for t in range(self.n_g):
                gather(t, 1 + j, peer(k), mine).wait_recv()
                passed.append(gather(t, 4 + j, peer(k), sibling))
                passed[-1].start()
        for t in range(self.n_g):
            gather(t, 0, sibling, mine).wait_recv()
            for j, k in enumerate(self.OTHER_CHIPS):
                gather(t, 4 + j, peer(k ^ 1), mine).wait_recv()
        for t in range(self.n_s):
            for k in range(1, N_DEV):
                landed(t, k).wait_recv()
        for cp in first_sends() + passed:
            cp.wait_send()
        for cp in local():
            cp.wait()


def _exchange_alone(name, ex):
    def body(*refs):
        ins, outs, sems = refs[:ex.n], refs[ex.n:2 * ex.n], refs[2 * ex.n:]
        ex.start(ins, outs, sems)
        ex.finish(ins, outs, sems)

    ex.take(pl.pallas_call(body, name=name, in_specs=[ANY] * ex.n, out_specs=[ANY] * ex.n, out_shape=ex.out_shape(),
                           scratch_shapes=ex.scratch())(*ex.operands))


def _call(name, body, args, *, grid, in_specs, out_specs, out_shape, scratch_shapes=(), sem=(), carry=None):
    if carry is None:
        return pl.pallas_call(body, name=name, grid=grid, in_specs=in_specs, out_specs=out_specs, out_shape=out_shape,
                              scratch_shapes=list(scratch_shapes), compiler_params=_params(*sem))(*args)
    ex = carry
    n_in, n_out, n_scr = len(args), len(out_shape), len(scratch_shapes)

    def carried(*refs):
        ins, cin = refs[:n_in], refs[n_in:n_in + ex.n]
        at = n_in + ex.n
        outs, cout = refs[at:at + n_out], refs[at + n_out:at + n_out + ex.n]
        at += n_out + ex.n
        scr, sems = refs[at:at + n_scr], refs[at + n_scr:]
        ids = [pl.program_id(a) for a in range(len(grid))]
        first = functools.reduce(jnp.logical_and, [i == 0 for i in ids])
        last = functools.reduce(jnp.logical_and, [i == g - 1 for i, g in zip(ids, grid)])

        @pl.when(first)
        def _():
            ex.start(cin, cout, sems)

        body(*ins, *outs, *scr)

        @pl.when(last)
        def _():
            ex.finish(cin, cout, sems)

    res = pl.pallas_call(
        carried, name=name, grid=grid, in_specs=list(in_specs) + [ANY] * ex.n, out_specs=list(out_specs) + [ANY] * ex.n,
        out_shape=list(out_shape) + ex.out_shape(), scratch_shapes=list(scratch_shapes) + ex.scratch(),
        compiler_params=_params(*("arbitrary",) * len(grid)),
    )(*args, *ex.operands)
    ex.take(res[n_out:])
    return res[:n_out]


class W:
    def __init__(self, arr, cols):
        self.arr, self.cols = arr, cols
        self.k, self.n = arr.shape
        self.shard_cols = self.n // N_DEV if cols else self.n


def _matmul(name, grid, operands, specs, dims, tile, extras, outs, out_specs, epilogue, carry=None):
    assert grid[2] == 1
    n_ex, n_out = len(extras), len(outs)

    def body(*refs):
        a_ref, b_ref = refs[0], refs[1]
        ex_refs = refs[2:2 + n_ex]
        out_refs = refs[2 + n_ex:2 + n_ex + n_out]
        acc = lax.dot_general(a_ref[...].astype(BF16), b_ref[...].astype(BF16), (dims, ((), ())),
                              preferred_element_type=F32)
        res = epilogue(acc, *[r[...] for r in ex_refs]) if epilogue else (acc,) * n_out
        for r, v in zip(out_refs, res):
            r[...] = v.astype(r.dtype)

    return _call(name, body, list(operands) + [a for a, _ in extras], grid=grid,
                 in_specs=list(specs) + [s for _, s in extras], out_specs=out_specs, out_shape=outs,
                 sem=("parallel", "parallel", "arbitrary"), carry=carry)


def _extra_specs(extras, tm, tn):
    out = []
    for arr, kind in extras:
        if kind == "tile":
            out.append((arr, pl.BlockSpec((tm, tn), lambda i, j, c: (i, j))))
        else:
            out.append((arr, pl.BlockSpec((1, tn), lambda i, j, c: (0, j))))
    return out


def _mm_nn(name, a, w, out_dtypes, epilogue=None, extras=(), carry=None):
    m, k = a.shape
    assert k == w.k
    tm = 1024 if a.dtype == BF16 else 512
    tn = _divisor(w.n, 512 if k <= 2048 else 256)
    assert tm * k * a.dtype.itemsize <= 16 * 2 ** 20, (name, tm, k)
    grid = (m // tm, w.n // tn, 1)
    specs = [pl.BlockSpec((tm, k), lambda i, j, c: (i, 0)), pl.BlockSpec((k, tn), lambda i, j, c: (0, j))]
    outs = [jax.ShapeDtypeStruct((m, w.n), d) for d in out_dtypes]
    out_specs = [pl.BlockSpec((tm, tn), lambda i, j, c: (i, j)) for _ in outs]
    return _matmul(name, grid, (a, w.arr), specs, ((1,), (0,)), (tm, tn), _extra_specs(extras, tm, tn), outs, out_specs,
                   epilogue, carry)


def _mm_nt(name, dy, w, out_dtypes, epilogue=None, extras=(), carry=None):
    m, n = dy.shape
    assert n == w.n and dy.dtype == BF16
    tm = 1024
    to = _divisor(w.k, 512 if n <= 2048 else 256)
    assert tm * n * dy.dtype.itemsize <= 16 * 2 ** 20, (name, tm, n)
    grid = (m // tm, w.k // to, 1)
    specs = [pl.BlockSpec((tm, n), lambda i, j, c: (i, 0)), pl.BlockSpec((to, n), lambda i, j, c: (j, 0))]
    outs = [jax.ShapeDtypeStruct((m, w.k), d) for d in out_dtypes]
    out_specs = [pl.BlockSpec((tm, to), lambda i, j, c: (i, j)) for _ in outs]
    return _matmul(name, grid, (dy, w.arr), specs, ((1,), (1,)), (tm, to), _extra_specs(extras, tm, to), outs, out_specs,
                   epilogue, carry)


def _mm_tn(name, a, dy, like, carry=None):
    m, k = a.shape
    n = dy.shape[1]
    assert (k, n) == (like.k, like.n)
    tk = _divisor(k, 1024 if a.dtype == BF16 else 512)
    tn = _divisor(n, 1024)
    grid = (k // tk, n // tn, 1)
    specs = [pl.BlockSpec((m, tk), lambda i, j, c: (0, i)), pl.BlockSpec((m, tn), lambda i, j, c: (0, j))]
    out_specs = [pl.BlockSpec((tk, tn), lambda i, j, c: (i, j))]
    (g,) = _matmul(name, grid, (a, dy), specs, ((0,), (0,)), (tk, tn), [], [jax.ShapeDtypeStruct((k, n), GRAD_DTYPE)],
                   out_specs, None, carry)
    return g if like.cols else g.reshape(N_DEV, k // N_DEV, n)


def _rows(arr, blk=0, width=None):
    return ("rows", arr, blk, width or arr.shape[1])


def _full(arr):
    return ("full", arr)


def _rowwise(name, fn, ins, outs, reds=(), ts=256, carry=None):
    s = next(i[1].shape[0] for i in ins if i[0] == "rows")
    n_in, n_out = len(ins), len(outs)
    in_specs = []
    for i in ins:
        if i[0] == "rows":
            in_specs.append(pl.BlockSpec((ts, i[3]), functools.partial(lambda t, blk: (t, blk), blk=i[2])))
        else:
            in_specs.append(pl.BlockSpec(i[1].shape, functools.partial(lambda t, nd: (0,) * nd, nd=i[1].ndim)))
    out_shape = [jax.ShapeDtypeStruct((s, w), d) for w, d in outs] + [jax.ShapeDtypeStruct(r, F32) for r in reds]
    out_specs = [pl.BlockSpec((ts, w), lambda t: (t, 0)) for w, _ in outs] + [pl.BlockSpec(r, lambda t: (0, 0)) for r in reds]

    def body(*refs):
        red_refs = refs[n_in + n_out:]
        if red_refs:
            @pl.when(pl.program_id(0) == 0)
            def _():
                for r in red_refs:
                    r[...] = jnp.zeros(r.shape, F32)
        fn(refs[:n_in], refs[n_in:n_in + n_out], red_refs)

    return _call(name, body, [i[1] for i in ins], grid=(s // ts,), in_specs=in_specs, out_specs=out_specs,
                 out_shape=out_shape, sem=("arbitrary" if reds else "parallel",), carry=carry)


def _ln_stats(x):
    mu = jnp.mean(x, axis=-1, keepdims=True)
    xc = x - mu
    var = jnp.mean(xc * xc, axis=-1, keepdims=True)
    return xc * lax.rsqrt(var + LN_EPS), lax.rsqrt(var + LN_EPS)


def _layer_norm(name, x, g, b, out_dtypes):
    def fn(i, o, r):
        xhat, _ = _ln_stats(i[0][...])
        y = xhat * i[1][...] + i[2][...]
        for ref in o:
            ref[...] = y.astype(ref.dtype)

    return _rowwise(name, fn, [_rows(x), _full(g), _full(b)], [(x.shape[1], d) for d in out_dtypes])


def _ln_bwd_tile(x, g, dy):
    xhat, rstd = _ln_stats(x)
    dyg = dy * g
    m1 = jnp.mean(dyg, axis=-1, keepdims=True)
    m2 = jnp.mean(dyg * xhat, axis=-1, keepdims=True)
    dx = rstd * (dyg - m1 - xhat * m2)
    return dx, jnp.sum(dy * xhat, axis=0, keepdims=True), jnp.sum(dy, axis=0, keepdims=True)


def _layer_norm_bwd(name, x, g, dy, extra=None):
    d = x.shape[1]

    def fn(i, o, r):
        dx, dg, db = _ln_bwd_tile(i[0][...], i[1][...], i[2][...])
        if extra is not None:
            dx = dx + i[3][...]
        o[0][...] = dx
        o[1][...] = dx.astype(BF16)
        r[0][...] += dg
        r[1][...] += db

    ins = [_rows(x), _full(g), _rows(dy)] + ([_rows(extra)] if extra is not None else [])
    return _rowwise(name, fn, ins, [(d, F32), (d, BF16)], [(1, d), (1, d)])


def _conv_fwd(name, glu, dw, dw_b, ts=512, carry=None):
    s, c = glu.shape
    tc = dw.shape[2]
    per = ts // CONV_HALO
    back = CONV_HALO - (CONV_WIDTH - 1)

    def body(cur_ref, prev_ref, w_ref, b_ref, out_ref, buf):
        i = pl.program_id(1)
        buf[pl.ds(0, CONV_HALO), :] = jnp.where(i > 0, prev_ref[...], 0.0)
        buf[pl.ds(CONV_HALO, ts), :] = cur_ref[...]
        for r0 in range(0, ts, CONV_ROWS):
            acc = jnp.broadcast_to(b_ref[...], (CONV_ROWS, tc))
            for j in range(CONV_WIDTH):
                acc = acc + w_ref[j:j + 1, :] * buf[pl.ds(r0 + back + j, CONV_ROWS), :]
            out_ref[pl.ds(r0, CONV_ROWS), :] = acc

    (out,) = _call(
        name, body, [glu, glu, dw, dw_b], grid=(c // tc, s // ts),
        in_specs=[pl.BlockSpec((ts, tc), lambda j, i: (i, j)),
                  pl.BlockSpec((CONV_HALO, tc), lambda j, i: (jnp.maximum(i * per - 1, 0), j)),
                  pl.BlockSpec((None, CONV_WIDTH, tc), lambda j, i: (j, 0, 0)),
                  pl.BlockSpec((1, tc), lambda j, i: (0, j))],
        out_specs=[pl.BlockSpec((ts, tc), lambda j, i: (i, j))],
        out_shape=[jax.ShapeDtypeStruct((s, c), F32)],
        scratch_shapes=[pltpu.VMEM((ts + CONV_HALO, tc), F32)],
        sem=("parallel", "parallel"), carry=carry)
    return out


def _conv_bwd(name, glu, dc, dw, ts=512, carry=None):
    s, c = glu.shape
    tc = dw.shape[2]
    per = ts // CONV_HALO
    back = CONV_HALO - (CONV_WIDTH - 1)
    last = s // ts - 1

    def body(g_ref, gprev_ref, dc_ref, dcnext_ref, w_ref, dglu_ref, ddw_ref, ddb_ref, gbuf, dbuf):
        i = pl.program_id(1)

        @pl.when(i == 0)
        def _():
            ddw_ref[...] = jnp.zeros(ddw_ref.shape, F32)
            ddb_ref[...] = jnp.zeros(ddb_ref.shape, F32)

        gbuf[pl.ds(0, CONV_HALO), :] = jnp.where(i > 0, gprev_ref[...], 0.0)
        gbuf[pl.ds(CONV_HALO, ts), :] = g_ref[...]
        dbuf[pl.ds(0, ts), :] = dc_ref[...]
        dbuf[pl.ds(ts, CONV_HALO), :] = jnp.where(i < last, dcnext_ref[...], 0.0)
        taps = [jnp.zeros((1, tc), F32)] * CONV_WIDTH
        for r0 in range(0, ts, CONV_ROWS):
            d_here = dbuf[pl.ds(r0, CONV_ROWS), :]
            acc = jnp.zeros((CONV_ROWS, tc), F32)
            for j in range(CONV_WIDTH):
                acc = acc + w_ref[j:j + 1, :] * dbuf[pl.ds(r0 + (CONV_WIDTH - 1) - j, CONV_ROWS), :]
                taps[j] = taps[j] + jnp.sum(d_here * gbuf[pl.ds(r0 + back + j, CONV_ROWS), :], axis=0, keepdims=True)
            dglu_ref[pl.ds(r0, CONV_ROWS), :] = acc
        for j in range(CONV_WIDTH):
            ddw_ref[j:j + 1, :] += taps[j]
        ddb_ref[...] += jnp.sum(dc_ref[...], axis=0, keepdims=True)

    return _call(
        name, body, [glu, glu, dc, dc, dw], grid=(c // tc, s // ts),
        in_specs=[pl.BlockSpec((ts, tc), lambda j, i: (i, j)),
                  pl.BlockSpec((CONV_HALO, tc), lambda j, i: (jnp.maximum(i * per - 1, 0), j)),
                  pl.BlockSpec((ts, tc), lambda j, i: (i, j)),
                  pl.BlockSpec((CONV_HALO, tc), lambda j, i: (jnp.minimum((i + 1) * per, (last + 1) * per - 1), j)),
                  pl.BlockSpec((None, CONV_WIDTH, tc), lambda j, i: (j, 0, 0))],
        out_specs=[pl.BlockSpec((ts, tc), lambda j, i: (i, j)),
                   pl.BlockSpec((CONV_WIDTH, tc), lambda j, i: (0, j)),
                   pl.BlockSpec((1, tc), lambda j, i: (0, j))],
        out_shape=[jax.ShapeDtypeStruct((s, c), F32), jax.ShapeDtypeStruct((CONV_WIDTH, c), F32),
                   jax.ShapeDtypeStruct((1, c), F32)],
        scratch_shapes=[pltpu.VMEM((ts + CONV_HALO, tc), F32), pltpu.VMEM((ts + CONV_HALO, tc), F32)],
        sem=("parallel", "arbitrary"), carry=carry)


def _rope_tables(positions):
    half = HEAD_DIM // 2
    inv = (np.float32(ROPE_THETA) ** (-np.arange(half, dtype=np.float32) * np.float32(2.0 / HEAD_DIM))).astype(np.float32)
    inv_freq = jnp.asarray(np.concatenate([inv, inv])[None, :])
    sign = jnp.asarray(np.concatenate([-np.ones(half, np.float32), np.ones(half, np.float32)])[None, :])

    def fn(i, o, r):
        ang = i[0][...].astype(F32) * i[1][...]
        o[0][...] = jnp.cos(ang)
        o[1][...] = jnp.sin(ang) * i[2][...]

    return _rowwise("rope_tables", fn, [_rows(positions), _full(inv_freq), _full(sign)], [(HEAD_DIM, F32), (HEAD_DIM, F32)], ts=512)


def _rot(x, cos, sin):
    return x * cos + pltpu.roll(x, HEAD_DIM // 2, 1) * sin


def _masks(n):
    row = lax.broadcasted_iota(jnp.int32, (ATTN_BLOCK, ATTN_BLOCK), 0)
    col = lax.broadcasted_iota(jnp.int32, (ATTN_BLOCK, ATTN_BLOCK), 1)
    return col <= row, jnp.logical_and(col >= row, n > 0)


_NT = (((1,), (1,)), ((), ()))
_TN = (((0,), (0,)), ((), ()))
_NN = (((1,), (0,)), ((), ()))


def _attn_fwd(name, qr, kr, vb, g, dil, carry=None):
    s, d = kr.shape
    nh = d // HEAD_DIM
    ln = s // dil
    nb = ln // ATTN_BLOCK
    scale = HEAD_DIM ** -0.5

    def body(q_ref, kc_ref, kp_ref, vc_ref, vp_ref, o_ref, l_ref):
        mask_c, mask_p = _masks(pl.program_id(1))
        for h in range(nh):
            hs = slice(h * HEAD_DIM, (h + 1) * HEAD_DIM)
            q = q_ref[:, hs]
            s_c = jnp.where(mask_c, lax.dot_general(q, kc_ref[:, hs], _NT, preferred_element_type=F32) * scale, NEG)
            s_p = jnp.where(mask_p, lax.dot_general(q, kp_ref[:, hs], _NT, preferred_element_type=F32) * scale, NEG)
            m = jnp.maximum(jnp.max(s_c, axis=1, keepdims=True), jnp.max(s_p, axis=1, keepdims=True))
            p_c = jnp.exp(s_c - m)
            p_p = jnp.exp(s_p - m)
            l = jnp.sum(p_c, axis=1, keepdims=True) + jnp.sum(p_p, axis=1, keepdims=True)
            o = (lax.dot_general(p_c.astype(BF16), vc_ref[:, hs], _NN, preferred_element_type=F32)
                 + lax.dot_general(p_p.astype(BF16), vp_ref[:, hs], _NN, preferred_element_type=F32))
            o_ref[:, hs] = o / l
            l_ref[:, hs] = jnp.broadcast_to(m + jnp.log(l), (ATTN_BLOCK, HEAD_DIM))

    cur = pl.BlockSpec((ATTN_BLOCK, d), lambda r, n: (n, r))
    prev = pl.BlockSpec((ATTN_BLOCK, d), lambda r, n: (jnp.maximum(n - 1, 0), r))
    view = lambda t: t.reshape(ln, dil * d)
    o, l = _call(
        name, body, [qr.reshape(ln, dil * N_GROUPS * d), view(kr), view(kr), view(vb), view(vb)], grid=(dil, nb),
        in_specs=[pl.BlockSpec((ATTN_BLOCK, d), lambda r, n: (n, r * N_GROUPS + g)), cur, prev, cur, prev],
        out_specs=[cur, cur], out_shape=[jax.ShapeDtypeStruct((ln, dil * d), F32)] * 2,
        sem=("parallel", "parallel"), carry=carry)
    return o.reshape(s, d), l.reshape(s, d)


def _attn_dq(name, qr, kr, vb, do, lse, dsum, g, dil, carry=None):
    s, d = kr.shape
    nh = d // HEAD_DIM
    ln = s // dil
    nb = ln // ATTN_BLOCK
    scale = HEAD_DIM ** -0.5

    def body(q_ref, kc_ref, kp_ref, vc_ref, vp_ref, do_ref, l_ref, d_ref, dq_ref):
        mask_c, mask_p = _masks(pl.program_id(1))
        for h in range(nh):
            hs = slice(h * HEAD_DIM, (h + 1) * HEAD_DIM)
            q, dout = q_ref[:, hs], do_ref[:, hs]
            lrow = l_ref[:, h * HEAD_DIM:h * HEAD_DIM + 1]
            drow = d_ref[:, h * HEAD_DIM:h * HEAD_DIM + 1]
            dq = jnp.zeros((ATTN_BLOCK, HEAD_DIM), F32)
            for mask, k_ref, v_ref in ((mask_c, kc_ref, vc_ref), (mask_p, kp_ref, vp_ref)):
                sc = lax.dot_general(q, k_ref[:, hs], _NT, preferred_element_type=F32) * scale
                p = jnp.where(mask, jnp.exp(jnp.where(mask, sc, NEG) - lrow), 0.0)
                dp = lax.dot_general(dout, v_ref[:, hs], _NT, preferred_element_type=F32)
                ds = p * (dp - drow)
                dq = dq + lax.dot_general(ds.astype(BF16), k_ref[:, hs], _NN, preferred_element_type=F32)
            dq_ref[:, hs] = dq * scale

    cur = pl.BlockSpec((ATTN_BLOCK, d), lambda r, n: (n, r))
    prev = pl.BlockSpec((ATTN_BLOCK, d), lambda r, n: (jnp.maximum(n - 1, 0), r))
    view = lambda t: t.reshape(ln, dil * d)
    (dq,) = _call(
        name, body, [qr.reshape(ln, dil * N_GROUPS * d), view(kr), view(kr), view(vb), view(vb), view(do), view(lse), view(dsum)],
        grid=(dil, nb),
        in_specs=[pl.BlockSpec((ATTN_BLOCK, d), lambda r, n: (n, r * N_GROUPS + g)), cur, prev, cur, prev, cur, cur, cur],
        out_specs=[cur], out_shape=[jax.ShapeDtypeStruct((ln, dil * d), F32)],
        sem=("parallel", "parallel"), carry=carry)
    return dq.reshape(s, d)


def _attn_dkv(name, qr, kr, vb, do, lse, dsum, g, dil, carry=None):
    s, d = kr.shape
    nh = d // HEAD_DIM
    ln = s // dil
    nb = ln // ATTN_BLOCK
    scale = HEAD_DIM ** -0.5

    def body(k_ref, v_ref, qc_ref, qn_ref, doc_ref, don_ref, lc_ref, lnx_ref, dc_ref, dn_ref, dk_ref, dv_ref):
        n = pl.program_id(1)
        row = lax.broadcasted_iota(jnp.int32, (ATTN_BLOCK, ATTN_BLOCK), 0)
        col = lax.broadcasted_iota(jnp.int32, (ATTN_BLOCK, ATTN_BLOCK), 1)
        masks = (col <= row, jnp.logical_and(col >= row, n < nb - 1))
        for h in range(nh):
            hs = slice(h * HEAD_DIM, (h + 1) * HEAD_DIM)
            k, v = k_ref[:, hs], v_ref[:, hs]
            dk = jnp.zeros((ATTN_BLOCK, HEAD_DIM), F32)
            dv = jnp.zeros((ATTN_BLOCK, HEAD_DIM), F32)
            for mask, q_ref, do_ref, l_ref, d_ref in ((masks[0], qc_ref, doc_ref, lc_ref, dc_ref),
                                                      (masks[1], qn_ref, don_ref, lnx_ref, dn_ref)):
                q, dout = q_ref[:, hs], do_ref[:, hs]
                lrow = l_ref[:, h * HEAD_DIM:h * HEAD_DIM + 1]
                drow = d_ref[:, h * HEAD_DIM:h * HEAD_DIM + 1]
                sc = lax.dot_general(q, k, _NT, preferred_element_type=F32) * scale
                p = jnp.where(mask, jnp.exp(jnp.where(mask, sc, NEG) - lrow), 0.0)
                dp = lax.dot_general(dout, v, _NT, preferred_element_type=F32)
                ds = p * (dp - drow)
                dv = dv + lax.dot_general(p.astype(BF16), dout, _TN, preferred_element_type=F32)
                dk = dk + lax.dot_general(ds.astype(BF16), q, _TN, preferred_element_type=F32)
            dk_ref[:, hs] = dk * scale
            dv_ref[:, hs] = dv

    cur = pl.BlockSpec((ATTN_BLOCK, d), lambda r, n: (n, r))
    nxt = pl.BlockSpec((ATTN_BLOCK, d), lambda r, n: (jnp.minimum(n + 1, nb - 1), r))
    qcur = pl.BlockSpec((ATTN_BLOCK, d), lambda r, n: (n, r * N_GROUPS + g))
    qnxt = pl.BlockSpec((ATTN_BLOCK, d), lambda r, n: (jnp.minimum(n + 1, nb - 1), r * N_GROUPS + g))
    view = lambda t: t.reshape(ln, dil * d)
    qv = qr.reshape(ln, dil * N_GROUPS * d)
    dk, dv = _call(
        name, body, [view(kr), view(vb), qv, qv, view(do), view(do), view(lse), view(lse), view(dsum), view(dsum)],
        grid=(dil, nb), in_specs=[cur, cur, qcur, qnxt, cur, nxt, cur, nxt, cur, nxt],
        out_specs=[cur, cur], out_shape=[jax.ShapeDtypeStruct((ln, dil * d), F32)] * 2,
        sem=("parallel", "parallel"), carry=carry)
    return dk.reshape(s, d), dv.reshape(s, d)


def _adamw_tile(w, g, m, v):
    m = ADAM_B1 * m + (1.0 - ADAM_B1) * g
    v = ADAM_B2 * v + (1.0 - ADAM_B2) * (g * g)
    m_hat = m / (1.0 - ADAM_B1 ** ADAM_STEP)
    v_hat = v / (1.0 - ADAM_B2 ** ADAM_STEP)
    delta = -ADAM_LR * (m_hat / (jnp.sqrt(v_hat) + ADAM_EPS) + ADAM_WD * w)
    return delta, m, v


def _adamw_big(name, parts, w, m, v):
    layers, r, c = w.shape
    assert len(parts) == layers
    chunk_rows = min(ch.shape[1] for per_layer in parts for ch in per_layer)
    tr = 16
    while tr * 2 <= min(chunk_rows, (256 * 1024) // c) and chunk_rows % (tr * 2) == 0:
        tr *= 2
    pieces = [(ly, j, ch.shape[1] // tr, ch) for ly, per_layer in enumerate(parts) for j, ch in enumerate(per_layer)]

    def body(*refs):
        part_refs = refs[:len(pieces)]
        w_ref, m_ref, v_ref, g_out, d_out, m_out, v_out = refs[len(pieces):]
        layer, i = pl.program_id(0), pl.program_id(1)
        for (ly, j, tiles, _), part_ref in zip(pieces, part_refs):
            @pl.when(jnp.logical_and(layer == ly, i // tiles == j))
            def _(part_ref=part_ref):
                g = part_ref[0].astype(F32)
                for dev in range(1, N_DEV):
                    g = g + part_ref[dev].astype(F32)
                delta, mn, vn = _adamw_tile(w_ref[...], g, m_ref[...], v_ref[...])
                g_out[...] = g
                d_out[...] = delta
                m_out[...] = mn
                v_out[...] = vn

    def part_index(layer, i, ly, j, tiles):
        return (0, jnp.where(jnp.logical_and(layer == ly, i // tiles == j), i % tiles, 0), 0)

    own = pl.BlockSpec((None, tr, c), lambda ly, i: (ly, i, 0))
    part_specs = [pl.BlockSpec((N_DEV, tr, c), functools.partial(part_index, ly=ly, j=j, tiles=tiles)) for ly, j, tiles, _ in pieces]
    return _call(name, body, [ch for _, _, _, ch in pieces] + [w, m, v], grid=(layers, r // tr), in_specs=part_specs + [own] * 3,
                 out_specs=[own] * 4, out_shape=[jax.ShapeDtypeStruct(w.shape, F32)] * 4, sem=("parallel", "parallel"))


def _sum_slots(name, slots):
    _, r, c = slots.shape

    def body(s_ref, o_ref):
        g = s_ref[0]
        for j in range(1, N_DEV):
            g = g + s_ref[j]
        o_ref[...] = g

    return pl.pallas_call(body, name=name, out_shape=jax.ShapeDtypeStruct((r, c), F32),
                          compiler_params=_params())(slots)


def _adamw_small(name, w, g, m, v):
    def body(w_ref, g_ref, m_ref, v_ref, d_out, m_out, v_out):
        delta, mn, vn = _adamw_tile(w_ref[...], g_ref[...], m_ref[...], v_ref[...])
        d_out[...] = delta
        m_out[...] = mn
        v_out[...] = vn

    return pl.pallas_call(body, name=name, out_shape=[jax.ShapeDtypeStruct(w.shape, F32)] * 3,
                          compiler_params=_params())(w, g, m, v)


class Weights:
    def __init__(self, shards):
        self.shards, self.full, self.parts = shards, {}, {}

    @staticmethod
    def by_columns(key):
        return key.rstrip("01") in COLUMN_SHARDED

    def gather(self, *keys):
        return Exchange(gathers=[(self.shards[k], self.by_columns(k)) for k in keys], keys=keys)

    def landed(self, ex):
        for key, full in zip(ex.keys, ex.gathered):
            cols = self.by_columns(key)
            self.full[key] = W(full if cols else full.reshape(-1, full.shape[-1]), cols)

    def scatter(self, grads, gathers=(), chunk=(0, 1)):
        return Exchange(gathers=gathers, scatters=[(g, self.by_columns(k)) for k, g in grads.items()], keys=list(grads), chunk=chunk)

    def received(self, ex):
        for key, part in zip(ex.keys, ex.parts):
            self.parts.setdefault(key, {})[ex.chunk[0]] = part

    def chunks(self, key):
        return [self.parts[key][j] for j in sorted(self.parts[key])]

    def __getitem__(self, key):
        return self.full[key]


def _mlp_ple_fwd(tag, z1, p_i, ln1_g, ln1_b, ln2_g, ln2_b, wt, carries):
    h1, h1b = _layer_norm(f"ln1_{tag}", z1, ln1_g, ln1_b, (F32, BF16))
    up, act = _mm_nn(f"mlp_up_{tag}", h1b, wt["mlp_up" + tag], (F32, BF16),
                     epilogue=lambda acc: (acc, jnp.square(jnp.maximum(acc, 0.0))), carry=carries.get("mlp_up"))
    if "mlp_up" in carries:
        wt.landed(carries["mlp_up"])
    (z2,) = _mm_nn(f"mlp_down_{tag}", act, wt["mlp_down" + tag], (F32,), epilogue=lambda acc, h: (ALPHA * h + acc,),
                   extras=[(h1, "tile")], carry=carries.get("mlp_down"))
    if "mlp_down" in carries:
        wt.landed(carries["mlp_down"])
    h2, h2b = _layer_norm(f"ln2_{tag}", z2, ln2_g, ln2_b, (F32, BF16))
    (pe,) = _mm_nn(f"ple_proj_{tag}", p_i, wt["ple_proj" + tag], (F32,))

    def gate(acc, h, e):
        out = h + e * _sigmoid(acc)
        return acc, out, out

    gp, out, outb = _mm_nn(f"ple_gate_{tag}", h2b, wt["ple_gate" + tag], (F32, F32, BF16), epilogue=gate,
                           extras=[(h2, "tile"), (pe, "tile")], carry=carries.get("ple_gate"))
    if "ple_gate" in carries:
        wt.landed(carries["ple_gate"])
    saved = dict(z1=z1, h1b=h1b, up=up, act=act, z2=z2, h2b=h2b, pe=pe, gp=gp, p=p_i)
    return out, outb, saved


def _mlp_ple_bwd(tag, d_out, sv, ln1_g, ln2_g, wt):
    d = d_out.shape[1]

    def fn(i, o, r):
        dy, pe, gp = i[0][...], i[1][...], i[2][...]
        sg = _sigmoid(gp)
        o[0][...] = (dy * sg).astype(BF16)
        o[1][...] = (dy * pe * sg * (1.0 - sg)).astype(BF16)

    d_pe, d_gp = _rowwise(f"ple_bwd_{tag}", fn, [_rows(d_out), _rows(sv["pe"]), _rows(sv["gp"])], [(d, BF16), (d, BF16)])
    g_proj = _mm_tn(f"g_ple_proj_{tag}", sv["p"], d_pe, wt["ple_proj" + tag])
    g_gate = _mm_tn(f"g_ple_gate_{tag}", sv["h2b"], d_gp, wt["ple_gate" + tag])
    (d_h2,) = _mm_nt(f"d_ple_gate_{tag}", d_gp, wt["ple_gate" + tag], (F32,), epilogue=lambda acc, dy: (dy + acc,),
                     extras=[(d_out, "tile")])
    d_z2, d_z2b, g_ln2_g, g_ln2_b = _layer_norm_bwd(f"ln2_bwd_{tag}", sv["z2"], ln2_g, d_h2)
    ex = wt.scatter({"ple_proj" + tag: g_proj, "ple_gate" + tag: g_gate})
    g_down = _mm_tn(f"g_mlp_down_{tag}", sv["act"], d_z2b, wt["mlp_down" + tag], carry=ex)
    wt.received(ex)
    ex = wt.scatter({"mlp_down" + tag: g_down}, chunk=(0, 2))
    (d_up,) = _mm_nt(f"d_mlp_down_{tag}", d_z2b, wt["mlp_down" + tag], (BF16,),
                     epilogue=lambda acc, u: (acc * (2.0 * jnp.maximum(u, 0.0)),), extras=[(sv["up"], "tile")], carry=ex)
    wt.received(ex)
    ex = wt.scatter({"mlp_down" + tag: g_down}, chunk=(1, 2))
    g_up = _mm_tn(f"g_mlp_up_{tag}", sv["h1b"], d_up, wt["mlp_up" + tag], carry=ex)
    wt.received(ex)
    ex = wt.scatter({"mlp_up" + tag: g_up}, chunk=(0, 2))
    (d_h1,) = _mm_nt(f"d_mlp_up_{tag}", d_up, wt["mlp_up" + tag], (F32,), epilogue=lambda acc, dz: (ALPHA * dz + acc,),
                     extras=[(d_z2, "tile")], carry=ex)
    wt.received(ex)
    d_z1, d_z1b, g_ln1_g, g_ln1_b = _layer_norm_bwd(f"ln1_bwd_{tag}", sv["z1"], ln1_g, d_h1)
    return d_z1, d_z1b, dict(ln1_g=g_ln1_g, ln1_b=g_ln1_b, ln2_g=g_ln2_g, ln2_b=g_ln2_b), g_up


def _local_step(x, p, positions, target, wt, small):
    s, d = x.shape
    nh = d // HEAD_DIM

    ex = wt.gather("ple_proj0", "ple_gate0")
    (u,) = _mm_nn("conv_in", x, wt["conv_w_in"], (F32,), epilogue=lambda acc, b: (acc + b,), extras=[(small["conv_b_in"], "row")],
                  carry=ex)
    wt.landed(ex)

    def glu_fn(i, o, r):
        o[0][...] = i[0][...] * _sigmoid(i[1][...])

    (glu,) = _rowwise("glu", glu_fn, [_rows(u, 0, d), _rows(u, 1, d)], [(d, F32)])
    ex = wt.gather("mlp_up0")
    c = _conv_fwd("dwconv", glu, small["conv_dw"], small["conv_dw_b"], carry=ex)
    wt.landed(ex)

    def ln_silu(i, o, r):
        xhat, _ = _ln_stats(i[0][...])
        n = xhat * i[1][...] + i[2][...]
        o[0][...] = (n * _sigmoid(n)).astype(BF16)

    (sb,) = _rowwise("conv_ln_silu", ln_silu, [_rows(c), _full(small["conv_ln_g"]), _full(small["conv_ln_b"])], [(d, BF16)])
    ex = wt.gather("attn_w_o")
    (z1,) = _mm_nn("conv_out", sb, wt["conv_w_out"], (F32,), epilogue=lambda acc, xt: (ALPHA * xt + acc,), extras=[(x, "tile")],
                   carry=ex)
    wt.landed(ex)
    x1, x1b, sv0 = _mlp_ple_fwd("0", z1, p[0], small["ln1_g"][0:1], small["ln1_b"][0:1], small["ln2_g"][0:1], small["ln2_b"][0:1], wt,
                                dict(mlp_up=wt.gather("mlp_down0"), mlp_down=wt.gather("attn_w_q"), ple_gate=wt.gather("w_kv")))

    (kvn,) = _layer_norm("kv_ln", x1, small["kv_ln_g"], small["kv_ln_b"], (BF16,))
    (kv,) = _mm_nn("kv_proj", kvn, wt["w_kv"], (F32,))
    (q,) = _mm_nn("q_proj", x1b, wt["attn_w_q"], (F32,))
    cos, sin = _rope_tables(positions)

    def rot_kv(i, o, r):
        cs, sn = i[2][...], i[3][...]
        for h in range(nh):
            hs = slice(h * HEAD_DIM, (h + 1) * HEAD_DIM)
            o[0][:, hs] = _rot(i[0][:, hs], cs, sn).astype(BF16)
        o[1][...] = i[1][...].astype(BF16)

    kr, vb = _rowwise("rotary_kv", rot_kv, [_rows(kv, 0, d), _rows(kv, 1, d), _rows(cos), _rows(sin)], [(d, BF16), (d, BF16)])

    def rot_q(i, o, r):
        cs, sn = i[1][...], i[2][...]
        for h in range(N_GROUPS * nh):
            hs = slice(h * HEAD_DIM, (h + 1) * HEAD_DIM)
            o[0][:, hs] = _rot(i[0][:, hs], cs, sn).astype(BF16)

    (qr,) = _rowwise("rotary_q", rot_q, [_rows(q), _rows(cos), _rows(sin)], [(N_GROUPS * d, BF16)])

    riders = [wt.gather("mlp_up1"), wt.gather("mlp_down1"), wt.gather("ple_proj1", "ple_gate1")]
    og, lg = zip(*[_attn_fwd(f"attn_fwd_{g}", qr, kr, vb, g, dil, carry=riders[g]) for g, dil in enumerate(GROUP_DILATIONS)])
    for ex in riders:
        wt.landed(ex)

    def merge(i, o, r):
        ls = [i[N_GROUPS + g][...] for g in range(N_GROUPS)]
        top = functools.reduce(jnp.maximum, ls)
        es = [jnp.exp(l - top) for l in ls]
        den = functools.reduce(lambda a, b: a + b, es)
        out = functools.reduce(lambda a, b: a + b, [e * i[g][...] for g, e in enumerate(es)]) / den
        o[0][...] = out
        o[1][...] = out.astype(BF16)
        o[2][...] = top + jnp.log(den)

    o, ob, lse = _rowwise("attn_merge", merge, [_rows(t) for t in og + lg], [(d, F32), (d, BF16), (d, F32)], ts=128)
    (z1b,) = _mm_nn("attn_out", ob, wt["attn_w_o"], (F32,), epilogue=lambda acc, xt: (ALPHA * xt + acc,), extras=[(x1, "tile")])
    y, _, sv1 = _mlp_ple_fwd("1", z1b, p[1], small["ln1_g"][1:2], small["ln1_b"][1:2], small["ln2_g"][1:2], small["ln2_b"][1:2], wt, {})

    def loss_fn(i, o, r):
        diff = i[0][...] - i[1][...]
        o[0][...] = diff * (1.0 / d)
        r[0][...] += jnp.broadcast_to(jnp.sum(diff * diff), (1, LANE))

    d_y, sq = _rowwise("loss", loss_fn, [_rows(y), _rows(target)], [(d, F32)], [(1, LANE)])
    loss = 0.5 * sq[0, 0] / d

    d_z1, d_z1b, g1, g_up1 = _mlp_ple_bwd("1", d_y, sv1, small["ln1_g"][1:2], small["ln2_g"][1:2], wt)
    g_wo = _mm_tn("g_attn_out", ob, d_z1b, wt["attn_w_o"])
    d_o, d_ob = _mm_nt("d_attn_out", d_z1b, wt["attn_w_o"], (F32, BF16))

    def dsum_fn(i, o, r):
        for h in range(nh):
            hs = slice(h * HEAD_DIM, (h + 1) * HEAD_DIM)
            o[0][:, hs] = jnp.broadcast_to(jnp.sum(i[0][:, hs] * i[1][:, hs], axis=1, keepdims=True), (i[0].shape[0], HEAD_DIM))

    (dsum,) = _rowwise("attn_dsum", dsum_fn, [_rows(d_o), _rows(o)], [(d, F32)])
    dqs, dks, dvs = [], [], []
    riders = [wt.scatter({"attn_w_o": g_wo}), wt.scatter({"mlp_up1": g_up1}, chunk=(1, 2))]
    for g, dil in enumerate(GROUP_DILATIONS):
        dqs.append(_attn_dq(f"attn_dq_{g}", qr, kr, vb, d_ob, lse, dsum, g, dil, carry=riders[0] if g == 0 else None))
        dk, dv = _attn_dkv(f"attn_dkv_{g}", qr, kr, vb, d_ob, lse, dsum, g, dil, carry=riders[1] if g == 0 else None)
        dks.append(dk)
        dvs.append(dv)
    for ex in riders:
        wt.received(ex)

    def unrot(x_, cs, sn):
        return x_ * cs - pltpu.roll(x_, HEAD_DIM // 2, 1) * sn

    def unrot_q(i, o, r):
        cs, sn = i[N_GROUPS][...], i[N_GROUPS + 1][...]
        for g in range(N_GROUPS):
            for h in range(nh):
                hs = slice(h * HEAD_DIM, (h + 1) * HEAD_DIM)
                o[0][:, g * d + h * HEAD_DIM:g * d + (h + 1) * HEAD_DIM] = unrot(i[g][:, hs], cs, sn).astype(BF16)

    (d_q,) = _rowwise("rotary_q_bwd", unrot_q, [_rows(t) for t in dqs] + [_rows(cos), _rows(sin)], [(N_GROUPS * d, BF16)])

    def unrot_kv(i, o, r):
        cs, sn = i[2 * N_GROUPS][...], i[2 * N_GROUPS + 1][...]
        for h in range(nh):
            hs = slice(h * HEAD_DIM, (h + 1) * HEAD_DIM)
            dk = functools.reduce(lambda a, b: a + b, [i[g][:, hs] for g in range(N_GROUPS)])
            o[0][:, hs] = unrot(dk, cs, sn).astype(BF16)
            dv = functools.reduce(lambda a, b: a + b, [i[N_GROUPS + g][:, hs] for g in range(N_GROUPS)])
            o[0][:, d + h * HEAD_DIM:d + (h + 1) * HEAD_DIM] = dv.astype(BF16)

    (d_kv,) = _rowwise("rotary_kv_bwd", unrot_kv, [_rows(t) for t in dks + dvs] + [_rows(cos), _rows(sin)], [(2 * d, BF16)], ts=128)
    g_wq = _mm_tn("g_q_proj", x1b, d_q, wt["attn_w_q"])
    ex = wt.scatter({"attn_w_q": g_wq}, chunk=(0, 2))
    g_wkv = _mm_tn("g_kv_proj", kvn, d_kv, wt["w_kv"], carry=ex)
    wt.received(ex)
    ex = wt.scatter({"attn_w_q": g_wq}, chunk=(1, 2))
    (d_x1a,) = _mm_nt("d_q_proj", d_q, wt["attn_w_q"], (F32,), epilogue=lambda acc, dz: (ALPHA * dz + acc,), extras=[(d_z1, "tile")],
                      carry=ex)
    wt.received(ex)
    ex = wt.scatter({"w_kv": g_wkv})
    (d_kvn,) = _mm_nt("d_kv_proj", d_kv, wt["w_kv"], (F32,), carry=ex)
    wt.received(ex)
    d_x1, _, g_kv_ln_g, g_kv_ln_b = _layer_norm_bwd("kv_ln_bwd", x1, small["kv_ln_g"], d_kvn, extra=d_x1a)

    d_z1, d_z1b, g0, g_up0 = _mlp_ple_bwd("0", d_x1, sv0, small["ln1_g"][0:1], small["ln2_g"][0:1], wt)
    g_wout = _mm_tn("g_conv_out", sb, d_z1b, wt["conv_w_out"])
    (d_s,) = _mm_nt("d_conv_out", d_z1b, wt["conv_w_out"], (F32,))

    def ln_silu_bwd(i, o, r):
        cx, gn, bn, ds_ = i[0][...], i[1][...], i[2][...], i[3][...]
        xhat, _ = _ln_stats(cx)
        n = xhat * gn + bn
        sg = _sigmoid(n)
        dn = ds_ * (sg * (1.0 + n * (1.0 - sg)))
        dx, dg, db = _ln_bwd_tile(cx, gn, dn)
        o[0][...] = dx
        r[0][...] += dg
        r[1][...] += db

    d_c, g_cln_g, g_cln_b = _rowwise("conv_ln_silu_bwd", ln_silu_bwd,
                                     [_rows(c), _full(small["conv_ln_g"]), _full(small["conv_ln_b"]), _rows(d_s)],
                                     [(d, F32)], [(1, d), (1, d)])
    ex = wt.scatter({"mlp_up0": g_up0}, chunk=(1, 2))
    d_glu, g_dw, g_dwb = _conv_bwd("dwconv_bwd", glu, d_c, small["conv_dw"], carry=ex)
    wt.received(ex)

    def glu_bwd(i, o, r):
        a, gt, dg_ = i[0][...], i[1][...], i[2][...]
        sg = _sigmoid(gt)
        da = dg_ * sg
        dgate = dg_ * a * sg * (1.0 - sg)
        o[0][:, 0:d] = da.astype(BF16)
        o[0][:, d:2 * d] = dgate.astype(BF16)
        r[0][:, 0:d] += jnp.sum(da, axis=0, keepdims=True)
        r[0][:, d:2 * d] += jnp.sum(dgate, axis=0, keepdims=True)

    d_u, g_bin = _rowwise("glu_bwd", glu_bwd, [_rows(u, 0, d), _rows(u, 1, d), _rows(d_glu)], [(2 * d, BF16)], [(1, 2 * d)])
    ex = wt.scatter({"conv_w_out": g_wout})
    g_win = _mm_tn("g_conv_in", x, d_u, wt["conv_w_in"], carry=ex)
    wt.received(ex)
    rows = [g_bin.reshape(2, d), g_dw, g_dwb, g_cln_g, g_cln_b, g_kv_ln_g, g_kv_ln_b]
    rows += [jnp.concatenate([g0[n], g1[n]], axis=0) for n in ("ln1_g", "ln1_b", "ln2_g", "ln2_b")]
    rows, offsets = _stack_rows(rows)
    ex = wt.scatter({"conv_w_in": g_win}, gathers=[(rows, False)])
    (grad_x,) = _mm_nt("d_conv_in", d_u, wt["conv_w_in"], (F32,), epilogue=lambda acc, dz: (ALPHA * dz + acc,), extras=[(d_z1, "tile")],
                       carry=ex)
    wt.received(ex)
    return loss, grad_x, ex.gathered[0], offsets


BIG = ("conv_w_in", "conv_w_out", "w_kv", "attn_w_q", "attn_w_o", "mlp_up", "mlp_down", "ple_proj", "ple_gate")
COLUMN_SHARDED = ("conv_w_in", "w_kv", "attn_w_q", "mlp_up", "ple_proj")
WEIGHTS = ("conv_w_in", "conv_b_in", "conv_dw", "conv_dw_b", "conv_ln_g", "conv_ln_b", "conv_w_out", "kv_ln_g", "kv_ln_b",
           "w_kv", "attn_w_q", "attn_w_o", "ln1_g", "ln1_b", "mlp_up", "mlp_down", "ln2_g", "ln2_b", "ple_proj", "ple_gate")


def kernel(x, p, positions, conv_w_in, conv_b_in, conv_dw, conv_dw_b, conv_ln_g, conv_ln_b, conv_w_out, kv_ln_g, kv_ln_b, w_kv, attn_w_q, attn_w_o, ln1_g, ln1_b, mlp_up, mlp_down, ln2_g, ln2_b, ple_proj, ple_gate, loss_target, m_conv_w_in, m_conv_b_in, m_conv_dw, m_conv_dw_b, m_conv_ln_g, m_conv_ln_b, m_conv_w_out, m_kv_ln_g, m_kv_ln_b, m_w_kv, m_attn_w_q, m_attn_w_o, m_ln1_g, m_ln1_b, m_mlp_up, m_mlp_down, m_ln2_g, m_ln2_b, m_ple_proj, m_ple_gate, v_conv_w_in, v_conv_b_in, v_conv_dw, v_conv_dw_b, v_conv_ln_g, v_conv_ln_b, v_conv_w_out, v_kv_ln_g, v_kv_ln_b, v_w_kv, v_attn_w_q, v_attn_w_o, v_ln1_g, v_ln1_b, v_mlp_up, v_mlp_down, v_ln2_g, v_ln2_b, v_ple_proj, v_ple_gate):
    given = dict(locals())
    wts = {n: given[n] for n in WEIGHTS}
    moms = {n: given["m_" + n] for n in WEIGHTS}
    vels = {n: given["v_" + n] for n in WEIGHTS}
    s, d = x.shape[1], x.shape[2]
    shard = d // N_DEV
    me = 4 * lax.axis_index("x") + 2 * lax.axis_index("y") + lax.axis_index("c")

    def layers_of(a):
        return a.reshape((-1,) + a.shape[-2:])

    shards = {}
    for n in BIG:
        w3 = layers_of(wts[n])
        for ly in range(w3.shape[0]):
            shards[n + str(ly) if w3.shape[0] > 1 else n] = w3[ly].astype(BF16)
    wt = Weights(shards)
    pack, at = _stack_rows([wts["conv_b_in"].reshape(2, shard), wts["conv_dw"].reshape(CONV_WIDTH, shard),
                            wts["conv_dw_b"], wts["conv_ln_g"], wts["conv_ln_b"]])
    ex = Exchange(gathers=[(shards["conv_w_in"], True), (shards["conv_w_out"], False), (pack, False)], keys=["conv_w_in", "conv_w_out"])
    _exchange_alone("gather_first", ex)
    wt.landed(ex)
    packed = ex.gathered[2]
    small = dict(conv_b_in=packed[:, at[0]:at[0] + 2].reshape(1, 2 * d), conv_dw=packed[:, at[1]:at[1] + CONV_WIDTH],
                 conv_dw_b=packed[:, at[2]].reshape(1, d), conv_ln_g=packed[:, at[3]].reshape(1, d),
                 conv_ln_b=packed[:, at[4]].reshape(1, d), kv_ln_g=kv_ln_g.reshape(1, d), kv_ln_b=kv_ln_b.reshape(1, d),
                 ln1_g=ln1_g, ln1_b=ln1_b, ln2_g=ln2_g, ln2_b=ln2_b)

    loss, grad_x, all_rows, at = _local_step(x[0], p[:, 0], positions.reshape(s, 1), loss_target[0], wt, small)
    loss = lax.psum(loss, ("x", "y", "c"))

    out = {}
    for n in BIG:
        w3 = layers_of(wts[n])
        keys = [n + str(ly) if w3.shape[0] > 1 else n for ly in range(w3.shape[0])]
        res = _adamw_big("adamw_" + n, [wt.chunks(k) for k in keys], w3, layers_of(moms[n]), layers_of(vels[n]))
        out[n] = [r.reshape(wts[n].shape) for r in res]
    tot = _sum_slots("sum_small_grads", all_rows)
    mine = lax.dynamic_slice_in_dim(tot, me * shard, shard, axis=1)
    b_in = lax.dynamic_slice_in_dim(tot[at[0]:at[0] + 2].reshape(1, 2 * d), me * 2 * shard, 2 * shard, axis=1)
    g_small = dict(conv_b_in=b_in, conv_dw=mine[at[1]:at[1] + CONV_WIDTH].reshape(conv_dw.shape), conv_dw_b=mine[at[2]:at[2] + 1],
                   conv_ln_g=mine[at[3]:at[3] + 1], conv_ln_b=mine[at[4]:at[4] + 1], kv_ln_g=tot[at[5]], kv_ln_b=tot[at[6]])
    for j, n in enumerate(("ln1_g", "ln1_b", "ln2_g", "ln2_b")):
        g_small[n] = tot[at[7 + j]:at[7 + j] + DEPTH]
    order = [n for n in WEIGHTS if n not in BIG]

    def flat(t):
        return _stack_rows([t[n].reshape(-1, shard) for n in order])

    (w_s, at), (g_s, _), (m_s, _), (v_s, _) = flat(wts), flat(g_small), flat(moms), flat(vels)
    d_s, m_s, v_s = _adamw_small("adamw_small", w_s, g_s, m_s, v_s)
    for n, a in zip(order, at):
        nrow = wts[n].size // shard
        out[n] = [g_small[n].reshape(wts[n].shape)] + [t[a:a + nrow].reshape(wts[n].shape) for t in (d_s, m_s, v_s)]
    return (loss, grad_x[None], *[out[n][0] for n in WEIGHTS], *[out[n][1] for n in WEIGHTS],
            *[out[n][2] for n in WEIGHTS], *[out[n][3] for n in WEIGHTS])
```

```python
import functools

import numpy as np
import jax
import jax.numpy as jnp
from jax import lax
from jax.experimental import pallas as pl
from jax.experimental.pallas import tpu as pltpu

F32, BF16 = jnp.float32, jnp.bfloat16

N_DEV = 8
HEAD_DIM = 128
ATTN_BLOCK = 128
GROUP_DILATIONS = (1, 4, 16)
N_GROUPS = len(GROUP_DILATIONS)
CONV_WIDTH = 31
CONV_HALO = 32
CONV_ROWS = 64
ROPE_THETA = 10000.0
LN_EPS = 1e-5
DEPTH = 2
ALPHA = (2 * DEPTH) ** 0.25
ADAM_LR, ADAM_B1, ADAM_B2, ADAM_EPS, ADAM_WD, ADAM_STEP = 0.001, 0.9, 0.999, 1e-08, 0.01, 10
NEG = -1e30
V7X_VMEM_LIMIT = 56 * 2 ** 20
LANE = 128
SUBLANES = 8
ROW_TILE = 256
GRAD_DTYPE = BF16

MESH = pl.DeviceIdType.MESH
ANY = pl.BlockSpec(memory_space=pl.ANY)


def _params(*sem):
    return pltpu.CompilerParams(dimension_semantics=sem or None, vmem_limit_bytes=V7X_VMEM_LIMIT)


def _sigmoid(x):
    return 1.0 / (1.0 + jnp.exp(-x))


def _divisor(n, most):
    best = None
    for t in range(LANE, min(n, most) + 1, LANE):
        if n % t == 0:
            best = t
    assert best is not None, (n, most)
    return best


def _stack_rows(parts):
    out, offsets, at = [], [], 0
    for a in parts:
        pad = -a.shape[0] % SUBLANES
        offsets.append(at)
        out.append(a)
        if pad:
            out.append(jnp.zeros((pad, a.shape[1]), a.dtype))
        at += a.shape[0] + pad
    return jnp.concatenate(out, axis=0), offsets


class Exchange:
    OTHER_CHIPS = (4, 2, 6)

    def __init__(self, gathers=(), scatters=(), keys=(), chunk=(0, 1)):
        self.gathers, self.g_cols = [a for a, _ in gathers], [c for _, c in gathers]
        self.scatters, self.s_cols = [a for a, _ in scatters], [c for _, c in scatters]
        self.keys, self.chunk = list(keys), chunk
        self.n_g, self.n_s = len(self.gathers), len(self.scatters)
        self.n = self.n_g + self.n_s
        self.operands = self.gathers + self.scatters
        self.gathered = self.parts = None

    def rows(self, t):
        a = self.scatters[t]
        per = (a.shape[0] if self.s_cols[t] else a.shape[1]) // self.chunk[1]
        return self.chunk[0] * per, per

    def out_shape(self):
        outs = []
        for a, cols in zip(self.gathers, self.g_cols):
            outs.append(jax.ShapeDtypeStruct((a.shape[0], N_DEV * a.shape[1]) if cols else (N_DEV,) + a.shape, a.dtype))
        for t, (a, cols) in enumerate(zip(self.scatters, self.s_cols)):
            outs.append(jax.ShapeDtypeStruct((N_DEV, self.rows(t)[1], a.shape[1] // N_DEV if cols else a.shape[2]), a.dtype))
        return outs

    def scratch(self):
        dma = pltpu.SemaphoreType.DMA
        return [dma((max(self.n_g, 1) * 7,)), dma((max(self.n_g, 1) * 7,)), dma((max(self.n_s, 1) * 7,)),
                dma((max(self.n_s, 1) * 7,)), dma((self.n,))]

    def take(self, results):
        self.gathered, self.parts = list(results[:self.n_g]), list(results[self.n_g:])

    def _copies(self, ins, outs, sems):
        n_g, n_s = self.n_g, self.n_s
        g_in, s_in, g_out, s_out = ins[:n_g], ins[n_g:], outs[:n_g], outs[n_g:]
        g_send, g_recv, s_send, s_recv, local_sem = sems
        x, y, c = lax.axis_index("x"), lax.axis_index("y"), lax.axis_index("c")

        def peer(k):
            return (1 - x if k & 4 else x, 1 - y if k & 2 else y, 1 - c if k & 1 else c)

        def number(p):
            return 4 * p[0] + 2 * p[1] + p[2]

        me = number((x, y, c))

        def slot(t, j):
            first, count = self.rows(t)
            if self.s_cols[t]:
                width = self.scatters[t].shape[1] // N_DEV
                return s_in[t].at[pl.ds(first, count), pl.ds(pl.multiple_of(j * width, LANE), width)]
            return s_in[t].at[j, pl.ds(first, count)]

        def place(t, j):
            if self.g_cols[t]:
                width = self.gathers[t].shape[1]
                return g_out[t].at[:, pl.ds(pl.multiple_of(j * width, LANE), width)]
            return g_out[t].at[j]

        def local():
            cps = [pltpu.make_async_copy(g_in[t], place(t, me), local_sem.at[t]) for t in range(n_g)]
            return cps + [pltpu.make_async_copy(slot(t, me), s_out[t].at[me], local_sem.at[n_g + t]) for t in range(n_s)]

        def scatter(t, k):
            p = peer(k)
            return pltpu.make_async_remote_copy(
                src_ref=slot(t, number(p)), dst_ref=s_out[t].at[me], send_sem=s_send.at[t * 7 + k - 1],
                recv_sem=s_recv.at[t * 7 + k - 1], device_id=p, device_id_type=MESH)

        def landed(t, k):
            p = peer(k)
            return pltpu.make_async_remote_copy(
                src_ref=slot(t, me), dst_ref=s_out[t].at[number(p)], send_sem=s_send.at[t * 7 + k - 1],
                recv_sem=s_recv.at[t * 7 + k - 1], device_id=p, device_id_type=MESH)

        def gather(t, pair, block, to, src=None):
            slot = place(t, number(block))
            return pltpu.make_async_remote_copy(
                src_ref=slot if src is None else src, dst_ref=slot, send_sem=g_send.at[t * 7 + pair],
                recv_sem=g_recv.at[t * 7 + pair], device_id=to, device_id_type=MESH)

        def first_sends():
            cps = []
            for t in range(n_g):
                cps.append(gather(t, 0, peer(0), peer(1), src=g_in[t]))
                cps += [gather(t, 1 + j, peer(0), peer(k), src=g_in[t]) for j, k in enumerate(self.OTHER_CHIPS)]
            for t in range(n_s):
                cps += [scatter(t, k) for k in range(1, N_DEV)]
            return cps

        return peer, local, landed, gather, first_sends

    def start(self, ins, outs, sems):
        _, local, _, _, first_sends = self._copies(ins, outs, sems)
        for cp in local() + first_sends():
            cp.start()

    def finish(self, ins, outs, sems):
        peer, local, landed, gather, first_sends = self._copies(ins, outs, sems)
        mine, sibling = peer(0), peer(1)
        passed = []
        for j, k in enumerate(self.OTHER_CHIPS):
            for t in range(self.n_g):
                gather(t, 1 + j, peer(k), mine).wait_recv()
                passed.append(gather(t, 4 + j, peer(k), sibling))
                passed[-1].start()
        for t in range(self.n_g):
            gather(t, 0, sibling, mine).wait_recv()
            for j, k in enumerate(self.OTHER_CHIPS):
                gather(t, 4 + j, peer(k ^ 1), mine).wait_recv()
        for t in range(self.n_s):
            for k in range(1, N_DEV):
                landed(t, k).wait_recv()
        for cp in first_sends() + passed:
            cp.wait_send()
        for cp in local():
            cp.wait()


def _exchange_alone(name, ex):
    def body(*refs):
        ins, outs, sems = refs[:ex.n], refs[ex.n:2 * ex.n], refs[2 * ex.n:]
        ex.start(ins, outs, sems)
        ex.finish(ins, outs, sems)

    ex.take(pl.pallas_call(body, name=name, in_specs=[ANY] * ex.n, out_specs=[ANY] * ex.n, out_shape=ex.out_shape(),
                           scratch_shapes=ex.scratch())(*ex.operands))


def _call(name, body, args, *, grid, in_specs, out_specs, out_shape, scratch_shapes=(), sem=(), carry=None):
    if carry is None:
        return pl.pallas_call(body, name=name, grid=grid, in_specs=in_specs, out_specs=out_specs, out_shape=out_shape,
                              scratch_shapes=list(scratch_shapes), compiler_params=_params(*sem))(*args)
    ex = carry
    n_in, n_out, n_scr = len(args), len(out_shape), len(scratch_shapes)

    def carried(*refs):
        ins, cin = refs[:n_in], refs[n_in:n_in + ex.n]
        at = n_in + ex.n
        outs, cout = refs[at:at + n_out], refs[at + n_out:at + n_out + ex.n]
        at += n_out + ex.n
        scr, sems = refs[at:at + n_scr], refs[at + n_scr:]
        ids = [pl.program_id(a) for a in range(len(grid))]
        first = functools.reduce(jnp.logical_and, [i == 0 for i in ids])
        last = functools.reduce(jnp.logical_and, [i == g - 1 for i, g in zip(ids, grid)])

        @pl.when(first)
        def _():
            ex.start(cin, cout, sems)

        body(*ins, *outs, *scr)

        @pl.when(last)
        def _():
            ex.finish(cin, cout, sems)

    res = pl.pallas_call(
        carried, name=name, grid=grid, in_specs=list(in_specs) + [ANY] * ex.n, out_specs=list(out_specs) + [ANY] * ex.n,
        out_shape=list(out_shape) + ex.out_shape(), scratch_shapes=list(scratch_shapes) + ex.scratch(),
        compiler_params=_params(*("arbitrary",) * len(grid)),
    )(*args, *ex.operands)
    ex.take(res[n_out:])
    return res[:n_out]


class W:
    def __init__(self, arr, cols):
        self.arr, self.cols = arr, cols
        self.k, self.n = arr.shape
        self.shard_cols = self.n // N_DEV if cols else self.n


def _matmul(name, grid, operands, specs, dims, tile, extras, outs, out_specs, epilogue, carry=None):
    assert grid[2] == 1
    n_ex, n_out = len(extras), len(outs)

    def body(*refs):
        a_ref, b_ref = refs[0], refs[1]
        ex_refs = refs[2:2 + n_ex]
        out_refs = refs[2 + n_ex:2 + n_ex + n_out]
        acc = lax.dot_general(a_ref[...].astype(BF16), b_ref[...].astype(BF16), (dims, ((), ())),
                              preferred_element_type=F32)
        res = epilogue(acc, *[r[...] for r in ex_refs]) if epilogue else (acc,) * n_out
        for r, v in zip(out_refs, res):
            r[...] = v.astype(r.dtype)

    return _call(name, body, list(operands) + [a for a, _ in extras], grid=grid,
                 in_specs=list(specs) + [s for _, s in extras], out_specs=out_specs, out_shape=outs,
                 sem=("parallel", "parallel", "arbitrary"), carry=carry)


def _extra_specs(extras, tm, tn):
    out = []
    for arr, kind in extras:
        if kind == "tile":
            out.append((arr, pl.BlockSpec((tm, tn), lambda i, j, c: (i, j))))
        else:
            out.append((arr, pl.BlockSpec((1, tn), lambda i, j, c: (0, j))))
    return out


def _mm_nn(name, a, w, out_dtypes, epilogue=None, extras=(), carry=None):
    m, k = a.shape
    assert k == w.k
    tm = 1024 if a.dtype == BF16 else 512
    tn = _divisor(w.n, 512 if k <= 2048 else 256)
    assert tm * k * a.dtype.itemsize <= 16 * 2 ** 20, (name, tm, k)
    grid = (m // tm, w.n // tn, 1)
    specs = [pl.BlockSpec((tm, k), lambda i, j, c: (i, 0)), pl.BlockSpec((k, tn), lambda i, j, c: (0, j))]
    outs = [jax.ShapeDtypeStruct((m, w.n), d) for d in out_dtypes]
    out_specs = [pl.BlockSpec((tm, tn), lambda i, j, c: (i, j)) for _ in outs]
    return _matmul(name, grid, (a, w.arr), specs, ((1,), (0,)), (tm, tn), _extra_specs(extras, tm, tn), outs, out_specs,
                   epilogue, carry)


def _mm_nt(name, dy, w, out_dtypes, epilogue=None, extras=(), carry=None):
    m, n = dy.shape
    assert n == w.n and dy.dtype == BF16
    tm = 1024
    to = _divisor(w.k, 512 if n <= 2048 else 256)
    assert tm * n * dy.dtype.itemsize <= 16 * 2 ** 20, (name, tm, n)
    grid = (m // tm, w.k // to, 1)
    specs = [pl.BlockSpec((tm, n), lambda i, j, c: (i, 0)), pl.BlockSpec((to, n), lambda i, j, c: (j, 0))]
    outs = [jax.ShapeDtypeStruct((m, w.k), d) for d in out_dtypes]
    out_specs = [pl.BlockSpec((tm, to), lambda i, j, c: (i, j)) for _ in outs]
    return _matmul(name, grid, (dy, w.arr), specs, ((1,), (1,)), (tm, to), _extra_specs(extras, tm, to), outs, out_specs,
                   epilogue, carry)


def _mm_tn(name, a, dy, like, carry=None):
    m, k = a.shape
    n = dy.shape[1]
    assert (k, n) == (like.k, like.n)
    tk = _divisor(k, 1024 if a.dtype == BF16 else 512)
    tn = _divisor(n, 1024)
    grid = (k // tk, n // tn, 1)
    specs = [pl.BlockSpec((m, tk), lambda i, j, c: (0, i)), pl.BlockSpec((m, tn), lambda i, j, c: (0, j))]
    out_specs = [pl.BlockSpec((tk, tn), lambda i, j, c: (i, j))]
    (g,) = _matmul(name, grid, (a, dy), specs, ((0,), (0,)), (tk, tn), [], [jax.ShapeDtypeStruct((k, n), GRAD_DTYPE)],
                   out_specs, None, carry)
    return g if like.cols else g.reshape(N_DEV, k // N_DEV, n)


def _rows(arr, blk=0, width=None):
    return ("rows", arr, blk, width or arr.shape[1])


def _full(arr):
    return ("full", arr)


def _by_residue(arr):
    return ("residue", arr)


def _rowwise(name, fn, ins, outs, reds=(), ts=256, carry=None, scratch=()):
    s = next(i[1].shape[0] if i[0] == "rows" else i[1].shape[0] * i[1].shape[1] for i in ins if i[0] != "full")
    n_in, n_out, n_red = len(ins), len(outs), len(reds)
    in_specs = []
    for i in ins:
        if i[0] == "rows":
            in_specs.append(pl.BlockSpec((ts, i[3]), functools.partial(lambda t, blk: (t, blk), blk=i[2])))
        elif i[0] == "residue":
            d, _, w = i[1].shape
            in_specs.append(pl.BlockSpec((d, ts // d, w), lambda t: (0, t, 0)))
        else:
            in_specs.append(pl.BlockSpec(i[1].shape, functools.partial(lambda t, nd: (0,) * nd, nd=i[1].ndim)))
    out_shape, out_specs = [], []
    for o in outs:
        if len(o) == 2:
            out_shape.append(jax.ShapeDtypeStruct((s, o[0]), o[1]))
            out_specs.append(pl.BlockSpec((ts, o[0]), lambda t: (t, 0)))
        else:
            out_shape.append(jax.ShapeDtypeStruct((o[2], s // o[2], o[0]), o[1]))
            out_specs.append(pl.BlockSpec((o[2], ts // o[2], o[0]), lambda t: (0, t, 0)))
    out_shape += [jax.ShapeDtypeStruct(r, F32) for r in reds]
    out_specs += [pl.BlockSpec(r, lambda t: (0, 0)) for r in reds]

    def body(*refs):
        red_refs = refs[n_in + n_out:n_in + n_out + n_red]
        if red_refs:
            @pl.when(pl.program_id(0) == 0)
            def _():
                for r in red_refs:
                    r[...] = jnp.zeros(r.shape, F32)
        fn(refs[:n_in], refs[n_in:n_in + n_out], red_refs, *refs[n_in + n_out + n_red:])

    return _call(name, body, [i[1] for i in ins], grid=(s // ts,), in_specs=in_specs, out_specs=out_specs,
                 out_shape=out_shape, scratch_shapes=list(scratch), sem=("arbitrary" if reds else "parallel",), carry=carry)


def _ln_stats(x):
    mu = jnp.mean(x, axis=-1, keepdims=True)
    xc = x - mu
    var = jnp.mean(xc * xc, axis=-1, keepdims=True)
    return xc * lax.rsqrt(var + LN_EPS), lax.rsqrt(var + LN_EPS)


def _layer_norm(name, x, g, b, out_dtypes):
    def fn(i, o, r):
        xhat, _ = _ln_stats(i[0][...])
        y = xhat * i[1][...] + i[2][...]
        for ref in o:
            ref[...] = y.astype(ref.dtype)

    return _rowwise(name, fn, [_rows(x), _full(g), _full(b)], [(x.shape[1], d) for d in out_dtypes])


def _ln_bwd_tile(x, g, dy):
    xhat, rstd = _ln_stats(x)
    dyg = dy * g
    m1 = jnp.mean(dyg, axis=-1, keepdims=True)
    m2 = jnp.mean(dyg * xhat, axis=-1, keepdims=True)
    dx = rstd * (dyg - m1 - xhat * m2)
    return dx, jnp.sum(dy * xhat, axis=0, keepdims=True), jnp.sum(dy, axis=0, keepdims=True)


def _layer_norm_bwd(name, x, g, dy, extra=None):
    d = x.shape[1]

    def fn(i, o, r):
        dx, dg, db = _ln_bwd_tile(i[0][...], i[1][...], i[2][...])
        if extra is not None:
            dx = dx + i[3][...]
        o[0][...] = dx
        o[1][...] = dx.astype(BF16)
        r[0][...] += dg
        r[1][...] += db

    ins = [_rows(x), _full(g), _rows(dy)] + ([_rows(extra)] if extra is not None else [])
    return _rowwise(name, fn, ins, [(d, F32), (d, BF16)], [(1, d), (1, d)])


def _conv_fwd(name, glu, dw, dw_b, ts=512, carry=None):
    s, c = glu.shape
    tc = dw.shape[2]
    per = ts // CONV_HALO
    back = CONV_HALO - (CONV_WIDTH - 1)

    def body(cur_ref, prev_ref, w_ref, b_ref, out_ref, buf):
        i = pl.program_id(1)
        buf[pl.ds(0, CONV_HALO), :] = jnp.where(i > 0, prev_ref[...], 0.0)
        buf[pl.ds(CONV_HALO, ts), :] = cur_ref[...]
        for r0 in range(0, ts, CONV_ROWS):
            acc = jnp.broadcast_to(b_ref[...], (CONV_ROWS, tc))
            for j in range(CONV_WIDTH):
                acc = acc + w_ref[j:j + 1, :] * buf[pl.ds(r0 + back + j, CONV_ROWS), :]
            out_ref[pl.ds(r0, CONV_ROWS), :] = acc

    (out,) = _call(
        name, body, [glu, glu, dw, dw_b], grid=(c // tc, s // ts),
        in_specs=[pl.BlockSpec((ts, tc), lambda j, i: (i, j)),
                  pl.BlockSpec((CONV_HALO, tc), lambda j, i: (jnp.maximum(i * per - 1, 0), j)),
                  pl.BlockSpec((None, CONV_WIDTH, tc), lambda j, i: (j, 0, 0)),
                  pl.BlockSpec((1, tc), lambda j, i: (0, j))],
        out_specs=[pl.BlockSpec((ts, tc), lambda j, i: (i, j))],
        out_shape=[jax.ShapeDtypeStruct((s, c), F32)],
        scratch_shapes=[pltpu.VMEM((ts + CONV_HALO, tc), F32)],
        sem=("parallel", "parallel"), carry=carry)
    return out


def _conv_bwd(name, glu, dc, dw, ts=512, carry=None):
    s, c = glu.shape
    tc = dw.shape[2]
    per = ts // CONV_HALO
    back = CONV_HALO - (CONV_WIDTH - 1)
    last = s // ts - 1

    def body(g_ref, gprev_ref, dc_ref, dcnext_ref, w_ref, dglu_ref, ddw_ref, ddb_ref, gbuf, dbuf):
        i = pl.program_id(1)

        @pl.when(i == 0)
        def _():
            ddw_ref[...] = jnp.zeros(ddw_ref.shape, F32)
            ddb_ref[...] = jnp.zeros(ddb_ref.shape, F32)

        gbuf[pl.ds(0, CONV_HALO), :] = jnp.where(i > 0, gprev_ref[...], 0.0)
        gbuf[pl.ds(CONV_HALO, ts), :] = g_ref[...]
        dbuf[pl.ds(0, ts), :] = dc_ref[...]
        dbuf[pl.ds(ts, CONV_HALO), :] = jnp.where(i < last, dcnext_ref[...], 0.0)
        taps = [jnp.zeros((1, tc), F32)] * CONV_WIDTH
        for r0 in range(0, ts, CONV_ROWS):
            d_here = dbuf[pl.ds(r0, CONV_ROWS), :]
            acc = jnp.zeros((CONV_ROWS, tc), F32)
            for j in range(CONV_WIDTH):
                acc = acc + w_ref[j:j + 1, :] * dbuf[pl.ds(r0 + (CONV_WIDTH - 1) - j, CONV_ROWS), :]
                taps[j] = taps[j] + jnp.sum(d_here * gbuf[pl.ds(r0 + back + j, CONV_ROWS), :], axis=0, keepdims=True)
            dglu_ref[pl.ds(r0, CONV_ROWS), :] = acc
        for j in range(CONV_WIDTH):
            ddw_ref[j:j + 1, :] += taps[j]
        ddb_ref[...] += jnp.sum(dc_ref[...], axis=0, keepdims=True)

    return _call(
        name, body, [glu, glu, dc, dc, dw], grid=(c // tc, s // ts),
        in_specs=[pl.BlockSpec((ts, tc), lambda j, i: (i, j)),
                  pl.BlockSpec((CONV_HALO, tc), lambda j, i: (jnp.maximum(i * per - 1, 0), j)),
                  pl.BlockSpec((ts, tc), lambda j, i: (i, j)),
                  pl.BlockSpec((CONV_HALO, tc), lambda j, i: (jnp.minimum((i + 1) * per, (last + 1) * per - 1), j)),
                  pl.BlockSpec((None, CONV_WIDTH, tc), lambda j, i: (j, 0, 0))],
        out_specs=[pl.BlockSpec((ts, tc), lambda j, i: (i, j)),
                   pl.BlockSpec((CONV_WIDTH, tc), lambda j, i: (0, j)),
                   pl.BlockSpec((1, tc), lambda j, i: (0, j))],
        out_shape=[jax.ShapeDtypeStruct((s, c), F32), jax.ShapeDtypeStruct((CONV_WIDTH, c), F32),
                   jax.ShapeDtypeStruct((1, c), F32)],
        scratch_shapes=[pltpu.VMEM((ts + CONV_HALO, tc), F32), pltpu.VMEM((ts + CONV_HALO, tc), F32)],
        sem=("parallel", "arbitrary"), carry=carry)


def _rope_tables(positions):
    half = HEAD_DIM // 2
    inv = (np.float32(ROPE_THETA) ** (-np.arange(half, dtype=np.float32) * np.float32(2.0 / HEAD_DIM))).astype(np.float32)
    inv_freq = jnp.asarray(np.concatenate([inv, inv])[None, :])
    sign = jnp.asarray(np.concatenate([-np.ones(half, np.float32), np.ones(half, np.float32)])[None, :])

    def fn(i, o, r):
        ang = i[0][...].astype(F32) * i[1][...]
        o[0][...] = jnp.cos(ang)
        o[1][...] = jnp.sin(ang) * i[2][...]

    return _rowwise("rope_tables", fn, [_rows(positions), _full(inv_freq), _full(sign)], [(HEAD_DIM, F32), (HEAD_DIM, F32)], ts=512)


def _rot(x, cos, sin):
    return x * cos + pltpu.roll(x, HEAD_DIM // 2, 1) * sin


def _unrot(x, cos, sin):
    return x * cos - pltpu.roll(x, HEAD_DIM // 2, 1) * sin


def _split_rows(scr, value, d):
    if d == 1:
        return [value]
    scr[...] = value
    return [scr[pl.ds(r, scr.shape[0] // d, stride=d), :] for r in range(d)]


def _join_rows(scr, planes):
    d = len(planes)
    if d == 1:
        return planes[0]
    for r, plane in enumerate(planes):
        scr[pl.ds(r, scr.shape[0] // d, stride=d), :] = plane
    return scr[...]


def _lane(h, shape):
    return lax.broadcasted_iota(jnp.int32, shape, 1) == h


def _attn_specs(width):
    cur = pl.BlockSpec((None, ATTN_BLOCK, width), lambda r, n: (r, n, 0))
    prev = pl.BlockSpec((None, ATTN_BLOCK, width), lambda r, n: (r, jnp.maximum(n - 1, 0), 0))
    return cur, prev


def _masks(n):
    row = lax.broadcasted_iota(jnp.int32, (ATTN_BLOCK, ATTN_BLOCK), 0)
    col = lax.broadcasted_iota(jnp.int32, (ATTN_BLOCK, ATTN_BLOCK), 1)
    return col <= row, jnp.logical_and(col >= row, n > 0)


_NT = (((1,), (1,)), ((), ()))
_TN = (((0,), (0,)), ((), ()))
_NN = (((1,), (0,)), ((), ()))


def _attn_fwd(name, q, k, v, carry=None):
    dil, ln, d = k.shape
    nh = d // HEAD_DIM
    nb = ln // ATTN_BLOCK
    scale = HEAD_DIM ** -0.5

    def body(q_ref, kc_ref, kp_ref, vc_ref, vp_ref, o_ref, l_ref):
        mask_c, mask_p = _masks(pl.program_id(1))
        stats = jnp.zeros((ATTN_BLOCK, LANE), F32)
        for h in range(nh):
            hs = slice(h * HEAD_DIM, (h + 1) * HEAD_DIM)
            qh = q_ref[:, hs]
            s_c = jnp.where(mask_c, lax.dot_general(qh, kc_ref[:, hs], _NT, preferred_element_type=F32) * scale, NEG)
            s_p = jnp.where(mask_p, lax.dot_general(qh, kp_ref[:, hs], _NT, preferred_element_type=F32) * scale, NEG)
            m = jnp.maximum(jnp.max(s_c, axis=1, keepdims=True), jnp.max(s_p, axis=1, keepdims=True))
            p_c = jnp.exp(s_c - m)
            p_p = jnp.exp(s_p - m)
            l = jnp.sum(p_c, axis=1, keepdims=True) + jnp.sum(p_p, axis=1, keepdims=True)
            o = (lax.dot_general(p_c.astype(BF16), vc_ref[:, hs], _NN, preferred_element_type=F32)
                 + lax.dot_general(p_p.astype(BF16), vp_ref[:, hs], _NN, preferred_element_type=F32))
            o_ref[:, hs] = o / l
            stats = jnp.where(_lane(h, stats.shape), m + jnp.log(l), stats)
        l_ref[...] = stats

    (cur, prev), (stat, _) = _attn_specs(d), _attn_specs(LANE)
    return _call(name, body, [q, k, k, v, v], grid=(dil, nb), in_specs=[cur, cur, prev, cur, prev], out_specs=[cur, stat],
                 out_shape=[jax.ShapeDtypeStruct((dil, ln, d), F32), jax.ShapeDtypeStruct((dil, ln, LANE), F32)],
                 sem=("parallel", "parallel"), carry=carry)


def _attn_dq(name, q, k, v, do, lse, dsum, carry=None):
    dil, ln, d = k.shape
    nh = d // HEAD_DIM
    nb = ln // ATTN_BLOCK
    scale = HEAD_DIM ** -0.5

    def body(q_ref, kc_ref, kp_ref, vc_ref, vp_ref, do_ref, l_ref, d_ref, dq_ref):
        mask_c, mask_p = _masks(pl.program_id(1))
        for h in range(nh):
            hs = slice(h * HEAD_DIM, (h + 1) * HEAD_DIM)
            qh, dout = q_ref[:, hs], do_ref[:, hs]
            lrow, drow = l_ref[:, h:h + 1], d_ref[:, h:h + 1]
            dq = jnp.zeros((ATTN_BLOCK, HEAD_DIM), F32)
            for mask, k_ref, v_ref in ((mask_c, kc_ref, vc_ref), (mask_p, kp_ref, vp_ref)):
                sc = lax.dot_general(qh, k_ref[:, hs], _NT, preferred_element_type=F32) * scale
                p = jnp.where(mask, jnp.exp(jnp.where(mask, sc, NEG) - lrow), 0.0)
                dp = lax.dot_general(dout, v_ref[:, hs], _NT, preferred_element_type=F32)
                ds = p * (dp - drow)
                dq = dq + lax.dot_general(ds.astype(BF16), k_ref[:, hs], _NN, preferred_element_type=F32)
            dq_ref[:, hs] = dq * scale

    (cur, prev), (stat, _) = _attn_specs(d), _attn_specs(LANE)
    (dq,) = _call(name, body, [q, k, k, v, v, do, lse, dsum], grid=(dil, nb),
                  in_specs=[cur, cur, prev, cur, prev, cur, stat, stat], out_specs=[cur],
                  out_shape=[jax.ShapeDtypeStruct((dil, ln, d), F32)], sem=("parallel", "parallel"), carry=carry)
    return dq


def _attn_dkv(name, q, k, v, do, lse, dsum, carry=None):
    dil, ln, d = k.shape
    nh = d // HEAD_DIM
    nb = ln // ATTN_BLOCK
    scale = HEAD_DIM ** -0.5

    def body(k_ref, v_ref, qc_ref, qn_ref, doc_ref, don_ref, lc_ref, lnx_ref, dc_ref, dn_ref, dk_ref, dv_ref):
        n = pl.program_id(1)
        row = lax.broadcasted_iota(jnp.int32, (ATTN_BLOCK, ATTN_BLOCK), 0)
        col = lax.broadcasted_iota(jnp.int32, (ATTN_BLOCK, ATTN_BLOCK), 1)
        masks = (col <= row, jnp.logical_and(col >= row, n < nb - 1))
        for h in range(nh):
            hs = slice(h * HEAD_DIM, (h + 1) * HEAD_DIM)
            kh, vh = k_ref[:, hs], v_ref[:, hs]
            dk = jnp.zeros((ATTN_BLOCK, HEAD_DIM), F32)
            dv = jnp.zeros((ATTN_BLOCK, HEAD_DIM), F32)
            for mask, q_ref, do_ref, l_ref, d_ref in ((masks[0], qc_ref, doc_ref, lc_ref, dc_ref),
                                                      (masks[1], qn_ref, don_ref, lnx_ref, dn_ref)):
                qh, dout = q_ref[:, hs], do_ref[:, hs]
                lrow, drow = l_ref[:, h:h + 1], d_ref[:, h:h + 1]
                sc = lax.dot_general(qh, kh, _NT, preferred_element_type=F32) * scale
                p = jnp.where(mask, jnp.exp(jnp.where(mask, sc, NEG) - lrow), 0.0)
                dp = lax.dot_general(dout, vh, _NT, preferred_element_type=F32)
                ds = p * (dp - drow)
                dv = dv + lax.dot_general(p.astype(BF16), dout, _TN, preferred_element_type=F32)
                dk = dk + lax.dot_general(ds.astype(BF16), qh, _TN, preferred_element_type=F32)
            dk_ref[:, hs] = dk * scale
            dv_ref[:, hs] = dv

    def specs(width):
        cur = pl.BlockSpec((None, ATTN_BLOCK, width), lambda r, n: (r, n, 0))
        nxt = pl.BlockSpec((None, ATTN_BLOCK, width), lambda r, n: (r, jnp.minimum(n + 1, nb - 1), 0))
        return cur, nxt

    (cur, nxt), (stat, stat_next) = specs(d), specs(LANE)
    return _call(name, body, [k, v, q, q, do, do, lse, lse, dsum, dsum], grid=(dil, nb),
                 in_specs=[cur, cur, cur, nxt, cur, nxt, stat, stat_next, stat, stat_next], out_specs=[cur, cur],
                 out_shape=[jax.ShapeDtypeStruct((dil, ln, d), F32)] * 2, sem=("parallel", "parallel"), carry=carry)


def _adamw_tile(w, g, m, v):
    m = ADAM_B1 * m + (1.0 - ADAM_B1) * g
    v = ADAM_B2 * v + (1.0 - ADAM_B2) * (g * g)
    m_hat = m / (1.0 - ADAM_B1 ** ADAM_STEP)
    v_hat = v / (1.0 - ADAM_B2 ** ADAM_STEP)
    delta = -ADAM_LR * (m_hat / (jnp.sqrt(v_hat) + ADAM_EPS) + ADAM_WD * w)
    return delta, m, v


def _adamw_big(name, parts, w, m, v):
    layers, r, c = w.shape
    assert len(parts) == layers
    chunk_rows = min(ch.shape[1] for per_layer in parts for ch in per_layer)
    tr = 16
    while tr * 2 <= min(chunk_rows, (256 * 1024) // c) and chunk_rows % (tr * 2) == 0:
        tr *= 2
    pieces = [(ly, j, ch.shape[1] // tr, ch) for ly, per_layer in enumerate(parts) for j, ch in enumerate(per_layer)]

    def body(*refs):
        part_refs = refs[:len(pieces)]
        w_ref, m_ref, v_ref, g_out, d_out, m_out, v_out = refs[len(pieces):]
        layer, i = pl.program_id(0), pl.program_id(1)
        for (ly, j, tiles, _), part_ref in zip(pieces, part_refs):
            @pl.when(jnp.logical_and(layer == ly, i // tiles == j))
            def _(part_ref=part_ref):
                g = part_ref[0].astype(F32)
                for dev in range(1, N_DEV):
                    g = g + part_ref[dev].astype(F32)
                delta, mn, vn = _adamw_tile(w_ref[...], g, m_ref[...], v_ref[...])
                g_out[...] = g
                d_out[...] = delta
                m_out[...] = mn
                v_out[...] = vn

    def part_index(layer, i, ly, j, tiles):
        return (0, jnp.where(jnp.logical_and(layer == ly, i // tiles == j), i % tiles, 0), 0)

    own = pl.BlockSpec((None, tr, c), lambda ly, i: (ly, i, 0))
    part_specs = [pl.BlockSpec((N_DEV, tr, c), functools.partial(part_index, ly=ly, j=j, tiles=tiles)) for ly, j, tiles, _ in pieces]
    return _call(name, body, [ch for _, _, _, ch in pieces] + [w, m, v], grid=(layers, r // tr), in_specs=part_specs + [own] * 3,
                 out_specs=[own] * 4, out_shape=[jax.ShapeDtypeStruct(w.shape, F32)] * 4, sem=("parallel", "parallel"))


def _sum_slots(name, slots):
    _, r, c = slots.shape

    def body(s_ref, o_ref):
        g = s_ref[0]
        for j in range(1, N_DEV):
            g = g + s_ref[j]
        o_ref[...] = g

    return pl.pallas_call(body, name=name, out_shape=jax.ShapeDtypeStruct((r, c), F32),
                          compiler_params=_params())(slots)


def _adamw_small(name, w, g, m, v):
    def body(w_ref, g_ref, m_ref, v_ref, d_out, m_out, v_out):
        delta, mn, vn = _adamw_tile(w_ref[...], g_ref[...], m_ref[...], v_ref[...])
        d_out[...] = delta
        m_out[...] = mn
        v_out[...] = vn

    return pl.pallas_call(body, name=name, out_shape=[jax.ShapeDtypeStruct(w.shape, F32)] * 3,
                          compiler_params=_params())(w, g, m, v)


class Weights:
    def __init__(self, shards):
        self.shards, self.full, self.parts = shards, {}, {}

    @staticmethod
    def by_columns(key):
        return key.rstrip("01") in COLUMN_SHARDED

    def gather(self, *keys):
        return Exchange(gathers=[(self.shards[k], self.by_columns(k)) for k in keys], keys=keys)

    def landed(self, ex):
        for key, full in zip(ex.keys, ex.gathered):
            cols = self.by_columns(key)
            self.full[key] = W(full if cols else full.reshape(-1, full.shape[-1]), cols)

    def scatter(self, grads, gathers=(), chunk=(0, 1)):
        return Exchange(gathers=gathers, scatters=[(g, self.by_columns(k)) for k, g in grads.items()], keys=list(grads), chunk=chunk)

    def received(self, ex):
        for key, part in zip(ex.keys, ex.parts):
            self.parts.setdefault(key, {})[ex.chunk[0]] = part

    def chunks(self, key):
        return [self.parts[key][j] for j in sorted(self.parts[key])]

    def __getitem__(self, key):
        return self.full[key]


def _mlp_ple_fwd(tag, z1, p_i, ln1_g, ln1_b, ln2_g, ln2_b, wt, carries):
    h1, h1b = _layer_norm(f"ln1_{tag}", z1, ln1_g, ln1_b, (F32, BF16))
    up, act = _mm_nn(f"mlp_up_{tag}", h1b, wt["mlp_up" + tag], (F32, BF16),
                     epilogue=lambda acc: (acc, jnp.square(jnp.maximum(acc, 0.0))), carry=carries.get("mlp_up"))
    if "mlp_up" in carries:
        wt.landed(carries["mlp_up"])
    (z2,) = _mm_nn(f"mlp_down_{tag}", act, wt["mlp_down" + tag], (F32,), epilogue=lambda acc, h: (ALPHA * h + acc,),
                   extras=[(h1, "tile")], carry=carries.get("mlp_down"))
    if "mlp_down" in carries:
        wt.landed(carries["mlp_down"])
    h2, h2b = _layer_norm(f"ln2_{tag}", z2, ln2_g, ln2_b, (F32, BF16))
    (pe,) = _mm_nn(f"ple_proj_{tag}", p_i, wt["ple_proj" + tag], (F32,))

    def gate(acc, h, e):
        out = h + e * _sigmoid(acc)
        return acc, out, out

    gp, out, outb = _mm_nn(f"ple_gate_{tag}", h2b, wt["ple_gate" + tag], (F32, F32, BF16), epilogue=gate,
                           extras=[(h2, "tile"), (pe, "tile")], carry=carries.get("ple_gate"))
    if "ple_gate" in carries:
        wt.landed(carries["ple_gate"])
    saved = dict(z1=z1, h1b=h1b, up=up, act=act, z2=z2, h2b=h2b, pe=pe, gp=gp, p=p_i)
    return out, outb, saved


def _mlp_ple_bwd(tag, d_out, sv, ln1_g, ln2_g, wt):
    d = d_out.shape[1]

    def fn(i, o, r):
        dy, pe, gp = i[0][...], i[1][...], i[2][...]
        sg = _sigmoid(gp)
        o[0][...] = (dy * sg).astype(BF16)
        o[1][...] = (dy * pe * sg * (1.0 - sg)).astype(BF16)

    d_pe, d_gp = _rowwise(f"ple_bwd_{tag}", fn, [_rows(d_out), _rows(sv["pe"]), _rows(sv["gp"])], [(d, BF16), (d, BF16)])
    g_proj = _mm_tn(f"g_ple_proj_{tag}", sv["p"], d_pe, wt["ple_proj" + tag])
    g_gate = _mm_tn(f"g_ple_gate_{tag}", sv["h2b"], d_gp, wt["ple_gate" + tag])
    (d_h2,) = _mm_nt(f"d_ple_gate_{tag}", d_gp, wt["ple_gate" + tag], (F32,), epilogue=lambda acc, dy: (dy + acc,),
                     extras=[(d_out, "tile")])
    d_z2, d_z2b, g_ln2_g, g_ln2_b = _layer_norm_bwd(f"ln2_bwd_{tag}", sv["z2"], ln2_g, d_h2)
    ex = wt.scatter({"ple_proj" + tag: g_proj, "ple_gate" + tag: g_gate})
    g_down = _mm_tn(f"g_mlp_down_{tag}", sv["act"], d_z2b, wt["mlp_down" + tag], carry=ex)
    wt.received(ex)
    ex = wt.scatter({"mlp_down" + tag: g_down}, chunk=(0, 2))
    (d_up,) = _mm_nt(f"d_mlp_down_{tag}", d_z2b, wt["mlp_down" + tag], (BF16,),
                     epilogue=lambda acc, u: (acc * (2.0 * jnp.maximum(u, 0.0)),), extras=[(sv["up"], "tile")], carry=ex)
    wt.received(ex)
    ex = wt.scatter({"mlp_down" + tag: g_down}, chunk=(1, 2))
    g_up = _mm_tn(f"g_mlp_up_{tag}", sv["h1b"], d_up, wt["mlp_up" + tag], carry=ex)
    wt.received(ex)
    ex = wt.scatter({"mlp_up" + tag: g_up}, chunk=(0, 2))
    (d_h1,) = _mm_nt(f"d_mlp_up_{tag}", d_up, wt["mlp_up" + tag], (F32,), epilogue=lambda acc, dz: (ALPHA * dz + acc,),
                     extras=[(d_z2, "tile")], carry=ex)
    wt.received(ex)
    d_z1, d_z1b, g_ln1_g, g_ln1_b = _layer_norm_bwd(f"ln1_bwd_{tag}", sv["z1"], ln1_g, d_h1)
    return d_z1, d_z1b, dict(ln1_g=g_ln1_g, ln1_b=g_ln1_b, ln2_g=g_ln2_g, ln2_b=g_ln2_b), g_up


def _local_step(x, p, positions, target, wt, small):
    s, d = x.shape
    nh = d // HEAD_DIM

    ex = wt.gather("ple_proj0", "ple_gate0")
    (u,) = _mm_nn("conv_in", x, wt["conv_w_in"], (F32,), epilogue=lambda acc, b: (acc + b,), extras=[(small["conv_b_in"], "row")],
                  carry=ex)
    wt.landed(ex)

    def glu_fn(i, o, r):
        o[0][...] = i[0][...] * _sigmoid(i[1][...])

    (glu,) = _rowwise("glu", glu_fn, [_rows(u, 0, d), _rows(u, 1, d)], [(d, F32)])
    ex = wt.gather("mlp_up0")
    c = _conv_fwd("dwconv", glu, small["conv_dw"], small["conv_dw_b"], carry=ex)
    wt.landed(ex)

    def ln_silu(i, o, r):
        xhat, _ = _ln_stats(i[0][...])
        n = xhat * i[1][...] + i[2][...]
        o[0][...] = (n * _sigmoid(n)).astype(BF16)

    (sb,) = _rowwise("conv_ln_silu", ln_silu, [_rows(c), _full(small["conv_ln_g"]), _full(small["conv_ln_b"])], [(d, BF16)])
    ex = wt.gather("attn_w_o")
    (z1,) = _mm_nn("conv_out", sb, wt["conv_w_out"], (F32,), epilogue=lambda acc, xt: (ALPHA * xt + acc,), extras=[(x, "tile")],
                   carry=ex)
    wt.landed(ex)
    x1, x1b, sv0 = _mlp_ple_fwd("0", z1, p[0], small["ln1_g"][0:1], small["ln1_b"][0:1], small["ln2_g"][0:1], small["ln2_b"][0:1], wt,
                                dict(mlp_up=wt.gather("mlp_down0"), mlp_down=wt.gather("attn_w_q"), ple_gate=wt.gather("w_kv")))

    (kvn,) = _layer_norm("kv_ln", x1, small["kv_ln_g"], small["kv_ln_b"], (BF16,))
    (kv,) = _mm_nn("kv_proj", kvn, wt["w_kv"], (F32,))
    (q,) = _mm_nn("q_proj", x1b, wt["attn_w_q"], (F32,))
    cos, sin = _rope_tables(positions)
    row_scratch = [pltpu.VMEM((ROW_TILE, LANE), F32)]

    def rot_kv(i, o, r, scr):
        cs, sn = i[2][...], i[3][...]
        for h in range(nh):
            hs = slice(h * HEAD_DIM, (h + 1) * HEAD_DIM)
            for base, val in ((0, _rot(i[0][:, hs], cs, sn)), (N_GROUPS, i[1][:, hs])):
                for g, dil in enumerate(GROUP_DILATIONS):
                    for res, plane in enumerate(_split_rows(scr, val, dil)):
                        o[base + g][res, :, hs] = plane.astype(BF16)

    by_group = [(d, BF16, dil) for dil in GROUP_DILATIONS]
    kv_groups = _rowwise("rotary_kv", rot_kv, [_rows(kv, 0, d), _rows(kv, 1, d), _rows(cos), _rows(sin)], by_group * 2,
                         ts=ROW_TILE, scratch=row_scratch)
    kg, vg = kv_groups[:N_GROUPS], kv_groups[N_GROUPS:]

    def rot_q(i, o, r, scr):
        cs, sn = i[1][...], i[2][...]
        for g, dil in enumerate(GROUP_DILATIONS):
            for h in range(nh):
                hs = slice(h * HEAD_DIM, (h + 1) * HEAD_DIM)
                val = _rot(i[0][:, g * d + h * HEAD_DIM:g * d + (h + 1) * HEAD_DIM], cs, sn)
                for res, plane in enumerate(_split_rows(scr, val, dil)):
                    o[g][res, :, hs] = plane.astype(BF16)

    qg = _rowwise("rotary_q", rot_q, [_rows(q), _rows(cos), _rows(sin)], by_group, ts=ROW_TILE, scratch=row_scratch)

    riders = [wt.gather("mlp_up1"), wt.gather("mlp_down1"), wt.gather("ple_proj1", "ple_gate1")]
    og, lg = zip(*[_attn_fwd(f"attn_fwd_{g}", qg[g], kg[g], vg[g], carry=riders[g]) for g in range(N_GROUPS)])
    for ex in riders:
        wt.landed(ex)

    def merge(i, o, r, scr):
        lses = []
        for g, dil in enumerate(GROUP_DILATIONS):
            lses.append(_join_rows(scr, [i[N_GROUPS + g][res] for res in range(dil)]))
        top = functools.reduce(jnp.maximum, lses)
        es = [jnp.exp(l - top) for l in lses]
        den = functools.reduce(lambda a, b: a + b, es)
        total = top + jnp.log(den)
        for g, dil in enumerate(GROUP_DILATIONS):
            for res, plane in enumerate(_split_rows(scr, total, dil)):
                o[2 + g][res] = plane
        ws = [e / den for e in es]
        for h in range(nh):
            hs = slice(h * HEAD_DIM, (h + 1) * HEAD_DIM)
            out = jnp.zeros((ROW_TILE, HEAD_DIM), F32)
            for g, dil in enumerate(GROUP_DILATIONS):
                og_h = _join_rows(scr, [i[g][res, :, hs] for res in range(dil)])
                out = out + ws[g][:, h:h + 1] * og_h
            o[0][:, hs] = out
            o[1][:, hs] = out.astype(BF16)

    merged = _rowwise("attn_merge", merge, [_by_residue(t) for t in og + lg],
                      [(d, F32), (d, BF16)] + [(LANE, F32, dil) for dil in GROUP_DILATIONS], ts=ROW_TILE, scratch=row_scratch)
    o, ob, lse_g = merged[0], merged[1], merged[2:]
    (z1b,) = _mm_nn("attn_out", ob, wt["attn_w_o"], (F32,), epilogue=lambda acc, xt: (ALPHA * xt + acc,), extras=[(x1, "tile")])
    y, _, sv1 = _mlp_ple_fwd("1", z1b, p[1], small["ln1_g"][1:2], small["ln1_b"][1:2], small["ln2_g"][1:2], small["ln2_b"][1:2], wt, {})

    def loss_fn(i, o, r):
        diff = i[0][...] - i[1][...]
        o[0][...] = diff * (1.0 / d)
        r[0][...] += jnp.broadcast_to(jnp.sum(diff * diff), (1, LANE))

    d_y, sq = _rowwise("loss", loss_fn, [_rows(y), _rows(target)], [(d, F32)], [(1, LANE)])
    loss = 0.5 * sq[0, 0] / d

    d_z1, d_z1b, g1, g_up1 = _mlp_ple_bwd("1", d_y, sv1, small["ln1_g"][1:2], small["ln2_g"][1:2], wt)
    g_wo = _mm_tn("g_attn_out", ob, d_z1b, wt["attn_w_o"])
    (d_o,) = _mm_nt("d_attn_out", d_z1b, wt["attn_w_o"], (F32,))

    def dsum_fn(i, o, r, scr):
        stats = jnp.zeros((ROW_TILE, LANE), F32)
        for h in range(nh):
            hs = slice(h * HEAD_DIM, (h + 1) * HEAD_DIM)
            dout = i[0][:, hs]
            stats = jnp.where(_lane(h, stats.shape), jnp.sum(dout * i[1][:, hs], axis=1, keepdims=True), stats)
            for g, dil in enumerate(GROUP_DILATIONS):
                for res, plane in enumerate(_split_rows(scr, dout, dil)):
                    o[g][res, :, hs] = plane.astype(BF16)
        for g, dil in enumerate(GROUP_DILATIONS):
            for res, plane in enumerate(_split_rows(scr, stats, dil)):
                o[N_GROUPS + g][res] = plane

    res_ = _rowwise("attn_dsum", dsum_fn, [_rows(d_o), _rows(o)], by_group + [(LANE, F32, dil) for dil in GROUP_DILATIONS],
                    ts=ROW_TILE, scratch=row_scratch)
    dog, dsum_g = res_[:N_GROUPS], res_[N_GROUPS:]
    dqs, dks, dvs = [], [], []
    riders = [wt.scatter({"attn_w_o": g_wo}), wt.scatter({"mlp_up1": g_up1}, chunk=(1, 2))]
    for g in range(N_GROUPS):
        dqs.append(_attn_dq(f"attn_dq_{g}", qg[g], kg[g], vg[g], dog[g], lse_g[g], dsum_g[g], carry=riders[0] if g == 0 else None))
        dk, dv = _attn_dkv(f"attn_dkv_{g}", qg[g], kg[g], vg[g], dog[g], lse_g[g], dsum_g[g], carry=riders[1] if g == 0 else None)
        dks.append(dk)
        dvs.append(dv)
    for ex in riders:
        wt.received(ex)

    def unrot_q(i, o, r, scr):
        cs, sn = i[N_GROUPS][...], i[N_GROUPS + 1][...]
        for g, dil in enumerate(GROUP_DILATIONS):
            for h in range(nh):
                hs = slice(h * HEAD_DIM, (h + 1) * HEAD_DIM)
                dq = _join_rows(scr, [i[g][res, :, hs] for res in range(dil)])
                o[0][:, g * d + h * HEAD_DIM:g * d + (h + 1) * HEAD_DIM] = _unrot(dq, cs, sn).astype(BF16)

    (d_q,) = _rowwise("rotary_q_bwd", unrot_q, [_by_residue(t) for t in dqs] + [_rows(cos), _rows(sin)], [(N_GROUPS * d, BF16)],
                      ts=ROW_TILE, scratch=row_scratch)

    def unrot_kv(i, o, r, scr):
        cs, sn = i[2 * N_GROUPS][...], i[2 * N_GROUPS + 1][...]
        for h in range(nh):
            hs = slice(h * HEAD_DIM, (h + 1) * HEAD_DIM)
            for base in (0, N_GROUPS):
                tot = jnp.zeros((ROW_TILE, HEAD_DIM), F32)
                for g, dil in enumerate(GROUP_DILATIONS):
                    tot = tot + _join_rows(scr, [i[base + g][res, :, hs] for res in range(dil)])
                if base == 0:
                    o[0][:, hs] = _unrot(tot, cs, sn).astype(BF16)
                else:
                    o[0][:, d + h * HEAD_DIM:d + (h + 1) * HEAD_DIM] = tot.astype(BF16)

    (d_kv,) = _rowwise("rotary_kv_bwd", unrot_kv, [_by_residue(t) for t in dks + dvs] + [_rows(cos), _rows(sin)], [(2 * d, BF16)],
                       ts=ROW_TILE, scratch=row_scratch)
    g_wq = _mm_tn("g_q_proj", x1b, d_q, wt["attn_w_q"])
    ex = wt.scatter({"attn_w_q": g_wq}, chunk=(0, 2))
    g_wkv = _mm_tn("g_kv_proj", kvn, d_kv, wt["w_kv"], carry=ex)
    wt.received(ex)
    ex = wt.scatter({"attn_w_q": g_wq}, chunk=(1, 2))
    (d_x1a,) = _mm_nt("d_q_proj", d_q, wt["attn_w_q"], (F32,), epilogue=lambda acc, dz: (ALPHA * dz + acc,), extras=[(d_z1, "tile")],
                      carry=ex)
    wt.received(ex)
    ex = wt.scatter({"w_kv": g_wkv})
    (d_kvn,) = _mm_nt("d_kv_proj", d_kv, wt["w_kv"], (F32,), carry=ex)
    wt.received(ex)
    d_x1, _, g_kv_ln_g, g_kv_ln_b = _layer_norm_bwd("kv_ln_bwd", x1, small["kv_ln_g"], d_kvn, extra=d_x1a)

    d_z1, d_z1b, g0, g_up0 = _mlp_ple_bwd("0", d_x1, sv0, small["ln1_g"][0:1], small["ln2_g"][0:1], wt)
    g_wout = _mm_tn("g_conv_out", sb, d_z1b, wt["conv_w_out"])
    (d_s,) = _mm_nt("d_conv_out", d_z1b, wt["conv_w_out"], (F32,))

    def ln_silu_bwd(i, o, r):
        cx, gn, bn, ds_ = i[0][...], i[1][...], i[2][...], i[3][...]
        xhat, _ = _ln_stats(cx)
        n = xhat * gn + bn
        sg = _sigmoid(n)
        dn = ds_ * (sg * (1.0 + n * (1.0 - sg)))
        dx, dg, db = _ln_bwd_tile(cx, gn, dn)
        o[0][...] = dx
        r[0][...] += dg
        r[1][...] += db

    d_c, g_cln_g, g_cln_b = _rowwise("conv_ln_silu_bwd", ln_silu_bwd,
                                     [_rows(c), _full(small["conv_ln_g"]), _full(small["conv_ln_b"]), _rows(d_s)],
                                     [(d, F32)], [(1, d), (1, d)])
    ex = wt.scatter({"mlp_up0": g_up0}, chunk=(1, 2))
    d_glu, g_dw, g_dwb = _conv_bwd("dwconv_bwd", glu, d_c, small["conv_dw"], carry=ex)
    wt.received(ex)

    def glu_bwd(i, o, r):
        a, gt, dg_ = i[0][...], i[1][...], i[2][...]
        sg = _sigmoid(gt)
        da = dg_ * sg
        dgate = dg_ * a * sg * (1.0 - sg)
        o[0][:, 0:d] = da.astype(BF16)
        o[0][:, d:2 * d] = dgate.astype(BF16)
        r[0][:, 0:d] += jnp.sum(da, axis=0, keepdims=True)
        r[0][:, d:2 * d] += jnp.sum(dgate, axis=0, keepdims=True)

    d_u, g_bin = _rowwise("glu_bwd", glu_bwd, [_rows(u, 0, d), _rows(u, 1, d), _rows(d_glu)], [(2 * d, BF16)], [(1, 2 * d)])
    ex = wt.scatter({"conv_w_out": g_wout})
    g_win = _mm_tn("g_conv_in", x, d_u, wt["conv_w_in"], carry=ex)
    wt.received(ex)
    rows = [g_bin.reshape(2, d), g_dw, g_dwb, g_cln_g, g_cln_b, g_kv_ln_g, g_kv_ln_b]
    rows += [jnp.concatenate([g0[n], g1[n]], axis=0) for n in ("ln1_g", "ln1_b", "ln2_g", "ln2_b")]
    rows, offsets = _stack_rows(rows)
    ex = wt.scatter({"conv_w_in": g_win}, gathers=[(rows, False)])
    (grad_x,) = _mm_nt("d_conv_in", d_u, wt["conv_w_in"], (F32,), epilogue=lambda acc, dz: (ALPHA * dz + acc,), extras=[(d_z1, "tile")],
                       carry=ex)
    wt.received(ex)
    return loss, grad_x, ex.gathered[0], offsets


BIG = ("conv_w_in", "conv_w_out", "w_kv", "attn_w_q", "attn_w_o", "mlp_up", "mlp_down", "ple_proj", "ple_gate")
COLUMN_SHARDED = ("conv_w_in", "w_kv", "attn_w_q", "mlp_up", "ple_proj")
WEIGHTS = ("conv_w_in", "conv_b_in", "conv_dw", "conv_dw_b", "conv_ln_g", "conv_ln_b", "conv_w_out", "kv_ln_g", "kv_ln_b",
           "w_kv", "attn_w_q", "attn_w_o", "ln1_g", "ln1_b", "mlp_up", "mlp_down", "ln2_g", "ln2_b", "ple_proj", "ple_gate")


def kernel(x, p, positions, conv_w_in, conv_b_in, conv_dw, conv_dw_b, conv_ln_g, conv_ln_b, conv_w_out, kv_ln_g, kv_ln_b, w_kv, attn_w_q, attn_w_o, ln1_g, ln1_b, mlp_up, mlp_down, ln2_g, ln2_b, ple_proj, ple_gate, loss_target, m_conv_w_in, m_conv_b_in, m_conv_dw, m_conv_dw_b, m_conv_ln_g, m_conv_ln_b, m_conv_w_out, m_kv_ln_g, m_kv_ln_b, m_w_kv, m_attn_w_q, m_attn_w_o, m_ln1_g, m_ln1_b, m_mlp_up, m_mlp_down, m_ln2_g, m_ln2_b, m_ple_proj, m_ple_gate, v_conv_w_in, v_conv_b_in, v_conv_dw, v_conv_dw_b, v_conv_ln_g, v_conv_ln_b, v_conv_w_out, v_kv_ln_g, v_kv_ln_b, v_w_kv, v_attn_w_q, v_attn_w_o, v_ln1_g, v_ln1_b, v_mlp_up, v_mlp_down, v_ln2_g, v_ln2_b, v_ple_proj, v_ple_gate):
    given = dict(locals())
    wts = {n: given[n] for n in WEIGHTS}
    moms = {n: given["m_" + n] for n in WEIGHTS}
    vels = {n: given["v_" + n] for n in WEIGHTS}
    s, d = x.shape[1], x.shape[2]
    shard = d // N_DEV
    me = 4 * lax.axis_index("x") + 2 * lax.axis_index("y") + lax.axis_index("c")

    def layers_of(a):
        return a.reshape((-1,) + a.shape[-2:])

    shards = {}
    for n in BIG:
        w3 = layers_of(wts[n])
        for ly in range(w3.shape[0]):
            shards[n + str(ly) if w3.shape[0] > 1 else n] = w3[ly].astype(BF16)
    wt = Weights(shards)
    pack, at = _stack_rows([wts["conv_b_in"].reshape(2, shard), wts["conv_dw"].reshape(CONV_WIDTH, shard),
                            wts["conv_dw_b"], wts["conv_ln_g"], wts["conv_ln_b"]])
    ex = Exchange(gathers=[(shards["conv_w_in"], True), (shards["conv_w_out"], False), (pack, False)], keys=["conv_w_in", "conv_w_out"])
    _exchange_alone("gather_first", ex)
    wt.landed(ex)
    packed = ex.gathered[2]
    small = dict(conv_b_in=packed[:, at[0]:at[0] + 2].reshape(1, 2 * d), conv_dw=packed[:, at[1]:at[1] + CONV_WIDTH],
                 conv_dw_b=packed[:, at[2]].reshape(1, d), conv_ln_g=packed[:, at[3]].reshape(1, d),
                 conv_ln_b=packed[:, at[4]].reshape(1, d), kv_ln_g=kv_ln_g.reshape(1, d), kv_ln_b=kv_ln_b.reshape(1, d),
                 ln1_g=ln1_g, ln1_b=ln1_b, ln2_g=ln2_g, ln2_b=ln2_b)

    loss, grad_x, all_rows, at = _local_step(x[0], p[:, 0], positions.reshape(s, 1), loss_target[0], wt, small)
    loss = lax.psum(loss, ("x", "y", "c"))

    out = {}
    for n in BIG:
        w3 = layers_of(wts[n])
        keys = [n + str(ly) if w3.shape[0] > 1 else n for ly in range(w3.shape[0])]
        res = _adamw_big("adamw_" + n, [wt.chunks(k) for k in keys], w3, layers_of(moms[n]), layers_of(vels[n]))
        out[n] = [r.reshape(wts[n].shape) for r in res]
    tot = _sum_slots("sum_small_grads", all_rows)
    mine = lax.dynamic_slice_in_dim(tot, me * shard, shard, axis=1)
    b_in = lax.dynamic_slice_in_dim(tot[at[0]:at[0] + 2].reshape(1, 2 * d), me * 2 * shard, 2 * shard, axis=1)
    g_small = dict(conv_b_in=b_in, conv_dw=mine[at[1]:at[1] + CONV_WIDTH].reshape(conv_dw.shape), conv_dw_b=mine[at[2]:at[2] + 1],
                   conv_ln_g=mine[at[3]:at[3] + 1], conv_ln_b=mine[at[4]:at[4] + 1], kv_ln_g=tot[at[5]], kv_ln_b=tot[at[6]])
    for j, n in enumerate(("ln1_g", "ln1_b", "ln2_g", "ln2_b")):
        g_small[n] = tot[at[7 + j]:at[7 + j] + DEPTH]
    order = [n for n in WEIGHTS if n not in BIG]

    def flat(t):
        return _stack_rows([t[n].reshape(-1, shard) for n in order])

    (w_s, at), (g_s, _), (m_s, _), (v_s, _) = flat(wts), flat(g_small), flat(moms), flat(vels)
    d_s, m_s, v_s = _adamw_small("adamw_small", w_s, g_s, m_s, v_s)
    for n, a in zip(order, at):
        nrow = wts[n].size // shard
        out[n] = [g_small[n].reshape(wts[n].shape)] + [t[a:a + nrow].reshape(wts[n].shape) for t in (d_s, m_s, v_s)]
    return (loss, grad_x[None], *[out[n][0] for n in WEIGHTS], *[out[n][1] for n in WEIGHTS],
            *[out[n][2] for n in WEIGHTS], *[out[n][3] for n in WEIGHTS])
```

```python
import functools

import numpy as np
import jax
import jax.numpy as jnp
from jax import lax
from jax.experimental import pallas as pl
from jax.experimental.pallas import tpu as pltpu

F32, BF16 = jnp.float32, jnp.bfloat16

N_DEV = 8
HEAD_DIM = 128
ATTN_BLOCK = 128
GROUP_DILATIONS = (1, 4, 16)
N_GROUPS = len(GROUP_DILATIONS)
CONV_WIDTH = 31
CONV_HALO = 32
CONV_ROWS = 64
ROPE_THETA = 10000.0
LN_EPS = 1e-5
DEPTH = 2
ALPHA = (2 * DEPTH) ** 0.25
ADAM_LR, ADAM_B1, ADAM_B2, ADAM_EPS, ADAM_WD, ADAM_STEP = 0.001, 0.9, 0.999, 1e-08, 0.01, 10
NEG = -1e30
V7X_VMEM_LIMIT = 56 * 2 ** 20
LANE = 128
SUBLANES = 8
ROW_TILE = 256
GRAD_DTYPE = BF16

MESH = pl.DeviceIdType.MESH
ANY = pl.BlockSpec(memory_space=pl.ANY)


def _params(*sem):
    return pltpu.CompilerParams(dimension_semantics=sem or None, vmem_limit_bytes=V7X_VMEM_LIMIT)


def _sigmoid(x):
    return 1.0 / (1.0 + jnp.exp(-x))


def _divisor(n, most):
    best = None
    for t in range(LANE, min(n, most) + 1, LANE):
        if n % t == 0:
            best = t
    assert best is not None, (n, most)
    return best


def _stack_rows(parts):
    out, offsets, at = [], [], 0
    for a in parts:
        pad = -a.shape[0] % SUBLANES
        offsets.append(at)
        out.append(a)
        if pad:
            out.append(jnp.zeros((pad, a.shape[1]), a.dtype))
        at += a.shape[0] + pad
    return jnp.concatenate(out, axis=0), offsets


class Exchange:
    OTHER_CHIPS = (4, 2, 6)

    def __init__(self, gathers=(), scatters=(), keys=()):
        self.gathers, self.g_cols = [a for a, _ in gathers], [c for _, c in gathers]
        self.scatters, self.s_cols, self.s_parts = [s[0] for s in scatters], [s[1] for s in scatters], [s[2] for s in scatters]
        self.keys = list(keys)
        self.n_g, self.n_s = len(self.gathers), len(self.scatters)
        self.n = self.n_g + self.n_s
        self.operands = self.gathers + self.scatters
        self.gathered = self.parts = None

    def rows(self, t):
        a = self.scatters[t]
        first, last, of = self.s_parts[t]
        per = (a.shape[0] if self.s_cols[t] else a.shape[1]) // of
        return first * per, (last - first) * per

    def out_shape(self):
        outs = []
        for a, cols in zip(self.gathers, self.g_cols):
            outs.append(jax.ShapeDtypeStruct((a.shape[0], N_DEV * a.shape[1]) if cols else (N_DEV,) + a.shape, a.dtype))
        for t, (a, cols) in enumerate(zip(self.scatters, self.s_cols)):
            outs.append(jax.ShapeDtypeStruct((N_DEV, self.rows(t)[1], a.shape[1] // N_DEV if cols else a.shape[2]), a.dtype))
        return outs

    def scratch(self):
        dma = pltpu.SemaphoreType.DMA
        return [dma((max(self.n_g, 1) * 7,)), dma((max(self.n_g, 1) * 7,)), dma((max(self.n_s, 1) * 7,)),
                dma((max(self.n_s, 1) * 7,)), dma((self.n,))]

    def take(self, results):
        self.gathered, self.parts = list(results[:self.n_g]), list(results[self.n_g:])

    def _copies(self, ins, outs, sems):
        n_g, n_s = self.n_g, self.n_s
        g_in, s_in, g_out, s_out = ins[:n_g], ins[n_g:], outs[:n_g], outs[n_g:]
        g_send, g_recv, s_send, s_recv, local_sem = sems
        x, y, c = lax.axis_index("x"), lax.axis_index("y"), lax.axis_index("c")

        def peer(k):
            return (1 - x if k & 4 else x, 1 - y if k & 2 else y, 1 - c if k & 1 else c)

        def number(p):
            return 4 * p[0] + 2 * p[1] + p[2]

        me = number((x, y, c))

        def slot(t, j):
            first, count = self.rows(t)
            if self.s_cols[t]:
                width = self.scatters[t].shape[1] // N_DEV
                return s_in[t].at[pl.ds(first, count), pl.ds(pl.multiple_of(j * width, LANE), width)]
            return s_in[t].at[j, pl.ds(first, count)]

        def place(t, j):
            if self.g_cols[t]:
                width = self.gathers[t].shape[1]
                return g_out[t].at[:, pl.ds(pl.multiple_of(j * width, LANE), width)]
            return g_out[t].at[j]

        def local():
            cps = [pltpu.make_async_copy(g_in[t], place(t, me), local_sem.at[t]) for t in range(n_g)]
            return cps + [pltpu.make_async_copy(slot(t, me), s_out[t].at[me], local_sem.at[n_g + t]) for t in range(n_s)]

        def scatter(t, k):
            p = peer(k)
            return pltpu.make_async_remote_copy(
                src_ref=slot(t, number(p)), dst_ref=s_out[t].at[me], send_sem=s_send.at[t * 7 + k - 1],
                recv_sem=s_recv.at[t * 7 + k - 1], device_id=p, device_id_type=MESH)

        def landed(t, k):
            p = peer(k)
            return pltpu.make_async_remote_copy(
                src_ref=slot(t, me), dst_ref=s_out[t].at[number(p)], send_sem=s_send.at[t * 7 + k - 1],
                recv_sem=s_recv.at[t * 7 + k - 1], device_id=p, device_id_type=MESH)

        def gather(t, pair, block, to, src=None):
            slot = place(t, number(block))
            return pltpu.make_async_remote_copy(
                src_ref=slot if src is None else src, dst_ref=slot, send_sem=g_send.at[t * 7 + pair],
                recv_sem=g_recv.at[t * 7 + pair], device_id=to, device_id_type=MESH)

        def first_sends():
            cps = []
            for t in range(n_g):
                cps.append(gather(t, 0, peer(0), peer(1), src=g_in[t]))
                cps += [gather(t, 1 + j, peer(0), peer(k), src=g_in[t]) for j, k in enumerate(self.OTHER_CHIPS)]
            for t in range(n_s):
                cps += [scatter(t, k) for k in range(1, N_DEV)]
            return cps

        return peer, local, landed, gather, first_sends

    def start(self, ins, outs, sems):
        _, local, _, _, first_sends = self._copies(ins, outs, sems)
        for cp in local() + first_sends():
            cp.start()

    def finish(self, ins, outs, sems):
        peer, local, landed, gather, first_sends = self._copies(ins, outs, sems)
        mine, sibling = peer(0), peer(1)
        passed = []
        for j, k in enumerate(self.OTHER_CHIPS):
            for t in range(self.n_g):
                gather(t, 1 + j, peer(k), mine).wait_recv()
                passed.append(gather(t, 4 + j, peer(k), sibling))
                passed[-1].start()
        for t in range(self.n_g):
            gather(t, 0, sibling, mine).wait_recv()
            for j, k in enumerate(self.OTHER_CHIPS):
                gather(t, 4 + j, peer(k ^ 1), mine).wait_recv()
        for t in range(self.n_s):
            for k in range(1, N_DEV):
                landed(t, k).wait_recv()
        for cp in first_sends() + passed:
            cp.wait_send()
        for cp in local():
            cp.wait()


def _exchange_alone(name, ex):
    def body(*refs):
        ins, outs, sems = refs[:ex.n], refs[ex.n:2 * ex.n], refs[2 * ex.n:]
        ex.start(ins, outs, sems)
        ex.finish(ins, outs, sems)

    ex.take(pl.pallas_call(body, name=name, in_specs=[ANY] * ex.n, out_specs=[ANY] * ex.n, out_shape=ex.out_shape(),
                           scratch_shapes=ex.scratch())(*ex.operands))


def _call(name, body, args, *, grid, in_specs, out_specs, out_shape, scratch_shapes=(), sem=(), carry=None):
    if carry is None:
        return pl.pallas_call(body, name=name, grid=grid, in_specs=in_specs, out_specs=out_specs, out_shape=out_shape,
                              scratch_shapes=list(scratch_shapes), compiler_params=_params(*sem))(*args)
    ex = carry
    n_in, n_out, n_scr = len(args), len(out_shape), len(scratch_shapes)

    def carried(*refs):
        ins, cin = refs[:n_in], refs[n_in:n_in + ex.n]
        at = n_in + ex.n
        outs, cout = refs[at:at + n_out], refs[at + n_out:at + n_out + ex.n]
        at += n_out + ex.n
        scr, sems = refs[at:at + n_scr], refs[at + n_scr:]
        ids = [pl.program_id(a) for a in range(len(grid))]
        first = functools.reduce(jnp.logical_and, [i == 0 for i in ids])
        last = functools.reduce(jnp.logical_and, [i == g - 1 for i, g in zip(ids, grid)])

        @pl.when(first)
        def _():
            ex.start(cin, cout, sems)

        body(*ins, *outs, *scr)

        @pl.when(last)
        def _():
            ex.finish(cin, cout, sems)

    res = pl.pallas_call(
        carried, name=name, grid=grid, in_specs=list(in_specs) + [ANY] * ex.n, out_specs=list(out_specs) + [ANY] * ex.n,
        out_shape=list(out_shape) + ex.out_shape(), scratch_shapes=list(scratch_shapes) + ex.scratch(),
        compiler_params=_params(*("arbitrary",) * len(grid)),
    )(*args, *ex.operands)
    ex.take(res[n_out:])
    return res[:n_out]


class W:
    def __init__(self, arr, cols):
        self.arr, self.cols = arr, cols
        self.k, self.n = arr.shape
        self.shard_cols = self.n // N_DEV if cols else self.n


def _matmul(name, grid, operands, specs, dims, tile, extras, outs, out_specs, epilogue, carry=None):
    assert grid[2] == 1
    n_ex, n_out = len(extras), len(outs)

    def body(*refs):
        a_ref, b_ref = refs[0], refs[1]
        ex_refs = refs[2:2 + n_ex]
        out_refs = refs[2 + n_ex:2 + n_ex + n_out]
        acc = lax.dot_general(a_ref[...].astype(BF16), b_ref[...].astype(BF16), (dims, ((), ())),
                              preferred_element_type=F32)
        res = epilogue(acc, *[r[...] for r in ex_refs]) if epilogue else (acc,) * n_out
        for r, v in zip(out_refs, res):
            r[...] = v.astype(r.dtype)

    return _call(name, body, list(operands) + [a for a, _ in extras], grid=grid,
                 in_specs=list(specs) + [s for _, s in extras], out_specs=out_specs, out_shape=outs,
                 sem=("parallel", "parallel", "arbitrary"), carry=carry)


def _extra_specs(extras, tm, tn):
    out = []
    for arr, kind in extras:
        if kind == "tile":
            out.append((arr, pl.BlockSpec((tm, tn), lambda i, j, c: (i, j))))
        else:
            out.append((arr, pl.BlockSpec((1, tn), lambda i, j, c: (0, j))))
    return out


def _mm_nn(name, a, w, out_dtypes, epilogue=None, extras=(), carry=None):
    m, k = a.shape
    assert k == w.k
    tm = 1024 if a.dtype == BF16 else 512
    tn = _divisor(w.n, 512 if k <= 2048 else 256)
    assert tm * k * a.dtype.itemsize <= 16 * 2 ** 20, (name, tm, k)
    grid = (m // tm, w.n // tn, 1)
    specs = [pl.BlockSpec((tm, k), lambda i, j, c: (i, 0)), pl.BlockSpec((k, tn), lambda i, j, c: (0, j))]
    outs = [jax.ShapeDtypeStruct((m, w.n), d) for d in out_dtypes]
    out_specs = [pl.BlockSpec((tm, tn), lambda i, j, c: (i, j)) for _ in outs]
    return _matmul(name, grid, (a, w.arr), specs, ((1,), (0,)), (tm, tn), _extra_specs(extras, tm, tn), outs, out_specs,
                   epilogue, carry)


def _mm_nt(name, dy, w, out_dtypes, epilogue=None, extras=(), carry=None):
    m, n = dy.shape
    assert n == w.n and dy.dtype == BF16
    tm = 1024
    to = _divisor(w.k, 512 if n <= 2048 else 256)
    assert tm * n * dy.dtype.itemsize <= 16 * 2 ** 20, (name, tm, n)
    grid = (m // tm, w.k // to, 1)
    specs = [pl.BlockSpec((tm, n), lambda i, j, c: (i, 0)), pl.BlockSpec((to, n), lambda i, j, c: (j, 0))]
    outs = [jax.ShapeDtypeStruct((m, w.k), d) for d in out_dtypes]
    out_specs = [pl.BlockSpec((tm, to), lambda i, j, c: (i, j)) for _ in outs]
    return _matmul(name, grid, (dy, w.arr), specs, ((1,), (1,)), (tm, to), _extra_specs(extras, tm, to), outs, out_specs,
                   epilogue, carry)


def _mm_tn(name, a, dy, like, carry=None):
    m, k = a.shape
    n = dy.shape[1]
    assert (k, n) == (like.k, like.n)
    tk = _divisor(k, 1024 if a.dtype == BF16 else 512)
    tn = _divisor(n, 1024)
    grid = (k // tk, n // tn, 1)
    specs = [pl.BlockSpec((m, tk), lambda i, j, c: (0, i)), pl.BlockSpec((m, tn), lambda i, j, c: (0, j))]
    out_specs = [pl.BlockSpec((tk, tn), lambda i, j, c: (i, j))]
    (g,) = _matmul(name, grid, (a, dy), specs, ((0,), (0,)), (tk, tn), [], [jax.ShapeDtypeStruct((k, n), GRAD_DTYPE)],
                   out_specs, None, carry)
    return g if like.cols else g.reshape(N_DEV, k // N_DEV, n)


def _rows(arr, blk=0, width=None):
    return ("rows", arr, blk, width or arr.shape[1])


def _full(arr):
    return ("full", arr)


def _by_residue(arr):
    return ("residue", arr)


def _rowwise(name, fn, ins, outs, reds=(), ts=256, carry=None, scratch=()):
    s = next(i[1].shape[0] if i[0] == "rows" else i[1].shape[0] * i[1].shape[1] for i in ins if i[0] != "full")
    n_in, n_out, n_red = len(ins), len(outs), len(reds)
    in_specs = []
    for i in ins:
        if i[0] == "rows":
            in_specs.append(pl.BlockSpec((ts, i[3]), functools.partial(lambda t, blk: (t, blk), blk=i[2])))
        elif i[0] == "residue":
            d, _, w = i[1].shape
            in_specs.append(pl.BlockSpec((d, ts // d, w), lambda t: (0, t, 0)))
        else:
            in_specs.append(pl.BlockSpec(i[1].shape, functools.partial(lambda t, nd: (0,) * nd, nd=i[1].ndim)))
    out_shape, out_specs = [], []
    for o in outs:
        if len(o) == 2:
            out_shape.append(jax.ShapeDtypeStruct((s, o[0]), o[1]))
            out_specs.append(pl.BlockSpec((ts, o[0]), lambda t: (t, 0)))
        else:
            out_shape.append(jax.ShapeDtypeStruct((o[2], s // o[2], o[0]), o[1]))
            out_specs.append(pl.BlockSpec((o[2], ts // o[2], o[0]), lambda t: (0, t, 0)))
    out_shape += [jax.ShapeDtypeStruct(r, F32) for r in reds]
    out_specs += [pl.BlockSpec(r, lambda t: (0, 0)) for r in reds]

    def body(*refs):
        red_refs = refs[n_in + n_out:n_in + n_out + n_red]
        if red_refs:
            @pl.when(pl.program_id(0) == 0)
            def _():
                for r in red_refs:
                    r[...] = jnp.zeros(r.shape, F32)
        fn(refs[:n_in], refs[n_in:n_in + n_out], red_refs, *refs[n_in + n_out + n_red:])

    return _call(name, body, [i[1] for i in ins], grid=(s // ts,), in_specs=in_specs, out_specs=out_specs,
                 out_shape=out_shape, scratch_shapes=list(scratch), sem=("arbitrary" if reds else "parallel",), carry=carry)


def _ln_stats(x):
    mu = jnp.mean(x, axis=-1, keepdims=True)
    xc = x - mu
    var = jnp.mean(xc * xc, axis=-1, keepdims=True)
    return xc * lax.rsqrt(var + LN_EPS), lax.rsqrt(var + LN_EPS)


def _layer_norm(name, x, g, b, out_dtypes):
    def fn(i, o, r):
        xhat, _ = _ln_stats(i[0][...])
        y = xhat * i[1][...] + i[2][...]
        for ref in o:
            ref[...] = y.astype(ref.dtype)

    return _rowwise(name, fn, [_rows(x), _full(g), _full(b)], [(x.shape[1], d) for d in out_dtypes])


def _ln_bwd_tile(x, g, dy):
    xhat, rstd = _ln_stats(x)
    dyg = dy * g
    m1 = jnp.mean(dyg, axis=-1, keepdims=True)
    m2 = jnp.mean(dyg * xhat, axis=-1, keepdims=True)
    dx = rstd * (dyg - m1 - xhat * m2)
    return dx, jnp.sum(dy * xhat, axis=0, keepdims=True), jnp.sum(dy, axis=0, keepdims=True)


def _layer_norm_bwd(name, x, g, dy, extra=None):
    d = x.shape[1]

    def fn(i, o, r):
        dx, dg, db = _ln_bwd_tile(i[0][...], i[1][...], i[2][...])
        if extra is not None:
            dx = dx + i[3][...]
        o[0][...] = dx
        o[1][...] = dx.astype(BF16)
        r[0][...] += dg
        r[1][...] += db

    ins = [_rows(x), _full(g), _rows(dy)] + ([_rows(extra)] if extra is not None else [])
    return _rowwise(name, fn, ins, [(d, F32), (d, BF16)], [(1, d), (1, d)])


def _conv_fwd(name, glu, dw, dw_b, ts=512, carry=None):
    s, c = glu.shape
    tc = dw.shape[2]
    per = ts // CONV_HALO
    back = CONV_HALO - (CONV_WIDTH - 1)

    def body(cur_ref, prev_ref, w_ref, b_ref, out_ref, buf):
        i = pl.program_id(1)
        buf[pl.ds(0, CONV_HALO), :] = jnp.where(i > 0, prev_ref[...], 0.0)
        buf[pl.ds(CONV_HALO, ts), :] = cur_ref[...]
        for r0 in range(0, ts, CONV_ROWS):
            acc = jnp.broadcast_to(b_ref[...], (CONV_ROWS, tc))
            for j in range(CONV_WIDTH):
                acc = acc + w_ref[j:j + 1, :] * buf[pl.ds(r0 + back + j, CONV_ROWS), :]
            out_ref[pl.ds(r0, CONV_ROWS), :] = acc

    (out,) = _call(
        name, body, [glu, glu, dw, dw_b], grid=(c // tc, s // ts),
        in_specs=[pl.BlockSpec((ts, tc), lambda j, i: (i, j)),
                  pl.BlockSpec((CONV_HALO, tc), lambda j, i: (jnp.maximum(i * per - 1, 0), j)),
                  pl.BlockSpec((None, CONV_WIDTH, tc), lambda j, i: (j, 0, 0)),
                  pl.BlockSpec((1, tc), lambda j, i: (0, j))],
        out_specs=[pl.BlockSpec((ts, tc), lambda j, i: (i, j))],
        out_shape=[jax.ShapeDtypeStruct((s, c), F32)],
        scratch_shapes=[pltpu.VMEM((ts + CONV_HALO, tc), F32)],
        sem=("parallel", "parallel"), carry=carry)
    return out


def _conv_bwd(name, glu, dc, dw, ts=512, carry=None):
    s, c = glu.shape
    tc = dw.shape[2]
    per = ts // CONV_HALO
    back = CONV_HALO - (CONV_WIDTH - 1)
    last = s // ts - 1

    def body(g_ref, gprev_ref, dc_ref, dcnext_ref, w_ref, dglu_ref, ddw_ref, ddb_ref, gbuf, dbuf):
        i = pl.program_id(1)

        @pl.when(i == 0)
        def _():
            ddw_ref[...] = jnp.zeros(ddw_ref.shape, F32)
            ddb_ref[...] = jnp.zeros(ddb_ref.shape, F32)

        gbuf[pl.ds(0, CONV_HALO), :] = jnp.where(i > 0, gprev_ref[...], 0.0)
        gbuf[pl.ds(CONV_HALO, ts), :] = g_ref[...]
        dbuf[pl.ds(0, ts), :] = dc_ref[...]
        dbuf[pl.ds(ts, CONV_HALO), :] = jnp.where(i < last, dcnext_ref[...], 0.0)
        taps = [jnp.zeros((1, tc), F32)] * CONV_WIDTH
        for r0 in range(0, ts, CONV_ROWS):
            d_here = dbuf[pl.ds(r0, CONV_ROWS), :]
            acc = jnp.zeros((CONV_ROWS, tc), F32)
            for j in range(CONV_WIDTH):
                acc = acc + w_ref[j:j + 1, :] * dbuf[pl.ds(r0 + (CONV_WIDTH - 1) - j, CONV_ROWS), :]
                taps[j] = taps[j] + jnp.sum(d_here * gbuf[pl.ds(r0 + back + j, CONV_ROWS), :], axis=0, keepdims=True)
            dglu_ref[pl.ds(r0, CONV_ROWS), :] = acc
        for j in range(CONV_WIDTH):
            ddw_ref[j:j + 1, :] += taps[j]
        ddb_ref[...] += jnp.sum(dc_ref[...], axis=0, keepdims=True)

    return _call(
        name, body, [glu, glu, dc, dc, dw], grid=(c // tc, s // ts),
        in_specs=[pl.BlockSpec((ts, tc), lambda j, i: (i, j)),
                  pl.BlockSpec((CONV_HALO, tc), lambda j, i: (jnp.maximum(i * per - 1, 0), j)),
                  pl.BlockSpec((ts, tc), lambda j, i: (i, j)),
                  pl.BlockSpec((CONV_HALO, tc), lambda j, i: (jnp.minimum((i + 1) * per, (last + 1) * per - 1), j)),
                  pl.BlockSpec((None, CONV_WIDTH, tc), lambda j, i: (j, 0, 0))],
        out_specs=[pl.BlockSpec((ts, tc), lambda j, i: (i, j)),
                   pl.BlockSpec((CONV_WIDTH, tc), lambda j, i: (0, j)),
                   pl.BlockSpec((1, tc), lambda j, i: (0, j))],
        out_shape=[jax.ShapeDtypeStruct((s, c), F32), jax.ShapeDtypeStruct((CONV_WIDTH, c), F32),
                   jax.ShapeDtypeStruct((1, c), F32)],
        scratch_shapes=[pltpu.VMEM((ts + CONV_HALO, tc), F32), pltpu.VMEM((ts + CONV_HALO, tc), F32)],
        sem=("parallel", "arbitrary"), carry=carry)


def _rope_tables(positions):
    half = HEAD_DIM // 2
    inv = (np.float32(ROPE_THETA) ** (-np.arange(half, dtype=np.float32) * np.float32(2.0 / HEAD_DIM))).astype(np.float32)
    inv_freq = jnp.asarray(np.concatenate([inv, inv])[None, :])
    sign = jnp.asarray(np.concatenate([-np.ones(half, np.float32), np.ones(half, np.float32)])[None, :])

    def fn(i, o, r):
        ang = i[0][...].astype(F32) * i[1][...]
        o[0][...] = jnp.cos(ang)
        o[1][...] = jnp.sin(ang) * i[2][...]

    return _rowwise("rope_tables", fn, [_rows(positions), _full(inv_freq), _full(sign)], [(HEAD_DIM, F32), (HEAD_DIM, F32)], ts=512)


def _rot(x, cos, sin):
    return x * cos + pltpu.roll(x, HEAD_DIM // 2, 1) * sin


def _unrot(x, cos, sin):
    return x * cos - pltpu.roll(x, HEAD_DIM // 2, 1) * sin


def _split_rows(scr, value, d):
    if d == 1:
        return [value]
    scr[...] = value
    return [scr[pl.ds(r, scr.shape[0] // d, stride=d), :] for r in range(d)]


def _join_rows(scr, planes):
    d = len(planes)
    if d == 1:
        return planes[0]
    for r, plane in enumerate(planes):
        scr[pl.ds(r, scr.shape[0] // d, stride=d), :] = plane
    return scr[...]


def _lane(h, shape):
    return lax.broadcasted_iota(jnp.int32, shape, 1) == h


def _attn_specs(width):
    cur = pl.BlockSpec((None, ATTN_BLOCK, width), lambda r, n: (r, n, 0))
    prev = pl.BlockSpec((None, ATTN_BLOCK, width), lambda r, n: (r, jnp.maximum(n - 1, 0), 0))
    return cur, prev


def _masks(n):
    row = lax.broadcasted_iota(jnp.int32, (ATTN_BLOCK, ATTN_BLOCK), 0)
    col = lax.broadcasted_iota(jnp.int32, (ATTN_BLOCK, ATTN_BLOCK), 1)
    return col <= row, jnp.logical_and(col >= row, n > 0)


_NT = (((1,), (1,)), ((), ()))
_TN = (((0,), (0,)), ((), ()))
_NN = (((1,), (0,)), ((), ()))


def _attn_fwd(name, q, k, v, carry=None):
    dil, ln, d = k.shape
    nh = d // HEAD_DIM
    nb = ln // ATTN_BLOCK
    scale = HEAD_DIM ** -0.5

    def body(q_ref, kc_ref, kp_ref, vc_ref, vp_ref, o_ref, l_ref):
        mask_c, mask_p = _masks(pl.program_id(1))
        mask = jnp.concatenate([mask_p, mask_c], axis=1)
        stats = jnp.zeros((ATTN_BLOCK, LANE), F32)
        for h in range(nh):
            hs = slice(h * HEAD_DIM, (h + 1) * HEAD_DIM)
            keys = jnp.concatenate([kp_ref[:, hs], kc_ref[:, hs]], axis=0)
            vals = jnp.concatenate([vp_ref[:, hs], vc_ref[:, hs]], axis=0)
            sc = jnp.where(mask, lax.dot_general(q_ref[:, hs], keys, _NT, preferred_element_type=F32) * scale, NEG)
            m = jnp.max(sc, axis=1, keepdims=True)
            p = jnp.exp(sc - m)
            l = jnp.sum(p, axis=1, keepdims=True)
            o_ref[:, hs] = lax.dot_general(p.astype(BF16), vals, _NN, preferred_element_type=F32) / l
            stats = jnp.where(_lane(h, stats.shape), m + jnp.log(l), stats)
        l_ref[...] = stats

    (cur, prev), (stat, _) = _attn_specs(d), _attn_specs(LANE)
    return _call(name, body, [q, k, k, v, v], grid=(dil, nb), in_specs=[cur, cur, prev, cur, prev], out_specs=[cur, stat],
                 out_shape=[jax.ShapeDtypeStruct((dil, ln, d), F32), jax.ShapeDtypeStruct((dil, ln, LANE), F32)],
                 sem=("parallel", "parallel"), carry=carry)


def _attn_dq(name, q, k, v, do, lse, dsum, carry=None):
    dil, ln, d = k.shape
    nh = d // HEAD_DIM
    nb = ln // ATTN_BLOCK
    scale = HEAD_DIM ** -0.5

    def body(q_ref, kc_ref, kp_ref, vc_ref, vp_ref, do_ref, l_ref, d_ref, dq_ref):
        mask_c, mask_p = _masks(pl.program_id(1))
        mask = jnp.concatenate([mask_p, mask_c], axis=1)
        for h in range(nh):
            hs = slice(h * HEAD_DIM, (h + 1) * HEAD_DIM)
            keys = jnp.concatenate([kp_ref[:, hs], kc_ref[:, hs]], axis=0)
            vals = jnp.concatenate([vp_ref[:, hs], vc_ref[:, hs]], axis=0)
            sc = lax.dot_general(q_ref[:, hs], keys, _NT, preferred_element_type=F32) * scale
            p = jnp.where(mask, jnp.exp(jnp.where(mask, sc, NEG) - l_ref[:, h:h + 1]), 0.0)
            dp = lax.dot_general(do_ref[:, hs], vals, _NT, preferred_element_type=F32)
            ds = p * (dp - d_ref[:, h:h + 1])
            dq_ref[:, hs] = lax.dot_general(ds.astype(BF16), keys, _NN, preferred_element_type=F32) * scale

    (cur, prev), (stat, _) = _attn_specs(d), _attn_specs(LANE)
    (dq,) = _call(name, body, [q, k, k, v, v, do, lse, dsum], grid=(dil, nb),
                  in_specs=[cur, cur, prev, cur, prev, cur, stat, stat], out_specs=[cur],
                  out_shape=[jax.ShapeDtypeStruct((dil, ln, d), F32)], sem=("parallel", "parallel"), carry=carry)
    return dq


def _attn_dkv(name, q, k, v, do, lse, dsum, carry=None):
    dil, ln, d = k.shape
    nh = d // HEAD_DIM
    nb = ln // ATTN_BLOCK
    scale = HEAD_DIM ** -0.5

    def body(k_ref, v_ref, qc_ref, qn_ref, doc_ref, don_ref, lc_ref, lnx_ref, dc_ref, dn_ref, dk_ref, dv_ref):
        n = pl.program_id(1)
        row = lax.broadcasted_iota(jnp.int32, (ATTN_BLOCK, ATTN_BLOCK), 0)
        col = lax.broadcasted_iota(jnp.int32, (ATTN_BLOCK, ATTN_BLOCK), 1)
        mask = jnp.concatenate([row <= col, jnp.logical_and(row >= col, n < nb - 1)], axis=1)
        lse_t = jnp.concatenate([lc_ref[...].T, lnx_ref[...].T], axis=1)
        dsum_t = jnp.concatenate([dc_ref[...].T, dn_ref[...].T], axis=1)
        for h in range(nh):
            hs = slice(h * HEAD_DIM, (h + 1) * HEAD_DIM)
            qs = jnp.concatenate([qc_ref[:, hs], qn_ref[:, hs]], axis=0)
            douts = jnp.concatenate([doc_ref[:, hs], don_ref[:, hs]], axis=0)
            sc = lax.dot_general(k_ref[:, hs], qs, _NT, preferred_element_type=F32) * scale
            p = jnp.where(mask, jnp.exp(jnp.where(mask, sc, NEG) - lse_t[h:h + 1, :]), 0.0)
            dp = lax.dot_general(v_ref[:, hs], douts, _NT, preferred_element_type=F32)
            ds = p * (dp - dsum_t[h:h + 1, :])
            dv_ref[:, hs] = lax.dot_general(p.astype(BF16), douts, _NN, preferred_element_type=F32)
            dk_ref[:, hs] = lax.dot_general(ds.astype(BF16), qs, _NN, preferred_element_type=F32) * scale

    def specs(width):
        cur = pl.BlockSpec((None, ATTN_BLOCK, width), lambda r, n: (r, n, 0))
        nxt = pl.BlockSpec((None, ATTN_BLOCK, width), lambda r, n: (r, jnp.minimum(n + 1, nb - 1), 0))
        return cur, nxt

    (cur, nxt), (stat, stat_next) = specs(d), specs(LANE)
    return _call(name, body, [k, v, q, q, do, do, lse, lse, dsum, dsum], grid=(dil, nb),
                 in_specs=[cur, cur, cur, nxt, cur, nxt, stat, stat_next, stat, stat_next], out_specs=[cur, cur],
                 out_shape=[jax.ShapeDtypeStruct((dil, ln, d), F32)] * 2, sem=("parallel", "parallel"), carry=carry)


def _adamw_tile(w, g, m, v):
    m = ADAM_B1 * m + (1.0 - ADAM_B1) * g
    v = ADAM_B2 * v + (1.0 - ADAM_B2) * (g * g)
    m_hat = m / (1.0 - ADAM_B1 ** ADAM_STEP)
    v_hat = v / (1.0 - ADAM_B2 ** ADAM_STEP)
    delta = -ADAM_LR * (m_hat / (jnp.sqrt(v_hat) + ADAM_EPS) + ADAM_WD * w)
    return delta, m, v


def _adamw_big(name, parts, w, m, v):
    layers, r, c = w.shape
    assert len(parts) == layers and all(sum(ch.shape[1] for ch in per_layer) == r for per_layer in parts)
    every = [ch for per_layer in parts for ch in per_layer]
    per_row = 2 * c * (len(every) * N_DEV * every[0].dtype.itemsize + 7 * 4)
    tr = 16
    while tr * 2 <= min(min(ch.shape[1] for ch in every), V7X_VMEM_LIMIT // 2 // per_row) and all(ch.shape[1] % (tr * 2) == 0 for ch in every):
        tr *= 2
    pieces = []
    for ly, per_layer in enumerate(parts):
        at = 0
        for ch in per_layer:
            pieces.append((ly, at, ch.shape[1] // tr, ch))
            at += ch.shape[1] // tr

    def within(layer, i, ly, first, tiles):
        return jnp.logical_and(layer == ly, jnp.logical_and(i >= first, i < first + tiles))

    def body(*refs):
        part_refs = refs[:len(pieces)]
        w_ref, m_ref, v_ref, g_out, d_out, m_out, v_out = refs[len(pieces):]
        layer, i = pl.program_id(0), pl.program_id(1)
        for (ly, first, tiles, _), part_ref in zip(pieces, part_refs):
            @pl.when(within(layer, i, ly, first, tiles))
            def _(part_ref=part_ref):
                g = part_ref[0].astype(F32)
                for dev in range(1, N_DEV):
                    g = g + part_ref[dev].astype(F32)
                delta, mn, vn = _adamw_tile(w_ref[...], g, m_ref[...], v_ref[...])
                g_out[...] = g
                d_out[...] = delta
                m_out[...] = mn
                v_out[...] = vn

    def part_index(layer, i, ly, first, tiles):
        return (0, jnp.where(within(layer, i, ly, first, tiles), i - first, 0), 0)

    own = pl.BlockSpec((None, tr, c), lambda ly, i: (ly, i, 0))
    part_specs = [pl.BlockSpec((N_DEV, tr, c), functools.partial(part_index, ly=ly, first=first, tiles=tiles))
                  for ly, first, tiles, _ in pieces]
    return _call(name, body, [ch for _, _, _, ch in pieces] + [w, m, v], grid=(layers, r // tr), in_specs=part_specs + [own] * 3,
                 out_specs=[own] * 4, out_shape=[jax.ShapeDtypeStruct(w.shape, F32)] * 4, sem=("parallel", "parallel"))


def _sum_slots(name, slots):
    _, r, c = slots.shape

    def body(s_ref, o_ref):
        g = s_ref[0]
        for j in range(1, N_DEV):
            g = g + s_ref[j]
        o_ref[...] = g

    return pl.pallas_call(body, name=name, out_shape=jax.ShapeDtypeStruct((r, c), F32),
                          compiler_params=_params())(slots)


def _adamw_small(name, w, g, m, v):
    def body(w_ref, g_ref, m_ref, v_ref, d_out, m_out, v_out):
        delta, mn, vn = _adamw_tile(w_ref[...], g_ref[...], m_ref[...], v_ref[...])
        d_out[...] = delta
        m_out[...] = mn
        v_out[...] = vn

    return pl.pallas_call(body, name=name, out_shape=[jax.ShapeDtypeStruct(w.shape, F32)] * 3,
                          compiler_params=_params())(w, g, m, v)


class Weights:
    def __init__(self, shards):
        self.shards, self.full, self.parts = shards, {}, {}

    @staticmethod
    def by_columns(key):
        return key.rstrip("01") in COLUMN_SHARDED

    def gather(self, *keys):
        return Exchange(gathers=[(self.shards[k], self.by_columns(k)) for k in keys], keys=keys)

    def landed(self, ex):
        for key, full in zip(ex.keys, ex.gathered):
            cols = self.by_columns(key)
            self.full[key] = W(full if cols else full.reshape(-1, full.shape[-1]), cols)

    def scatter(self, grads, gathers=()):
        return Exchange(gathers=gathers, scatters=[(g, self.by_columns(k), part) for k, (g, part) in grads.items()], keys=list(grads))

    def received(self, ex):
        for key, (first, _, of), part in zip(ex.keys, ex.s_parts, ex.parts):
            self.parts.setdefault(key, {})[first / of] = part

    def chunks(self, key):
        return [self.parts[key][j] for j in sorted(self.parts[key])]

    def __getitem__(self, key):
        return self.full[key]


def _mlp_ple_fwd(tag, z1, p_i, ln1_g, ln1_b, ln2_g, ln2_b, wt, carries):
    h1, h1b = _layer_norm(f"ln1_{tag}", z1, ln1_g, ln1_b, (F32, BF16))
    up, act = _mm_nn(f"mlp_up_{tag}", h1b, wt["mlp_up" + tag], (F32, BF16),
                     epilogue=lambda acc: (acc, jnp.square(jnp.maximum(acc, 0.0))), carry=carries.get("mlp_up"))
    if "mlp_up" in carries:
        wt.landed(carries["mlp_up"])
    (z2,) = _mm_nn(f"mlp_down_{tag}", act, wt["mlp_down" + tag], (F32,), epilogue=lambda acc, h: (ALPHA * h + acc,),
                   extras=[(h1, "tile")], carry=carries.get("mlp_down"))
    if "mlp_down" in carries:
        wt.landed(carries["mlp_down"])
    h2, h2b = _layer_norm(f"ln2_{tag}", z2, ln2_g, ln2_b, (F32, BF16))
    (pe,) = _mm_nn(f"ple_proj_{tag}", p_i, wt["ple_proj" + tag], (F32,))

    def gate(acc, h, e):
        out = h + e * _sigmoid(acc)
        return acc, out, out

    gp, out, outb = _mm_nn(f"ple_gate_{tag}", h2b, wt["ple_gate" + tag], (F32, F32, BF16), epilogue=gate,
                           extras=[(h2, "tile"), (pe, "tile")], carry=carries.get("ple_gate"))
    if "ple_gate" in carries:
        wt.landed(carries["ple_gate"])
    saved = dict(z1=z1, h1b=h1b, up=up, act=act, z2=z2, h2b=h2b, pe=pe, gp=gp, p=p_i)
    return out, outb, saved


WHOLE = (0, 1, 1)


def _mlp_ple_bwd(tag, d_out, sv, ln1_g, ln2_g, wt, waiting):
    d = d_out.shape[1]

    def fn(i, o, r):
        dy, pe, gp = i[0][...], i[1][...], i[2][...]
        sg = _sigmoid(gp)
        o[0][...] = (dy * sg).astype(BF16)
        o[1][...] = (dy * pe * sg * (1.0 - sg)).astype(BF16)

    d_pe, d_gp = _rowwise(f"ple_bwd_{tag}", fn, [_rows(d_out), _rows(sv["pe"]), _rows(sv["gp"])], [(d, BF16), (d, BF16)])
    g_proj = _mm_tn(f"g_ple_proj_{tag}", sv["p"], d_pe, wt["ple_proj" + tag])
    g_gate = _mm_tn(f"g_ple_gate_{tag}", sv["h2b"], d_gp, wt["ple_gate" + tag])
    (d_h2,) = _mm_nt(f"d_ple_gate_{tag}", d_gp, wt["ple_gate" + tag], (F32,), epilogue=lambda acc, dy: (dy + acc,),
                     extras=[(d_out, "tile")])
    d_z2, d_z2b, g_ln2_g, g_ln2_b = _layer_norm_bwd(f"ln2_bwd_{tag}", sv["z2"], ln2_g, d_h2)
    ex = wt.scatter({"ple_proj" + tag: (g_proj, WHOLE), "ple_gate" + tag: (g_gate, WHOLE), **waiting})
    g_down = _mm_tn(f"g_mlp_down_{tag}", sv["act"], d_z2b, wt["mlp_down" + tag], carry=ex)
    wt.received(ex)
    ex = wt.scatter({"mlp_down" + tag: (g_down, (0, 2, 4))})
    (d_up,) = _mm_nt(f"d_mlp_down_{tag}", d_z2b, wt["mlp_down" + tag], (BF16,),
                     epilogue=lambda acc, u: (acc * (2.0 * jnp.maximum(u, 0.0)),), extras=[(sv["up"], "tile")], carry=ex)
    wt.received(ex)
    ex = wt.scatter({"mlp_down" + tag: (g_down, (2, 3, 4))})
    g_up = _mm_tn(f"g_mlp_up_{tag}", sv["h1b"], d_up, wt["mlp_up" + tag], carry=ex)
    wt.received(ex)
    ex = wt.scatter({"mlp_down" + tag: (g_down, (3, 4, 4)), "mlp_up" + tag: (g_up, (0, 1, 4))})
    (d_h1,) = _mm_nt(f"d_mlp_up_{tag}", d_up, wt["mlp_up" + tag], (F32,), epilogue=lambda acc, dz: (ALPHA * dz + acc,),
                     extras=[(d_z2, "tile")], carry=ex)
    wt.received(ex)
    d_z1, d_z1b, g_ln1_g, g_ln1_b = _layer_norm_bwd(f"ln1_bwd_{tag}", sv["z1"], ln1_g, d_h1)
    return d_z1, d_z1b, dict(ln1_g=g_ln1_g, ln1_b=g_ln1_b, ln2_g=g_ln2_g, ln2_b=g_ln2_b), g_up


def _local_step(x, p, positions, target, wt, small):
    s, d = x.shape
    nh = d // HEAD_DIM

    ex = wt.gather("conv_w_out", "ple_proj0", "ple_gate0")
    (u,) = _mm_nn("conv_in", x, wt["conv_w_in"], (F32,), epilogue=lambda acc, b: (acc + b,), extras=[(small["conv_b_in"], "row")],
                  carry=ex)
    wt.landed(ex)

    def glu_fn(i, o, r):
        o[0][...] = i[0][...] * _sigmoid(i[1][...])

    (glu,) = _rowwise("glu", glu_fn, [_rows(u, 0, d), _rows(u, 1, d)], [(d, F32)])
    ex = wt.gather("mlp_up0")
    c = _conv_fwd("dwconv", glu, small["conv_dw"], small["conv_dw_b"], carry=ex)
    wt.landed(ex)

    def ln_silu(i, o, r):
        xhat, _ = _ln_stats(i[0][...])
        n = xhat * i[1][...] + i[2][...]
        o[0][...] = (n * _sigmoid(n)).astype(BF16)

    (sb,) = _rowwise("conv_ln_silu", ln_silu, [_rows(c), _full(small["conv_ln_g"]), _full(small["conv_ln_b"])], [(d, BF16)])
    ex = wt.gather("attn_w_o")
    (z1,) = _mm_nn("conv_out", sb, wt["conv_w_out"], (F32,), epilogue=lambda acc, xt: (ALPHA * xt + acc,), extras=[(x, "tile")],
                   carry=ex)
    wt.landed(ex)
    x1, x1b, sv0 = _mlp_ple_fwd("0", z1, p[0], small["ln1_g"][0:1], small["ln1_b"][0:1], small["ln2_g"][0:1], small["ln2_b"][0:1], wt,
                                dict(mlp_up=wt.gather("mlp_down0"), mlp_down=wt.gather("attn_w_q"), ple_gate=wt.gather("w_kv")))

    (kvn,) = _layer_norm("kv_ln", x1, small["kv_ln_g"], small["kv_ln_b"], (BF16,))
    riders = [wt.gather("ple_proj1", "ple_gate1"), wt.gather("mlp_up1")]
    (kv,) = _mm_nn("kv_proj", kvn, wt["w_kv"], (F32,), carry=riders[0])
    (q,) = _mm_nn("q_proj", x1b, wt["attn_w_q"], (F32,), carry=riders[1])
    for ex in riders:
        wt.landed(ex)
    cos, sin = _rope_tables(positions)
    row_scratch = [pltpu.VMEM((ROW_TILE, LANE), F32)]

    def rot_kv(i, o, r, scr):
        cs, sn = i[2][...], i[3][...]
        for h in range(nh):
            hs = slice(h * HEAD_DIM, (h + 1) * HEAD_DIM)
            for base, val in ((0, _rot(i[0][:, hs], cs, sn)), (N_GROUPS, i[1][:, hs])):
                for g, dil in enumerate(GROUP_DILATIONS):
                    for res, plane in enumerate(_split_rows(scr, val, dil)):
                        o[base + g][res, :, hs] = plane.astype(BF16)

    by_group = [(d, BF16, dil) for dil in GROUP_DILATIONS]
    kv_groups = _rowwise("rotary_kv", rot_kv, [_rows(kv, 0, d), _rows(kv, 1, d), _rows(cos), _rows(sin)], by_group * 2,
                         ts=ROW_TILE, scratch=row_scratch)
    kg, vg = kv_groups[:N_GROUPS], kv_groups[N_GROUPS:]

    def rot_q(i, o, r, scr):
        cs, sn = i[1][...], i[2][...]
        for g, dil in enumerate(GROUP_DILATIONS):
            for h in range(nh):
                hs = slice(h * HEAD_DIM, (h + 1) * HEAD_DIM)
                val = _rot(i[0][:, g * d + h * HEAD_DIM:g * d + (h + 1) * HEAD_DIM], cs, sn)
                for res, plane in enumerate(_split_rows(scr, val, dil)):
                    o[g][res, :, hs] = plane.astype(BF16)

    qg = _rowwise("rotary_q", rot_q, [_rows(q), _rows(cos), _rows(sin)], by_group, ts=ROW_TILE, scratch=row_scratch)

    og, lg = zip(*[_attn_fwd(f"attn_fwd_{g}", qg[g], kg[g], vg[g]) for g in range(N_GROUPS)])

    def merge(i, o, r, scr):
        lses = []
        for g, dil in enumerate(GROUP_DILATIONS):
            lses.append(_join_rows(scr, [i[N_GROUPS + g][res] for res in range(dil)]))
        top = functools.reduce(jnp.maximum, lses)
        es = [jnp.exp(l - top) for l in lses]
        den = functools.reduce(lambda a, b: a + b, es)
        total = top + jnp.log(den)
        for g, dil in enumerate(GROUP_DILATIONS):
            for res, plane in enumerate(_split_rows(scr, total, dil)):
                o[2 + g][res] = plane
        ws = [e / den for e in es]
        for h in range(nh):
            hs = slice(h * HEAD_DIM, (h + 1) * HEAD_DIM)
            out = jnp.zeros((ROW_TILE, HEAD_DIM), F32)
            for g, dil in enumerate(GROUP_DILATIONS):
                og_h = _join_rows(scr, [i[g][res, :, hs] for res in range(dil)])
                out = out + ws[g][:, h:h + 1] * og_h
            o[0][:, hs] = out
            o[1][:, hs] = out.astype(BF16)

    merged = _rowwise("attn_merge", merge, [_by_residue(t) for t in og + lg],
                      [(d, F32), (d, BF16)] + [(LANE, F32, dil) for dil in GROUP_DILATIONS], ts=ROW_TILE, scratch=row_scratch)
    o, ob, lse_g = merged[0], merged[1], merged[2:]
    (z1b,) = _mm_nn("attn_out", ob, wt["attn_w_o"], (F32,), epilogue=lambda acc, xt: (ALPHA * xt + acc,), extras=[(x1, "tile")])
    y, _, sv1 = _mlp_ple_fwd("1", z1b, p[1], small["ln1_g"][1:2], small["ln1_b"][1:2], small["ln2_g"][1:2], small["ln2_b"][1:2], wt,
                             dict(mlp_up=wt.gather("mlp_down1")))

    def loss_fn(i, o, r):
        diff = i[0][...] - i[1][...]
        o[0][...] = diff * (1.0 / d)
        r[0][...] += jnp.broadcast_to(jnp.sum(diff * diff), (1, LANE))

    d_y, sq = _rowwise("loss", loss_fn, [_rows(y), _rows(target)], [(d, F32)], [(1, LANE)])
    loss = 0.5 * sq[0, 0] / d

    d_z1, d_z1b, g1, g_up1 = _mlp_ple_bwd("1", d_y, sv1, small["ln1_g"][1:2], small["ln2_g"][1:2], wt, {})
    g_wo = _mm_tn("g_attn_out", ob, d_z1b, wt["attn_w_o"])
    (d_o,) = _mm_nt("d_attn_out", d_z1b, wt["attn_w_o"], (F32,))

    def dsum_fn(i, o, r, scr):
        stats = jnp.zeros((ROW_TILE, LANE), F32)
        for h in range(nh):
            hs = slice(h * HEAD_DIM, (h + 1) * HEAD_DIM)
            dout = i[0][:, hs]
            stats = jnp.where(_lane(h, stats.shape), jnp.sum(dout * i[1][:, hs], axis=1, keepdims=True), stats)
            for g, dil in enumerate(GROUP_DILATIONS):
                for res, plane in enumerate(_split_rows(scr, dout, dil)):
                    o[g][res, :, hs] = plane.astype(BF16)
        for g, dil in enumerate(GROUP_DILATIONS):
            for res, plane in enumerate(_split_rows(scr, stats, dil)):
                o[N_GROUPS + g][res] = plane

    res_ = _rowwise("attn_dsum", dsum_fn, [_rows(d_o), _rows(o)], by_group + [(LANE, F32, dil) for dil in GROUP_DILATIONS],
                    ts=ROW_TILE, scratch=row_scratch)
    dog, dsum_g = res_[:N_GROUPS], res_[N_GROUPS:]
    dqs, dks, dvs = [], [], []
    riders = [wt.scatter({"mlp_up1": (g_up1, (1, 2, 4))}), wt.scatter({"mlp_up1": (g_up1, (2, 3, 4))}),
              wt.scatter({"mlp_up1": (g_up1, (3, 4, 4))}), wt.scatter({"attn_w_o": (g_wo, WHOLE)}), None, None]
    for g in range(N_GROUPS):
        dqs.append(_attn_dq(f"attn_dq_{g}", qg[g], kg[g], vg[g], dog[g], lse_g[g], dsum_g[g], carry=riders[2 * g]))
        dk, dv = _attn_dkv(f"attn_dkv_{g}", qg[g], kg[g], vg[g], dog[g], lse_g[g], dsum_g[g], carry=riders[2 * g + 1])
        dks.append(dk)
        dvs.append(dv)
    for ex in riders:
        if ex is not None:
            wt.received(ex)

    def unrot_q(i, o, r, scr):
        cs, sn = i[N_GROUPS][...], i[N_GROUPS + 1][...]
        for g, dil in enumerate(GROUP_DILATIONS):
            for h in range(nh):
                hs = slice(h * HEAD_DIM, (h + 1) * HEAD_DIM)
                dq = _join_rows(scr, [i[g][res, :, hs] for res in range(dil)])
                o[0][:, g * d + h * HEAD_DIM:g * d + (h + 1) * HEAD_DIM] = _unrot(dq, cs, sn).astype(BF16)

    (d_q,) = _rowwise("rotary_q_bwd", unrot_q, [_by_residue(t) for t in dqs] + [_rows(cos), _rows(sin)], [(N_GROUPS * d, BF16)],
                      ts=ROW_TILE, scratch=row_scratch)

    def unrot_kv(i, o, r, scr):
        cs, sn = i[2 * N_GROUPS][...], i[2 * N_GROUPS + 1][...]
        for h in range(nh):
            hs = slice(h * HEAD_DIM, (h + 1) * HEAD_DIM)
            for base in (0, N_GROUPS):
                tot = jnp.zeros((ROW_TILE, HEAD_DIM), F32)
                for g, dil in enumerate(GROUP_DILATIONS):
                    tot = tot + _join_rows(scr, [i[base + g][res, :, hs] for res in range(dil)])
                if base == 0:
                    o[0][:, hs] = _unrot(tot, cs, sn).astype(BF16)
                else:
                    o[0][:, d + h * HEAD_DIM:d + (h + 1) * HEAD_DIM] = tot.astype(BF16)

    (d_kv,) = _rowwise("rotary_kv_bwd", unrot_kv, [_by_residue(t) for t in dks + dvs] + [_rows(cos), _rows(sin)], [(2 * d, BF16)],
                       ts=ROW_TILE, scratch=row_scratch)
    g_wq = _mm_tn("g_q_proj", x1b, d_q, wt["attn_w_q"])
    ex = wt.scatter({"attn_w_q": (g_wq, (0, 1, 2))})
    g_wkv = _mm_tn("g_kv_proj", kvn, d_kv, wt["w_kv"], carry=ex)
    wt.received(ex)
    ex = wt.scatter({"attn_w_q": (g_wq, (1, 2, 2))})
    (d_x1a,) = _mm_nt("d_q_proj", d_q, wt["attn_w_q"], (F32,), epilogue=lambda acc, dz: (ALPHA * dz + acc,), extras=[(d_z1, "tile")],
                      carry=ex)
    wt.received(ex)
    ex = wt.scatter({"w_kv": (g_wkv, (0, 1, 2))})
    (d_kvn,) = _mm_nt("d_kv_proj", d_kv, wt["w_kv"], (F32,), carry=ex)
    wt.received(ex)
    d_x1, _, g_kv_ln_g, g_kv_ln_b = _layer_norm_bwd("kv_ln_bwd", x1, small["kv_ln_g"], d_kvn, extra=d_x1a)

    d_z1, d_z1b, g0, g_up0 = _mlp_ple_bwd("0", d_x1, sv0, small["ln1_g"][0:1], small["ln2_g"][0:1], wt,
                                          {"w_kv": (g_wkv, (1, 2, 2))})
    g_wout = _mm_tn("g_conv_out", sb, d_z1b, wt["conv_w_out"])
    (d_s,) = _mm_nt("d_conv_out", d_z1b, wt["conv_w_out"], (F32,))

    def ln_silu_bwd(i, o, r):
        cx, gn, bn, ds_ = i[0][...], i[1][...], i[2][...], i[3][...]
        xhat, _ = _ln_stats(cx)
        n = xhat * gn + bn
        sg = _sigmoid(n)
        dn = ds_ * (sg * (1.0 + n * (1.0 - sg)))
        dx, dg, db = _ln_bwd_tile(cx, gn, dn)
        o[0][...] = dx
        r[0][...] += dg
        r[1][...] += db

    d_c, g_cln_g, g_cln_b = _rowwise("conv_ln_silu_bwd", ln_silu_bwd,
                                     [_rows(c), _full(small["conv_ln_g"]), _full(small["conv_ln_b"]), _rows(d_s)],
                                     [(d, F32)], [(1, d), (1, d)])
    ex = wt.scatter({"mlp_up0": (g_up0, (1, 4, 4))})
    d_glu, g_dw, g_dwb = _conv_bwd("dwconv_bwd", glu, d_c, small["conv_dw"], carry=ex)
    wt.received(ex)

    def glu_bwd(i, o, r):
        a, gt, dg_ = i[0][...], i[1][...], i[2][...]
        sg = _sigmoid(gt)
        da = dg_ * sg
        dgate = dg_ * a * sg * (1.0 - sg)
        o[0][:, 0:d] = da.astype(BF16)
        o[0][:, d:2 * d] = dgate.astype(BF16)
        r[0][:, 0:d] += jnp.sum(da, axis=0, keepdims=True)
        r[0][:, d:2 * d] += jnp.sum(dgate, axis=0, keepdims=True)

    d_u, g_bin = _rowwise("glu_bwd", glu_bwd, [_rows(u, 0, d), _rows(u, 1, d), _rows(d_glu)], [(2 * d, BF16)], [(1, 2 * d)])
    ex = wt.scatter({"conv_w_out": (g_wout, WHOLE)})
    g_win = _mm_tn("g_conv_in", x, d_u, wt["conv_w_in"], carry=ex)
    wt.received(ex)
    rows = [g_bin.reshape(2, d), g_dw, g_dwb, g_cln_g, g_cln_b, g_kv_ln_g, g_kv_ln_b]
    rows += [jnp.concatenate([g0[n], g1[n]], axis=0) for n in ("ln1_g", "ln1_b", "ln2_g", "ln2_b")]
    rows, offsets = _stack_rows(rows)
    ex = wt.scatter({"conv_w_in": (g_win, WHOLE)}, gathers=[(rows, False)])
    (grad_x,) = _mm_nt("d_conv_in", d_u, wt["conv_w_in"], (F32,), epilogue=lambda acc, dz: (ALPHA * dz + acc,), extras=[(d_z1, "tile")],
                       carry=ex)
    wt.received(ex)
    return loss, grad_x, ex.gathered[0], offsets


BIG = ("conv_w_in", "conv_w_out", "w_kv", "attn_w_q", "attn_w_o", "mlp_up", "mlp_down", "ple_proj", "ple_gate")
COLUMN_SHARDED = ("conv_w_in", "w_kv", "attn_w_q", "mlp_up", "ple_proj")
WEIGHTS = ("conv_w_in", "conv_b_in", "conv_dw", "conv_dw_b", "conv_ln_g", "conv_ln_b", "conv_w_out", "kv_ln_g", "kv_ln_b",
           "w_kv", "attn_w_q", "attn_w_o", "ln1_g", "ln1_b", "mlp_up", "mlp_down", "ln2_g", "ln2_b", "ple_proj", "ple_gate")


def kernel(x, p, positions, conv_w_in, conv_b_in, conv_dw, conv_dw_b, conv_ln_g, conv_ln_b, conv_w_out, kv_ln_g, kv_ln_b, w_kv, attn_w_q, attn_w_o, ln1_g, ln1_b, mlp_up, mlp_down, ln2_g, ln2_b, ple_proj, ple_gate, loss_target, m_conv_w_in, m_conv_b_in, m_conv_dw, m_conv_dw_b, m_conv_ln_g, m_conv_ln_b, m_conv_w_out, m_kv_ln_g, m_kv_ln_b, m_w_kv, m_attn_w_q, m_attn_w_o, m_ln1_g, m_ln1_b, m_mlp_up, m_mlp_down, m_ln2_g, m_ln2_b, m_ple_proj, m_ple_gate, v_conv_w_in, v_conv_b_in, v_conv_dw, v_conv_dw_b, v_conv_ln_g, v_conv_ln_b, v_conv_w_out, v_kv_ln_g, v_kv_ln_b, v_w_kv, v_attn_w_q, v_attn_w_o, v_ln1_g, v_ln1_b, v_mlp_up, v_mlp_down, v_ln2_g, v_ln2_b, v_ple_proj, v_ple_gate):
    given = dict(locals())
    wts = {n: given[n] for n in WEIGHTS}
    moms = {n: given["m_" + n] for n in WEIGHTS}
    vels = {n: given["v_" + n] for n in WEIGHTS}
    s, d = x.shape[1], x.shape[2]
    shard = d // N_DEV
    me = 4 * lax.axis_index("x") + 2 * lax.axis_index("y") + lax.axis_index("c")

    def layers_of(a):
        return a.reshape((-1,) + a.shape[-2:])

    shards = {}
    for n in BIG:
        w3 = layers_of(wts[n])
        for ly in range(w3.shape[0]):
            shards[n + str(ly) if w3.shape[0] > 1 else n] = w3[ly].astype(BF16)
    wt = Weights(shards)
    pack, at = _stack_rows([wts["conv_b_in"].reshape(2, shard), wts["conv_dw"].reshape(CONV_WIDTH, shard),
                            wts["conv_dw_b"], wts["conv_ln_g"], wts["conv_ln_b"]])
    ex = Exchange(gathers=[(shards["conv_w_in"], True), (pack, False)], keys=["conv_w_in"])
    _exchange_alone("gather_first", ex)
    wt.landed(ex)
    packed = ex.gathered[1]
    small = dict(conv_b_in=packed[:, at[0]:at[0] + 2].reshape(1, 2 * d), conv_dw=packed[:, at[1]:at[1] + CONV_WIDTH],
                 conv_dw_b=packed[:, at[2]].reshape(1, d), conv_ln_g=packed[:, at[3]].reshape(1, d),
                 conv_ln_b=packed[:, at[4]].reshape(1, d), kv_ln_g=kv_ln_g.reshape(1, d), kv_ln_b=kv_ln_b.reshape(1, d),
                 ln1_g=ln1_g, ln1_b=ln1_b, ln2_g=ln2_g, ln2_b=ln2_b)

    loss, grad_x, all_rows, at = _local_step(x[0], p[:, 0], positions.reshape(s, 1), loss_target[0], wt, small)
    loss = lax.psum(loss, ("x", "y", "c"))

    out = {}
    for n in BIG:
        w3 = layers_of(wts[n])
        keys = [n + str(ly) if w3.shape[0] > 1 else n for ly in range(w3.shape[0])]
        res = _adamw_big("adamw_" + n, [wt.chunks(k) for k in keys], w3, layers_of(moms[n]), layers_of(vels[n]))
        out[n] = [r.reshape(wts[n].shape) for r in res]
    tot = _sum_slots("sum_small_grads", all_rows)
    mine = lax.dynamic_slice_in_dim(tot, me * shard, shard, axis=1)
    b_in = lax.dynamic_slice_in_dim(tot[at[0]:at[0] + 2].reshape(1, 2 * d), me * 2 * shard, 2 * shard, axis=1)
    g_small = dict(conv_b_in=b_in, conv_dw=mine[at[1]:at[1] + CONV_WIDTH].reshape(conv_dw.shape), conv_dw_b=mine[at[2]:at[2] + 1],
                   conv_ln_g=mine[at[3]:at[3] + 1], conv_ln_b=mine[at[4]:at[4] + 1], kv_ln_g=tot[at[5]], kv_ln_b=tot[at[6]])
    for j, n in enumerate(("ln1_g", "ln1_b", "ln2_g", "ln2_b")):
        g_small[n] = tot[at[7 + j]:at[7 + j] + DEPTH]
    order = [n for n in WEIGHTS if n not in BIG]

    def flat(t):
        return _stack_rows([t[n].reshape(-1, shard) for n in order])

    (w_s, at), (g_s, _), (m_s, _), (v_s, _) = flat(wts), flat(g_small), flat(moms), flat(vels)
    d_s, m_s, v_s = _adamw_small("adamw_small", w_s, g_s, m_s, v_s)
    for n, a in zip(order, at):
        nrow = wts[n].size // shard
        out[n] = [g_small[n].reshape(wts[n].shape)] + [t[a:a + nrow].reshape(wts[n].shape) for t in (d_s, m_s, v_s)]
    return (loss, grad_x[None], *[out[n][0] for n in WEIGHTS], *[out[n][1] for n in WEIGHTS],
            *[out[n][2] for n in WEIGHTS], *[out[n][3] for n in WEIGHTS])
```

```python
import functools

import numpy as np
import jax
import jax.numpy as jnp
from jax import lax
from jax.experimental import pallas as pl
from jax.experimental.pallas import tpu as pltpu

F32, BF16 = jnp.float32, jnp.bfloat16

N_DEV = 8
HEAD_DIM = 128
ATTN_BLOCK = 128
GROUP_DILATIONS = (1, 4, 16)
N_GROUPS = len(GROUP_DILATIONS)
CONV_WIDTH = 31
CONV_HALO = 32
CONV_ROWS = 64
ROPE_THETA = 10000.0
LN_EPS = 1e-5
DEPTH = 2
ALPHA = (2 * DEPTH) ** 0.25
ADAM_LR, ADAM_B1, ADAM_B2, ADAM_EPS, ADAM_WD, ADAM_STEP = 0.001, 0.9, 0.999, 1e-08, 0.01, 10
NEG = -1e30
V7X_VMEM_LIMIT = 56 * 2 ** 20
LANE = 128
SUBLANES = 8
ROW_TILE = 256
GRAD_DTYPE = BF16

MESH = pl.DeviceIdType.MESH
ANY = pl.BlockSpec(memory_space=pl.ANY)


def _params(*sem):
    return pltpu.CompilerParams(dimension_semantics=sem or None, vmem_limit_bytes=V7X_VMEM_LIMIT)


def _sigmoid(x):
    return 1.0 / (1.0 + jnp.exp(-x))


def _divisor(n, most):
    best = None
    for t in range(LANE, min(n, most) + 1, LANE):
        if n % t == 0:
            best = t
    assert best is not None, (n, most)
    return best


def _stack_rows(parts):
    out, offsets, at = [], [], 0
    for a in parts:
        pad = -a.shape[0] % SUBLANES
        offsets.append(at)
        out.append(a)
        if pad:
            out.append(jnp.zeros((pad, a.shape[1]), a.dtype))
        at += a.shape[0] + pad
    return jnp.concatenate(out, axis=0), offsets


class Exchange:
    OTHER_CHIPS = (4, 2, 6)

    def __init__(self, gathers=(), scatters=(), keys=()):
        self.gathers, self.g_cols = [a for a, _ in gathers], [c for _, c in gathers]
        self.scatters, self.s_cols, self.s_parts = [s[0] for s in scatters], [s[1] for s in scatters], [s[2] for s in scatters]
        self.keys = list(keys)
        self.n_g, self.n_s = len(self.gathers), len(self.scatters)
        self.n = self.n_g + self.n_s
        self.operands = self.gathers + self.scatters
        self.gathered = self.parts = None

    def rows(self, t):
        a = self.scatters[t]
        first, last, of = self.s_parts[t]
        per = (a.shape[0] if self.s_cols[t] else a.shape[1]) // of
        return first * per, (last - first) * per

    def out_shape(self):
        outs = []
        for a, cols in zip(self.gathers, self.g_cols):
            outs.append(jax.ShapeDtypeStruct((a.shape[0], N_DEV * a.shape[1]) if cols else (N_DEV,) + a.shape, a.dtype))
        for t, (a, cols) in enumerate(zip(self.scatters, self.s_cols)):
            outs.append(jax.ShapeDtypeStruct((N_DEV, self.rows(t)[1], a.shape[1] // N_DEV if cols else a.shape[2]), a.dtype))
        return outs

    def scratch(self):
        dma = pltpu.SemaphoreType.DMA
        return [dma((max(self.n_g, 1) * 7,)), dma((max(self.n_g, 1) * 7,)), dma((max(self.n_s, 1) * 7,)),
                dma((max(self.n_s, 1) * 7,)), dma((self.n,))]

    def take(self, results):
        self.gathered, self.parts = list(results[:self.n_g]), list(results[self.n_g:])

    def _copies(self, ins, outs, sems):
        n_g, n_s = self.n_g, self.n_s
        g_in, s_in, g_out, s_out = ins[:n_g], ins[n_g:], outs[:n_g], outs[n_g:]
        g_send, g_recv, s_send, s_recv, local_sem = sems
        x, y, c = lax.axis_index("x"), lax.axis_index("y"), lax.axis_index("c")

        def peer(k):
            return (1 - x if k & 4 else x, 1 - y if k & 2 else y, 1 - c if k & 1 else c)

        def number(p):
            return 4 * p[0] + 2 * p[1] + p[2]

        me = number((x, y, c))

        def slot(t, j):
            first, count = self.rows(t)
            if self.s_cols[t]:
                width = self.scatters[t].shape[1] // N_DEV
                return s_in[t].at[pl.ds(first, count), pl.ds(pl.multiple_of(j * width, LANE), width)]
            return s_in[t].at[j, pl.ds(first, count)]

        def place(t, j):
            if self.g_cols[t]:
                width = self.gathers[t].shape[1]
                return g_out[t].at[:, pl.ds(pl.multiple_of(j * width, LANE), width)]
            return g_out[t].at[j]

        def local():
            cps = [pltpu.make_async_copy(g_in[t], place(t, me), local_sem.at[t]) for t in range(n_g)]
            return cps + [pltpu.make_async_copy(slot(t, me), s_out[t].at[me], local_sem.at[n_g + t]) for t in range(n_s)]

        def scatter(t, k):
            p = peer(k)
            return pltpu.make_async_remote_copy(
                src_ref=slot(t, number(p)), dst_ref=s_out[t].at[me], send_sem=s_send.at[t * 7 + k - 1],
                recv_sem=s_recv.at[t * 7 + k - 1], device_id=p, device_id_type=MESH)

        def landed(t, k):
            p = peer(k)
            return pltpu.make_async_remote_copy(
                src_ref=slot(t, me), dst_ref=s_out[t].at[number(p)], send_sem=s_send.at[t * 7 + k - 1],
                recv_sem=s_recv.at[t * 7 + k - 1], device_id=p, device_id_type=MESH)

        def gather(t, pair, block, to, src=None):
            slot = place(t, number(block))
            return pltpu.make_async_remote_copy(
                src_ref=slot if src is None else src, dst_ref=slot, send_sem=g_send.at[t * 7 + pair],
                recv_sem=g_recv.at[t * 7 + pair], device_id=to, device_id_type=MESH)

        def first_sends():
            cps = []
            for t in range(n_g):
                cps.append(gather(t, 0, peer(0), peer(1), src=g_in[t]))
                cps += [gather(t, 1 + j, peer(0), peer(k), src=g_in[t]) for j, k in enumerate(self.OTHER_CHIPS)]
            for t in range(n_s):
                cps += [scatter(t, k) for k in range(1, N_DEV)]
            return cps

        return peer, local, landed, gather, first_sends

    def start(self, ins, outs, sems):
        _, local, _, _, first_sends = self._copies(ins, outs, sems)
        for cp in local() + first_sends():
            cp.start()

    def finish(self, ins, outs, sems):
        peer, local, landed, gather, first_sends = self._copies(ins, outs, sems)
        mine, sibling = peer(0), peer(1)
        passed = []
        for j, k in enumerate(self.OTHER_CHIPS):
            for t in range(self.n_g):
                gather(t, 1 + j, peer(k), mine).wait_recv()
                passed.append(gather(t, 4 + j, peer(k), sibling))
                passed[-1].start()
        for t in range(self.n_g):
            gather(t, 0, sibling, mine).wait_recv()
            for j, k in enumerate(self.OTHER_CHIPS):
                gather(t, 4 + j, peer(k ^ 1), mine).wait_recv()
        for t in range(self.n_s):
            for k in range(1, N_DEV):
                landed(t, k).wait_recv()
        for cp in first_sends() + passed:
            cp.wait_send()
        for cp in local():
            cp.wait()


def _exchange_alone(name, ex):
    def body(*refs):
        ins, outs, sems = refs[:ex.n], refs[ex.n:2 * ex.n], refs[2 * ex.n:]
        ex.start(ins, outs, sems)
        ex.finish(ins, outs, sems)

    ex.take(pl.pallas_call(body, name=name, in_specs=[ANY] * ex.n, out_specs=[ANY] * ex.n, out_shape=ex.out_shape(),
                           scratch_shapes=ex.scratch())(*ex.operands))


def _call(name, body, args, *, grid, in_specs, out_specs, out_shape, scratch_shapes=(), sem=(), carry=None):
    if carry is None:
        return pl.pallas_call(body, name=name, grid=grid, in_specs=in_specs, out_specs=out_specs, out_shape=out_shape,
                              scratch_shapes=list(scratch_shapes), compiler_params=_params(*sem))(*args)
    ex = carry
    n_in, n_out, n_scr = len(args), len(out_shape), len(scratch_shapes)

    def carried(*refs):
        ins, cin = refs[:n_in], refs[n_in:n_in + ex.n]
        at = n_in + ex.n
        outs, cout = refs[at:at + n_out], refs[at + n_out:at + n_out + ex.n]
        at += n_out + ex.n
        scr, sems = refs[at:at + n_scr], refs[at + n_scr:]
        ids = [pl.program_id(a) for a in range(len(grid))]
        first = functools.reduce(jnp.logical_and, [i == 0 for i in ids])
        last = functools.reduce(jnp.logical_and, [i == g - 1 for i, g in zip(ids, grid)])

        @pl.when(first)
        def _():
            ex.start(cin, cout, sems)

        body(*ins, *outs, *scr)

        @pl.when(last)
        def _():
            ex.finish(cin, cout, sems)

    res = pl.pallas_call(
        carried, name=name, grid=grid, in_specs=list(in_specs) + [ANY] * ex.n, out_specs=list(out_specs) + [ANY] * ex.n,
        out_shape=list(out_shape) + ex.out_shape(), scratch_shapes=list(scratch_shapes) + ex.scratch(),
        compiler_params=_params(*("arbitrary",) * len(grid)),
    )(*args, *ex.operands)
    ex.take(res[n_out:])
    return res[:n_out]


class W:
    def __init__(self, arr, cols):
        self.arr, self.cols = arr, cols
        self.k, self.n = arr.shape
        self.shard_cols = self.n // N_DEV if cols else self.n


def _matmul(name, grid, operands, specs, dims, tile, extras, outs, out_specs, epilogue, carry=None, sums=0):
    assert grid[2] == 1
    n_ex, n_out = len(extras), len(outs)

    def body(*refs):
        a_ref, b_ref = refs[0], refs[1]
        ex_refs = refs[2:2 + n_ex]
        out_refs = refs[2 + n_ex:2 + n_ex + n_out]
        sum_refs = refs[2 + n_ex + n_out:]
        if sums:
            @pl.when(jnp.logical_and(pl.program_id(0) == 0, pl.program_id(1) == 0))
            def _():
                for r in sum_refs:
                    r[...] = jnp.zeros(r.shape, F32)
        acc = lax.dot_general(a_ref[...].astype(BF16), b_ref[...].astype(BF16), (dims, ((), ())),
                              preferred_element_type=F32)
        res = epilogue(acc, *[r[...] for r in ex_refs]) if epilogue else (acc,) * n_out
        for r, v in zip(out_refs, res[:n_out]):
            r[...] = v.astype(r.dtype)
        for r, v in zip(sum_refs, res[n_out:]):
            r[...] += v

    total = pl.BlockSpec((1, LANE), lambda i, j, c: (0, 0))
    return _call(name, body, list(operands) + [a for a, _ in extras], grid=grid,
                 in_specs=list(specs) + [s for _, s in extras], out_specs=list(out_specs) + [total] * sums,
                 out_shape=list(outs) + [jax.ShapeDtypeStruct((1, LANE), F32)] * sums,
                 sem=("arbitrary",) * 3 if sums else ("parallel", "parallel", "arbitrary"), carry=carry)


def _extra_specs(extras, tm, tn):
    out = []
    for arr, kind in extras:
        if kind == "tile":
            out.append((arr, pl.BlockSpec((tm, tn), lambda i, j, c: (i, j))))
        else:
            out.append((arr, pl.BlockSpec((1, tn), lambda i, j, c: (0, j))))
    return out


def _tile_cols(contraction, streams):
    left = V7X_VMEM_LIMIT - V7X_VMEM_LIMIT // 8 - 2 * 1024 * contraction * 2
    for cols in (1024, 512, 256):
        if 2 * cols * (1024 * 4 * streams + contraction * 2) <= left:
            return cols
    return LANE


def _mm_nn(name, a, w, out_dtypes, epilogue=None, extras=(), carry=None, sums=0):
    m, k = a.shape
    assert k == w.k
    tm = 1024 if a.dtype == BF16 else 512
    tn = _divisor(w.n, _tile_cols(k, len(out_dtypes) + sum(kind == "tile" for _, kind in extras)))
    assert tm * k * a.dtype.itemsize <= 16 * 2 ** 20, (name, tm, k)
    grid = (m // tm, w.n // tn, 1)
    specs = [pl.BlockSpec((tm, k), lambda i, j, c: (i, 0)), pl.BlockSpec((k, tn), lambda i, j, c: (0, j))]
    outs = [jax.ShapeDtypeStruct((m, w.n), d) for d in out_dtypes]
    out_specs = [pl.BlockSpec((tm, tn), lambda i, j, c: (i, j)) for _ in outs]
    return _matmul(name, grid, (a, w.arr), specs, ((1,), (0,)), (tm, tn), _extra_specs(extras, tm, tn), outs, out_specs,
                   epilogue, carry, sums)


def _mm_nt(name, dy, w, out_dtypes, epilogue=None, extras=(), carry=None):
    m, n = dy.shape
    assert n == w.n and dy.dtype == BF16
    tm = 1024
    to = _divisor(w.k, _tile_cols(n, len(out_dtypes) + sum(kind == "tile" for _, kind in extras)))
    assert tm * n * dy.dtype.itemsize <= 16 * 2 ** 20, (name, tm, n)
    grid = (m // tm, w.k // to, 1)
    specs = [pl.BlockSpec((tm, n), lambda i, j, c: (i, 0)), pl.BlockSpec((to, n), lambda i, j, c: (j, 0))]
    outs = [jax.ShapeDtypeStruct((m, w.k), d) for d in out_dtypes]
    out_specs = [pl.BlockSpec((tm, to), lambda i, j, c: (i, j)) for _ in outs]
    return _matmul(name, grid, (dy, w.arr), specs, ((1,), (1,)), (tm, to), _extra_specs(extras, tm, to), outs, out_specs,
                   epilogue, carry)


def _mm_tn(name, a, dy, like, carry=None):
    m, k = a.shape
    n = dy.shape[1]
    assert (k, n) == (like.k, like.n)
    tk = _divisor(k, 1024 if a.dtype == BF16 else 512)
    tn = _divisor(n, 1024)
    grid = (k // tk, n // tn, 1)
    specs = [pl.BlockSpec((m, tk), lambda i, j, c: (0, i)), pl.BlockSpec((m, tn), lambda i, j, c: (0, j))]
    out_specs = [pl.BlockSpec((tk, tn), lambda i, j, c: (i, j))]
    (g,) = _matmul(name, grid, (a, dy), specs, ((0,), (0,)), (tk, tn), [], [jax.ShapeDtypeStruct((k, n), GRAD_DTYPE)],
                   out_specs, None, carry)
    return g if like.cols else g.reshape(N_DEV, k // N_DEV, n)


def _rows(arr, blk=0, width=None):
    return ("rows", arr, blk, width or arr.shape[1])


def _full(arr):
    return ("full", arr)


def _by_residue(arr):
    return ("residue", arr)


def _rowwise(name, fn, ins, outs, reds=(), ts=256, carry=None, scratch=()):
    s = next(i[1].shape[0] if i[0] == "rows" else i[1].shape[0] * i[1].shape[1] for i in ins if i[0] != "full")
    n_in, n_out, n_red = len(ins), len(outs), len(reds)
    in_specs = []
    for i in ins:
        if i[0] == "rows":
            in_specs.append(pl.BlockSpec((ts, i[3]), functools.partial(lambda t, blk: (t, blk), blk=i[2])))
        elif i[0] == "residue":
            d, _, w = i[1].shape
            in_specs.append(pl.BlockSpec((d, ts // d, w), lambda t: (0, t, 0)))
        else:
            in_specs.append(pl.BlockSpec(i[1].shape, functools.partial(lambda t, nd: (0,) * nd, nd=i[1].ndim)))
    out_shape, out_specs = [], []
    for o in outs:
        if len(o) == 2:
            out_shape.append(jax.ShapeDtypeStruct((s, o[0]), o[1]))
            out_specs.append(pl.BlockSpec((ts, o[0]), lambda t: (t, 0)))
        else:
            out_shape.append(jax.ShapeDtypeStruct((o[2], s // o[2], o[0]), o[1]))
            out_specs.append(pl.BlockSpec((o[2], ts // o[2], o[0]), lambda t: (0, t, 0)))
    out_shape += [jax.ShapeDtypeStruct(r, F32) for r in reds]
    out_specs += [pl.BlockSpec(r, lambda t: (0, 0)) for r in reds]

    def body(*refs):
        red_refs = refs[n_in + n_out:n_in + n_out + n_red]
        if red_refs:
            @pl.when(pl.program_id(0) == 0)
            def _():
                for r in red_refs:
                    r[...] = jnp.zeros(r.shape, F32)
        fn(refs[:n_in], refs[n_in:n_in + n_out], red_refs, *refs[n_in + n_out + n_red:])

    return _call(name, body, [i[1] for i in ins], grid=(s // ts,), in_specs=in_specs, out_specs=out_specs,
                 out_shape=out_shape, scratch_shapes=list(scratch), sem=("arbitrary" if reds else "parallel",), carry=carry)


def _ln_stats(x):
    mu = jnp.mean(x, axis=-1, keepdims=True)
    xc = x - mu
    var = jnp.mean(xc * xc, axis=-1, keepdims=True)
    return xc * lax.rsqrt(var + LN_EPS), lax.rsqrt(var + LN_EPS)


def _layer_norm(name, x, g, b, out_dtypes):
    def fn(i, o, r):
        xhat, _ = _ln_stats(i[0][...])
        y = xhat * i[1][...] + i[2][...]
        for ref in o:
            ref[...] = y.astype(ref.dtype)

    return _rowwise(name, fn, [_rows(x), _full(g), _full(b)], [(x.shape[1], d) for d in out_dtypes])


def _ln_bwd_tile(x, g, dy):
    xhat, rstd = _ln_stats(x)
    dyg = dy * g
    m1 = jnp.mean(dyg, axis=-1, keepdims=True)
    m2 = jnp.mean(dyg * xhat, axis=-1, keepdims=True)
    dx = rstd * (dyg - m1 - xhat * m2)
    return dx, jnp.sum(dy * xhat, axis=0, keepdims=True), jnp.sum(dy, axis=0, keepdims=True)


def _layer_norm_bwd(name, x, g, dy):
    d = x.shape[1]

    def fn(i, o, r):
        dx, dg, db = _ln_bwd_tile(i[0][...], i[1][...], i[2][...])
        o[0][...] = dx
        o[1][...] = dx.astype(BF16)
        r[0][...] += dg
        r[1][...] += db

    return _rowwise(name, fn, [_rows(x), _full(g), _rows(dy)], [(d, F32), (d, BF16)], [(1, d), (1, d)])


def _conv_fwd(name, glu, dw, dw_b, ts=512, carry=None):
    s, c = glu.shape
    tc = dw.shape[2]
    per = ts // CONV_HALO
    back = CONV_HALO - (CONV_WIDTH - 1)

    def body(cur_ref, prev_ref, w_ref, b_ref, out_ref, buf):
        i = pl.program_id(1)
        buf[pl.ds(0, CONV_HALO), :] = jnp.where(i > 0, prev_ref[...], 0.0)
        buf[pl.ds(CONV_HALO, ts), :] = cur_ref[...]
        for r0 in range(0, ts, CONV_ROWS):
            acc = jnp.broadcast_to(b_ref[...], (CONV_ROWS, tc))
            for j in range(CONV_WIDTH):
                acc = acc + w_ref[j:j + 1, :] * buf[pl.ds(r0 + back + j, CONV_ROWS), :]
            out_ref[pl.ds(r0, CONV_ROWS), :] = acc

    (out,) = _call(
        name, body, [glu, glu, dw, dw_b], grid=(c // tc, s // ts),
        in_specs=[pl.BlockSpec((ts, tc), lambda j, i: (i, j)),
                  pl.BlockSpec((CONV_HALO, tc), lambda j, i: (jnp.maximum(i * per - 1, 0), j)),
                  pl.BlockSpec((None, CONV_WIDTH, tc), lambda j, i: (j, 0, 0)),
                  pl.BlockSpec((1, tc), lambda j, i: (0, j))],
        out_specs=[pl.BlockSpec((ts, tc), lambda j, i: (i, j))],
        out_shape=[jax.ShapeDtypeStruct((s, c), F32)],
        scratch_shapes=[pltpu.VMEM((ts + CONV_HALO, tc), F32)],
        sem=("parallel", "parallel"), carry=carry)
    return out


def _conv_bwd(name, glu, dc, dw, ts=512, carry=None):
    s, c = glu.shape
    tc = dw.shape[2]
    per = ts // CONV_HALO
    back = CONV_HALO - (CONV_WIDTH - 1)
    last = s // ts - 1

    def body(g_ref, gprev_ref, dc_ref, dcnext_ref, w_ref, dglu_ref, ddw_ref, ddb_ref, gbuf, dbuf):
        i = pl.program_id(1)

        @pl.when(i == 0)
        def _():
            ddw_ref[...] = jnp.zeros(ddw_ref.shape, F32)
            ddb_ref[...] = jnp.zeros(ddb_ref.shape, F32)

        gbuf[pl.ds(0, CONV_HALO), :] = jnp.where(i > 0, gprev_ref[...], 0.0)
        gbuf[pl.ds(CONV_HALO, ts), :] = g_ref[...]
        dbuf[pl.ds(0, ts), :] = dc_ref[...]
        dbuf[pl.ds(ts, CONV_HALO), :] = jnp.where(i < last, dcnext_ref[...], 0.0)
        taps = [jnp.zeros((1, tc), F32)] * CONV_WIDTH
        for r0 in range(0, ts, CONV_ROWS):
            d_here = dbuf[pl.ds(r0, CONV_ROWS), :]
            acc = jnp.zeros((CONV_ROWS, tc), F32)
            for j in range(CONV_WIDTH):
                acc = acc + w_ref[j:j + 1, :] * dbuf[pl.ds(r0 + (CONV_WIDTH - 1) - j, CONV_ROWS), :]
                taps[j] = taps[j] + jnp.sum(d_here * gbuf[pl.ds(r0 + back + j, CONV_ROWS), :], axis=0, keepdims=True)
            dglu_ref[pl.ds(r0, CONV_ROWS), :] = acc
        for j in range(CONV_WIDTH):
            ddw_ref[j:j + 1, :] += taps[j]
        ddb_ref[...] += jnp.sum(dc_ref[...], axis=0, keepdims=True)

    return _call(
        name, body, [glu, glu, dc, dc, dw], grid=(c // tc, s // ts),
        in_specs=[pl.BlockSpec((ts, tc), lambda j, i: (i, j)),
                  pl.BlockSpec((CONV_HALO, tc), lambda j, i: (jnp.maximum(i * per - 1, 0), j)),
                  pl.BlockSpec((ts, tc), lambda j, i: (i, j)),
                  pl.BlockSpec((CONV_HALO, tc), lambda j, i: (jnp.minimum((i + 1) * per, (last + 1) * per - 1), j)),
                  pl.BlockSpec((None, CONV_WIDTH, tc), lambda j, i: (j, 0, 0))],
        out_specs=[pl.BlockSpec((ts, tc), lambda j, i: (i, j)),
                   pl.BlockSpec((CONV_WIDTH, tc), lambda j, i: (0, j)),
                   pl.BlockSpec((1, tc), lambda j, i: (0, j))],
        out_shape=[jax.ShapeDtypeStruct((s, c), F32), jax.ShapeDtypeStruct((CONV_WIDTH, c), F32),
                   jax.ShapeDtypeStruct((1, c), F32)],
        scratch_shapes=[pltpu.VMEM((ts + CONV_HALO, tc), F32), pltpu.VMEM((ts + CONV_HALO, tc), F32)],
        sem=("parallel", "arbitrary"), carry=carry)


def _rope_tables(positions):
    half = HEAD_DIM // 2
    inv = (np.float32(ROPE_THETA) ** (-np.arange(half, dtype=np.float32) * np.float32(2.0 / HEAD_DIM))).astype(np.float32)
    inv_freq = jnp.asarray(np.concatenate([inv, inv])[None, :])
    sign = jnp.asarray(np.concatenate([-np.ones(half, np.float32), np.ones(half, np.float32)])[None, :])

    def fn(i, o, r):
        ang = i[0][...].astype(F32) * i[1][...]
        o[0][...] = jnp.cos(ang)
        o[1][...] = jnp.sin(ang) * i[2][...]

    return _rowwise("rope_tables", fn, [_rows(positions), _full(inv_freq), _full(sign)], [(HEAD_DIM, F32), (HEAD_DIM, F32)], ts=512)


def _rot(x, cos, sin):
    return x * cos + pltpu.roll(x, HEAD_DIM // 2, 1) * sin


def _unrot(x, cos, sin):
    return x * cos - pltpu.roll(x, HEAD_DIM // 2, 1) * sin


def _split_rows(scr, value, d):
    if d == 1:
        return [value]
    scr[...] = value
    return [scr[pl.ds(r, scr.shape[0] // d, stride=d), :] for r in range(d)]


def _join_rows(scr, planes):
    d = len(planes)
    if d == 1:
        return planes[0]
    for r, plane in enumerate(planes):
        scr[pl.ds(r, scr.shape[0] // d, stride=d), :] = plane
    return scr[...]


def _lane(h, shape):
    return lax.broadcasted_iota(jnp.int32, shape, 1) == h


def _attn_specs(width):
    cur = pl.BlockSpec((None, ATTN_BLOCK, width), lambda r, n: (r, n, 0))
    prev = pl.BlockSpec((None, ATTN_BLOCK, width), lambda r, n: (r, jnp.maximum(n - 1, 0), 0))
    return cur, prev


def _masks(n):
    row = lax.broadcasted_iota(jnp.int32, (ATTN_BLOCK, ATTN_BLOCK), 0)
    col = lax.broadcasted_iota(jnp.int32, (ATTN_BLOCK, ATTN_BLOCK), 1)
    return col <= row, jnp.logical_and(col >= row, n > 0)


_NT = (((1,), (1,)), ((), ()))
_TN = (((0,), (0,)), ((), ()))
_NN = (((1,), (0,)), ((), ()))


def _attn_fwd(name, q, k, v, carry=None):
    dil, ln, d = k.shape
    nh = d // HEAD_DIM
    nb = ln // ATTN_BLOCK
    scale = HEAD_DIM ** -0.5

    def body(q_ref, kc_ref, kp_ref, vc_ref, vp_ref, o_ref, l_ref):
        mask_c, mask_p = _masks(pl.program_id(1))
        mask = jnp.concatenate([mask_p, mask_c], axis=1)
        stats = jnp.zeros((ATTN_BLOCK, LANE), F32)
        for h in range(nh):
            hs = slice(h * HEAD_DIM, (h + 1) * HEAD_DIM)
            keys = jnp.concatenate([kp_ref[:, hs], kc_ref[:, hs]], axis=0)
            vals = jnp.concatenate([vp_ref[:, hs], vc_ref[:, hs]], axis=0)
            sc = jnp.where(mask, lax.dot_general(q_ref[:, hs], keys, _NT, preferred_element_type=F32) * scale, NEG)
            m = jnp.max(sc, axis=1, keepdims=True)
            p = jnp.exp(sc - m)
            l = jnp.sum(p, axis=1, keepdims=True)
            o_ref[:, hs] = lax.dot_general(p.astype(BF16), vals, _NN, preferred_element_type=F32) / l
            stats = jnp.where(_lane(h, stats.shape), m + jnp.log(l), stats)
        l_ref[...] = stats

    (cur, prev), (stat, _) = _attn_specs(d), _attn_specs(LANE)
    return _call(name, body, [q, k, k, v, v], grid=(dil, nb), in_specs=[cur, cur, prev, cur, prev], out_specs=[cur, stat],
                 out_shape=[jax.ShapeDtypeStruct((dil, ln, d), F32), jax.ShapeDtypeStruct((dil, ln, LANE), F32)],
                 sem=("parallel", "parallel"), carry=carry)


def _attn_dq(name, q, k, v, do, lse, dsum, carry=None):
    dil, ln, d = k.shape
    nh = d // HEAD_DIM
    nb = ln // ATTN_BLOCK
    scale = HEAD_DIM ** -0.5

    def body(q_ref, kc_ref, kp_ref, vc_ref, vp_ref, do_ref, l_ref, d_ref, dq_ref):
        mask_c, mask_p = _masks(pl.program_id(1))
        mask = jnp.concatenate([mask_p, mask_c], axis=1)
        for h in range(nh):
            hs = slice(h * HEAD_DIM, (h + 1) * HEAD_DIM)
            keys = jnp.concatenate([kp_ref[:, hs], kc_ref[:, hs]], axis=0)
            vals = jnp.concatenate([vp_ref[:, hs], vc_ref[:, hs]], axis=0)
            sc = lax.dot_general(q_ref[:, hs], keys, _NT, preferred_element_type=F32) * scale
            p = jnp.where(mask, jnp.exp(jnp.where(mask, sc, NEG) - l_ref[:, h:h + 1]), 0.0)
            dp = lax.dot_general(do_ref[:, hs], vals, _NT, preferred_element_type=F32)
            ds = p * (dp - d_ref[:, h:h + 1])
            dq_ref[:, hs] = lax.dot_general(ds.astype(BF16), keys, _NN, preferred_element_type=F32) * scale

    (cur, prev), (stat, _) = _attn_specs(d), _attn_specs(LANE)
    (dq,) = _call(name, body, [q, k, k, v, v, do, lse, dsum], grid=(dil, nb),
                  in_specs=[cur, cur, prev, cur, prev, cur, stat, stat], out_specs=[cur],
                  out_shape=[jax.ShapeDtypeStruct((dil, ln, d), F32)], sem=("parallel", "parallel"), carry=carry)
    return dq


def _attn_dkv(name, q, k, v, do, lse, dsum, carry=None):
    dil, ln, d = k.shape
    nh = d // HEAD_DIM
    nb = ln // ATTN_BLOCK
    scale = HEAD_DIM ** -0.5

    def body(k_ref, v_ref, qc_ref, qn_ref, doc_ref, don_ref, lc_ref, lnx_ref, dc_ref, dn_ref, dk_ref, dv_ref):
        n = pl.program_id(1)
        row = lax.broadcasted_iota(jnp.int32, (ATTN_BLOCK, ATTN_BLOCK), 0)
        col = lax.broadcasted_iota(jnp.int32, (ATTN_BLOCK, ATTN_BLOCK), 1)
        mask = jnp.concatenate([row <= col, jnp.logical_and(row >= col, n < nb - 1)], axis=1)
        lse_t = jnp.concatenate([lc_ref[...].T, lnx_ref[...].T], axis=1)
        dsum_t = jnp.concatenate([dc_ref[...].T, dn_ref[...].T], axis=1)
        for h in range(nh):
            hs = slice(h * HEAD_DIM, (h + 1) * HEAD_DIM)
            qs = jnp.concatenate([qc_ref[:, hs], qn_ref[:, hs]], axis=0)
            douts = jnp.concatenate([doc_ref[:, hs], don_ref[:, hs]], axis=0)
            sc = lax.dot_general(k_ref[:, hs], qs, _NT, preferred_element_type=F32) * scale
            p = jnp.where(mask, jnp.exp(jnp.where(mask, sc, NEG) - lse_t[h:h + 1, :]), 0.0)
            dp = lax.dot_general(v_ref[:, hs], douts, _NT, preferred_element_type=F32)
            ds = p * (dp - dsum_t[h:h + 1, :])
            dv_ref[:, hs] = lax.dot_general(p.astype(BF16), douts, _NN, preferred_element_type=F32)
            dk_ref[:, hs] = lax.dot_general(ds.astype(BF16), qs, _NN, preferred_element_type=F32) * scale

    def specs(width):
        cur = pl.BlockSpec((None, ATTN_BLOCK, width), lambda r, n: (r, n, 0))
        nxt = pl.BlockSpec((None, ATTN_BLOCK, width), lambda r, n: (r, jnp.minimum(n + 1, nb - 1), 0))
        return cur, nxt

    (cur, nxt), (stat, stat_next) = specs(d), specs(LANE)
    return _call(name, body, [k, v, q, q, do, do, lse, lse, dsum, dsum], grid=(dil, nb),
                 in_specs=[cur, cur, cur, nxt, cur, nxt, stat, stat_next, stat, stat_next], out_specs=[cur, cur],
                 out_shape=[jax.ShapeDtypeStruct((dil, ln, d), F32)] * 2, sem=("parallel", "parallel"), carry=carry)


def _adamw_tile(w, g, m, v):
    m = ADAM_B1 * m + (1.0 - ADAM_B1) * g
    v = ADAM_B2 * v + (1.0 - ADAM_B2) * (g * g)
    m_hat = m / (1.0 - ADAM_B1 ** ADAM_STEP)
    v_hat = v / (1.0 - ADAM_B2 ** ADAM_STEP)
    delta = -ADAM_LR * (m_hat / (jnp.sqrt(v_hat) + ADAM_EPS) + ADAM_WD * w)
    return delta, m, v


def _adamw_big(name, parts, w, m, v):
    layers, r, c = w.shape
    assert len(parts) == layers and all(sum(ch.shape[1] for ch in per_layer) == r for per_layer in parts)
    every = [ch for per_layer in parts for ch in per_layer]
    per_row = 2 * c * (len(every) * N_DEV * every[0].dtype.itemsize + 7 * 4)
    tr = 16
    while tr * 2 <= min(min(ch.shape[1] for ch in every), V7X_VMEM_LIMIT // 2 // per_row) and all(ch.shape[1] % (tr * 2) == 0 for ch in every):
        tr *= 2
    pieces = []
    for ly, per_layer in enumerate(parts):
        at = 0
        for ch in per_layer:
            pieces.append((ly, at, ch.shape[1] // tr, ch))
            at += ch.shape[1] // tr

    def within(layer, i, ly, first, tiles):
        return jnp.logical_and(layer == ly, jnp.logical_and(i >= first, i < first + tiles))

    def body(*refs):
        part_refs = refs[:len(pieces)]
        w_ref, m_ref, v_ref, g_out, d_out, m_out, v_out = refs[len(pieces):]
        layer, i = pl.program_id(0), pl.program_id(1)
        for (ly, first, tiles, _), part_ref in zip(pieces, part_refs):
            @pl.when(within(layer, i, ly, first, tiles))
            def _(part_ref=part_ref):
                g = part_ref[0].astype(F32)
                for dev in range(1, N_DEV):
                    g = g + part_ref[dev].astype(F32)
                delta, mn, vn = _adamw_tile(w_ref[...], g, m_ref[...], v_ref[...])
                g_out[...] = g
                d_out[...] = delta
                m_out[...] = mn
                v_out[...] = vn

    def part_index(layer, i, ly, first, tiles):
        return (0, jnp.where(within(layer, i, ly, first, tiles), i - first, 0), 0)

    own = pl.BlockSpec((None, tr, c), lambda ly, i: (ly, i, 0))
    part_specs = [pl.BlockSpec((N_DEV, tr, c), functools.partial(part_index, ly=ly, first=first, tiles=tiles))
                  for ly, first, tiles, _ in pieces]
    return _call(name, body, [ch for _, _, _, ch in pieces] + [w, m, v], grid=(layers, r // tr), in_specs=part_specs + [own] * 3,
                 out_specs=[own] * 4, out_shape=[jax.ShapeDtypeStruct(w.shape, F32)] * 4, sem=("parallel", "parallel"))


def _sum_slots(name, slots):
    _, r, c = slots.shape

    def body(s_ref, o_ref):
        g = s_ref[0]
        for j in range(1, N_DEV):
            g = g + s_ref[j]
        o_ref[...] = g

    return pl.pallas_call(body, name=name, out_shape=jax.ShapeDtypeStruct((r, c), F32),
                          compiler_params=_params())(slots)


def _adamw_small(name, w, g, m, v):
    def body(w_ref, g_ref, m_ref, v_ref, d_out, m_out, v_out):
        delta, mn, vn = _adamw_tile(w_ref[...], g_ref[...], m_ref[...], v_ref[...])
        d_out[...] = delta
        m_out[...] = mn
        v_out[...] = vn

    return pl.pallas_call(body, name=name, out_shape=[jax.ShapeDtypeStruct(w.shape, F32)] * 3,
                          compiler_params=_params())(w, g, m, v)


class Weights:
    def __init__(self, shards):
        self.shards, self.full, self.parts = shards, {}, {}

    @staticmethod
    def by_columns(key):
        return key.rstrip("01") in COLUMN_SHARDED

    def gather(self, *keys):
        return Exchange(gathers=[(self.shards[k], self.by_columns(k)) for k in keys], keys=keys)

    def landed(self, ex):
        for key, full in zip(ex.keys, ex.gathered):
            cols = self.by_columns(key)
            self.full[key] = W(full if cols else full.reshape(-1, full.shape[-1]), cols)

    def scatter(self, grads, gathers=()):
        return Exchange(gathers=gathers, scatters=[(g, self.by_columns(k), part) for k, (g, part) in grads.items()], keys=list(grads))

    def received(self, ex):
        for key, (first, _, of), part in zip(ex.keys, ex.s_parts, ex.parts):
            self.parts.setdefault(key, {})[first / of] = part

    def chunks(self, key):
        return [self.parts[key][j] for j in sorted(self.parts[key])]

    def __getitem__(self, key):
        return self.full[key]


def _mlp_ple_fwd(tag, z1, p_i, ln1_g, ln1_b, ln2_g, ln2_b, wt, carries, target=None):
    h1, h1b = _layer_norm(f"ln1_{tag}", z1, ln1_g, ln1_b, (F32, BF16))
    up, act = _mm_nn(f"mlp_up_{tag}", h1b, wt["mlp_up" + tag], (F32, BF16),
                     epilogue=lambda acc: (acc, jnp.square(jnp.maximum(acc, 0.0))), carry=carries.get("mlp_up"))
    if "mlp_up" in carries:
        wt.landed(carries["mlp_up"])
    (z2,) = _mm_nn(f"mlp_down_{tag}", act, wt["mlp_down" + tag], (F32,), epilogue=lambda acc, h: (ALPHA * h + acc,),
                   extras=[(h1, "tile")], carry=carries.get("mlp_down"))
    if "mlp_down" in carries:
        wt.landed(carries["mlp_down"])
    h2, h2b = _layer_norm(f"ln2_{tag}", z2, ln2_g, ln2_b, (F32, BF16))
    (pe,) = _mm_nn(f"ple_proj_{tag}", p_i, wt["ple_proj" + tag], (F32,))

    saved = dict(z1=z1, h1b=h1b, up=up, act=act, z2=z2, h2b=h2b, p=p_i)
    if target is None:
        def gate(acc, h, e):
            out = h + e * _sigmoid(acc)
            return acc, out, out

        gp, out, outb = _mm_nn(f"ple_gate_{tag}", h2b, wt["ple_gate" + tag], (F32, F32, BF16), epilogue=gate,
                               extras=[(h2, "tile"), (pe, "tile")], carry=carries.get("ple_gate"))
        if "ple_gate" in carries:
            wt.landed(carries["ple_gate"])
        saved.update(pe=pe, gp=gp)
        return out, outb, saved

    width = z1.shape[1]

    def gate_and_loss(acc, h, e, goal):
        sg = _sigmoid(acc)
        diff = h + e * sg - goal
        d_y = diff * (1.0 / width)
        return d_y, d_y * sg, d_y * e * sg * (1.0 - sg), jnp.broadcast_to(jnp.sum(diff * diff), (1, LANE))

    d_y, d_pe, d_gp, sq = _mm_nn(f"ple_gate_{tag}", h2b, wt["ple_gate" + tag], (F32, BF16, BF16), epilogue=gate_and_loss,
                                 extras=[(h2, "tile"), (pe, "tile"), (target, "tile")], sums=1)
    saved.update(d_pe=d_pe, d_gp=d_gp)
    return d_y, 0.5 * sq[0, 0] / width, saved


WHOLE = (0, 1, 1)


def _mlp_ple_bwd(tag, d_out, sv, ln1_g, ln2_g, wt, waiting):
    d_pe, d_gp = sv["d_pe"], sv["d_gp"]
    g_proj = _mm_tn(f"g_ple_proj_{tag}", sv["p"], d_pe, wt["ple_proj" + tag])
    g_gate = _mm_tn(f"g_ple_gate_{tag}", sv["h2b"], d_gp, wt["ple_gate" + tag])
    (d_h2,) = _mm_nt(f"d_ple_gate_{tag}", d_gp, wt["ple_gate" + tag], (F32,), epilogue=lambda acc, dy: (dy + acc,),
                     extras=[(d_out, "tile")])
    d_z2, d_z2b, g_ln2_g, g_ln2_b = _layer_norm_bwd(f"ln2_bwd_{tag}", sv["z2"], ln2_g, d_h2)
    ex = wt.scatter({"ple_proj" + tag: (g_proj, WHOLE), "ple_gate" + tag: (g_gate, WHOLE), **waiting})
    g_down = _mm_tn(f"g_mlp_down_{tag}", sv["act"], d_z2b, wt["mlp_down" + tag], carry=ex)
    wt.received(ex)
    ex = wt.scatter({"mlp_down" + tag: (g_down, (0, 2, 4))})
    (d_up,) = _mm_nt(f"d_mlp_down_{tag}", d_z2b, wt["mlp_down" + tag], (BF16,),
                     epilogue=lambda acc, u: (acc * (2.0 * jnp.maximum(u, 0.0)),), extras=[(sv["up"], "tile")], carry=ex)
    wt.received(ex)
    ex = wt.scatter({"mlp_down" + tag: (g_down, (2, 3, 4))})
    g_up = _mm_tn(f"g_mlp_up_{tag}", sv["h1b"], d_up, wt["mlp_up" + tag], carry=ex)
    wt.received(ex)
    ex = wt.scatter({"mlp_down" + tag: (g_down, (3, 4, 4)), "mlp_up" + tag: (g_up, (0, 1, 4))})
    (d_h1,) = _mm_nt(f"d_mlp_up_{tag}", d_up, wt["mlp_up" + tag], (F32,), epilogue=lambda acc, dz: (ALPHA * dz + acc,),
                     extras=[(d_z2, "tile")], carry=ex)
    wt.received(ex)
    d_z1, d_z1b, g_ln1_g, g_ln1_b = _layer_norm_bwd(f"ln1_bwd_{tag}", sv["z1"], ln1_g, d_h1)
    return d_z1, d_z1b, dict(ln1_g=g_ln1_g, ln1_b=g_ln1_b, ln2_g=g_ln2_g, ln2_b=g_ln2_b), g_up


def _local_step(x, p, positions, target, wt, small):
    s, d = x.shape
    nh = d // HEAD_DIM
    xb, p = x.astype(BF16), p.astype(BF16)

    ex = wt.gather("conv_w_out", "ple_proj0", "ple_gate0")
    (u,) = _mm_nn("conv_in", xb, wt["conv_w_in"], (F32,), epilogue=lambda acc, b: (acc + b,), extras=[(small["conv_b_in"], "row")],
                  carry=ex)
    wt.landed(ex)

    def glu_fn(i, o, r):
        o[0][...] = i[0][...] * _sigmoid(i[1][...])

    (glu,) = _rowwise("glu", glu_fn, [_rows(u, 0, d), _rows(u, 1, d)], [(d, F32)])
    ex = wt.gather("mlp_up0")
    c = _conv_fwd("dwconv", glu, small["conv_dw"], small["conv_dw_b"], carry=ex)
    wt.landed(ex)

    def ln_silu(i, o, r):
        xhat, _ = _ln_stats(i[0][...])
        n = xhat * i[1][...] + i[2][...]
        o[0][...] = (n * _sigmoid(n)).astype(BF16)

    (sb,) = _rowwise("conv_ln_silu", ln_silu, [_rows(c), _full(small["conv_ln_g"]), _full(small["conv_ln_b"])], [(d, BF16)])
    ex = wt.gather("attn_w_o")
    (z1,) = _mm_nn("conv_out", sb, wt["conv_w_out"], (F32,), epilogue=lambda acc, xt: (ALPHA * xt + acc,), extras=[(x, "tile")],
                   carry=ex)
    wt.landed(ex)
    x1, x1b, sv0 = _mlp_ple_fwd("0", z1, p[0], small["ln1_g"][0:1], small["ln1_b"][0:1], small["ln2_g"][0:1], small["ln2_b"][0:1], wt,
                                dict(mlp_up=wt.gather("mlp_down0"), mlp_down=wt.gather("attn_w_q"), ple_gate=wt.gather("w_kv")))

    (kvn,) = _layer_norm("kv_ln", x1, small["kv_ln_g"], small["kv_ln_b"], (BF16,))
    riders = [wt.gather("ple_proj1", "ple_gate1"), wt.gather("mlp_up1")]
    (kv,) = _mm_nn("kv_proj", kvn, wt["w_kv"], (F32,), carry=riders[0])
    (q,) = _mm_nn("q_proj", x1b, wt["attn_w_q"], (F32,), carry=riders[1])
    for ex in riders:
        wt.landed(ex)
    cos, sin = _rope_tables(positions)
    row_scratch = [pltpu.VMEM((ROW_TILE, LANE), F32)]

    def rot_kv(i, o, r, scr):
        cs, sn = i[2][...], i[3][...]
        for h in range(nh):
            hs = slice(h * HEAD_DIM, (h + 1) * HEAD_DIM)
            for base, val in ((0, _rot(i[0][:, hs], cs, sn)), (N_GROUPS, i[1][:, hs])):
                for g, dil in enumerate(GROUP_DILATIONS):
                    for res, plane in enumerate(_split_rows(scr, val, dil)):
                        o[base + g][res, :, hs] = plane.astype(BF16)

    by_group = [(d, BF16, dil) for dil in GROUP_DILATIONS]
    kv_groups = _rowwise("rotary_kv", rot_kv, [_rows(kv, 0, d), _rows(kv, 1, d), _rows(cos), _rows(sin)], by_group * 2,
                         ts=ROW_TILE, scratch=row_scratch)
    kg, vg = kv_groups[:N_GROUPS], kv_groups[N_GROUPS:]

    def rot_q(i, o, r, scr):
        cs, sn = i[1][...], i[2][...]
        for g, dil in enumerate(GROUP_DILATIONS):
            for h in range(nh):
                hs = slice(h * HEAD_DIM, (h + 1) * HEAD_DIM)
                val = _rot(i[0][:, g * d + h * HEAD_DIM:g * d + (h + 1) * HEAD_DIM], cs, sn)
                for res, plane in enumerate(_split_rows(scr, val, dil)):
                    o[g][res, :, hs] = plane.astype(BF16)

    qg = _rowwise("rotary_q", rot_q, [_rows(q), _rows(cos), _rows(sin)], by_group, ts=ROW_TILE, scratch=row_scratch)

    og, lg = zip(*[_attn_fwd(f"attn_fwd_{g}", qg[g], kg[g], vg[g]) for g in range(N_GROUPS)])

    def merge(i, o, r, scr):
        lses = []
        for g, dil in enumerate(GROUP_DILATIONS):
            lses.append(_join_rows(scr, [i[N_GROUPS + g][res] for res in range(dil)]))
        top = functools.reduce(jnp.maximum, lses)
        es = [jnp.exp(l - top) for l in lses]
        den = functools.reduce(lambda a, b: a + b, es)
        total = top + jnp.log(den)
        for g, dil in enumerate(GROUP_DILATIONS):
            for res, plane in enumerate(_split_rows(scr, total, dil)):
                o[2 + g][res] = plane
        ws = [e / den for e in es]
        for h in range(nh):
            hs = slice(h * HEAD_DIM, (h + 1) * HEAD_DIM)
            out = jnp.zeros((ROW_TILE, HEAD_DIM), F32)
            for g, dil in enumerate(GROUP_DILATIONS):
                og_h = _join_rows(scr, [i[g][res, :, hs] for res in range(dil)])
                out = out + ws[g][:, h:h + 1] * og_h
            o[0][:, hs] = out
            o[1][:, hs] = out.astype(BF16)

    merged = _rowwise("attn_merge", merge, [_by_residue(t) for t in og + lg],
                      [(d, F32), (d, BF16)] + [(LANE, F32, dil) for dil in GROUP_DILATIONS], ts=ROW_TILE, scratch=row_scratch)
    o, ob, lse_g = merged[0], merged[1], merged[2:]
    (z1b,) = _mm_nn("attn_out", ob, wt["attn_w_o"], (F32,), epilogue=lambda acc, xt: (ALPHA * xt + acc,), extras=[(x1, "tile")])
    d_y, loss, sv1 = _mlp_ple_fwd("1", z1b, p[1], small["ln1_g"][1:2], small["ln1_b"][1:2], small["ln2_g"][1:2], small["ln2_b"][1:2],
                                  wt, dict(mlp_up=wt.gather("mlp_down1")), target=target)

    d_z1, d_z1b, g1, g_up1 = _mlp_ple_bwd("1", d_y, sv1, small["ln1_g"][1:2], small["ln2_g"][1:2], wt, {})
    g_wo = _mm_tn("g_attn_out", ob, d_z1b, wt["attn_w_o"])
    (d_o,) = _mm_nt("d_attn_out", d_z1b, wt["attn_w_o"], (F32,))

    def dsum_fn(i, o, r, scr):
        stats = jnp.zeros((ROW_TILE, LANE), F32)
        for h in range(nh):
            hs = slice(h * HEAD_DIM, (h + 1) * HEAD_DIM)
            dout = i[0][:, hs]
            stats = jnp.where(_lane(h, stats.shape), jnp.sum(dout * i[1][:, hs], axis=1, keepdims=True), stats)
            for g, dil in enumerate(GROUP_DILATIONS):
                for res, plane in enumerate(_split_rows(scr, dout, dil)):
                    o[g][res, :, hs] = plane.astype(BF16)
        for g, dil in enumerate(GROUP_DILATIONS):
            for res, plane in enumerate(_split_rows(scr, stats, dil)):
                o[N_GROUPS + g][res] = plane

    res_ = _rowwise("attn_dsum", dsum_fn, [_rows(d_o), _rows(o)], by_group + [(LANE, F32, dil) for dil in GROUP_DILATIONS],
                    ts=ROW_TILE, scratch=row_scratch)
    dog, dsum_g = res_[:N_GROUPS], res_[N_GROUPS:]
    dqs, dks, dvs = [], [], []
    riders = [wt.scatter({"mlp_up1": (g_up1, (1, 2, 4))}), wt.scatter({"mlp_up1": (g_up1, (2, 3, 4))}),
              wt.scatter({"mlp_up1": (g_up1, (3, 4, 4))}), wt.scatter({"attn_w_o": (g_wo, WHOLE)}), None, None]
    for g in range(N_GROUPS):
        dqs.append(_attn_dq(f"attn_dq_{g}", qg[g], kg[g], vg[g], dog[g], lse_g[g], dsum_g[g], carry=riders[2 * g]))
        dk, dv = _attn_dkv(f"attn_dkv_{g}", qg[g], kg[g], vg[g], dog[g], lse_g[g], dsum_g[g], carry=riders[2 * g + 1])
        dks.append(dk)
        dvs.append(dv)
    for ex in riders:
        if ex is not None:
            wt.received(ex)

    def unrot_q(i, o, r, scr):
        cs, sn = i[N_GROUPS][...], i[N_GROUPS + 1][...]
        for g, dil in enumerate(GROUP_DILATIONS):
            for h in range(nh):
                hs = slice(h * HEAD_DIM, (h + 1) * HEAD_DIM)
                dq = _join_rows(scr, [i[g][res, :, hs] for res in range(dil)])
                o[0][:, g * d + h * HEAD_DIM:g * d + (h + 1) * HEAD_DIM] = _unrot(dq, cs, sn).astype(BF16)

    (d_q,) = _rowwise("rotary_q_bwd", unrot_q, [_by_residue(t) for t in dqs] + [_rows(cos), _rows(sin)], [(N_GROUPS * d, BF16)],
                      ts=ROW_TILE, scratch=row_scratch)

    def unrot_kv(i, o, r, scr):
        cs, sn = i[2 * N_GROUPS][...], i[2 * N_GROUPS + 1][...]
        for h in range(nh):
            hs = slice(h * HEAD_DIM, (h + 1) * HEAD_DIM)
            for base in (0, N_GROUPS):
                tot = jnp.zeros((ROW_TILE, HEAD_DIM), F32)
                for g, dil in enumerate(GROUP_DILATIONS):
                    tot = tot + _join_rows(scr, [i[base + g][res, :, hs] for res in range(dil)])
                if base == 0:
                    o[0][:, hs] = _unrot(tot, cs, sn).astype(BF16)
                else:
                    o[0][:, d + h * HEAD_DIM:d + (h + 1) * HEAD_DIM] = tot.astype(BF16)

    (d_kv,) = _rowwise("rotary_kv_bwd", unrot_kv, [_by_residue(t) for t in dks + dvs] + [_rows(cos), _rows(sin)], [(2 * d, BF16)],
                       ts=ROW_TILE, scratch=row_scratch)
    g_wq = _mm_tn("g_q_proj", x1b, d_q, wt["attn_w_q"])
    ex = wt.scatter({"attn_w_q": (g_wq, (0, 1, 2))})
    g_wkv = _mm_tn("g_kv_proj", kvn, d_kv, wt["w_kv"], carry=ex)
    wt.received(ex)
    ex = wt.scatter({"attn_w_q": (g_wq, (1, 2, 2))})
    (d_x1a,) = _mm_nt("d_q_proj", d_q, wt["attn_w_q"], (F32,), epilogue=lambda acc, dz: (ALPHA * dz + acc,), extras=[(d_z1, "tile")],
                      carry=ex)
    wt.received(ex)
    ex = wt.scatter({"w_kv": (g_wkv, (0, 1, 2))})
    (d_kvn,) = _mm_nt("d_kv_proj", d_kv, wt["w_kv"], (F32,), carry=ex)
    wt.received(ex)
    def kv_ln_bwd(i, o, r):
        dx, dg, db = _ln_bwd_tile(i[0][...], i[1][...], i[2][...])
        dx = dx + i[3][...]
        sg = _sigmoid(i[5][...])
        o[0][...] = dx
        o[1][...] = (dx * sg).astype(BF16)
        o[2][...] = (dx * i[4][...] * sg * (1.0 - sg)).astype(BF16)
        r[0][...] += dg
        r[1][...] += db

    d_x1, sv0["d_pe"], sv0["d_gp"], g_kv_ln_g, g_kv_ln_b = _rowwise(
        "kv_ln_bwd", kv_ln_bwd, [_rows(x1), _full(small["kv_ln_g"]), _rows(d_kvn), _rows(d_x1a), _rows(sv0["pe"]), _rows(sv0["gp"])],
        [(d, F32), (d, BF16), (d, BF16)], [(1, d), (1, d)], ts=128)

    d_z1, d_z1b, g0, g_up0 = _mlp_ple_bwd("0", d_x1, sv0, small["ln1_g"][0:1], small["ln2_g"][0:1], wt,
                                          {"w_kv": (g_wkv, (1, 2, 2))})
    g_wout = _mm_tn("g_conv_out", sb, d_z1b, wt["conv_w_out"])
    (d_s,) = _mm_nt("d_conv_out", d_z1b, wt["conv_w_out"], (F32,))

    def ln_silu_bwd(i, o, r):
        cx, gn, bn, ds_ = i[0][...], i[1][...], i[2][...], i[3][...]
        xhat, _ = _ln_stats(cx)
        n = xhat * gn + bn
        sg = _sigmoid(n)
        dn = ds_ * (sg * (1.0 + n * (1.0 - sg)))
        dx, dg, db = _ln_bwd_tile(cx, gn, dn)
        o[0][...] = dx
        r[0][...] += dg
        r[1][...] += db

    d_c, g_cln_g, g_cln_b = _rowwise("conv_ln_silu_bwd", ln_silu_bwd,
                                     [_rows(c), _full(small["conv_ln_g"]), _full(small["conv_ln_b"]), _rows(d_s)],
                                     [(d, F32)], [(1, d), (1, d)])
    ex = wt.scatter({"mlp_up0": (g_up0, (1, 4, 4))})
    d_glu, g_dw, g_dwb = _conv_bwd("dwconv_bwd", glu, d_c, small["conv_dw"], carry=ex)
    wt.received(ex)

    def glu_bwd(i, o, r):
        a, gt, dg_ = i[0][...], i[1][...], i[2][...]
        sg = _sigmoid(gt)
        da = dg_ * sg
        dgate = dg_ * a * sg * (1.0 - sg)
        o[0][:, 0:d] = da.astype(BF16)
        o[0][:, d:2 * d] = dgate.astype(BF16)
        r[0][:, 0:d] += jnp.sum(da, axis=0, keepdims=True)
        r[0][:, d:2 * d] += jnp.sum(dgate, axis=0, keepdims=True)

    d_u, g_bin = _rowwise("glu_bwd", glu_bwd, [_rows(u, 0, d), _rows(u, 1, d), _rows(d_glu)], [(2 * d, BF16)], [(1, 2 * d)])
    ex = wt.scatter({"conv_w_out": (g_wout, WHOLE)})
    g_win = _mm_tn("g_conv_in", xb, d_u, wt["conv_w_in"], carry=ex)
    wt.received(ex)
    rows = [g_bin.reshape(2, d), g_dw, g_dwb, g_cln_g, g_cln_b, g_kv_ln_g, g_kv_ln_b]
    rows += [jnp.concatenate([g0[n], g1[n]], axis=0) for n in ("ln1_g", "ln1_b", "ln2_g", "ln2_b")]
    rows, offsets = _stack_rows(rows)
    ex = wt.scatter({"conv_w_in": (g_win, WHOLE)}, gathers=[(rows, False)])
    (grad_x,) = _mm_nt("d_conv_in", d_u, wt["conv_w_in"], (F32,), epilogue=lambda acc, dz: (ALPHA * dz + acc,), extras=[(d_z1, "tile")],
                       carry=ex)
    wt.received(ex)
    return loss, grad_x, ex.gathered[0], offsets


BIG = ("conv_w_in", "conv_w_out", "w_kv", "attn_w_q", "attn_w_o", "mlp_up", "mlp_down", "ple_proj", "ple_gate")
COLUMN_SHARDED = ("conv_w_in", "w_kv", "attn_w_q", "mlp_up", "ple_proj")
WEIGHTS = ("conv_w_in", "conv_b_in", "conv_dw", "conv_dw_b", "conv_ln_g", "conv_ln_b", "conv_w_out", "kv_ln_g", "kv_ln_b",
           "w_kv", "attn_w_q", "attn_w_o", "ln1_g", "ln1_b", "mlp_up", "mlp_down", "ln2_g", "ln2_b", "ple_proj", "ple_gate")


def kernel(x, p, positions, conv_w_in, conv_b_in, conv_dw, conv_dw_b, conv_ln_g, conv_ln_b, conv_w_out, kv_ln_g, kv_ln_b, w_kv, attn_w_q, attn_w_o, ln1_g, ln1_b, mlp_up, mlp_down, ln2_g, ln2_b, ple_proj, ple_gate, loss_target, m_conv_w_in, m_conv_b_in, m_conv_dw, m_conv_dw_b, m_conv_ln_g, m_conv_ln_b, m_conv_w_out, m_kv_ln_g, m_kv_ln_b, m_w_kv, m_attn_w_q, m_attn_w_o, m_ln1_g, m_ln1_b, m_mlp_up, m_mlp_down, m_ln2_g, m_ln2_b, m_ple_proj, m_ple_gate, v_conv_w_in, v_conv_b_in, v_conv_dw, v_conv_dw_b, v_conv_ln_g, v_conv_ln_b, v_conv_w_out, v_kv_ln_g, v_kv_ln_b, v_w_kv, v_attn_w_q, v_attn_w_o, v_ln1_g, v_ln1_b, v_mlp_up, v_mlp_down, v_ln2_g, v_ln2_b, v_ple_proj, v_ple_gate):
    given = dict(locals())
    wts = {n: given[n] for n in WEIGHTS}
    moms = {n: given["m_" + n] for n in WEIGHTS}
    vels = {n: given["v_" + n] for n in WEIGHTS}
    s, d = x.shape[1], x.shape[2]
    shard = d // N_DEV
    me = 4 * lax.axis_index("x") + 2 * lax.axis_index("y") + lax.axis_index("c")

    def layers_of(a):
        return a.reshape((-1,) + a.shape[-2:])

    shards = {}
    for n in BIG:
        w3 = layers_of(wts[n])
        for ly in range(w3.shape[0]):
            shards[n + str(ly) if w3.shape[0] > 1 else n] = w3[ly].astype(BF16)
    wt = Weights(shards)
    pack, at = _stack_rows([wts["conv_b_in"].reshape(2, shard), wts["conv_dw"].reshape(CONV_WIDTH, shard),
                            wts["conv_dw_b"], wts["conv_ln_g"], wts["conv_ln_b"]])
    ex = Exchange(gathers=[(shards["conv_w_in"], True), (pack, False)], keys=["conv_w_in"])
    _exchange_alone("gather_first", ex)
    wt.landed(ex)
    packed = ex.gathered[1]
    small = dict(conv_b_in=packed[:, at[0]:at[0] + 2].reshape(1, 2 * d), conv_dw=packed[:, at[1]:at[1] + CONV_WIDTH],
                 conv_dw_b=packed[:, at[2]].reshape(1, d), conv_ln_g=packed[:, at[3]].reshape(1, d),
                 conv_ln_b=packed[:, at[4]].reshape(1, d), kv_ln_g=kv_ln_g.reshape(1, d), kv_ln_b=kv_ln_b.reshape(1, d),
                 ln1_g=ln1_g, ln1_b=ln1_b, ln2_g=ln2_g, ln2_b=ln2_b)

    loss, grad_x, all_rows, at = _local_step(x[0], p[:, 0], positions.reshape(s, 1), loss_target[0], wt, small)
    loss = lax.psum(loss, ("x", "y", "c"))

    out = {}
    for n in BIG:
        w3 = layers_of(wts[n])
        keys = [n + str(ly) if w3.shape[0] > 1 else n for ly in range(w3.shape[0])]
        res = _adamw_big("adamw_" + n, [wt.chunks(k) for k in keys], w3, layers_of(moms[n]), layers_of(vels[n]))
        out[n] = [r.reshape(wts[n].shape) for r in res]
    tot = _sum_slots("sum_small_grads", all_rows)
    mine = lax.dynamic_slice_in_dim(tot, me * shard, shard, axis=1)
    b_in = lax.dynamic_slice_in_dim(tot[at[0]:at[0] + 2].reshape(1, 2 * d), me * 2 * shard, 2 * shard, axis=1)
    g_small = dict(conv_b_in=b_in, conv_dw=mine[at[1]:at[1] + CONV_WIDTH].reshape(conv_dw.shape), conv_dw_b=mine[at[2]:at[2] + 1],
                   conv_ln_g=mine[at[3]:at[3] + 1], conv_ln_b=mine[at[4]:at[4] + 1], kv_ln_g=tot[at[5]], kv_ln_b=tot[at[6]])
    for j, n in enumerate(("ln1_g", "ln1_b", "ln2_g", "ln2_b")):
        g_small[n] = tot[at[7 + j]:at[7 + j] + DEPTH]
    order = [n for n in WEIGHTS if n not in BIG]

    def flat(t):
        return _stack_rows([t[n].reshape(-1, shard) for n in order])

    (w_s, at), (g_s, _), (m_s, _), (v_s, _) = flat(wts), flat(g_small), flat(moms), flat(vels)
    d_s, m_s, v_s = _adamw_small("adamw_small", w_s, g_s, m_s, v_s)
    for n, a in zip(order, at):
        nrow = wts[n].size // shard
        out[n] = [g_small[n].reshape(wts[n].shape)] + [t[a:a + nrow].reshape(wts[n].shape) for t in (d_s, m_s, v_s)]
    return (loss, grad_x[None], *[out[n][0] for n in WEIGHTS], *[out[n][1] for n in WEIGHTS],
            *[out[n][2] for n in WEIGHTS], *[out[n][3] for n in WEIGHTS])
```

```python
import functools

import numpy as np
import jax
import jax.numpy as jnp
from jax import lax
from jax.experimental import pallas as pl
from jax.experimental.pallas import tpu as pltpu

F32, BF16 = jnp.float32, jnp.bfloat16

N_DEV = 8
HEAD_DIM = 128
ATTN_BLOCK = 128
GROUP_DILATIONS = (1, 4, 16)
N_GROUPS = len(GROUP_DILATIONS)
CONV_WIDTH = 31
CONV_HALO = 32
CONV_ROWS = 64
ROPE_THETA = 10000.0
LN_EPS = 1e-5
DEPTH = 2
ALPHA = (2 * DEPTH) ** 0.25
ADAM_LR, ADAM_B1, ADAM_B2, ADAM_EPS, ADAM_WD, ADAM_STEP = 0.001, 0.9, 0.999, 1e-08, 0.01, 10
NEG = -1e30
V7X_VMEM_LIMIT = 56 * 2 ** 20
LANE = 128
SUBLANES = 8
ROW_TILE = 256
GRAD_DTYPE = BF16

MESH = pl.DeviceIdType.MESH
ANY = pl.BlockSpec(memory_space=pl.ANY)


def _params(*sem):
    return pltpu.CompilerParams(dimension_semantics=sem or None, vmem_limit_bytes=V7X_VMEM_LIMIT)


def _sigmoid(x):
    return 1.0 / (1.0 + jnp.exp(-x))


def _divisor(n, most):
    best = None
    for t in range(LANE, min(n, most) + 1, LANE):
        if n % t == 0:
            best = t
    assert best is not None, (n, most)
    return best


def _stack_rows(parts):
    out, offsets, at = [], [], 0
    for a in parts:
        pad = -a.shape[0] % SUBLANES
        offsets.append(at)
        out.append(a)
        if pad:
            out.append(jnp.zeros((pad, a.shape[1]), a.dtype))
        at += a.shape[0] + pad
    return jnp.concatenate(out, axis=0), offsets


class Exchange:
    OTHER_CHIPS = (4, 2, 6)

    def __init__(self, gathers=(), scatters=(), keys=()):
        self.gathers, self.g_cols = [a for a, _ in gathers], [c for _, c in gathers]
        self.scatters, self.s_cols, self.s_parts = [s[0] for s in scatters], [s[1] for s in scatters], [s[2] for s in scatters]
        self.keys = list(keys)
        self.n_g, self.n_s = len(self.gathers), len(self.scatters)
        self.n = self.n_g + self.n_s
        self.operands = self.gathers + self.scatters
        self.gathered = self.parts = None

    def rows(self, t):
        a = self.scatters[t]
        first, last, of = self.s_parts[t]
        per = (a.shape[0] if self.s_cols[t] else a.shape[1]) // of
        return first * per, (last - first) * per

    def out_shape(self):
        outs = []
        for a, cols in zip(self.gathers, self.g_cols):
            outs.append(jax.ShapeDtypeStruct((a.shape[0], N_DEV * a.shape[1]) if cols else (N_DEV,) + a.shape, a.dtype))
        for t, (a, cols) in enumerate(zip(self.scatters, self.s_cols)):
            outs.append(jax.ShapeDtypeStruct((N_DEV, self.rows(t)[1], a.shape[1] // N_DEV if cols else a.shape[2]), a.dtype))
        return outs

    def scratch(self):
        dma = pltpu.SemaphoreType.DMA
        return [dma((max(self.n_g, 1) * 7,)), dma((max(self.n_g, 1) * 7,)), dma((max(self.n_s, 1) * 7,)),
                dma((max(self.n_s, 1) * 7,)), dma((self.n,))]

    def take(self, results):
        self.gathered, self.parts = list(results[:self.n_g]), list(results[self.n_g:])

    def _copies(self, ins, outs, sems):
        n_g, n_s = self.n_g, self.n_s
        g_in, s_in, g_out, s_out = ins[:n_g], ins[n_g:], outs[:n_g], outs[n_g:]
        g_send, g_recv, s_send, s_recv, local_sem = sems
        x, y, c = lax.axis_index("x"), lax.axis_index("y"), lax.axis_index("c")

        def peer(k):
            return (1 - x if k & 4 else x, 1 - y if k & 2 else y, 1 - c if k & 1 else c)

        def number(p):
            return 4 * p[0] + 2 * p[1] + p[2]

        me = number((x, y, c))

        def slot(t, j):
            first, count = self.rows(t)
            if self.s_cols[t]:
                width = self.scatters[t].shape[1] // N_DEV
                return s_in[t].at[pl.ds(first, count), pl.ds(pl.multiple_of(j * width, LANE), width)]
            return s_in[t].at[j, pl.ds(first, count)]

        def place(t, j):
            if self.g_cols[t]:
                width = self.gathers[t].shape[1]
                return g_out[t].at[:, pl.ds(pl.multiple_of(j * width, LANE), width)]
            return g_out[t].at[j]

        def local():
            cps = [pltpu.make_async_copy(g_in[t], place(t, me), local_sem.at[t]) for t in range(n_g)]
            return cps + [pltpu.make_async_copy(slot(t, me), s_out[t].at[me], local_sem.at[n_g + t]) for t in range(n_s)]

        def scatter(t, k):
            p = peer(k)
            return pltpu.make_async_remote_copy(
                src_ref=slot(t, number(p)), dst_ref=s_out[t].at[me], send_sem=s_send.at[t * 7 + k - 1],
                recv_sem=s_recv.at[t * 7 + k - 1], device_id=p, device_id_type=MESH)

        def landed(t, k):
            p = peer(k)
            return pltpu.make_async_remote_copy(
                src_ref=slot(t, me), dst_ref=s_out[t].at[number(p)], send_sem=s_send.at[t * 7 + k - 1],
                recv_sem=s_recv.at[t * 7 + k - 1], device_id=p, device_id_type=MESH)

        def gather(t, pair, block, to, src=None):
            slot = place(t, number(block))
            return pltpu.make_async_remote_copy(
                src_ref=slot if src is None else src, dst_ref=slot, send_sem=g_send.at[t * 7 + pair],
                recv_sem=g_recv.at[t * 7 + pair], device_id=to, device_id_type=MESH)

        def first_sends():
            cps = []
            for t in range(n_g):
                cps.append(gather(t, 0, peer(0), peer(1), src=g_in[t]))
                cps += [gather(t, 1 + j, peer(0), peer(k), src=g_in[t]) for j, k in enumerate(self.OTHER_CHIPS)]
            for t in range(n_s):
                cps += [scatter(t, k) for k in range(1, N_DEV)]
            return cps

        return peer, local, landed, gather, first_sends

    def start(self, ins, outs, sems):
        _, local, _, _, first_sends = self._copies(ins, outs, sems)
        for cp in local() + first_sends():
            cp.start()

    def finish(self, ins, outs, sems):
        peer, local, landed, gather, first_sends = self._copies(ins, outs, sems)
        mine, sibling = peer(0), peer(1)
        passed = []
        for j, k in enumerate(self.OTHER_CHIPS):
            for t in range(self.n_g):
                gather(t, 1 + j, peer(k), mine).wait_recv()
                passed.append(gather(t, 4 + j, peer(k), sibling))
                passed[-1].start()
        for t in range(self.n_g):
            gather(t, 0, sibling, mine).wait_recv()
            for j, k in enumerate(self.OTHER_CHIPS):
                gather(t, 4 + j, peer(k ^ 1), mine).wait_recv()
        for t in range(self.n_s):
            for k in range(1, N_DEV):
                landed(t, k).wait_recv()
        for cp in first_sends() + passed:
            cp.wait_send()
        for cp in local():
            cp.wait()


def _exchange_alone(name, ex):
    def body(*refs):
        ins, outs, sems = refs[:ex.n], refs[ex.n:2 * ex.n], refs[2 * ex.n:]
        ex.start(ins, outs, sems)
        ex.finish(ins, outs, sems)

    ex.take(pl.pallas_call(body, name=name, in_specs=[ANY] * ex.n, out_specs=[ANY] * ex.n, out_shape=ex.out_shape(),
                           scratch_shapes=ex.scratch())(*ex.operands))


def _call(name, body, args, *, grid, in_specs, out_specs, out_shape, scratch_shapes=(), sem=(), carry=None):
    if carry is None:
        return pl.pallas_call(body, name=name, grid=grid, in_specs=in_specs, out_specs=out_specs, out_shape=out_shape,
                              scratch_shapes=list(scratch_shapes), compiler_params=_params(*sem))(*args)
    ex = carry
    n_in, n_out, n_scr = len(args), len(out_shape), len(scratch_shapes)

    def carried(*refs):
        ins, cin = refs[:n_in], refs[n_in:n_in + ex.n]
        at = n_in + ex.n
        outs, cout = refs[at:at + n_out], refs[at + n_out:at + n_out + ex.n]
        at += n_out + ex.n
        scr, sems = refs[at:at + n_scr], refs[at + n_scr:]
        ids = [pl.program_id(a) for a in range(len(grid))]
        first = functools.reduce(jnp.logical_and, [i == 0 for i in ids])
        last = functools.reduce(jnp.logical_and, [i == g - 1 for i, g in zip(ids, grid)])

        @pl.when(first)
        def _():
            ex.start(cin, cout, sems)

        body(*ins, *outs, *scr)

        @pl.when(last)
        def _():
            ex.finish(cin, cout, sems)

    res = pl.pallas_call(
        carried, name=name, grid=grid, in_specs=list(in_specs) + [ANY] * ex.n, out_specs=list(out_specs) + [ANY] * ex.n,
        out_shape=list(out_shape) + ex.out_shape(), scratch_shapes=list(scratch_shapes) + ex.scratch(),
        compiler_params=_params(*("arbitrary",) * len(grid)),
    )(*args, *ex.operands)
    ex.take(res[n_out:])
    return res[:n_out]


class W:
    def __init__(self, arr, cols):
        self.arr, self.cols = arr, cols
        self.k, self.n = arr.shape
        self.shard_cols = self.n // N_DEV if cols else self.n


def _matmul(name, grid, operands, specs, dims, tile, extras, outs, out_specs, epilogue, carry=None, sums=0):
    assert grid[2] == 1
    n_ex, n_out = len(extras), len(outs)

    def body(*refs):
        a_ref, b_ref = refs[0], refs[1]
        ex_refs = refs[2:2 + n_ex]
        out_refs = refs[2 + n_ex:2 + n_ex + n_out]
        sum_refs = refs[2 + n_ex + n_out:]
        if sums:
            @pl.when(jnp.logical_and(pl.program_id(0) == 0, pl.program_id(1) == 0))
            def _():
                for r in sum_refs:
                    r[...] = jnp.zeros(r.shape, F32)
        acc = lax.dot_general(a_ref[...].astype(BF16), b_ref[...].astype(BF16), (dims, ((), ())),
                              preferred_element_type=F32)
        res = epilogue(acc, *[r[...] for r in ex_refs]) if epilogue else (acc,) * n_out
        for r, v in zip(out_refs, res[:n_out]):
            r[...] = v.astype(r.dtype)
        for r, v in zip(sum_refs, res[n_out:]):
            r[...] += v

    total = pl.BlockSpec((1, LANE), lambda i, j, c: (0, 0))
    return _call(name, body, list(operands) + [a for a, _ in extras], grid=grid,
                 in_specs=list(specs) + [s for _, s in extras], out_specs=list(out_specs) + [total] * sums,
                 out_shape=list(outs) + [jax.ShapeDtypeStruct((1, LANE), F32)] * sums,
                 sem=("arbitrary",) * 3 if sums else ("parallel", "parallel", "arbitrary"), carry=carry)


def _extra_specs(extras, tm, tn):
    out = []
    for arr, kind in extras:
        if kind == "tile":
            out.append((arr, pl.BlockSpec((tm, tn), lambda i, j, c: (i, j))))
        else:
            out.append((arr, pl.BlockSpec((1, tn), lambda i, j, c: (0, j))))
    return out


def _tile_cols(contraction, streams):
    left = V7X_VMEM_LIMIT - V7X_VMEM_LIMIT // 8 - 2 * 1024 * contraction * 2
    for cols in (1024, 512, 256):
        if 2 * cols * (1024 * 4 * streams + contraction * 2) <= left:
            return cols
    return LANE


def _mm_nn(name, a, w, out_dtypes, epilogue=None, extras=(), carry=None, sums=0):
    m, k = a.shape
    assert k == w.k
    tm = 1024 if a.dtype == BF16 else 512
    tn = _divisor(w.n, _tile_cols(k, len(out_dtypes) + sum(kind == "tile" for _, kind in extras)))
    assert tm * k * a.dtype.itemsize <= 16 * 2 ** 20, (name, tm, k)
    grid = (m // tm, w.n // tn, 1)
    specs = [pl.BlockSpec((tm, k), lambda i, j, c: (i, 0)), pl.BlockSpec((k, tn), lambda i, j, c: (0, j))]
    outs = [jax.ShapeDtypeStruct((m, w.n), d) for d in out_dtypes]
    out_specs = [pl.BlockSpec((tm, tn), lambda i, j, c: (i, j)) for _ in outs]
    return _matmul(name, grid, (a, w.arr), specs, ((1,), (0,)), (tm, tn), _extra_specs(extras, tm, tn), outs, out_specs,
                   epilogue, carry, sums)


def _mm_nt(name, dy, w, out_dtypes, epilogue=None, extras=(), carry=None):
    m, n = dy.shape
    assert n == w.n and dy.dtype == BF16
    tm = 1024
    to = _divisor(w.k, _tile_cols(n, len(out_dtypes) + sum(kind == "tile" for _, kind in extras)))
    assert tm * n * dy.dtype.itemsize <= 16 * 2 ** 20, (name, tm, n)
    grid = (m // tm, w.k // to, 1)
    specs = [pl.BlockSpec((tm, n), lambda i, j, c: (i, 0)), pl.BlockSpec((to, n), lambda i, j, c: (j, 0))]
    outs = [jax.ShapeDtypeStruct((m, w.k), d) for d in out_dtypes]
    out_specs = [pl.BlockSpec((tm, to), lambda i, j, c: (i, j)) for _ in outs]
    return _matmul(name, grid, (dy, w.arr), specs, ((1,), (1,)), (tm, to), _extra_specs(extras, tm, to), outs, out_specs,
                   epilogue, carry)


def _mm_tn(name, a, dy, like, carry=None):
    m, k = a.shape
    n = dy.shape[1]
    assert (k, n) == (like.k, like.n)
    tk = _divisor(k, 1024 if a.dtype == BF16 else 512)
    tn = _divisor(n, 1024)
    grid = (k // tk, n // tn, 1)
    specs = [pl.BlockSpec((m, tk), lambda i, j, c: (0, i)), pl.BlockSpec((m, tn), lambda i, j, c: (0, j))]
    out_specs = [pl.BlockSpec((tk, tn), lambda i, j, c: (i, j))]
    (g,) = _matmul(name, grid, (a, dy), specs, ((0,), (0,)), (tk, tn), [], [jax.ShapeDtypeStruct((k, n), GRAD_DTYPE)],
                   out_specs, None, carry)
    return g if like.cols else g.reshape(N_DEV, k // N_DEV, n)


def _rows(arr, blk=0, width=None):
    return ("rows", arr, blk, width or arr.shape[1])


def _full(arr):
    return ("full", arr)


def _by_residue(arr):
    return ("residue", arr)


def _rowwise(name, fn, ins, outs, reds=(), ts=256, carry=None, scratch=()):
    s = next(i[1].shape[0] if i[0] == "rows" else i[1].shape[0] * i[1].shape[1] for i in ins if i[0] != "full")
    n_in, n_out, n_red = len(ins), len(outs), len(reds)
    in_specs = []
    for i in ins:
        if i[0] == "rows":
            in_specs.append(pl.BlockSpec((ts, i[3]), functools.partial(lambda t, blk: (t, blk), blk=i[2])))
        elif i[0] == "residue":
            d, _, w = i[1].shape
            in_specs.append(pl.BlockSpec((d, ts // d, w), lambda t: (0, t, 0)))
        else:
            in_specs.append(pl.BlockSpec(i[1].shape, functools.partial(lambda t, nd: (0,) * nd, nd=i[1].ndim)))
    out_shape, out_specs = [], []
    for o in outs:
        if len(o) == 2:
            out_shape.append(jax.ShapeDtypeStruct((s, o[0]), o[1]))
            out_specs.append(pl.BlockSpec((ts, o[0]), lambda t: (t, 0)))
        else:
            out_shape.append(jax.ShapeDtypeStruct((o[2], s // o[2], o[0]), o[1]))
            out_specs.append(pl.BlockSpec((o[2], ts // o[2], o[0]), lambda t: (0, t, 0)))
    out_shape += [jax.ShapeDtypeStruct(r, F32) for r in reds]
    out_specs += [pl.BlockSpec(r, lambda t: (0, 0)) for r in reds]

    def body(*refs):
        red_refs = refs[n_in + n_out:n_in + n_out + n_red]
        if red_refs:
            @pl.when(pl.program_id(0) == 0)
            def _():
                for r in red_refs:
                    r[...] = jnp.zeros(r.shape, F32)
        fn(refs[:n_in], refs[n_in:n_in + n_out], red_refs, *refs[n_in + n_out + n_red:])

    return _call(name, body, [i[1] for i in ins], grid=(s // ts,), in_specs=in_specs, out_specs=out_specs,
                 out_shape=out_shape, scratch_shapes=list(scratch), sem=("arbitrary" if reds else "parallel",), carry=carry)


def _ln_stats(x):
    mu = jnp.mean(x, axis=-1, keepdims=True)
    xc = x - mu
    var = jnp.mean(xc * xc, axis=-1, keepdims=True)
    return xc * lax.rsqrt(var + LN_EPS), lax.rsqrt(var + LN_EPS)


def _layer_norm(name, x, g, b, out_dtypes):
    def fn(i, o, r):
        xhat, _ = _ln_stats(i[0][...])
        y = xhat * i[1][...] + i[2][...]
        for ref in o:
            ref[...] = y.astype(ref.dtype)

    return _rowwise(name, fn, [_rows(x), _full(g), _full(b)], [(x.shape[1], d) for d in out_dtypes])


def _ln_bwd_tile(x, g, dy):
    xhat, rstd = _ln_stats(x)
    dyg = dy * g
    m1 = jnp.mean(dyg, axis=-1, keepdims=True)
    m2 = jnp.mean(dyg * xhat, axis=-1, keepdims=True)
    dx = rstd * (dyg - m1 - xhat * m2)
    return dx, jnp.sum(dy * xhat, axis=0, keepdims=True), jnp.sum(dy, axis=0, keepdims=True)


def _layer_norm_bwd(name, x, g, dy):
    d = x.shape[1]

    def fn(i, o, r):
        dx, dg, db = _ln_bwd_tile(i[0][...], i[1][...], i[2][...])
        o[0][...] = dx
        o[1][...] = dx.astype(BF16)
        r[0][...] += dg
        r[1][...] += db

    return _rowwise(name, fn, [_rows(x), _full(g), _rows(dy)], [(d, F32), (d, BF16)], [(1, d), (1, d)])


def _conv_fwd(name, glu, dw, dw_b, ts=512, carry=None):
    s, c = glu.shape
    tc = dw.shape[2]
    per = ts // CONV_HALO
    back = CONV_HALO - (CONV_WIDTH - 1)

    def body(cur_ref, prev_ref, w_ref, b_ref, out_ref, buf):
        i = pl.program_id(1)
        buf[pl.ds(0, CONV_HALO), :] = jnp.where(i > 0, prev_ref[...], 0.0)
        buf[pl.ds(CONV_HALO, ts), :] = cur_ref[...]
        for r0 in range(0, ts, CONV_ROWS):
            acc = jnp.broadcast_to(b_ref[...], (CONV_ROWS, tc))
            for j in range(CONV_WIDTH):
                acc = acc + w_ref[j:j + 1, :] * buf[pl.ds(r0 + back + j, CONV_ROWS), :]
            out_ref[pl.ds(r0, CONV_ROWS), :] = acc

    (out,) = _call(
        name, body, [glu, glu, dw, dw_b], grid=(c // tc, s // ts),
        in_specs=[pl.BlockSpec((ts, tc), lambda j, i: (i, j)),
                  pl.BlockSpec((CONV_HALO, tc), lambda j, i: (jnp.maximum(i * per - 1, 0), j)),
                  pl.BlockSpec((None, CONV_WIDTH, tc), lambda j, i: (j, 0, 0)),
                  pl.BlockSpec((1, tc), lambda j, i: (0, j))],
        out_specs=[pl.BlockSpec((ts, tc), lambda j, i: (i, j))],
        out_shape=[jax.ShapeDtypeStruct((s, c), F32)],
        scratch_shapes=[pltpu.VMEM((ts + CONV_HALO, tc), F32)],
        sem=("parallel", "parallel"), carry=carry)
    return out


def _conv_bwd(name, glu, dc, dw, ts=512, carry=None):
    s, c = glu.shape
    tc = dw.shape[2]
    per = ts // CONV_HALO
    back = CONV_HALO - (CONV_WIDTH - 1)
    last = s // ts - 1

    def body(g_ref, gprev_ref, dc_ref, dcnext_ref, w_ref, dglu_ref, ddw_ref, ddb_ref, gbuf, dbuf):
        i = pl.program_id(1)

        @pl.when(i == 0)
        def _():
            ddw_ref[...] = jnp.zeros(ddw_ref.shape, F32)
            ddb_ref[...] = jnp.zeros(ddb_ref.shape, F32)

        gbuf[pl.ds(0, CONV_HALO), :] = jnp.where(i > 0, gprev_ref[...], 0.0)
        gbuf[pl.ds(CONV_HALO, ts), :] = g_ref[...]
        dbuf[pl.ds(0, ts), :] = dc_ref[...]
        dbuf[pl.ds(ts, CONV_HALO), :] = jnp.where(i < last, dcnext_ref[...], 0.0)
        taps = [jnp.zeros((1, tc), F32)] * CONV_WIDTH
        for r0 in range(0, ts, CONV_ROWS):
            d_here = dbuf[pl.ds(r0, CONV_ROWS), :]
            acc = jnp.zeros((CONV_ROWS, tc), F32)
            for j in range(CONV_WIDTH):
                acc = acc + w_ref[j:j + 1, :] * dbuf[pl.ds(r0 + (CONV_WIDTH - 1) - j, CONV_ROWS), :]
                taps[j] = taps[j] + jnp.sum(d_here * gbuf[pl.ds(r0 + back + j, CONV_ROWS), :], axis=0, keepdims=True)
            dglu_ref[pl.ds(r0, CONV_ROWS), :] = acc
        for j in range(CONV_WIDTH):
            ddw_ref[j:j + 1, :] += taps[j]
        ddb_ref[...] += jnp.sum(dc_ref[...], axis=0, keepdims=True)

    return _call(
        name, body, [glu, glu, dc, dc, dw], grid=(c // tc, s // ts),
        in_specs=[pl.BlockSpec((ts, tc), lambda j, i: (i, j)),
                  pl.BlockSpec((CONV_HALO, tc), lambda j, i: (jnp.maximum(i * per - 1, 0), j)),
                  pl.BlockSpec((ts, tc), lambda j, i: (i, j)),
                  pl.BlockSpec((CONV_HALO, tc), lambda j, i: (jnp.minimum((i + 1) * per, (last + 1) * per - 1), j)),
                  pl.BlockSpec((None, CONV_WIDTH, tc), lambda j, i: (j, 0, 0))],
        out_specs=[pl.BlockSpec((ts, tc), lambda j, i: (i, j)),
                   pl.BlockSpec((CONV_WIDTH, tc), lambda j, i: (0, j)),
                   pl.BlockSpec((1, tc), lambda j, i: (0, j))],
        out_shape=[jax.ShapeDtypeStruct((s, c), F32), jax.ShapeDtypeStruct((CONV_WIDTH, c), F32),
                   jax.ShapeDtypeStruct((1, c), F32)],
        scratch_shapes=[pltpu.VMEM((ts + CONV_HALO, tc), F32), pltpu.VMEM((ts + CONV_HALO, tc), F32)],
        sem=("parallel", "arbitrary"), carry=carry)


def _rope_tables(positions):
    half = HEAD_DIM // 2
    inv = (np.float32(ROPE_THETA) ** (-np.arange(half, dtype=np.float32) * np.float32(2.0 / HEAD_DIM))).astype(np.float32)
    inv_freq = jnp.asarray(np.concatenate([inv, inv])[None, :])
    sign = jnp.asarray(np.concatenate([-np.ones(half, np.float32), np.ones(half, np.float32)])[None, :])

    def fn(i, o, r):
        ang = i[0][...].astype(F32) * i[1][...]
        o[0][...] = jnp.cos(ang)
        o[1][...] = jnp.sin(ang) * i[2][...]

    return _rowwise("rope_tables", fn, [_rows(positions), _full(inv_freq), _full(sign)], [(HEAD_DIM, F32), (HEAD_DIM, F32)], ts=512)


def _rot(x, cos, sin):
    return x * cos + pltpu.roll(x, HEAD_DIM // 2, 1) * sin


def _unrot(x, cos, sin):
    return x * cos - pltpu.roll(x, HEAD_DIM // 2, 1) * sin


def _split_rows(scr, value, d):
    if d == 1:
        return [value]
    scr[...] = value
    return [scr[pl.ds(r, scr.shape[0] // d, stride=d), :] for r in range(d)]


def _join_rows(scr, planes):
    d = len(planes)
    if d == 1:
        return planes[0]
    for r, plane in enumerate(planes):
        scr[pl.ds(r, scr.shape[0] // d, stride=d), :] = plane
    return scr[...]


def _lane(h, shape):
    return lax.broadcasted_iota(jnp.int32, shape, 1) == h


def _attn_specs(width):
    cur = pl.BlockSpec((None, ATTN_BLOCK, width), lambda r, n: (r, n, 0))
    prev = pl.BlockSpec((None, ATTN_BLOCK, width), lambda r, n: (r, jnp.maximum(n - 1, 0), 0))
    return cur, prev


def _masks(n):
    row = lax.broadcasted_iota(jnp.int32, (ATTN_BLOCK, ATTN_BLOCK), 0)
    col = lax.broadcasted_iota(jnp.int32, (ATTN_BLOCK, ATTN_BLOCK), 1)
    return col <= row, jnp.logical_and(col >= row, n > 0)


_NT = (((1,), (1,)), ((), ()))
_TN = (((0,), (0,)), ((), ()))
_NN = (((1,), (0,)), ((), ()))


def _attn_fwd(name, q, k, v, carry=None):
    dil, ln, d = k.shape
    nh = d // HEAD_DIM
    nb = ln // ATTN_BLOCK
    scale = HEAD_DIM ** -0.5

    def body(q_ref, kc_ref, kp_ref, vc_ref, vp_ref, o_ref, l_ref):
        mask_c, mask_p = _masks(pl.program_id(1))
        mask = jnp.concatenate([mask_p, mask_c], axis=1)
        stats = jnp.zeros((ATTN_BLOCK, LANE), F32)
        for h in range(nh):
            hs = slice(h * HEAD_DIM, (h + 1) * HEAD_DIM)
            keys = jnp.concatenate([kp_ref[:, hs], kc_ref[:, hs]], axis=0)
            vals = jnp.concatenate([vp_ref[:, hs], vc_ref[:, hs]], axis=0)
            sc = jnp.where(mask, lax.dot_general(q_ref[:, hs], keys, _NT, preferred_element_type=F32) * scale, NEG)
            m = jnp.max(sc, axis=1, keepdims=True)
            p = jnp.exp(sc - m)
            l = jnp.sum(p, axis=1, keepdims=True)
            o_ref[:, hs] = lax.dot_general(p.astype(BF16), vals, _NN, preferred_element_type=F32) / l
            stats = jnp.where(_lane(h, stats.shape), m + jnp.log(l), stats)
        l_ref[...] = stats

    (cur, prev), (stat, _) = _attn_specs(d), _attn_specs(LANE)
    return _call(name, body, [q, k, k, v, v], grid=(dil, nb), in_specs=[cur, cur, prev, cur, prev], out_specs=[cur, stat],
                 out_shape=[jax.ShapeDtypeStruct((dil, ln, d), F32), jax.ShapeDtypeStruct((dil, ln, LANE), F32)],
                 sem=("parallel", "parallel"), carry=carry)


def _attn_dq(name, q, k, v, do, lse, dsum, carry=None):
    dil, ln, d = k.shape
    nh = d // HEAD_DIM
    nb = ln // ATTN_BLOCK
    scale = HEAD_DIM ** -0.5

    def body(q_ref, kc_ref, kp_ref, vc_ref, vp_ref, do_ref, l_ref, d_ref, dq_ref):
        mask_c, mask_p = _masks(pl.program_id(1))
        mask = jnp.concatenate([mask_p, mask_c], axis=1)
        for h in range(nh):
            hs = slice(h * HEAD_DIM, (h + 1) * HEAD_DIM)
            keys = jnp.concatenate([kp_ref[:, hs], kc_ref[:, hs]], axis=0)
            vals = jnp.concatenate([vp_ref[:, hs], vc_ref[:, hs]], axis=0)
            sc = lax.dot_general(q_ref[:, hs], keys, _NT, preferred_element_type=F32) * scale
            p = jnp.where(mask, jnp.exp(jnp.where(mask, sc, NEG) - l_ref[:, h:h + 1]), 0.0)
            dp = lax.dot_general(do_ref[:, hs], vals, _NT, preferred_element_type=F32)
            ds = p * (dp - d_ref[:, h:h + 1])
            dq_ref[:, hs] = lax.dot_general(ds.astype(BF16), keys, _NN, preferred_element_type=F32) * scale

    (cur, prev), (stat, _) = _attn_specs(d), _attn_specs(LANE)
    (dq,) = _call(name, body, [q, k, k, v, v, do, lse, dsum], grid=(dil, nb),
                  in_specs=[cur, cur, prev, cur, prev, cur, stat, stat], out_specs=[cur],
                  out_shape=[jax.ShapeDtypeStruct((dil, ln, d), F32)], sem=("parallel", "parallel"), carry=carry)
    return dq


def _attn_dkv(name, q, k, v, do, lse, dsum, carry=None):
    dil, ln, d = k.shape
    nh = d // HEAD_DIM
    nb = ln // ATTN_BLOCK
    scale = HEAD_DIM ** -0.5

    def body(k_ref, v_ref, qc_ref, qn_ref, doc_ref, don_ref, lc_ref, lnx_ref, dc_ref, dn_ref, dk_ref, dv_ref):
        n = pl.program_id(1)
        row = lax.broadcasted_iota(jnp.int32, (ATTN_BLOCK, ATTN_BLOCK), 0)
        col = lax.broadcasted_iota(jnp.int32, (ATTN_BLOCK, ATTN_BLOCK), 1)
        mask = jnp.concatenate([row <= col, jnp.logical_and(row >= col, n < nb - 1)], axis=1)
        lse_t = jnp.concatenate([lc_ref[...].T, lnx_ref[...].T], axis=1)
        dsum_t = jnp.concatenate([dc_ref[...].T, dn_ref[...].T], axis=1)
        for h in range(nh):
            hs = slice(h * HEAD_DIM, (h + 1) * HEAD_DIM)
            qs = jnp.concatenate([qc_ref[:, hs], qn_ref[:, hs]], axis=0)
            douts = jnp.concatenate([doc_ref[:, hs], don_ref[:, hs]], axis=0)
            sc = lax.dot_general(k_ref[:, hs], qs, _NT, preferred_element_type=F32) * scale
            p = jnp.where(mask, jnp.exp(jnp.where(mask, sc, NEG) - lse_t[h:h + 1, :]), 0.0)
            dp = lax.dot_general(v_ref[:, hs], douts, _NT, preferred_element_type=F32)
            ds = p * (dp - dsum_t[h:h + 1, :])
            dv_ref[:, hs] = lax.dot_general(p.astype(BF16), douts, _NN, preferred_element_type=F32)
            dk_ref[:, hs] = lax.dot_general(ds.astype(BF16), qs, _NN, preferred_element_type=F32) * scale

    def specs(width):
        cur = pl.BlockSpec((None, ATTN_BLOCK, width), lambda r, n: (r, n, 0))
        nxt = pl.BlockSpec((None, ATTN_BLOCK, width), lambda r, n: (r, jnp.minimum(n + 1, nb - 1), 0))
        return cur, nxt

    (cur, nxt), (stat, stat_next) = specs(d), specs(LANE)
    return _call(name, body, [k, v, q, q, do, do, lse, lse, dsum, dsum], grid=(dil, nb),
                 in_specs=[cur, cur, cur, nxt, cur, nxt, stat, stat_next, stat, stat_next], out_specs=[cur, cur],
                 out_shape=[jax.ShapeDtypeStruct((dil, ln, d), F32)] * 2, sem=("parallel", "parallel"), carry=carry)


def _adamw_tile(w, g, m, v):
    m = ADAM_B1 * m + (1.0 - ADAM_B1) * g
    v = ADAM_B2 * v + (1.0 - ADAM_B2) * (g * g)
    m_hat = m / (1.0 - ADAM_B1 ** ADAM_STEP)
    v_hat = v / (1.0 - ADAM_B2 ** ADAM_STEP)
    delta = -ADAM_LR * (m_hat / (jnp.sqrt(v_hat) + ADAM_EPS) + ADAM_WD * w)
    return delta, m, v


def _adamw_big(name, parts, w, m, v, carry=None):
    layers, r, c = w.shape
    assert len(parts) == layers and all(sum(ch.shape[1] for ch in per_layer) == r for per_layer in parts)
    every = [ch for per_layer in parts for ch in per_layer]
    per_row = 2 * c * (len(every) * N_DEV * every[0].dtype.itemsize + 7 * 4)
    tr = 16
    while tr * 2 <= min(min(ch.shape[1] for ch in every), V7X_VMEM_LIMIT // 2 // per_row) and all(ch.shape[1] % (tr * 2) == 0 for ch in every):
        tr *= 2
    pieces = []
    for ly, per_layer in enumerate(parts):
        at = 0
        for ch in per_layer:
            pieces.append((ly, at, ch.shape[1] // tr, ch))
            at += ch.shape[1] // tr

    def within(layer, i, ly, first, tiles):
        return jnp.logical_and(layer == ly, jnp.logical_and(i >= first, i < first + tiles))

    def body(*refs):
        part_refs = refs[:len(pieces)]
        w_ref, m_ref, v_ref, g_out, d_out, m_out, v_out = refs[len(pieces):]
        layer, i = pl.program_id(0), pl.program_id(1)
        for (ly, first, tiles, _), part_ref in zip(pieces, part_refs):
            @pl.when(within(layer, i, ly, first, tiles))
            def _(part_ref=part_ref):
                g = part_ref[0].astype(F32)
                for dev in range(1, N_DEV):
                    g = g + part_ref[dev].astype(F32)
                delta, mn, vn = _adamw_tile(w_ref[...], g, m_ref[...], v_ref[...])
                g_out[...] = g
                d_out[...] = delta
                m_out[...] = mn
                v_out[...] = vn

    def part_index(layer, i, ly, first, tiles):
        return (0, jnp.where(within(layer, i, ly, first, tiles), i - first, 0), 0)

    own = pl.BlockSpec((None, tr, c), lambda ly, i: (ly, i, 0))
    part_specs = [pl.BlockSpec((N_DEV, tr, c), functools.partial(part_index, ly=ly, first=first, tiles=tiles))
                  for ly, first, tiles, _ in pieces]
    return _call(name, body, [ch for _, _, _, ch in pieces] + [w, m, v], grid=(layers, r // tr), in_specs=part_specs + [own] * 3,
                 out_specs=[own] * 4, out_shape=[jax.ShapeDtypeStruct(w.shape, F32)] * 4, sem=("parallel", "parallel"),
                 carry=carry)


def _sum_slots(name, slots):
    _, r, c = slots.shape

    def body(s_ref, o_ref):
        g = s_ref[0]
        for j in range(1, N_DEV):
            g = g + s_ref[j]
        o_ref[...] = g

    return pl.pallas_call(body, name=name, out_shape=jax.ShapeDtypeStruct((r, c), F32),
                          compiler_params=_params())(slots)


def _adamw_small(name, w, g, m, v):
    def body(w_ref, g_ref, m_ref, v_ref, d_out, m_out, v_out):
        delta, mn, vn = _adamw_tile(w_ref[...], g_ref[...], m_ref[...], v_ref[...])
        d_out[...] = delta
        m_out[...] = mn
        v_out[...] = vn

    return pl.pallas_call(body, name=name, out_shape=[jax.ShapeDtypeStruct(w.shape, F32)] * 3,
                          compiler_params=_params())(w, g, m, v)


class Weights:
    def __init__(self, shards):
        self.shards, self.full, self.parts = shards, {}, {}

    @staticmethod
    def by_columns(key):
        return key.rstrip("01") in COLUMN_SHARDED

    def gather(self, *keys):
        return Exchange(gathers=[(self.shards[k], self.by_columns(k)) for k in keys], keys=keys)

    def landed(self, ex):
        for key, full in zip(ex.keys, ex.gathered):
            cols = self.by_columns(key)
            self.full[key] = W(full if cols else full.reshape(-1, full.shape[-1]), cols)

    def scatter(self, grads, gathers=()):
        if not grads and not gathers:
            return None
        return Exchange(gathers=gathers, scatters=[(g, self.by_columns(k), part) for k, (g, part) in grads.items()], keys=list(grads))

    def received(self, ex):
        for key, (first, _, of), part in zip(ex.keys, ex.s_parts, ex.parts) if ex is not None else ():
            self.parts.setdefault(key, {})[first / of] = part

    def chunks(self, key):
        return [self.parts[key][j] for j in sorted(self.parts[key])]

    def __getitem__(self, key):
        return self.full[key]


def _mlp_ple_fwd(tag, z1, p_i, ln1_g, ln1_b, ln2_g, ln2_b, wt, carries, target=None):
    h1, h1b = _layer_norm(f"ln1_{tag}", z1, ln1_g, ln1_b, (F32, BF16))
    up, act = _mm_nn(f"mlp_up_{tag}", h1b, wt["mlp_up" + tag], (F32, BF16),
                     epilogue=lambda acc: (acc, jnp.square(jnp.maximum(acc, 0.0))), carry=carries.get("mlp_up"))
    if "mlp_up" in carries:
        wt.landed(carries["mlp_up"])
    (z2,) = _mm_nn(f"mlp_down_{tag}", act, wt["mlp_down" + tag], (F32,), epilogue=lambda acc, h: (ALPHA * h + acc,),
                   extras=[(h1, "tile")], carry=carries.get("mlp_down"))
    if "mlp_down" in carries:
        wt.landed(carries["mlp_down"])
    h2, h2b = _layer_norm(f"ln2_{tag}", z2, ln2_g, ln2_b, (F32, BF16))
    (pe,) = _mm_nn(f"ple_proj_{tag}", p_i, wt["ple_proj" + tag], (F32,))

    saved = dict(z1=z1, h1b=h1b, up=up, act=act, z2=z2, h2b=h2b, p=p_i)
    if target is None:
        def gate(acc, h, e):
            out = h + e * _sigmoid(acc)
            return acc, out, out

        gp, out, outb = _mm_nn(f"ple_gate_{tag}", h2b, wt["ple_gate" + tag], (F32, F32, BF16), epilogue=gate,
                               extras=[(h2, "tile"), (pe, "tile")], carry=carries.get("ple_gate"))
        if "ple_gate" in carries:
            wt.landed(carries["ple_gate"])
        saved.update(pe=pe, gp=gp)
        return out, outb, saved

    width = z1.shape[1]

    def gate_and_loss(acc, h, e, goal):
        sg = _sigmoid(acc)
        diff = h + e * sg - goal
        d_y = diff * (1.0 / width)
        return d_y, d_y * sg, d_y * e * sg * (1.0 - sg), jnp.broadcast_to(jnp.sum(diff * diff), (1, LANE))

    d_y, d_pe, d_gp, sq = _mm_nn(f"ple_gate_{tag}", h2b, wt["ple_gate" + tag], (F32, BF16, BF16), epilogue=gate_and_loss,
                                 extras=[(h2, "tile"), (pe, "tile"), (target, "tile")], sums=1)
    saved.update(d_pe=d_pe, d_gp=d_gp)
    return d_y, 0.5 * sq[0, 0] / width, saved


WHOLE = (0, 1, 1)


def _mlp_ple_bwd(tag, d_out, sv, ln1_g, ln2_g, wt, plan, waiting):
    made = {}

    def riders(kernel, extra=()):
        items = {name + tag: (made[name], part) for name, part in plan[kernel]}
        return wt.scatter({**items, **dict(extra)})

    d_pe, d_gp = sv["d_pe"], sv["d_gp"]
    made["ple_proj"] = _mm_tn(f"g_ple_proj_{tag}", sv["p"], d_pe, wt["ple_proj" + tag])
    made["ple_gate"] = _mm_tn(f"g_ple_gate_{tag}", sv["h2b"], d_gp, wt["ple_gate" + tag])
    (d_h2,) = _mm_nt(f"d_ple_gate_{tag}", d_gp, wt["ple_gate" + tag], (F32,), epilogue=lambda acc, dy: (dy + acc,),
                     extras=[(d_out, "tile")])
    d_z2, d_z2b, g_ln2_g, g_ln2_b = _layer_norm_bwd(f"ln2_bwd_{tag}", sv["z2"], ln2_g, d_h2)
    ex = riders("g_mlp_down", waiting.items())
    made["mlp_down"] = _mm_tn(f"g_mlp_down_{tag}", sv["act"], d_z2b, wt["mlp_down" + tag], carry=ex)
    wt.received(ex)
    ex = riders("d_mlp_down")
    (d_up,) = _mm_nt(f"d_mlp_down_{tag}", d_z2b, wt["mlp_down" + tag], (BF16,),
                     epilogue=lambda acc, u: (acc * (2.0 * jnp.maximum(u, 0.0)),), extras=[(sv["up"], "tile")], carry=ex)
    wt.received(ex)
    ex = riders("g_mlp_up")
    made["mlp_up"] = _mm_tn(f"g_mlp_up_{tag}", sv["h1b"], d_up, wt["mlp_up" + tag], carry=ex)
    wt.received(ex)
    ex = riders("d_mlp_up")
    (d_h1,) = _mm_nt(f"d_mlp_up_{tag}", d_up, wt["mlp_up" + tag], (F32,), epilogue=lambda acc, dz: (ALPHA * dz + acc,),
                     extras=[(d_z2, "tile")], carry=ex)
    wt.received(ex)
    d_z1, d_z1b, g_ln1_g, g_ln1_b = _layer_norm_bwd(f"ln1_bwd_{tag}", sv["z1"], ln1_g, d_h1)
    return d_z1, d_z1b, dict(ln1_g=g_ln1_g, ln1_b=g_ln1_b, ln2_g=g_ln2_g, ln2_b=g_ln2_b), made["mlp_down"], made["mlp_up"]


def _local_step(x, p, positions, target, wt, small):
    s, d = x.shape
    nh = d // HEAD_DIM
    xb, p = x.astype(BF16), p.astype(BF16)

    ex = wt.gather("conv_w_out", "ple_proj0", "ple_gate0")
    (u,) = _mm_nn("conv_in", xb, wt["conv_w_in"], (F32,), epilogue=lambda acc, b: (acc + b,), extras=[(small["conv_b_in"], "row")],
                  carry=ex)
    wt.landed(ex)

    def glu_fn(i, o, r):
        o[0][...] = i[0][...] * _sigmoid(i[1][...])

    (glu,) = _rowwise("glu", glu_fn, [_rows(u, 0, d), _rows(u, 1, d)], [(d, F32)])
    ex = wt.gather("mlp_up0")
    c = _conv_fwd("dwconv", glu, small["conv_dw"], small["conv_dw_b"], carry=ex)
    wt.landed(ex)

    def ln_silu(i, o, r):
        xhat, _ = _ln_stats(i[0][...])
        n = xhat * i[1][...] + i[2][...]
        o[0][...] = (n * _sigmoid(n)).astype(BF16)

    (sb,) = _rowwise("conv_ln_silu", ln_silu, [_rows(c), _full(small["conv_ln_g"]), _full(small["conv_ln_b"])], [(d, BF16)])
    ex = wt.gather("attn_w_o")
    (z1,) = _mm_nn("conv_out", sb, wt["conv_w_out"], (F32,), epilogue=lambda acc, xt: (ALPHA * xt + acc,), extras=[(x, "tile")],
                   carry=ex)
    wt.landed(ex)
    x1, x1b, sv0 = _mlp_ple_fwd("0", z1, p[0], small["ln1_g"][0:1], small["ln1_b"][0:1], small["ln2_g"][0:1], small["ln2_b"][0:1], wt,
                                dict(mlp_up=wt.gather("mlp_down0"), mlp_down=wt.gather("attn_w_q"), ple_gate=wt.gather("w_kv")))

    (kvn,) = _layer_norm("kv_ln", x1, small["kv_ln_g"], small["kv_ln_b"], (BF16,))
    riders = [wt.gather("ple_proj1", "ple_gate1"), wt.gather("mlp_up1")]
    (kv,) = _mm_nn("kv_proj", kvn, wt["w_kv"], (F32,), carry=riders[0])
    (q,) = _mm_nn("q_proj", x1b, wt["attn_w_q"], (F32,), carry=riders[1])
    for ex in riders:
        wt.landed(ex)
    cos, sin = _rope_tables(positions)
    row_scratch = [pltpu.VMEM((ROW_TILE, LANE), F32)]

    def rot_kv(i, o, r, scr):
        cs, sn = i[2][...], i[3][...]
        for h in range(nh):
            hs = slice(h * HEAD_DIM, (h + 1) * HEAD_DIM)
            for base, val in ((0, _rot(i[0][:, hs], cs, sn)), (N_GROUPS, i[1][:, hs])):
                for g, dil in enumerate(GROUP_DILATIONS):
                    for res, plane in enumerate(_split_rows(scr, val, dil)):
                        o[base + g][res, :, hs] = plane.astype(BF16)

    by_group = [(d, BF16, dil) for dil in GROUP_DILATIONS]
    kv_groups = _rowwise("rotary_kv", rot_kv, [_rows(kv, 0, d), _rows(kv, 1, d), _rows(cos), _rows(sin)], by_group * 2,
                         ts=ROW_TILE, scratch=row_scratch)
    kg, vg = kv_groups[:N_GROUPS], kv_groups[N_GROUPS:]

    def rot_q(i, o, r, scr):
        cs, sn = i[1][...], i[2][...]
        for g, dil in enumerate(GROUP_DILATIONS):
            for h in range(nh):
                hs = slice(h * HEAD_DIM, (h + 1) * HEAD_DIM)
                val = _rot(i[0][:, g * d + h * HEAD_DIM:g * d + (h + 1) * HEAD_DIM], cs, sn)
                for res, plane in enumerate(_split_rows(scr, val, dil)):
                    o[g][res, :, hs] = plane.astype(BF16)

    qg = _rowwise("rotary_q", rot_q, [_rows(q), _rows(cos), _rows(sin)], by_group, ts=ROW_TILE, scratch=row_scratch)

    og, lg = zip(*[_attn_fwd(f"attn_fwd_{g}", qg[g], kg[g], vg[g]) for g in range(N_GROUPS)])

    def merge(i, o, r, scr):
        lses = []
        for g, dil in enumerate(GROUP_DILATIONS):
            lses.append(_join_rows(scr, [i[N_GROUPS + g][res] for res in range(dil)]))
        top = functools.reduce(jnp.maximum, lses)
        es = [jnp.exp(l - top) for l in lses]
        den = functools.reduce(lambda a, b: a + b, es)
        total = top + jnp.log(den)
        for g, dil in enumerate(GROUP_DILATIONS):
            for res, plane in enumerate(_split_rows(scr, total, dil)):
                o[2 + g][res] = plane
        ws = [e / den for e in es]
        for h in range(nh):
            hs = slice(h * HEAD_DIM, (h + 1) * HEAD_DIM)
            out = jnp.zeros((ROW_TILE, HEAD_DIM), F32)
            for g, dil in enumerate(GROUP_DILATIONS):
                og_h = _join_rows(scr, [i[g][res, :, hs] for res in range(dil)])
                out = out + ws[g][:, h:h + 1] * og_h
            o[0][:, hs] = out
            o[1][:, hs] = out.astype(BF16)

    merged = _rowwise("attn_merge", merge, [_by_residue(t) for t in og + lg],
                      [(d, F32), (d, BF16)] + [(LANE, F32, dil) for dil in GROUP_DILATIONS], ts=ROW_TILE, scratch=row_scratch)
    o, ob, lse_g = merged[0], merged[1], merged[2:]
    (z1b,) = _mm_nn("attn_out", ob, wt["attn_w_o"], (F32,), epilogue=lambda acc, xt: (ALPHA * xt + acc,), extras=[(x1, "tile")])
    d_y, loss, sv1 = _mlp_ple_fwd("1", z1b, p[1], small["ln1_g"][1:2], small["ln1_b"][1:2], small["ln2_g"][1:2], small["ln2_b"][1:2],
                                  wt, dict(mlp_up=wt.gather("mlp_down1")), target=target)

    plan = dict(g_mlp_down=[("ple_proj", WHOLE), ("ple_gate", WHOLE)], d_mlp_down=[("mlp_down", (0, 3, 8))],
                g_mlp_up=[("mlp_down", (3, 6, 8))], d_mlp_up=[("mlp_down", (6, 8, 8))])
    d_z1, d_z1b, g1, _, g_up1 = _mlp_ple_bwd("1", d_y, sv1, small["ln1_g"][1:2], small["ln2_g"][1:2], wt, plan, {})
    g_wo = _mm_tn("g_attn_out", ob, d_z1b, wt["attn_w_o"])
    (d_o,) = _mm_nt("d_attn_out", d_z1b, wt["attn_w_o"], (F32,))

    def dsum_fn(i, o, r, scr):
        stats = jnp.zeros((ROW_TILE, LANE), F32)
        for h in range(nh):
            hs = slice(h * HEAD_DIM, (h + 1) * HEAD_DIM)
            dout = i[0][:, hs]
            stats = jnp.where(_lane(h, stats.shape), jnp.sum(dout * i[1][:, hs], axis=1, keepdims=True), stats)
            for g, dil in enumerate(GROUP_DILATIONS):
                for res, plane in enumerate(_split_rows(scr, dout, dil)):
                    o[g][res, :, hs] = plane.astype(BF16)
        for g, dil in enumerate(GROUP_DILATIONS):
            for res, plane in enumerate(_split_rows(scr, stats, dil)):
                o[N_GROUPS + g][res] = plane

    res_ = _rowwise("attn_dsum", dsum_fn, [_rows(d_o), _rows(o)], by_group + [(LANE, F32, dil) for dil in GROUP_DILATIONS],
                    ts=ROW_TILE, scratch=row_scratch)
    dog, dsum_g = res_[:N_GROUPS], res_[N_GROUPS:]
    dqs, dks, dvs = [], [], []
    riders = [wt.scatter({"mlp_up1": (g_up1, (q, q + 1, 4))}) for q in range(4)] + [wt.scatter({"attn_w_o": (g_wo, WHOLE)}), None]
    for g in range(N_GROUPS):
        dqs.append(_attn_dq(f"attn_dq_{g}", qg[g], kg[g], vg[g], dog[g], lse_g[g], dsum_g[g], carry=riders[2 * g]))
        dk, dv = _attn_dkv(f"attn_dkv_{g}", qg[g], kg[g], vg[g], dog[g], lse_g[g], dsum_g[g], carry=riders[2 * g + 1])
        dks.append(dk)
        dvs.append(dv)
    for ex in riders:
        wt.received(ex)

    def unrot_q(i, o, r, scr):
        cs, sn = i[N_GROUPS][...], i[N_GROUPS + 1][...]
        for g, dil in enumerate(GROUP_DILATIONS):
            for h in range(nh):
                hs = slice(h * HEAD_DIM, (h + 1) * HEAD_DIM)
                dq = _join_rows(scr, [i[g][res, :, hs] for res in range(dil)])
                o[0][:, g * d + h * HEAD_DIM:g * d + (h + 1) * HEAD_DIM] = _unrot(dq, cs, sn).astype(BF16)

    (d_q,) = _rowwise("rotary_q_bwd", unrot_q, [_by_residue(t) for t in dqs] + [_rows(cos), _rows(sin)], [(N_GROUPS * d, BF16)],
                      ts=ROW_TILE, scratch=row_scratch)

    def unrot_kv(i, o, r, scr):
        cs, sn = i[2 * N_GROUPS][...], i[2 * N_GROUPS + 1][...]
        for h in range(nh):
            hs = slice(h * HEAD_DIM, (h + 1) * HEAD_DIM)
            for base in (0, N_GROUPS):
                tot = jnp.zeros((ROW_TILE, HEAD_DIM), F32)
                for g, dil in enumerate(GROUP_DILATIONS):
                    tot = tot + _join_rows(scr, [i[base + g][res, :, hs] for res in range(dil)])
                if base == 0:
                    o[0][:, hs] = _unrot(tot, cs, sn).astype(BF16)
                else:
                    o[0][:, d + h * HEAD_DIM:d + (h + 1) * HEAD_DIM] = tot.astype(BF16)

    (d_kv,) = _rowwise("rotary_kv_bwd", unrot_kv, [_by_residue(t) for t in dks + dvs] + [_rows(cos), _rows(sin)], [(2 * d, BF16)],
                       ts=ROW_TILE, scratch=row_scratch)
    g_wq = _mm_tn("g_q_proj", x1b, d_q, wt["attn_w_q"])
    ex = wt.scatter({"attn_w_q": (g_wq, (0, 3, 8))})
    g_wkv = _mm_tn("g_kv_proj", kvn, d_kv, wt["w_kv"], carry=ex)
    wt.received(ex)
    ex = wt.scatter({"attn_w_q": (g_wq, (3, 6, 8))})
    (d_x1a,) = _mm_nt("d_q_proj", d_q, wt["attn_w_q"], (F32,), epilogue=lambda acc, dz: (ALPHA * dz + acc,), extras=[(d_z1, "tile")],
                      carry=ex)
    wt.received(ex)
    ex = wt.scatter({"attn_w_q": (g_wq, (6, 8, 8)), "w_kv": (g_wkv, (0, 1, 8))})
    (d_kvn,) = _mm_nt("d_kv_proj", d_kv, wt["w_kv"], (F32,), carry=ex)
    wt.received(ex)

    def kv_ln_bwd(i, o, r):
        dx, dg, db = _ln_bwd_tile(i[0][...], i[1][...], i[2][...])
        dx = dx + i[3][...]
        sg = _sigmoid(i[5][...])
        o[0][...] = dx
        o[1][...] = (dx * sg).astype(BF16)
        o[2][...] = (dx * i[4][...] * sg * (1.0 - sg)).astype(BF16)
        r[0][...] += dg
        r[1][...] += db

    ex = wt.scatter({"w_kv": (g_wkv, (1, 4, 8))})
    d_x1, sv0["d_pe"], sv0["d_gp"], g_kv_ln_g, g_kv_ln_b = _rowwise(
        "kv_ln_bwd", kv_ln_bwd, [_rows(x1), _full(small["kv_ln_g"]), _rows(d_kvn), _rows(d_x1a), _rows(sv0["pe"]), _rows(sv0["gp"])],
        [(d, F32), (d, BF16), (d, BF16)], [(1, d), (1, d)], ts=128, carry=ex)
    wt.received(ex)

    plan = dict(g_mlp_down=[("ple_proj", WHOLE)], d_mlp_down=[("ple_gate", WHOLE), ("mlp_down", (0, 1, 8))],
                g_mlp_up=[("mlp_down", (1, 4, 8))], d_mlp_up=[("mlp_down", (4, 7, 8))])
    d_z1, d_z1b, g0, g_down0, g_up0 = _mlp_ple_bwd("0", d_x1, sv0, small["ln1_g"][0:1], small["ln2_g"][0:1], wt, plan,
                                                   {"w_kv": (g_wkv, (4, 8, 8))})
    g_wout = _mm_tn("g_conv_out", sb, d_z1b, wt["conv_w_out"])
    (d_s,) = _mm_nt("d_conv_out", d_z1b, wt["conv_w_out"], (F32,))

    def ln_silu_bwd(i, o, r):
        cx, gn, bn, ds_ = i[0][...], i[1][...], i[2][...], i[3][...]
        xhat, _ = _ln_stats(cx)
        n = xhat * gn + bn
        sg = _sigmoid(n)
        dn = ds_ * (sg * (1.0 + n * (1.0 - sg)))
        dx, dg, db = _ln_bwd_tile(cx, gn, dn)
        o[0][...] = dx
        r[0][...] += dg
        r[1][...] += db

    d_c, g_cln_g, g_cln_b = _rowwise("conv_ln_silu_bwd", ln_silu_bwd,
                                     [_rows(c), _full(small["conv_ln_g"]), _full(small["conv_ln_b"]), _rows(d_s)],
                                     [(d, F32)], [(1, d), (1, d)])
    ex = wt.scatter({"mlp_down0": (g_down0, (7, 8, 8)), "mlp_up0": (g_up0, (0, 5, 8))})
    d_glu, g_dw, g_dwb = _conv_bwd("dwconv_bwd", glu, d_c, small["conv_dw"], carry=ex)
    wt.received(ex)

    def glu_bwd(i, o, r):
        a, gt, dg_ = i[0][...], i[1][...], i[2][...]
        sg = _sigmoid(gt)
        da = dg_ * sg
        dgate = dg_ * a * sg * (1.0 - sg)
        o[0][:, 0:d] = da.astype(BF16)
        o[0][:, d:2 * d] = dgate.astype(BF16)
        r[0][:, 0:d] += jnp.sum(da, axis=0, keepdims=True)
        r[0][:, d:2 * d] += jnp.sum(dgate, axis=0, keepdims=True)

    d_u, g_bin = _rowwise("glu_bwd", glu_bwd, [_rows(u, 0, d), _rows(u, 1, d), _rows(d_glu)], [(2 * d, BF16)], [(1, 2 * d)])
    ex = wt.scatter({"mlp_up0": (g_up0, (5, 7, 8))})
    g_win = _mm_tn("g_conv_in", xb, d_u, wt["conv_w_in"], carry=ex)
    wt.received(ex)
    rows = [g_bin.reshape(2, d), g_dw, g_dwb, g_cln_g, g_cln_b, g_kv_ln_g, g_kv_ln_b]
    rows += [jnp.concatenate([g0[n], g1[n]], axis=0) for n in ("ln1_g", "ln1_b", "ln2_g", "ln2_b")]
    rows, offsets = _stack_rows(rows)
    ex = wt.scatter({"mlp_up0": (g_up0, (7, 8, 8)), "conv_w_out": (g_wout, (0, 1, 2))}, gathers=[(rows, False)])
    (grad_x,) = _mm_nt("d_conv_in", d_u, wt["conv_w_in"], (F32,), epilogue=lambda acc, dz: (ALPHA * dz + acc,), extras=[(d_z1, "tile")],
                       carry=ex)
    wt.received(ex)
    return loss, grad_x, ex.gathered[0], offsets, dict(conv_w_out=g_wout, conv_w_in=g_win)


BIG = ("conv_w_in", "conv_w_out", "w_kv", "attn_w_q", "attn_w_o", "mlp_up", "mlp_down", "ple_proj", "ple_gate")
COLUMN_SHARDED = ("conv_w_in", "w_kv", "attn_w_q", "mlp_up", "ple_proj")
WEIGHTS = ("conv_w_in", "conv_b_in", "conv_dw", "conv_dw_b", "conv_ln_g", "conv_ln_b", "conv_w_out", "kv_ln_g", "kv_ln_b",
           "w_kv", "attn_w_q", "attn_w_o", "ln1_g", "ln1_b", "mlp_up", "mlp_down", "ln2_g", "ln2_b", "ple_proj", "ple_gate")


def kernel(x, p, positions, conv_w_in, conv_b_in, conv_dw, conv_dw_b, conv_ln_g, conv_ln_b, conv_w_out, kv_ln_g, kv_ln_b, w_kv, attn_w_q, attn_w_o, ln1_g, ln1_b, mlp_up, mlp_down, ln2_g, ln2_b, ple_proj, ple_gate, loss_target, m_conv_w_in, m_conv_b_in, m_conv_dw, m_conv_dw_b, m_conv_ln_g, m_conv_ln_b, m_conv_w_out, m_kv_ln_g, m_kv_ln_b, m_w_kv, m_attn_w_q, m_attn_w_o, m_ln1_g, m_ln1_b, m_mlp_up, m_mlp_down, m_ln2_g, m_ln2_b, m_ple_proj, m_ple_gate, v_conv_w_in, v_conv_b_in, v_conv_dw, v_conv_dw_b, v_conv_ln_g, v_conv_ln_b, v_conv_w_out, v_kv_ln_g, v_kv_ln_b, v_w_kv, v_attn_w_q, v_attn_w_o, v_ln1_g, v_ln1_b, v_mlp_up, v_mlp_down, v_ln2_g, v_ln2_b, v_ple_proj, v_ple_gate):
    given = dict(locals())
    wts = {n: given[n] for n in WEIGHTS}
    moms = {n: given["m_" + n] for n in WEIGHTS}
    vels = {n: given["v_" + n] for n in WEIGHTS}
    s, d = x.shape[1], x.shape[2]
    shard = d // N_DEV
    me = 4 * lax.axis_index("x") + 2 * lax.axis_index("y") + lax.axis_index("c")

    def layers_of(a):
        return a.reshape((-1,) + a.shape[-2:])

    shards = {}
    for n in BIG:
        w3 = layers_of(wts[n])
        for ly in range(w3.shape[0]):
            shards[n + str(ly) if w3.shape[0] > 1 else n] = w3[ly].astype(BF16)
    wt = Weights(shards)
    pack, at = _stack_rows([wts["conv_b_in"].reshape(2, shard), wts["conv_dw"].reshape(CONV_WIDTH, shard),
                            wts["conv_dw_b"], wts["conv_ln_g"], wts["conv_ln_b"]])
    ex = Exchange(gathers=[(shards["conv_w_in"], True), (pack, False)], keys=["conv_w_in"])
    _exchange_alone("gather_first", ex)
    wt.landed(ex)
    packed = ex.gathered[1]
    small = dict(conv_b_in=packed[:, at[0]:at[0] + 2].reshape(1, 2 * d), conv_dw=packed[:, at[1]:at[1] + CONV_WIDTH],
                 conv_dw_b=packed[:, at[2]].reshape(1, d), conv_ln_g=packed[:, at[3]].reshape(1, d),
                 conv_ln_b=packed[:, at[4]].reshape(1, d), kv_ln_g=kv_ln_g.reshape(1, d), kv_ln_b=kv_ln_b.reshape(1, d),
                 ln1_g=ln1_g, ln1_b=ln1_b, ln2_g=ln2_g, ln2_b=ln2_b)

    loss, grad_x, all_rows, at, late = _local_step(x[0], p[:, 0], positions.reshape(s, 1), loss_target[0], wt, small)
    loss = lax.psum(loss, ("x", "y", "c"))

    g_wout, g_win = late["conv_w_out"], late["conv_w_in"]
    riding = dict(mlp_down={"conv_w_out": (g_wout, (1, 2, 2)), "conv_w_in": (g_win, (0, 1, 8))},
                  mlp_up={"conv_w_in": (g_win, (1, 4, 8))}, attn_w_q={"conv_w_in": (g_win, (4, 5, 8))},
                  ple_gate={"conv_w_in": (g_win, (5, 6, 8))}, w_kv={"conv_w_in": (g_win, (6, 7, 8))},
                  attn_w_o={"conv_w_in": (g_win, (7, 8, 8))})
    out = {}
    for n in list(riding) + [n for n in BIG if n not in riding]:
        w3 = layers_of(wts[n])
        keys = [n + str(ly) if w3.shape[0] > 1 else n for ly in range(w3.shape[0])]
        ex = wt.scatter(riding.get(n, {}))
        res = _adamw_big("adamw_" + n, [wt.chunks(k) for k in keys], w3, layers_of(moms[n]), layers_of(vels[n]), carry=ex)
        wt.received(ex)
        out[n] = [r.reshape(wts[n].shape) for r in res]
    tot = _sum_slots("sum_small_grads", all_rows)
    mine = lax.dynamic_slice_in_dim(tot, me * shard, shard, axis=1)
    b_in = lax.dynamic_slice_in_dim(tot[at[0]:at[0] + 2].reshape(1, 2 * d), me * 2 * shard, 2 * shard, axis=1)
    g_small = dict(conv_b_in=b_in, conv_dw=mine[at[1]:at[1] + CONV_WIDTH].reshape(conv_dw.shape), conv_dw_b=mine[at[2]:at[2] + 1],
                   conv_ln_g=mine[at[3]:at[3] + 1], conv_ln_b=mine[at[4]:at[4] + 1], kv_ln_g=tot[at[5]], kv_ln_b=tot[at[6]])
    for j, n in enumerate(("ln1_g", "ln1_b", "ln2_g", "ln2_b")):
        g_small[n] = tot[at[7 + j]:at[7 + j] + DEPTH]
    order = [n for n in WEIGHTS if n not in BIG]

    def flat(t):
        return _stack_rows([t[n].reshape(-1, shard) for n in order])

    (w_s, at), (g_s, _), (m_s, _), (v_s, _) = flat(wts), flat(g_small), flat(moms), flat(vels)
    d_s, m_s, v_s = _adamw_small("adamw_small", w_s, g_s, m_s, v_s)
    for n, a in zip(order, at):
        nrow = wts[n].size // shard
        out[n] = [g_small[n].reshape(wts[n].shape)] + [t[a:a + nrow].reshape(wts[n].shape) for t in (d_s, m_s, v_s)]
    return (loss, grad_x[None], *[out[n][0] for n in WEIGHTS], *[out[n][1] for n in WEIGHTS],
            *[out[n][2] for n in WEIGHTS], *[out[n][3] for n in WEIGHTS])
```

```python
import functools

import numpy as np
import jax
import jax.numpy as jnp
from jax import lax
from jax.experimental import pallas as pl
from jax.experimental.pallas import tpu as pltpu

F32, BF16 = jnp.float32, jnp.bfloat16

N_DEV = 8
HEAD_DIM = 128
ATTN_BLOCK = 128
GROUP_DILATIONS = (1, 4, 16)
N_GROUPS = len(GROUP_DILATIONS)
CONV_WIDTH = 31
CONV_HALO = 32
CONV_ROWS = 64
ROPE_THETA = 10000.0
LN_EPS = 1e-5
DEPTH = 2
ALPHA = (2 * DEPTH) ** 0.25
ADAM_LR, ADAM_B1, ADAM_B2, ADAM_EPS, ADAM_WD, ADAM_STEP = 0.001, 0.9, 0.999, 1e-08, 0.01, 10
NEG = -1e30
V7X_VMEM_LIMIT = 56 * 2 ** 20
LANE = 128
SUBLANES = 8
ROW_TILE = 256
GRAD_DTYPE = BF16

MESH = pl.DeviceIdType.MESH
ANY = pl.BlockSpec(memory_space=pl.ANY)


def _params(*sem):
    return pltpu.CompilerParams(dimension_semantics=sem or None, vmem_limit_bytes=V7X_VMEM_LIMIT)


def _sigmoid(x):
    return 1.0 / (1.0 + jnp.exp(-x))


def _divisor(n, most):
    best = None
    for t in range(LANE, min(n, most) + 1, LANE):
        if n % t == 0:
            best = t
    assert best is not None, (n, most)
    return best


def _stack_rows(parts):
    out, offsets, at = [], [], 0
    for a in parts:
        pad = -a.shape[0] % SUBLANES
        offsets.append(at)
        out.append(a)
        if pad:
            out.append(jnp.zeros((pad, a.shape[1]), a.dtype))
        at += a.shape[0] + pad
    return jnp.concatenate(out, axis=0), offsets


class Exchange:
    OTHER_CHIPS = (4, 2, 6)

    def __init__(self, gathers=(), scatters=(), keys=()):
        self.gathers, self.g_cols = [a for a, _ in gathers], [c for _, c in gathers]
        self.scatters, self.s_cols, self.s_parts = [s[0] for s in scatters], [s[1] for s in scatters], [s[2] for s in scatters]
        self.keys = list(keys)
        self.n_g, self.n_s = len(self.gathers), len(self.scatters)
        self.n = self.n_g + self.n_s
        self.operands = self.gathers + self.scatters
        self.gathered = self.parts = None

    def rows(self, t):
        a = self.scatters[t]
        first, last, of = self.s_parts[t]
        per = (a.shape[0] if self.s_cols[t] else a.shape[1]) // of
        return first * per, (last - first) * per

    def out_shape(self):
        outs = []
        for a, cols in zip(self.gathers, self.g_cols):
            outs.append(jax.ShapeDtypeStruct((a.shape[0], N_DEV * a.shape[1]) if cols else (N_DEV,) + a.shape, a.dtype))
        for t, (a, cols) in enumerate(zip(self.scatters, self.s_cols)):
            outs.append(jax.ShapeDtypeStruct((N_DEV, self.rows(t)[1], a.shape[1] // N_DEV if cols else a.shape[2]), a.dtype))
        return outs

    def scratch(self):
        dma = pltpu.SemaphoreType.DMA
        return [dma((max(self.n_g, 1) * 7,)), dma((max(self.n_g, 1) * 7,)), dma((max(self.n_s, 1) * 7,)),
                dma((max(self.n_s, 1) * 7,)), dma((self.n,))]

    def take(self, results):
        self.gathered, self.parts = list(results[:self.n_g]), list(results[self.n_g:])

    def _copies(self, ins, outs, sems):
        n_g, n_s = self.n_g, self.n_s
        g_in, s_in, g_out, s_out = ins[:n_g], ins[n_g:], outs[:n_g], outs[n_g:]
        g_send, g_recv, s_send, s_recv, local_sem = sems
        x, y, c = lax.axis_index("x"), lax.axis_index("y"), lax.axis_index("c")

        def peer(k):
            return (1 - x if k & 4 else x, 1 - y if k & 2 else y, 1 - c if k & 1 else c)

        def number(p):
            return 4 * p[0] + 2 * p[1] + p[2]

        me = number((x, y, c))

        def slot(t, j):
            first, count = self.rows(t)
            if self.s_cols[t]:
                width = self.scatters[t].shape[1] // N_DEV
                return s_in[t].at[pl.ds(first, count), pl.ds(pl.multiple_of(j * width, LANE), width)]
            return s_in[t].at[j, pl.ds(first, count)]

        def place(t, j):
            if self.g_cols[t]:
                width = self.gathers[t].shape[1]
                return g_out[t].at[:, pl.ds(pl.multiple_of(j * width, LANE), width)]
            return g_out[t].at[j]

        def local():
            cps = [pltpu.make_async_copy(g_in[t], place(t, me), local_sem.at[t]) for t in range(n_g)]
            return cps + [pltpu.make_async_copy(slot(t, me), s_out[t].at[me], local_sem.at[n_g + t]) for t in range(n_s)]

        def scatter(t, k):
            p = peer(k)
            return pltpu.make_async_remote_copy(
                src_ref=slot(t, number(p)), dst_ref=s_out[t].at[me], send_sem=s_send.at[t * 7 + k - 1],
                recv_sem=s_recv.at[t * 7 + k - 1], device_id=p, device_id_type=MESH)

        def landed(t, k):
            p = peer(k)
            return pltpu.make_async_remote_copy(
                src_ref=slot(t, me), dst_ref=s_out[t].at[number(p)], send_sem=s_send.at[t * 7 + k - 1],
                recv_sem=s_recv.at[t * 7 + k - 1], device_id=p, device_id_type=MESH)

        def gather(t, pair, block, to, src=None):
            slot = place(t, number(block))
            return pltpu.make_async_remote_copy(
                src_ref=slot if src is None else src, dst_ref=slot, send_sem=g_send.at[t * 7 + pair],
                recv_sem=g_recv.at[t * 7 + pair], device_id=to, device_id_type=MESH)

        def first_sends():
            cps = []
            for t in range(n_g):
                cps.append(gather(t, 0, peer(0), peer(1), src=g_in[t]))
                cps += [gather(t, 1 + j, peer(0), peer(k), src=g_in[t]) for j, k in enumerate(self.OTHER_CHIPS)]
            for t in range(n_s):
                cps += [scatter(t, k) for k in range(1, N_DEV)]
            return cps

        return peer, local, landed, gather, first_sends

    def start(self, ins, outs, sems):
        _, local, _, _, first_sends = self._copies(ins, outs, sems)
        for cp in local() + first_sends():
            cp.start()

    def finish(self, ins, outs, sems):
        peer, local, landed, gather, first_sends = self._copies(ins, outs, sems)
        mine, sibling = peer(0), peer(1)
        passed = []
        for j, k in enumerate(self.OTHER_CHIPS):
            for t in range(self.n_g):
                gather(t, 1 + j, peer(k), mine).wait_recv()
                passed.append(gather(t, 4 + j, peer(k), sibling))
                passed[-1].start()
        for t in range(self.n_g):
            gather(t, 0, sibling, mine).wait_recv()
            for j, k in enumerate(self.OTHER_CHIPS):
                gather(t, 4 + j, peer(k ^ 1), mine).wait_recv()
        for t in range(self.n_s):
            for k in range(1, N_DEV):
                landed(t, k).wait_recv()
        for cp in first_sends() + passed:
            cp.wait_send()
        for cp in local():
            cp.wait()


def _exchange_alone(name, ex):
    def body(*refs):
        ins, outs, sems = refs[:ex.n], refs[ex.n:2 * ex.n], refs[2 * ex.n:]
        ex.start(ins, outs, sems)
        ex.finish(ins, outs, sems)

    ex.take(pl.pallas_call(body, name=name, in_specs=[ANY] * ex.n, out_specs=[ANY] * ex.n, out_shape=ex.out_shape(),
                           scratch_shapes=ex.scratch())(*ex.operands))


def _call(name, body, args, *, grid, in_specs, out_specs, out_shape, scratch_shapes=(), sem=(), carry=None):
    if carry is None:
        return pl.pallas_call(body, name=name, grid=grid, in_specs=in_specs, out_specs=out_specs, out_shape=out_shape,
                              scratch_shapes=list(scratch_shapes), compiler_params=_params(*sem))(*args)
    ex = carry
    n_in, n_out, n_scr = len(args), len(out_shape), len(scratch_shapes)

    def carried(*refs):
        ins, cin = refs[:n_in], refs[n_in:n_in + ex.n]
        at = n_in + ex.n
        outs, cout = refs[at:at + n_out], refs[at + n_out:at + n_out + ex.n]
        at += n_out + ex.n
        scr, sems = refs[at:at + n_scr], refs[at + n_scr:]
        ids = [pl.program_id(a) for a in range(len(grid))]
        first = functools.reduce(jnp.logical_and, [i == 0 for i in ids])
        last = functools.reduce(jnp.logical_and, [i == g - 1 for i, g in zip(ids, grid)])

        @pl.when(first)
        def _():
            ex.start(cin, cout, sems)

        body(*ins, *outs, *scr)

        @pl.when(last)
        def _():
            ex.finish(cin, cout, sems)

    res = pl.pallas_call(
        carried, name=name, grid=grid, in_specs=list(in_specs) + [ANY] * ex.n, out_specs=list(out_specs) + [ANY] * ex.n,
        out_shape=list(out_shape) + ex.out_shape(), scratch_shapes=list(scratch_shapes) + ex.scratch(),
        compiler_params=_params(*("arbitrary",) * len(grid)),
    )(*args, *ex.operands)
    ex.take(res[n_out:])
    return res[:n_out]


class W:
    def __init__(self, arr, cols):
        self.arr, self.cols = arr, cols
        self.k, self.n = arr.shape
        self.shard_cols = self.n // N_DEV if cols else self.n


def _matmul(name, grid, operands, specs, dims, tile, extras, outs, out_specs, epilogue, carry=None, sums=0):
    assert grid[2] == 1
    n_ex, n_out = len(extras), len(outs)

    def body(*refs):
        a_ref, b_ref = refs[0], refs[1]
        ex_refs = refs[2:2 + n_ex]
        out_refs = refs[2 + n_ex:2 + n_ex + n_out]
        sum_refs = refs[2 + n_ex + n_out:]
        if sums:
            @pl.when(jnp.logical_and(pl.program_id(0) == 0, pl.program_id(1) == 0))
            def _():
                for r in sum_refs:
                    r[...] = jnp.zeros(r.shape, F32)
        acc = lax.dot_general(a_ref[...].astype(BF16), b_ref[...].astype(BF16), (dims, ((), ())),
                              preferred_element_type=F32)
        res = epilogue(acc, *[r[...] for r in ex_refs]) if epilogue else (acc,) * n_out
        for r, v in zip(out_refs, res[:n_out]):
            r[...] = v.astype(r.dtype)
        for r, v in zip(sum_refs, res[n_out:]):
            r[...] += v

    total = pl.BlockSpec((1, LANE), lambda i, j, c: (0, 0))
    return _call(name, body, list(operands) + [a for a, _ in extras], grid=grid,
                 in_specs=list(specs) + [s for _, s in extras], out_specs=list(out_specs) + [total] * sums,
                 out_shape=list(outs) + [jax.ShapeDtypeStruct((1, LANE), F32)] * sums,
                 sem=("arbitrary",) * 3 if sums else ("parallel", "parallel", "arbitrary"), carry=carry)


def _extra_specs(extras, tm, tn):
    out = []
    for arr, kind in extras:
        if kind == "tile":
            out.append((arr, pl.BlockSpec((tm, tn), lambda i, j, c: (i, j))))
        else:
            out.append((arr, pl.BlockSpec((1, tn), lambda i, j, c: (0, j))))
    return out


def _tile_cols(contraction, streams):
    left = V7X_VMEM_LIMIT - V7X_VMEM_LIMIT // 8 - 2 * 1024 * contraction * 2
    for cols in (1024, 512, 256):
        if 2 * cols * (1024 * 4 * streams + contraction * 2) <= left:
            return cols
    return LANE


def _mm_nn(name, a, w, out_dtypes, epilogue=None, extras=(), carry=None, sums=0):
    m, k = a.shape
    assert k == w.k
    tm = 1024 if a.dtype == BF16 else 512
    tn = _divisor(w.n, _tile_cols(k, len(out_dtypes) + sum(kind == "tile" for _, kind in extras)))
    assert tm * k * a.dtype.itemsize <= 16 * 2 ** 20, (name, tm, k)
    grid = (m // tm, w.n // tn, 1)
    specs = [pl.BlockSpec((tm, k), lambda i, j, c: (i, 0)), pl.BlockSpec((k, tn), lambda i, j, c: (0, j))]
    outs = [jax.ShapeDtypeStruct((m, w.n), d) for d in out_dtypes]
    out_specs = [pl.BlockSpec((tm, tn), lambda i, j, c: (i, j)) for _ in outs]
    return _matmul(name, grid, (a, w.arr), specs, ((1,), (0,)), (tm, tn), _extra_specs(extras, tm, tn), outs, out_specs,
                   epilogue, carry, sums)


def _mm_nt(name, dy, w, out_dtypes, epilogue=None, extras=(), carry=None):
    m, n = dy.shape
    assert n == w.n and dy.dtype == BF16
    tm = 1024
    to = _divisor(w.k, _tile_cols(n, len(out_dtypes) + sum(kind == "tile" for _, kind in extras)))
    assert tm * n * dy.dtype.itemsize <= 16 * 2 ** 20, (name, tm, n)
    grid = (m // tm, w.k // to, 1)
    specs = [pl.BlockSpec((tm, n), lambda i, j, c: (i, 0)), pl.BlockSpec((to, n), lambda i, j, c: (j, 0))]
    outs = [jax.ShapeDtypeStruct((m, w.k), d) for d in out_dtypes]
    out_specs = [pl.BlockSpec((tm, to), lambda i, j, c: (i, j)) for _ in outs]
    return _matmul(name, grid, (dy, w.arr), specs, ((1,), (1,)), (tm, to), _extra_specs(extras, tm, to), outs, out_specs,
                   epilogue, carry)


def _mm_tn(name, a, dy, like, carry=None):
    m, k = a.shape
    n = dy.shape[1]
    assert (k, n) == (like.k, like.n)
    tk = _divisor(k, 1024 if a.dtype == BF16 else 512)
    tn = _divisor(n, 1024)
    grid = (k // tk, n // tn, 1)
    specs = [pl.BlockSpec((m, tk), lambda i, j, c: (0, i)), pl.BlockSpec((m, tn), lambda i, j, c: (0, j))]
    out_specs = [pl.BlockSpec((tk, tn), lambda i, j, c: (i, j))]
    (g,) = _matmul(name, grid, (a, dy), specs, ((0,), (0,)), (tk, tn), [], [jax.ShapeDtypeStruct((k, n), GRAD_DTYPE)],
                   out_specs, None, carry)
    return g if like.cols else g.reshape(N_DEV, k // N_DEV, n)


def _rows(arr, blk=0, width=None):
    return ("rows", arr, blk, width or arr.shape[1])


def _full(arr):
    return ("full", arr)


def _by_residue(arr):
    return ("residue", arr)


def _rowwise(name, fn, ins, outs, reds=(), ts=256, carry=None, scratch=()):
    s = next(i[1].shape[0] if i[0] == "rows" else i[1].shape[0] * i[1].shape[1] for i in ins if i[0] != "full")
    n_in, n_out, n_red = len(ins), len(outs), len(reds)
    in_specs = []
    for i in ins:
        if i[0] == "rows":
            in_specs.append(pl.BlockSpec((ts, i[3]), functools.partial(lambda t, blk: (t, blk), blk=i[2])))
        elif i[0] == "residue":
            d, _, w = i[1].shape
            in_specs.append(pl.BlockSpec((d, ts // d, w), lambda t: (0, t, 0)))
        else:
            in_specs.append(pl.BlockSpec(i[1].shape, functools.partial(lambda t, nd: (0,) * nd, nd=i[1].ndim)))
    out_shape, out_specs = [], []
    for o in outs:
        if len(o) == 2:
            out_shape.append(jax.ShapeDtypeStruct((s, o[0]), o[1]))
            out_specs.append(pl.BlockSpec((ts, o[0]), lambda t: (t, 0)))
        else:
            out_shape.append(jax.ShapeDtypeStruct((o[2], s // o[2], o[0]), o[1]))
            out_specs.append(pl.BlockSpec((o[2], ts // o[2], o[0]), lambda t: (0, t, 0)))
    out_shape += [jax.ShapeDtypeStruct(r, F32) for r in reds]
    out_specs += [pl.BlockSpec(r, lambda t: (0, 0)) for r in reds]

    def body(*refs):
        red_refs = refs[n_in + n_out:n_in + n_out + n_red]
        if red_refs:
            @pl.when(pl.program_id(0) == 0)
            def _():
                for r in red_refs:
                    r[...] = jnp.zeros(r.shape, F32)
        fn(refs[:n_in], refs[n_in:n_in + n_out], red_refs, *refs[n_in + n_out + n_red:])

    return _call(name, body, [i[1] for i in ins], grid=(s // ts,), in_specs=in_specs, out_specs=out_specs,
                 out_shape=out_shape, scratch_shapes=list(scratch), sem=("arbitrary" if reds else "parallel",), carry=carry)


def _ln_stats(x):
    mu = jnp.mean(x, axis=-1, keepdims=True)
    xc = x - mu
    var = jnp.mean(xc * xc, axis=-1, keepdims=True)
    return xc * lax.rsqrt(var + LN_EPS), lax.rsqrt(var + LN_EPS)


def _layer_norm(name, x, g, b, out_dtypes):
    def fn(i, o, r):
        xhat, _ = _ln_stats(i[0][...])
        y = xhat * i[1][...] + i[2][...]
        for ref in o:
            ref[...] = y.astype(ref.dtype)

    return _rowwise(name, fn, [_rows(x), _full(g), _full(b)], [(x.shape[1], d) for d in out_dtypes])


def _ln_bwd_tile(x, g, dy):
    xhat, rstd = _ln_stats(x)
    dyg = dy * g
    m1 = jnp.mean(dyg, axis=-1, keepdims=True)
    m2 = jnp.mean(dyg * xhat, axis=-1, keepdims=True)
    dx = rstd * (dyg - m1 - xhat * m2)
    return dx, jnp.sum(dy * xhat, axis=0, keepdims=True), jnp.sum(dy, axis=0, keepdims=True)


def _layer_norm_bwd(name, x, g, dy, carry=None):
    d = x.shape[1]

    def fn(i, o, r):
        dx, dg, db = _ln_bwd_tile(i[0][...], i[1][...], i[2][...])
        o[0][...] = dx
        o[1][...] = dx.astype(BF16)
        r[0][...] += dg
        r[1][...] += db

    return _rowwise(name, fn, [_rows(x), _full(g), _rows(dy)], [(d, F32), (d, BF16)], [(1, d), (1, d)], carry=carry)


def _conv_fwd(name, glu, dw, dw_b, ts=512, carry=None):
    s, c = glu.shape
    tc = dw.shape[2]
    per = ts // CONV_HALO
    back = CONV_HALO - (CONV_WIDTH - 1)

    def body(cur_ref, prev_ref, w_ref, b_ref, out_ref, buf):
        i = pl.program_id(1)
        buf[pl.ds(0, CONV_HALO), :] = jnp.where(i > 0, prev_ref[...], 0.0)
        buf[pl.ds(CONV_HALO, ts), :] = cur_ref[...]
        for r0 in range(0, ts, CONV_ROWS):
            acc = jnp.broadcast_to(b_ref[...], (CONV_ROWS, tc))
            for j in range(CONV_WIDTH):
                acc = acc + w_ref[j:j + 1, :] * buf[pl.ds(r0 + back + j, CONV_ROWS), :]
            out_ref[pl.ds(r0, CONV_ROWS), :] = acc

    (out,) = _call(
        name, body, [glu, glu, dw, dw_b], grid=(c // tc, s // ts),
        in_specs=[pl.BlockSpec((ts, tc), lambda j, i: (i, j)),
                  pl.BlockSpec((CONV_HALO, tc), lambda j, i: (jnp.maximum(i * per - 1, 0), j)),
                  pl.BlockSpec((None, CONV_WIDTH, tc), lambda j, i: (j, 0, 0)),
                  pl.BlockSpec((1, tc), lambda j, i: (0, j))],
        out_specs=[pl.BlockSpec((ts, tc), lambda j, i: (i, j))],
        out_shape=[jax.ShapeDtypeStruct((s, c), F32)],
        scratch_shapes=[pltpu.VMEM((ts + CONV_HALO, tc), F32)],
        sem=("parallel", "parallel"), carry=carry)
    return out


def _conv_bwd(name, glu, dc, dw, ts=512, carry=None):
    s, c = glu.shape
    tc = dw.shape[2]
    per = ts // CONV_HALO
    back = CONV_HALO - (CONV_WIDTH - 1)
    last = s // ts - 1

    def body(g_ref, gprev_ref, dc_ref, dcnext_ref, w_ref, dglu_ref, ddw_ref, ddb_ref, gbuf, dbuf):
        i = pl.program_id(1)

        @pl.when(i == 0)
        def _():
            ddw_ref[...] = jnp.zeros(ddw_ref.shape, F32)
            ddb_ref[...] = jnp.zeros(ddb_ref.shape, F32)

        gbuf[pl.ds(0, CONV_HALO), :] = jnp.where(i > 0, gprev_ref[...], 0.0)
        gbuf[pl.ds(CONV_HALO, ts), :] = g_ref[...]
        dbuf[pl.ds(0, ts), :] = dc_ref[...]
        dbuf[pl.ds(ts, CONV_HALO), :] = jnp.where(i < last, dcnext_ref[...], 0.0)
        taps = [jnp.zeros((1, tc), F32)] * CONV_WIDTH
        for r0 in range(0, ts, CONV_ROWS):
            d_here = dbuf[pl.ds(r0, CONV_ROWS), :]
            acc = jnp.zeros((CONV_ROWS, tc), F32)
            for j in range(CONV_WIDTH):
                acc = acc + w_ref[j:j + 1, :] * dbuf[pl.ds(r0 + (CONV_WIDTH - 1) - j, CONV_ROWS), :]
                taps[j] = taps[j] + jnp.sum(d_here * gbuf[pl.ds(r0 + back + j, CONV_ROWS), :], axis=0, keepdims=True)
            dglu_ref[pl.ds(r0, CONV_ROWS), :] = acc
        for j in range(CONV_WIDTH):
            ddw_ref[j:j + 1, :] += taps[j]
        ddb_ref[...] += jnp.sum(dc_ref[...], axis=0, keepdims=True)

    return _call(
        name, body, [glu, glu, dc, dc, dw], grid=(c // tc, s // ts),
        in_specs=[pl.BlockSpec((ts, tc), lambda j, i: (i, j)),
                  pl.BlockSpec((CONV_HALO, tc), lambda j, i: (jnp.maximum(i * per - 1, 0), j)),
                  pl.BlockSpec((ts, tc), lambda j, i: (i, j)),
                  pl.BlockSpec((CONV_HALO, tc), lambda j, i: (jnp.minimum((i + 1) * per, (last + 1) * per - 1), j)),
                  pl.BlockSpec((None, CONV_WIDTH, tc), lambda j, i: (j, 0, 0))],
        out_specs=[pl.BlockSpec((ts, tc), lambda j, i: (i, j)),
                   pl.BlockSpec((CONV_WIDTH, tc), lambda j, i: (0, j)),
                   pl.BlockSpec((1, tc), lambda j, i: (0, j))],
        out_shape=[jax.ShapeDtypeStruct((s, c), F32), jax.ShapeDtypeStruct((CONV_WIDTH, c), F32),
                   jax.ShapeDtypeStruct((1, c), F32)],
        scratch_shapes=[pltpu.VMEM((ts + CONV_HALO, tc), F32), pltpu.VMEM((ts + CONV_HALO, tc), F32)],
        sem=("parallel", "arbitrary"), carry=carry)


def _rope_tables(positions):
    half = HEAD_DIM // 2
    inv = (np.float32(ROPE_THETA) ** (-np.arange(half, dtype=np.float32) * np.float32(2.0 / HEAD_DIM))).astype(np.float32)
    inv_freq = jnp.asarray(np.concatenate([inv, inv])[None, :])
    sign = jnp.asarray(np.concatenate([-np.ones(half, np.float32), np.ones(half, np.float32)])[None, :])

    def fn(i, o, r):
        ang = i[0][...].astype(F32) * i[1][...]
        o[0][...] = jnp.cos(ang)
        o[1][...] = jnp.sin(ang) * i[2][...]

    return _rowwise("rope_tables", fn, [_rows(positions), _full(inv_freq), _full(sign)], [(HEAD_DIM, F32), (HEAD_DIM, F32)], ts=512)


def _rot(x, cos, sin):
    return x * cos + pltpu.roll(x, HEAD_DIM // 2, 1) * sin


def _unrot(x, cos, sin):
    return x * cos - pltpu.roll(x, HEAD_DIM // 2, 1) * sin


def _split_rows(scr, value, d):
    if d == 1:
        return [value]
    scr[...] = value
    return [scr[pl.ds(r, scr.shape[0] // d, stride=d), :] for r in range(d)]


def _join_rows(scr, planes):
    d = len(planes)
    if d == 1:
        return planes[0]
    for r, plane in enumerate(planes):
        scr[pl.ds(r, scr.shape[0] // d, stride=d), :] = plane
    return scr[...]


def _lane(h, shape):
    return lax.broadcasted_iota(jnp.int32, shape, 1) == h


def _attn_specs(width):
    cur = pl.BlockSpec((None, ATTN_BLOCK, width), lambda r, n: (r, n, 0))
    prev = pl.BlockSpec((None, ATTN_BLOCK, width), lambda r, n: (r, jnp.maximum(n - 1, 0), 0))
    return cur, prev


def _masks(n):
    row = lax.broadcasted_iota(jnp.int32, (ATTN_BLOCK, ATTN_BLOCK), 0)
    col = lax.broadcasted_iota(jnp.int32, (ATTN_BLOCK, ATTN_BLOCK), 1)
    return col <= row, jnp.logical_and(col >= row, n > 0)


_NT = (((1,), (1,)), ((), ()))
_TN = (((0,), (0,)), ((), ()))
_NN = (((1,), (0,)), ((), ()))


def _attn_fwd(name, q, k, v, carry=None):
    dil, ln, d = k.shape
    nh = d // HEAD_DIM
    nb = ln // ATTN_BLOCK
    scale = HEAD_DIM ** -0.5

    def body(q_ref, kc_ref, kp_ref, vc_ref, vp_ref, o_ref, l_ref):
        mask_c, mask_p = _masks(pl.program_id(1))
        mask = jnp.concatenate([mask_p, mask_c], axis=1)
        stats = jnp.zeros((ATTN_BLOCK, LANE), F32)
        for h in range(nh):
            hs = slice(h * HEAD_DIM, (h + 1) * HEAD_DIM)
            keys = jnp.concatenate([kp_ref[:, hs], kc_ref[:, hs]], axis=0)
            vals = jnp.concatenate([vp_ref[:, hs], vc_ref[:, hs]], axis=0)
            sc = jnp.where(mask, lax.dot_general(q_ref[:, hs], keys, _NT, preferred_element_type=F32) * scale, NEG)
            m = jnp.max(sc, axis=1, keepdims=True)
            p = jnp.exp(sc - m)
            l = jnp.sum(p, axis=1, keepdims=True)
            o_ref[:, hs] = lax.dot_general(p.astype(BF16), vals, _NN, preferred_element_type=F32) / l
            stats = jnp.where(_lane(h, stats.shape), m + jnp.log(l), stats)
        l_ref[...] = stats

    (cur, prev), (stat, _) = _attn_specs(d), _attn_specs(LANE)
    return _call(name, body, [q, k, k, v, v], grid=(dil, nb), in_specs=[cur, cur, prev, cur, prev], out_specs=[cur, stat],
                 out_shape=[jax.ShapeDtypeStruct((dil, ln, d), F32), jax.ShapeDtypeStruct((dil, ln, LANE), F32)],
                 sem=("parallel", "parallel"), carry=carry)


def _attn_dq(name, q, k, v, do, lse, dsum, carry=None):
    dil, ln, d = k.shape
    nh = d // HEAD_DIM
    nb = ln // ATTN_BLOCK
    scale = HEAD_DIM ** -0.5

    def body(q_ref, kc_ref, kp_ref, vc_ref, vp_ref, do_ref, l_ref, d_ref, dq_ref):
        mask_c, mask_p = _masks(pl.program_id(1))
        mask = jnp.concatenate([mask_p, mask_c], axis=1)
        for h in range(nh):
            hs = slice(h * HEAD_DIM, (h + 1) * HEAD_DIM)
            keys = jnp.concatenate([kp_ref[:, hs], kc_ref[:, hs]], axis=0)
            vals = jnp.concatenate([vp_ref[:, hs], vc_ref[:, hs]], axis=0)
            sc = lax.dot_general(q_ref[:, hs], keys, _NT, preferred_element_type=F32) * scale
            p = jnp.where(mask, jnp.exp(jnp.where(mask, sc, NEG) - l_ref[:, h:h + 1]), 0.0)
            dp = lax.dot_general(do_ref[:, hs], vals, _NT, preferred_element_type=F32)
            ds = p * (dp - d_ref[:, h:h + 1])
            dq_ref[:, hs] = lax.dot_general(ds.astype(BF16), keys, _NN, preferred_element_type=F32) * scale

    (cur, prev), (stat, _) = _attn_specs(d), _attn_specs(LANE)
    (dq,) = _call(name, body, [q, k, k, v, v, do, lse, dsum], grid=(dil, nb),
                  in_specs=[cur, cur, prev, cur, prev, cur, stat, stat], out_specs=[cur],
                  out_shape=[jax.ShapeDtypeStruct((dil, ln, d), F32)], sem=("parallel", "parallel"), carry=carry)
    return dq


def _attn_dkv(name, q, k, v, do, lse, dsum, carry=None):
    dil, ln, d = k.shape
    nh = d // HEAD_DIM
    nb = ln // ATTN_BLOCK
    scale = HEAD_DIM ** -0.5

    def body(k_ref, v_ref, qc_ref, qn_ref, doc_ref, don_ref, lc_ref, lnx_ref, dc_ref, dn_ref, dk_ref, dv_ref):
        n = pl.program_id(1)
        row = lax.broadcasted_iota(jnp.int32, (ATTN_BLOCK, ATTN_BLOCK), 0)
        col = lax.broadcasted_iota(jnp.int32, (ATTN_BLOCK, ATTN_BLOCK), 1)
        mask = jnp.concatenate([row <= col, jnp.logical_and(row >= col, n < nb - 1)], axis=1)
        lse_t = jnp.concatenate([lc_ref[...].T, lnx_ref[...].T], axis=1)
        dsum_t = jnp.concatenate([dc_ref[...].T, dn_ref[...].T], axis=1)
        for h in range(nh):
            hs = slice(h * HEAD_DIM, (h + 1) * HEAD_DIM)
            qs = jnp.concatenate([qc_ref[:, hs], qn_ref[:, hs]], axis=0)
            douts = jnp.concatenate([doc_ref[:, hs], don_ref[:, hs]], axis=0)
            sc = lax.dot_general(k_ref[:, hs], qs, _NT, preferred_element_type=F32) * scale
            p = jnp.where(mask, jnp.exp(jnp.where(mask, sc, NEG) - lse_t[h:h + 1, :]), 0.0)
            dp = lax.dot_general(v_ref[:, hs], douts, _NT, preferred_element_type=F32)
            ds = p * (dp - dsum_t[h:h + 1, :])
            dv_ref[:, hs] = lax.dot_general(p.astype(BF16), douts, _NN, preferred_element_type=F32)
            dk_ref[:, hs] = lax.dot_general(ds.astype(BF16), qs, _NN, preferred_element_type=F32) * scale

    def specs(width):
        cur = pl.BlockSpec((None, ATTN_BLOCK, width), lambda r, n: (r, n, 0))
        nxt = pl.BlockSpec((None, ATTN_BLOCK, width), lambda r, n: (r, jnp.minimum(n + 1, nb - 1), 0))
        return cur, nxt

    (cur, nxt), (stat, stat_next) = specs(d), specs(LANE)
    return _call(name, body, [k, v, q, q, do, do, lse, lse, dsum, dsum], grid=(dil, nb),
                 in_specs=[cur, cur, cur, nxt, cur, nxt, stat, stat_next, stat, stat_next], out_specs=[cur, cur],
                 out_shape=[jax.ShapeDtypeStruct((dil, ln, d), F32)] * 2, sem=("parallel", "parallel"), carry=carry)


def _adamw_tile(w, g, m, v):
    m = ADAM_B1 * m + (1.0 - ADAM_B1) * g
    v = ADAM_B2 * v + (1.0 - ADAM_B2) * (g * g)
    m_hat = m / (1.0 - ADAM_B1 ** ADAM_STEP)
    v_hat = v / (1.0 - ADAM_B2 ** ADAM_STEP)
    delta = -ADAM_LR * (m_hat / (jnp.sqrt(v_hat) + ADAM_EPS) + ADAM_WD * w)
    return delta, m, v


def _adamw_big(name, parts, w, m, v, carry=None):
    layers, r, c = w.shape
    assert len(parts) == layers and all(sum(ch.shape[1] for ch in per_layer) == r for per_layer in parts)
    every = [ch for per_layer in parts for ch in per_layer]
    per_row = 2 * c * (len(every) * N_DEV * every[0].dtype.itemsize + 7 * 4)
    tr = 16
    while tr * 2 <= min(min(ch.shape[1] for ch in every), V7X_VMEM_LIMIT // 2 // per_row) and all(ch.shape[1] % (tr * 2) == 0 for ch in every):
        tr *= 2
    pieces = []
    for ly, per_layer in enumerate(parts):
        at = 0
        for ch in per_layer:
            pieces.append((ly, at, ch.shape[1] // tr, ch))
            at += ch.shape[1] // tr

    def within(layer, i, ly, first, tiles):
        return jnp.logical_and(layer == ly, jnp.logical_and(i >= first, i < first + tiles))

    def body(*refs):
        part_refs = refs[:len(pieces)]
        w_ref, m_ref, v_ref, g_out, d_out, m_out, v_out = refs[len(pieces):]
        layer, i = pl.program_id(0), pl.program_id(1)
        for (ly, first, tiles, _), part_ref in zip(pieces, part_refs):
            @pl.when(within(layer, i, ly, first, tiles))
            def _(part_ref=part_ref):
                g = part_ref[0].astype(F32)
                for dev in range(1, N_DEV):
                    g = g + part_ref[dev].astype(F32)
                delta, mn, vn = _adamw_tile(w_ref[...], g, m_ref[...], v_ref[...])
                g_out[...] = g
                d_out[...] = delta
                m_out[...] = mn
                v_out[...] = vn

    def part_index(layer, i, ly, first, tiles):
        return (0, jnp.where(within(layer, i, ly, first, tiles), i - first, 0), 0)

    own = pl.BlockSpec((None, tr, c), lambda ly, i: (ly, i, 0))
    part_specs = [pl.BlockSpec((N_DEV, tr, c), functools.partial(part_index, ly=ly, first=first, tiles=tiles))
                  for ly, first, tiles, _ in pieces]
    return _call(name, body, [ch for _, _, _, ch in pieces] + [w, m, v], grid=(layers, r // tr), in_specs=part_specs + [own] * 3,
                 out_specs=[own] * 4, out_shape=[jax.ShapeDtypeStruct(w.shape, F32)] * 4, sem=("parallel", "parallel"),
                 carry=carry)


def _sum_slots(name, slots):
    _, r, c = slots.shape

    def body(s_ref, o_ref):
        g = s_ref[0]
        for j in range(1, N_DEV):
            g = g + s_ref[j]
        o_ref[...] = g

    return pl.pallas_call(body, name=name, out_shape=jax.ShapeDtypeStruct((r, c), F32),
                          compiler_params=_params())(slots)


def _adamw_small(name, w, g, m, v):
    def body(w_ref, g_ref, m_ref, v_ref, d_out, m_out, v_out):
        delta, mn, vn = _adamw_tile(w_ref[...], g_ref[...], m_ref[...], v_ref[...])
        d_out[...] = delta
        m_out[...] = mn
        v_out[...] = vn

    return pl.pallas_call(body, name=name, out_shape=[jax.ShapeDtypeStruct(w.shape, F32)] * 3,
                          compiler_params=_params())(w, g, m, v)


class Weights:
    def __init__(self, shards):
        self.shards, self.full, self.parts = shards, {}, {}

    @staticmethod
    def by_columns(key):
        return key.rstrip("01") in COLUMN_SHARDED

    def gather(self, *keys):
        return Exchange(gathers=[(self.shards[k], self.by_columns(k)) for k in keys], keys=keys)

    def landed(self, ex):
        for key, full in zip(ex.keys, ex.gathered):
            cols = self.by_columns(key)
            self.full[key] = W(full if cols else full.reshape(-1, full.shape[-1]), cols)

    def scatter(self, grads, gathers=()):
        if not grads and not gathers:
            return None
        return Exchange(gathers=gathers, scatters=[(g, self.by_columns(k), part) for k, (g, part) in grads.items()], keys=list(grads))

    def received(self, ex):
        for key, (first, _, of), part in zip(ex.keys, ex.s_parts, ex.parts) if ex is not None else ():
            self.parts.setdefault(key, {})[first / of] = part

    def chunks(self, key):
        return [self.parts[key][j] for j in sorted(self.parts[key])]

    def __getitem__(self, key):
        return self.full[key]


def _mlp_ple_fwd(tag, z1, p_i, ln1_g, ln1_b, ln2_g, ln2_b, wt, carries, target=None):
    h1, h1b = _layer_norm(f"ln1_{tag}", z1, ln1_g, ln1_b, (F32, BF16))
    up, act = _mm_nn(f"mlp_up_{tag}", h1b, wt["mlp_up" + tag], (F32, BF16),
                     epilogue=lambda acc: (acc, jnp.square(jnp.maximum(acc, 0.0))), carry=carries.get("mlp_up"))
    if "mlp_up" in carries:
        wt.landed(carries["mlp_up"])
    (z2,) = _mm_nn(f"mlp_down_{tag}", act, wt["mlp_down" + tag], (F32,), epilogue=lambda acc, h: (ALPHA * h + acc,),
                   extras=[(h1, "tile")], carry=carries.get("mlp_down"))
    if "mlp_down" in carries:
        wt.landed(carries["mlp_down"])
    h2, h2b = _layer_norm(f"ln2_{tag}", z2, ln2_g, ln2_b, (F32, BF16))
    (pe,) = _mm_nn(f"ple_proj_{tag}", p_i, wt["ple_proj" + tag], (F32,))

    saved = dict(z1=z1, h1b=h1b, up=up, act=act, z2=z2, h2b=h2b, p=p_i)
    if target is None:
        def gate(acc, h, e):
            out = h + e * _sigmoid(acc)
            return acc, out, out

        gp, out, outb = _mm_nn(f"ple_gate_{tag}", h2b, wt["ple_gate" + tag], (F32, F32, BF16), epilogue=gate,
                               extras=[(h2, "tile"), (pe, "tile")], carry=carries.get("ple_gate"))
        if "ple_gate" in carries:
            wt.landed(carries["ple_gate"])
        saved.update(pe=pe, gp=gp)
        return out, outb, saved

    width = z1.shape[1]

    def gate_and_loss(acc, h, e, goal):
        sg = _sigmoid(acc)
        diff = h + e * sg - goal
        d_y = diff * (1.0 / width)
        return d_y, d_y * sg, d_y * e * sg * (1.0 - sg), jnp.broadcast_to(jnp.sum(diff * diff), (1, LANE))

    d_y, d_pe, d_gp, sq = _mm_nn(f"ple_gate_{tag}", h2b, wt["ple_gate" + tag], (F32, BF16, BF16), epilogue=gate_and_loss,
                                 extras=[(h2, "tile"), (pe, "tile"), (target, "tile")], sums=1)
    saved.update(d_pe=d_pe, d_gp=d_gp)
    return d_y, 0.5 * sq[0, 0] / width, saved


WHOLE = (0, 1, 1)


def _mlp_ple_bwd(tag, d_out, sv, ln1_g, ln2_g, wt, plan, waiting):
    made = {}

    def riders(kernel, extra=()):
        items = {name + tag: (made[name], part) for name, part in plan[kernel]}
        return wt.scatter({**items, **dict(extra)})

    d_pe, d_gp = sv["d_pe"], sv["d_gp"]
    made["ple_proj"] = _mm_tn(f"g_ple_proj_{tag}", sv["p"], d_pe, wt["ple_proj" + tag])
    made["ple_gate"] = _mm_tn(f"g_ple_gate_{tag}", sv["h2b"], d_gp, wt["ple_gate" + tag])
    (d_h2,) = _mm_nt(f"d_ple_gate_{tag}", d_gp, wt["ple_gate" + tag], (F32,), epilogue=lambda acc, dy: (dy + acc,),
                     extras=[(d_out, "tile")])
    d_z2, d_z2b, g_ln2_g, g_ln2_b = _layer_norm_bwd(f"ln2_bwd_{tag}", sv["z2"], ln2_g, d_h2)
    ex = riders("g_mlp_down", waiting.items())
    made["mlp_down"] = _mm_tn(f"g_mlp_down_{tag}", sv["act"], d_z2b, wt["mlp_down" + tag], carry=ex)
    wt.received(ex)
    ex = riders("d_mlp_down")
    (d_up,) = _mm_nt(f"d_mlp_down_{tag}", d_z2b, wt["mlp_down" + tag], (BF16,),
                     epilogue=lambda acc, u: (acc * (2.0 * jnp.maximum(u, 0.0)),), extras=[(sv["up"], "tile")], carry=ex)
    wt.received(ex)
    ex = riders("g_mlp_up")
    made["mlp_up"] = _mm_tn(f"g_mlp_up_{tag}", sv["h1b"], d_up, wt["mlp_up" + tag], carry=ex)
    wt.received(ex)
    ex = riders("d_mlp_up")
    (d_h1,) = _mm_nt(f"d_mlp_up_{tag}", d_up, wt["mlp_up" + tag], (F32,), epilogue=lambda acc, dz: (ALPHA * dz + acc,),
                     extras=[(d_z2, "tile")], carry=ex)
    wt.received(ex)
    ex = riders("ln1_bwd")
    d_z1, d_z1b, g_ln1_g, g_ln1_b = _layer_norm_bwd(f"ln1_bwd_{tag}", sv["z1"], ln1_g, d_h1, carry=ex)
    wt.received(ex)
    return d_z1, d_z1b, dict(ln1_g=g_ln1_g, ln1_b=g_ln1_b, ln2_g=g_ln2_g, ln2_b=g_ln2_b), made["mlp_down"], made["mlp_up"]


def _local_step(x, p, positions, target, wt, small):
    s, d = x.shape
    nh = d // HEAD_DIM
    xb, p = x.astype(BF16), p.astype(BF16)

    ex = wt.gather("conv_w_out", "ple_proj0", "ple_gate0")
    (u,) = _mm_nn("conv_in", xb, wt["conv_w_in"], (F32,), epilogue=lambda acc, b: (acc + b,), extras=[(small["conv_b_in"], "row")],
                  carry=ex)
    wt.landed(ex)

    def glu_fn(i, o, r):
        o[0][...] = i[0][...] * _sigmoid(i[1][...])

    (glu,) = _rowwise("glu", glu_fn, [_rows(u, 0, d), _rows(u, 1, d)], [(d, F32)])
    ex = wt.gather("mlp_up0")
    c = _conv_fwd("dwconv", glu, small["conv_dw"], small["conv_dw_b"], carry=ex)
    wt.landed(ex)

    def ln_silu(i, o, r):
        xhat, _ = _ln_stats(i[0][...])
        n = xhat * i[1][...] + i[2][...]
        o[0][...] = (n * _sigmoid(n)).astype(BF16)

    (sb,) = _rowwise("conv_ln_silu", ln_silu, [_rows(c), _full(small["conv_ln_g"]), _full(small["conv_ln_b"])], [(d, BF16)])
    ex = wt.gather("attn_w_o")
    (z1,) = _mm_nn("conv_out", sb, wt["conv_w_out"], (F32,), epilogue=lambda acc, xt: (ALPHA * xt + acc,), extras=[(x, "tile")],
                   carry=ex)
    wt.landed(ex)
    x1, x1b, sv0 = _mlp_ple_fwd("0", z1, p[0], small["ln1_g"][0:1], small["ln1_b"][0:1], small["ln2_g"][0:1], small["ln2_b"][0:1], wt,
                                dict(mlp_up=wt.gather("mlp_down0"), mlp_down=wt.gather("attn_w_q"), ple_gate=wt.gather("w_kv")))

    (kvn,) = _layer_norm("kv_ln", x1, small["kv_ln_g"], small["kv_ln_b"], (BF16,))
    (kv,) = _mm_nn("kv_proj", kvn, wt["w_kv"], (F32,))
    ex = wt.gather("mlp_up1")
    (q,) = _mm_nn("q_proj", x1b, wt["attn_w_q"], (F32,), carry=ex)
    wt.landed(ex)
    cos, sin = _rope_tables(positions)
    row_scratch = [pltpu.VMEM((ROW_TILE, LANE), F32)]

    def rot_kv(i, o, r, scr):
        cs, sn = i[2][...], i[3][...]
        for h in range(nh):
            hs = slice(h * HEAD_DIM, (h + 1) * HEAD_DIM)
            for base, val in ((0, _rot(i[0][:, hs], cs, sn)), (N_GROUPS, i[1][:, hs])):
                for g, dil in enumerate(GROUP_DILATIONS):
                    for res, plane in enumerate(_split_rows(scr, val, dil)):
                        o[base + g][res, :, hs] = plane.astype(BF16)

    by_group = [(d, BF16, dil) for dil in GROUP_DILATIONS]
    kv_groups = _rowwise("rotary_kv", rot_kv, [_rows(kv, 0, d), _rows(kv, 1, d), _rows(cos), _rows(sin)], by_group * 2,
                         ts=ROW_TILE, scratch=row_scratch)
    kg, vg = kv_groups[:N_GROUPS], kv_groups[N_GROUPS:]

    def rot_q(i, o, r, scr):
        cs, sn = i[1][...], i[2][...]
        for g, dil in enumerate(GROUP_DILATIONS):
            for h in range(nh):
                hs = slice(h * HEAD_DIM, (h + 1) * HEAD_DIM)
                val = _rot(i[0][:, g * d + h * HEAD_DIM:g * d + (h + 1) * HEAD_DIM], cs, sn)
                for res, plane in enumerate(_split_rows(scr, val, dil)):
                    o[g][res, :, hs] = plane.astype(BF16)

    qg = _rowwise("rotary_q", rot_q, [_rows(q), _rows(cos), _rows(sin)], by_group, ts=ROW_TILE, scratch=row_scratch)

    og, lg = zip(*[_attn_fwd(f"attn_fwd_{g}", qg[g], kg[g], vg[g]) for g in range(N_GROUPS)])

    def merge(i, o, r, scr):
        lses = []
        for g, dil in enumerate(GROUP_DILATIONS):
            lses.append(_join_rows(scr, [i[N_GROUPS + g][res] for res in range(dil)]))
        top = functools.reduce(jnp.maximum, lses)
        es = [jnp.exp(l - top) for l in lses]
        den = functools.reduce(lambda a, b: a + b, es)
        total = top + jnp.log(den)
        for g, dil in enumerate(GROUP_DILATIONS):
            for res, plane in enumerate(_split_rows(scr, total, dil)):
                o[2 + g][res] = plane
        ws = [e / den for e in es]
        for h in range(nh):
            hs = slice(h * HEAD_DIM, (h + 1) * HEAD_DIM)
            out = jnp.zeros((ROW_TILE, HEAD_DIM), F32)
            for g, dil in enumerate(GROUP_DILATIONS):
                og_h = _join_rows(scr, [i[g][res, :, hs] for res in range(dil)])
                out = out + ws[g][:, h:h + 1] * og_h
            o[0][:, hs] = out
            o[1][:, hs] = out.astype(BF16)

    merged = _rowwise("attn_merge", merge, [_by_residue(t) for t in og + lg],
                      [(d, F32), (d, BF16)] + [(LANE, F32, dil) for dil in GROUP_DILATIONS], ts=ROW_TILE, scratch=row_scratch)
    o, ob, lse_g = merged[0], merged[1], merged[2:]
    (z1b,) = _mm_nn("attn_out", ob, wt["attn_w_o"], (F32,), epilogue=lambda acc, xt: (ALPHA * xt + acc,), extras=[(x1, "tile")])
    d_y, loss, sv1 = _mlp_ple_fwd("1", z1b, p[1], small["ln1_g"][1:2], small["ln1_b"][1:2], small["ln2_g"][1:2], small["ln2_b"][1:2],
                                  wt, dict(mlp_up=wt.gather("mlp_down1"), mlp_down=wt.gather("ple_proj1", "ple_gate1")), target=target)

    plan = dict(g_mlp_down=[("ple_proj", WHOLE), ("ple_gate", WHOLE)], d_mlp_down=[("mlp_down", (0, 3, 8))],
                g_mlp_up=[("mlp_down", (3, 6, 8))], d_mlp_up=[("mlp_down", (6, 8, 8))], ln1_bwd=[])
    d_z1, d_z1b, g1, _, g_up1 = _mlp_ple_bwd("1", d_y, sv1, small["ln1_g"][1:2], small["ln2_g"][1:2], wt, plan, {})
    g_wo = _mm_tn("g_attn_out", ob, d_z1b, wt["attn_w_o"])
    (d_o,) = _mm_nt("d_attn_out", d_z1b, wt["attn_w_o"], (F32,))

    def dsum_fn(i, o, r, scr):
        stats = jnp.zeros((ROW_TILE, LANE), F32)
        for h in range(nh):
            hs = slice(h * HEAD_DIM, (h + 1) * HEAD_DIM)
            dout = i[0][:, hs]
            stats = jnp.where(_lane(h, stats.shape), jnp.sum(dout * i[1][:, hs], axis=1, keepdims=True), stats)
            for g, dil in enumerate(GROUP_DILATIONS):
                for res, plane in enumerate(_split_rows(scr, dout, dil)):
                    o[g][res, :, hs] = plane.astype(BF16)
        for g, dil in enumerate(GROUP_DILATIONS):
            for res, plane in enumerate(_split_rows(scr, stats, dil)):
                o[N_GROUPS + g][res] = plane

    res_ = _rowwise("attn_dsum", dsum_fn, [_rows(d_o), _rows(o)], by_group + [(LANE, F32, dil) for dil in GROUP_DILATIONS],
                    ts=ROW_TILE, scratch=row_scratch)
    dog, dsum_g = res_[:N_GROUPS], res_[N_GROUPS:]
    dqs, dks, dvs = [], [], []
    riders = [wt.scatter({"mlp_up1": (g_up1, (q, q + 1, 4))}) for q in range(4)] + [wt.scatter({"attn_w_o": (g_wo, WHOLE)}), None]
    for g in range(N_GROUPS):
        dqs.append(_attn_dq(f"attn_dq_{g}", qg[g], kg[g], vg[g], dog[g], lse_g[g], dsum_g[g], carry=riders[2 * g]))
        dk, dv = _attn_dkv(f"attn_dkv_{g}", qg[g], kg[g], vg[g], dog[g], lse_g[g], dsum_g[g], carry=riders[2 * g + 1])
        dks.append(dk)
        dvs.append(dv)
    for ex in riders:
        wt.received(ex)

    def unrot_q(i, o, r, scr):
        cs, sn = i[N_GROUPS][...], i[N_GROUPS + 1][...]
        for g, dil in enumerate(GROUP_DILATIONS):
            for h in range(nh):
                hs = slice(h * HEAD_DIM, (h + 1) * HEAD_DIM)
                dq = _join_rows(scr, [i[g][res, :, hs] for res in range(dil)])
                o[0][:, g * d + h * HEAD_DIM:g * d + (h + 1) * HEAD_DIM] = _unrot(dq, cs, sn).astype(BF16)

    (d_q,) = _rowwise("rotary_q_bwd", unrot_q, [_by_residue(t) for t in dqs] + [_rows(cos), _rows(sin)], [(N_GROUPS * d, BF16)],
                      ts=ROW_TILE, scratch=row_scratch)

    def unrot_kv(i, o, r, scr):
        cs, sn = i[2 * N_GROUPS][...], i[2 * N_GROUPS + 1][...]
        for h in range(nh):
            hs = slice(h * HEAD_DIM, (h + 1) * HEAD_DIM)
            for base in (0, N_GROUPS):
                tot = jnp.zeros((ROW_TILE, HEAD_DIM), F32)
                for g, dil in enumerate(GROUP_DILATIONS):
                    tot = tot + _join_rows(scr, [i[base + g][res, :, hs] for res in range(dil)])
                if base == 0:
                    o[0][:, hs] = _unrot(tot, cs, sn).astype(BF16)
                else:
                    o[0][:, d + h * HEAD_DIM:d + (h + 1) * HEAD_DIM] = tot.astype(BF16)

    (d_kv,) = _rowwise("rotary_kv_bwd", unrot_kv, [_by_residue(t) for t in dks + dvs] + [_rows(cos), _rows(sin)], [(2 * d, BF16)],
                       ts=ROW_TILE, scratch=row_scratch)
    g_wq = _mm_tn("g_q_proj", x1b, d_q, wt["attn_w_q"])
    ex = wt.scatter({"attn_w_q": (g_wq, (0, 3, 8))})
    g_wkv = _mm_tn("g_kv_proj", kvn, d_kv, wt["w_kv"], carry=ex)
    wt.received(ex)
    ex = wt.scatter({"attn_w_q": (g_wq, (3, 6, 8))})
    (d_x1a,) = _mm_nt("d_q_proj", d_q, wt["attn_w_q"], (F32,), epilogue=lambda acc, dz: (ALPHA * dz + acc,), extras=[(d_z1, "tile")],
                      carry=ex)
    wt.received(ex)
    ex = wt.scatter({"attn_w_q": (g_wq, (6, 8, 8)), "w_kv": (g_wkv, (0, 1, 8))})
    (d_kvn,) = _mm_nt("d_kv_proj", d_kv, wt["w_kv"], (F32,), carry=ex)
    wt.received(ex)

    def kv_ln_bwd(i, o, r):
        dx, dg, db = _ln_bwd_tile(i[0][...], i[1][...], i[2][...])
        dx = dx + i[3][...]
        sg = _sigmoid(i[5][...])
        o[0][...] = dx
        o[1][...] = (dx * sg).astype(BF16)
        o[2][...] = (dx * i[4][...] * sg * (1.0 - sg)).astype(BF16)
        r[0][...] += dg
        r[1][...] += db

    ex = wt.scatter({"w_kv": (g_wkv, (1, 4, 8))})
    d_x1, sv0["d_pe"], sv0["d_gp"], g_kv_ln_g, g_kv_ln_b = _rowwise(
        "kv_ln_bwd", kv_ln_bwd, [_rows(x1), _full(small["kv_ln_g"]), _rows(d_kvn), _rows(d_x1a), _rows(sv0["pe"]), _rows(sv0["gp"])],
        [(d, F32), (d, BF16), (d, BF16)], [(1, d), (1, d)], ts=128, carry=ex)
    wt.received(ex)

    plan = dict(g_mlp_down=[("ple_proj", WHOLE)], d_mlp_down=[("ple_gate", WHOLE), ("mlp_down", (0, 1, 8))],
                g_mlp_up=[("mlp_down", (1, 4, 8))], d_mlp_up=[("mlp_down", (4, 7, 8))], ln1_bwd=[("mlp_up", (0, 1, 8))])
    d_z1, d_z1b, g0, g_down0, g_up0 = _mlp_ple_bwd("0", d_x1, sv0, small["ln1_g"][0:1], small["ln2_g"][0:1], wt, plan,
                                                   {"w_kv": (g_wkv, (4, 8, 8))})
    ex = wt.scatter({"mlp_up0": (g_up0, (1, 2, 8))})
    g_wout = _mm_tn("g_conv_out", sb, d_z1b, wt["conv_w_out"], carry=ex)
    wt.received(ex)
    ex = wt.scatter({"mlp_up0": (g_up0, (2, 3, 8))})
    (d_s,) = _mm_nt("d_conv_out", d_z1b, wt["conv_w_out"], (F32,), carry=ex)
    wt.received(ex)

    def ln_silu_bwd(i, o, r):
        cx, gn, bn, ds_ = i[0][...], i[1][...], i[2][...], i[3][...]
        xhat, _ = _ln_stats(cx)
        n = xhat * gn + bn
        sg = _sigmoid(n)
        dn = ds_ * (sg * (1.0 + n * (1.0 - sg)))
        dx, dg, db = _ln_bwd_tile(cx, gn, dn)
        o[0][...] = dx
        r[0][...] += dg
        r[1][...] += db

    ex = wt.scatter({"mlp_up0": (g_up0, (3, 4, 8))})
    d_c, g_cln_g, g_cln_b = _rowwise("conv_ln_silu_bwd", ln_silu_bwd,
                                     [_rows(c), _full(small["conv_ln_g"]), _full(small["conv_ln_b"]), _rows(d_s)],
                                     [(d, F32)], [(1, d), (1, d)], carry=ex)
    wt.received(ex)
    ex = wt.scatter({"mlp_down0": (g_down0, (7, 8, 8)), "mlp_up0": (g_up0, (4, 8, 8))})
    d_glu, g_dw, g_dwb = _conv_bwd("dwconv_bwd", glu, d_c, small["conv_dw"], carry=ex)
    wt.received(ex)

    def glu_bwd(i, o, r):
        a, gt, dg_ = i[0][...], i[1][...], i[2][...]
        sg = _sigmoid(gt)
        da = dg_ * sg
        dgate = dg_ * a * sg * (1.0 - sg)
        o[0][:, 0:d] = da.astype(BF16)
        o[0][:, d:2 * d] = dgate.astype(BF16)
        r[0][:, 0:d] += jnp.sum(da, axis=0, keepdims=True)
        r[0][:, d:2 * d] += jnp.sum(dgate, axis=0, keepdims=True)

    ex = wt.scatter({"conv_w_out": (g_wout, (0, 1, 2))})
    d_u, g_bin = _rowwise("glu_bwd", glu_bwd, [_rows(u, 0, d), _rows(u, 1, d), _rows(d_glu)], [(2 * d, BF16)], [(1, 2 * d)],
                          carry=ex)
    wt.received(ex)
    ex = wt.scatter({"conv_w_out": (g_wout, (1, 2, 2))})
    g_win = _mm_tn("g_conv_in", xb, d_u, wt["conv_w_in"], carry=ex)
    wt.received(ex)
    rows = [g_bin.reshape(2, d), g_dw, g_dwb, g_cln_g, g_cln_b, g_kv_ln_g, g_kv_ln_b]
    rows += [jnp.concatenate([g0[n], g1[n]], axis=0) for n in ("ln1_g", "ln1_b", "ln2_g", "ln2_b")]
    rows, offsets = _stack_rows(rows)
    ex = wt.scatter({"conv_w_in": (g_win, (0, 4, 8))}, gathers=[(rows, False)])
    (grad_x,) = _mm_nt("d_conv_in", d_u, wt["conv_w_in"], (F32,), epilogue=lambda acc, dz: (ALPHA * dz + acc,), extras=[(d_z1, "tile")],
                       carry=ex)
    wt.received(ex)
    return loss, grad_x, ex.gathered[0], offsets, g_win


BIG = ("conv_w_in", "conv_w_out", "w_kv", "attn_w_q", "attn_w_o", "mlp_up", "mlp_down", "ple_proj", "ple_gate")
COLUMN_SHARDED = ("conv_w_in", "w_kv", "attn_w_q", "mlp_up", "ple_proj")
WEIGHTS = ("conv_w_in", "conv_b_in", "conv_dw", "conv_dw_b", "conv_ln_g", "conv_ln_b", "conv_w_out", "kv_ln_g", "kv_ln_b",
           "w_kv", "attn_w_q", "attn_w_o", "ln1_g", "ln1_b", "mlp_up", "mlp_down", "ln2_g", "ln2_b", "ple_proj", "ple_gate")


def kernel(x, p, positions, conv_w_in, conv_b_in, conv_dw, conv_dw_b, conv_ln_g, conv_ln_b, conv_w_out, kv_ln_g, kv_ln_b, w_kv, attn_w_q, attn_w_o, ln1_g, ln1_b, mlp_up, mlp_down, ln2_g, ln2_b, ple_proj, ple_gate, loss_target, m_conv_w_in, m_conv_b_in, m_conv_dw, m_conv_dw_b, m_conv_ln_g, m_conv_ln_b, m_conv_w_out, m_kv_ln_g, m_kv_ln_b, m_w_kv, m_attn_w_q, m_attn_w_o, m_ln1_g, m_ln1_b, m_mlp_up, m_mlp_down, m_ln2_g, m_ln2_b, m_ple_proj, m_ple_gate, v_conv_w_in, v_conv_b_in, v_conv_dw, v_conv_dw_b, v_conv_ln_g, v_conv_ln_b, v_conv_w_out, v_kv_ln_g, v_kv_ln_b, v_w_kv, v_attn_w_q, v_attn_w_o, v_ln1_g, v_ln1_b, v_mlp_up, v_mlp_down, v_ln2_g, v_ln2_b, v_ple_proj, v_ple_gate):
    given = dict(locals())
    wts = {n: given[n] for n in WEIGHTS}
    moms = {n: given["m_" + n] for n in WEIGHTS}
    vels = {n: given["v_" + n] for n in WEIGHTS}
    s, d = x.shape[1], x.shape[2]
    shard = d // N_DEV
    me = 4 * lax.axis_index("x") + 2 * lax.axis_index("y") + lax.axis_index("c")

    def layers_of(a):
        return a.reshape((-1,) + a.shape[-2:])

    shards = {}
    for n in BIG:
        w3 = layers_of(wts[n])
        for ly in range(w3.shape[0]):
            shards[n + str(ly) if w3.shape[0] > 1 else n] = w3[ly].astype(BF16)
    wt = Weights(shards)
    pack, at = _stack_rows([wts["conv_b_in"].reshape(2, shard), wts["conv_dw"].reshape(CONV_WIDTH, shard),
                            wts["conv_dw_b"], wts["conv_ln_g"], wts["conv_ln_b"]])
    ex = Exchange(gathers=[(shards["conv_w_in"], True), (pack, False)], keys=["conv_w_in"])
    _exchange_alone("gather_first", ex)
    wt.landed(ex)
    packed = ex.gathered[1]
    small = dict(conv_b_in=packed[:, at[0]:at[0] + 2].reshape(1, 2 * d), conv_dw=packed[:, at[1]:at[1] + CONV_WIDTH],
                 conv_dw_b=packed[:, at[2]].reshape(1, d), conv_ln_g=packed[:, at[3]].reshape(1, d),
                 conv_ln_b=packed[:, at[4]].reshape(1, d), kv_ln_g=kv_ln_g.reshape(1, d), kv_ln_b=kv_ln_b.reshape(1, d),
                 ln1_g=ln1_g, ln1_b=ln1_b, ln2_g=ln2_g, ln2_b=ln2_b)

    loss, grad_x, all_rows, at, g_win = _local_step(x[0], p[:, 0], positions.reshape(s, 1), loss_target[0], wt, small)
    loss = lax.psum(loss, ("x", "y", "c"))

    riding = dict(mlp_down={"conv_w_in": (g_win, (4, 8, 8))})
    out = {}
    for n in list(riding) + [n for n in BIG if n not in riding]:
        w3 = layers_of(wts[n])
        keys = [n + str(ly) if w3.shape[0] > 1 else n for ly in range(w3.shape[0])]
        ex = wt.scatter(riding.get(n, {}))
        res = _adamw_big("adamw_" + n, [wt.chunks(k) for k in keys], w3, layers_of(moms[n]), layers_of(vels[n]), carry=ex)
        wt.received(ex)
        out[n] = [r.reshape(wts[n].shape) for r in res]
    tot = _sum_slots("sum_small_grads", all_rows)
    mine = lax.dynamic_slice_in_dim(tot, me * shard, shard, axis=1)
    b_in = lax.dynamic_slice_in_dim(tot[at[0]:at[0] + 2].reshape(1, 2 * d), me * 2 * shard, 2 * shard, axis=1)
    g_small = dict(conv_b_in=b_in, conv_dw=mine[at[1]:at[1] + CONV_WIDTH].reshape(conv_dw.shape), conv_dw_b=mine[at[2]:at[2] + 1],
                   conv_ln_g=mine[at[3]:at[3] + 1], conv_ln_b=mine[at[4]:at[4] + 1], kv_ln_g=tot[at[5]], kv_ln_b=tot[at[6]])
    for j, n in enumerate(("ln1_g", "ln1_b", "ln2_g", "ln2_b")):
        g_small[n] = tot[at[7 + j]:at[7 + j] + DEPTH]
    order = [n for n in WEIGHTS if n not in BIG]

    def flat(t):
        return _stack_rows([t[n].reshape(-1, shard) for n in order])

    (w_s, at), (g_s, _), (m_s, _), (v_s, _) = flat(wts), flat(g_small), flat(moms), flat(vels)
    d_s, m_s, v_s = _adamw_small("adamw_small", w_s, g_s, m_s, v_s)
    for n, a in zip(order, at):
        nrow = wts[n].size // shard
        out[n] = [g_small[n].reshape(wts[n].shape)] + [t[a:a + nrow].reshape(wts[n].shape) for t in (d_s, m_s, v_s)]
    return (loss, grad_x[None], *[out[n][0] for n in WEIGHTS], *[out[n][1] for n in WEIGHTS],
            *[out[n][2] for n in WEIGHTS], *[out[n][3] for n in WEIGHTS])
```

```python
import functools

import numpy as np
import jax
import jax.numpy as jnp
from jax import lax
from jax.experimental import pallas as pl
from jax.experimental.pallas import tpu as pltpu

F32, BF16 = jnp.float32, jnp.bfloat16

N_DEV = 8
HEAD_DIM = 128
ATTN_BLOCK = 128
GROUP_DILATIONS = (1, 4, 16)
N_GROUPS = len(GROUP_DILATIONS)
CONV_WIDTH = 31
CONV_HALO = 32
CONV_ROWS = 64
ROPE_THETA = 10000.0
LN_EPS = 1e-5
DEPTH = 2
ALPHA = (2 * DEPTH) ** 0.25
ADAM_LR, ADAM_B1, ADAM_B2, ADAM_EPS, ADAM_WD, ADAM_STEP = 0.001, 0.9, 0.999, 1e-08, 0.01, 10
NEG = -1e30
V7X_VMEM_LIMIT = 56 * 2 ** 20
LANE = 128
SUBLANES = 8
ROW_TILE = 256
GRAD_DTYPE = BF16

MESH = pl.DeviceIdType.MESH
ANY = pl.BlockSpec(memory_space=pl.ANY)


def _params(*sem):
    return pltpu.CompilerParams(dimension_semantics=sem or None, vmem_limit_bytes=V7X_VMEM_LIMIT)


def _sigmoid(x):
    return 1.0 / (1.0 + jnp.exp(-x))


def _divisor(n, most):
    best = None
    for t in range(LANE, min(n, most) + 1, LANE):
        if n % t == 0:
            best = t
    assert best is not None, (n, most)
    return best


def _stack_rows(parts):
    out, offsets, at = [], [], 0
    for a in parts:
        pad = -a.shape[0] % SUBLANES
        offsets.append(at)
        out.append(a)
        if pad:
            out.append(jnp.zeros((pad, a.shape[1]), a.dtype))
        at += a.shape[0] + pad
    return jnp.concatenate(out, axis=0), offsets


class Exchange:
    OTHER_CHIPS = (4, 2, 6)

    def __init__(self, gathers=(), scatters=(), keys=()):
        self.gathers, self.g_cols = [a for a, _ in gathers], [c for _, c in gathers]
        self.scatters, self.s_cols, self.s_parts = [s[0] for s in scatters], [s[1] for s in scatters], [s[2] for s in scatters]
        self.keys = list(keys)
        self.n_g, self.n_s = len(self.gathers), len(self.scatters)
        self.n = self.n_g + self.n_s
        self.operands = self.gathers + self.scatters
        self.gathered = self.parts = None

    def rows(self, t):
        a = self.scatters[t]
        first, last, of = self.s_parts[t]
        per = (a.shape[0] if self.s_cols[t] else a.shape[1]) // of
        return first * per, (last - first) * per

    def out_shape(self):
        outs = []
        for a, cols in zip(self.gathers, self.g_cols):
            outs.append(jax.ShapeDtypeStruct((a.shape[0], N_DEV * a.shape[1]) if cols else (N_DEV,) + a.shape, a.dtype))
        for t, (a, cols) in enumerate(zip(self.scatters, self.s_cols)):
            outs.append(jax.ShapeDtypeStruct((N_DEV, self.rows(t)[1], a.shape[1] // N_DEV if cols else a.shape[2]), a.dtype))
        return outs

    def scratch(self):
        dma = pltpu.SemaphoreType.DMA
        return [dma((max(self.n_g, 1) * 7,)), dma((max(self.n_g, 1) * 7,)), dma((max(self.n_s, 1) * 7,)),
                dma((max(self.n_s, 1) * 7,)), dma((self.n,))]

    def take(self, results):
        self.gathered, self.parts = list(results[:self.n_g]), list(results[self.n_g:])

    def _copies(self, ins, outs, sems):
        n_g, n_s = self.n_g, self.n_s
        g_in, s_in, g_out, s_out = ins[:n_g], ins[n_g:], outs[:n_g], outs[n_g:]
        g_send, g_recv, s_send, s_recv, local_sem = sems
        x, y, c = lax.axis_index("x"), lax.axis_index("y"), lax.axis_index("c")

        def peer(k):
            return (1 - x if k & 4 else x, 1 - y if k & 2 else y, 1 - c if k & 1 else c)

        def number(p):
            return 4 * p[0] + 2 * p[1] + p[2]

        me = number((x, y, c))

        def slot(t, j):
            first, count = self.rows(t)
            if self.s_cols[t]:
                width = self.scatters[t].shape[1] // N_DEV
                return s_in[t].at[pl.ds(first, count), pl.ds(pl.multiple_of(j * width, LANE), width)]
            return s_in[t].at[j, pl.ds(first, count)]

        def place(t, j):
            if self.g_cols[t]:
                width = self.gathers[t].shape[1]
                return g_out[t].at[:, pl.ds(pl.multiple_of(j * width, LANE), width)]
            return g_out[t].at[j]

        def local():
            cps = [pltpu.make_async_copy(g_in[t], place(t, me), local_sem.at[t]) for t in range(n_g)]
            return cps + [pltpu.make_async_copy(slot(t, me), s_out[t].at[me], local_sem.at[n_g + t]) for t in range(n_s)]

        def scatter(t, k):
            p = peer(k)
            return pltpu.make_async_remote_copy(
                src_ref=slot(t, number(p)), dst_ref=s_out[t].at[me], send_sem=s_send.at[t * 7 + k - 1],
                recv_sem=s_recv.at[t * 7 + k - 1], device_id=p, device_id_type=MESH)

        def landed(t, k):
            p = peer(k)
            return pltpu.make_async_remote_copy(
                src_ref=slot(t, me), dst_ref=s_out[t].at[number(p)], send_sem=s_send.at[t * 7 + k - 1],
                recv_sem=s_recv.at[t * 7 + k - 1], device_id=p, device_id_type=MESH)

        def gather(t, pair, block, to, src=None):
            slot = place(t, number(block))
            return pltpu.make_async_remote_copy(
                src_ref=slot if src is None else src, dst_ref=slot, send_sem=g_send.at[t * 7 + pair],
                recv_sem=g_recv.at[t * 7 + pair], device_id=to, device_id_type=MESH)

        def first_sends():
            cps = []
            for t in range(n_g):
                cps.append(gather(t, 0, peer(0), peer(1), src=g_in[t]))
                cps += [gather(t, 1 + j, peer(0), peer(k), src=g_in[t]) for j, k in enumerate(self.OTHER_CHIPS)]
            for t in range(n_s):
                cps += [scatter(t, k) for k in range(1, N_DEV)]
            return cps

        return peer, local, landed, gather, first_sends

    def start(self, ins, outs, sems):
        _, local, _, _, first_sends = self._copies(ins, outs, sems)
        for cp in local() + first_sends():
            cp.start()

    def finish(self, ins, outs, sems):
        peer, local, landed, gather, first_sends = self._copies(ins, outs, sems)
        mine, sibling = peer(0), peer(1)
        passed = []
        for j, k in enumerate(self.OTHER_CHIPS):
            for t in range(self.n_g):
                gather(t, 1 + j, peer(k), mine).wait_recv()
                passed.append(gather(t, 4 + j, peer(k), sibling))
                passed[-1].start()
        for t in range(self.n_g):
            gather(t, 0, sibling, mine).wait_recv()
            for j, k in enumerate(self.OTHER_CHIPS):
                gather(t, 4 + j, peer(k ^ 1), mine).wait_recv()
        for t in range(self.n_s):
            for k in range(1, N_DEV):
                landed(t, k).wait_recv()
        for cp in first_sends() + passed:
            cp.wait_send()
        for cp in local():
            cp.wait()


def _exchange_alone(name, ex):
    def body(*refs):
        ins, outs, sems = refs[:ex.n], refs[ex.n:2 * ex.n], refs[2 * ex.n:]
        ex.start(ins, outs, sems)
        ex.finish(ins, outs, sems)

    ex.take(pl.pallas_call(body, name=name, in_specs=[ANY] * ex.n, out_specs=[ANY] * ex.n, out_shape=ex.out_shape(),
                           scratch_shapes=ex.scratch())(*ex.operands))


def _call(name, body, args, *, grid, in_specs, out_specs, out_shape, scratch_shapes=(), sem=(), carry=None):
    if carry is None:
        return pl.pallas_call(body, name=name, grid=grid, in_specs=in_specs, out_specs=out_specs, out_shape=out_shape,
                              scratch_shapes=list(scratch_shapes), compiler_params=_params(*sem))(*args)
    ex = carry
    n_in, n_out, n_scr = len(args), len(out_shape), len(scratch_shapes)

    def carried(*refs):
        ins, cin = refs[:n_in], refs[n_in:n_in + ex.n]
        at = n_in + ex.n
        outs, cout = refs[at:at + n_out], refs[at + n_out:at + n_out + ex.n]
        at += n_out + ex.n
        scr, sems = refs[at:at + n_scr], refs[at + n_scr:]
        ids = [pl.program_id(a) for a in range(len(grid))]
        first = functools.reduce(jnp.logical_and, [i == 0 for i in ids])
        last = functools.reduce(jnp.logical_and, [i == g - 1 for i, g in zip(ids, grid)])

        @pl.when(first)
        def _():
            ex.start(cin, cout, sems)

        body(*ins, *outs, *scr)

        @pl.when(last)
        def _():
            ex.finish(cin, cout, sems)

    res = pl.pallas_call(
        carried, name=name, grid=grid, in_specs=list(in_specs) + [ANY] * ex.n, out_specs=list(out_specs) + [ANY] * ex.n,
        out_shape=list(out_shape) + ex.out_shape(), scratch_shapes=list(scratch_shapes) + ex.scratch(),
        compiler_params=_params(*("arbitrary",) * len(grid)),
    )(*args, *ex.operands)
    ex.take(res[n_out:])
    return res[:n_out]


class W:
    def __init__(self, arr, cols):
        self.arr, self.cols = arr, cols
        self.k, self.n = arr.shape
        self.shard_cols = self.n // N_DEV if cols else self.n


def _matmul(name, grid, operands, specs, dims, tile, extras, outs, out_specs, epilogue, carry=None, sums=0):
    assert grid[2] == 1
    n_ex, n_out = len(extras), len(outs)

    def body(*refs):
        a_ref, b_ref = refs[0], refs[1]
        ex_refs = refs[2:2 + n_ex]
        out_refs = refs[2 + n_ex:2 + n_ex + n_out]
        sum_refs = refs[2 + n_ex + n_out:]
        if sums:
            @pl.when(jnp.logical_and(pl.program_id(0) == 0, pl.program_id(1) == 0))
            def _():
                for r in sum_refs:
                    r[...] = jnp.zeros(r.shape, F32)
        acc = lax.dot_general(a_ref[...].astype(BF16), b_ref[...].astype(BF16), (dims, ((), ())),
                              preferred_element_type=F32)
        res = epilogue(acc, *[r[...] for r in ex_refs]) if epilogue else (acc,) * n_out
        for r, v in zip(out_refs, res[:n_out]):
            r[...] = v.astype(r.dtype)
        for r, v in zip(sum_refs, res[n_out:]):
            r[...] += v

    total = pl.BlockSpec((1, LANE), lambda i, j, c: (0, 0))
    return _call(name, body, list(operands) + [a for a, _ in extras], grid=grid,
                 in_specs=list(specs) + [s for _, s in extras], out_specs=list(out_specs) + [total] * sums,
                 out_shape=list(outs) + [jax.ShapeDtypeStruct((1, LANE), F32)] * sums,
                 sem=("arbitrary",) * 3 if sums else ("parallel", "parallel", "arbitrary"), carry=carry)


def _extra_specs(extras, tm, tn):
    out = []
    for arr, kind in extras:
        if kind == "tile":
            out.append((arr, pl.BlockSpec((tm, tn), lambda i, j, c: (i, j))))
        else:
            out.append((arr, pl.BlockSpec((1, tn), lambda i, j, c: (0, j))))
    return out


def _tile_cols(contraction, streams):
    left = V7X_VMEM_LIMIT - V7X_VMEM_LIMIT // 8 - 2 * 1024 * contraction * 2
    for cols in (1024, 512, 256):
        if 2 * cols * (1024 * 4 * streams + contraction * 2) <= left:
            return cols
    return LANE


def _mm_nn(name, a, w, out_dtypes, epilogue=None, extras=(), carry=None, sums=0):
    m, k = a.shape
    assert k == w.k
    tm = 1024 if a.dtype == BF16 else 512
    tn = _divisor(w.n, _tile_cols(k, len(out_dtypes) + sum(kind == "tile" for _, kind in extras)))
    assert tm * k * a.dtype.itemsize <= 16 * 2 ** 20, (name, tm, k)
    grid = (m // tm, w.n // tn, 1)
    specs = [pl.BlockSpec((tm, k), lambda i, j, c: (i, 0)), pl.BlockSpec((k, tn), lambda i, j, c: (0, j))]
    outs = [jax.ShapeDtypeStruct((m, w.n), d) for d in out_dtypes]
    out_specs = [pl.BlockSpec((tm, tn), lambda i, j, c: (i, j)) for _ in outs]
    return _matmul(name, grid, (a, w.arr), specs, ((1,), (0,)), (tm, tn), _extra_specs(extras, tm, tn), outs, out_specs,
                   epilogue, carry, sums)


def _mm_nt(name, dy, w, out_dtypes, epilogue=None, extras=(), carry=None):
    m, n = dy.shape
    assert n == w.n and dy.dtype == BF16
    tm = 1024
    to = _divisor(w.k, _tile_cols(n, len(out_dtypes) + sum(kind == "tile" for _, kind in extras)))
    assert tm * n * dy.dtype.itemsize <= 16 * 2 ** 20, (name, tm, n)
    grid = (m // tm, w.k // to, 1)
    specs = [pl.BlockSpec((tm, n), lambda i, j, c: (i, 0)), pl.BlockSpec((to, n), lambda i, j, c: (j, 0))]
    outs = [jax.ShapeDtypeStruct((m, w.k), d) for d in out_dtypes]
    out_specs = [pl.BlockSpec((tm, to), lambda i, j, c: (i, j)) for _ in outs]
    return _matmul(name, grid, (dy, w.arr), specs, ((1,), (1,)), (tm, to), _extra_specs(extras, tm, to), outs, out_specs,
                   epilogue, carry)


def _mm_tn(name, a, dy, like, carry=None):
    m, k = a.shape
    n = dy.shape[1]
    assert (k, n) == (like.k, like.n)
    tk = _divisor(k, 1024 if a.dtype == BF16 else 512)
    tn = _divisor(n, 1024)
    grid = (k // tk, n // tn, 1)
    specs = [pl.BlockSpec((m, tk), lambda i, j, c: (0, i)), pl.BlockSpec((m, tn), lambda i, j, c: (0, j))]
    out_specs = [pl.BlockSpec((tk, tn), lambda i, j, c: (i, j))]
    (g,) = _matmul(name, grid, (a, dy), specs, ((0,), (0,)), (tk, tn), [], [jax.ShapeDtypeStruct((k, n), GRAD_DTYPE)],
                   out_specs, None, carry)
    return g if like.cols else g.reshape(N_DEV, k // N_DEV, n)


def _rows(arr, blk=0, width=None):
    return ("rows", arr, blk, width or arr.shape[1])


def _full(arr):
    return ("full", arr)


def _by_residue(arr):
    return ("residue", arr)


def _rowwise(name, fn, ins, outs, reds=(), ts=256, carry=None, scratch=()):
    s = next(i[1].shape[0] if i[0] == "rows" else i[1].shape[0] * i[1].shape[1] for i in ins if i[0] != "full")
    n_in, n_out, n_red = len(ins), len(outs), len(reds)
    in_specs = []
    for i in ins:
        if i[0] == "rows":
            in_specs.append(pl.BlockSpec((ts, i[3]), functools.partial(lambda t, blk: (t, blk), blk=i[2])))
        elif i[0] == "residue":
            d, _, w = i[1].shape
            in_specs.append(pl.BlockSpec((d, ts // d, w), lambda t: (0, t, 0)))
        else:
            in_specs.append(pl.BlockSpec(i[1].shape, functools.partial(lambda t, nd: (0,) * nd, nd=i[1].ndim)))
    out_shape, out_specs = [], []
    for o in outs:
        if len(o) == 2:
            out_shape.append(jax.ShapeDtypeStruct((s, o[0]), o[1]))
            out_specs.append(pl.BlockSpec((ts, o[0]), lambda t: (t, 0)))
        else:
            out_shape.append(jax.ShapeDtypeStruct((o[2], s // o[2], o[0]), o[1]))
            out_specs.append(pl.BlockSpec((o[2], ts // o[2], o[0]), lambda t: (0, t, 0)))
    out_shape += [jax.ShapeDtypeStruct(r, F32) for r in reds]
    out_specs += [pl.BlockSpec(r, lambda t: (0, 0)) for r in reds]

    def body(*refs):
        red_refs = refs[n_in + n_out:n_in + n_out + n_red]
        if red_refs:
            @pl.when(pl.program_id(0) == 0)
            def _():
                for r in red_refs:
                    r[...] = jnp.zeros(r.shape, F32)
        fn(refs[:n_in], refs[n_in:n_in + n_out], red_refs, *refs[n_in + n_out + n_red:])

    return _call(name, body, [i[1] for i in ins], grid=(s // ts,), in_specs=in_specs, out_specs=out_specs,
                 out_shape=out_shape, scratch_shapes=list(scratch), sem=("arbitrary" if reds else "parallel",), carry=carry)


def _ln_stats(x):
    mu = jnp.mean(x, axis=-1, keepdims=True)
    xc = x - mu
    var = jnp.mean(xc * xc, axis=-1, keepdims=True)
    return xc * lax.rsqrt(var + LN_EPS), lax.rsqrt(var + LN_EPS)


def _layer_norm(name, x, g, b, out_dtypes):
    def fn(i, o, r):
        xhat, _ = _ln_stats(i[0][...])
        y = xhat * i[1][...] + i[2][...]
        for ref in o:
            ref[...] = y.astype(ref.dtype)

    return _rowwise(name, fn, [_rows(x), _full(g), _full(b)], [(x.shape[1], d) for d in out_dtypes])


def _ln_bwd_tile(x, g, dy):
    xhat, rstd = _ln_stats(x)
    dyg = dy * g
    m1 = jnp.mean(dyg, axis=-1, keepdims=True)
    m2 = jnp.mean(dyg * xhat, axis=-1, keepdims=True)
    dx = rstd * (dyg - m1 - xhat * m2)
    return dx, jnp.sum(dy * xhat, axis=0, keepdims=True), jnp.sum(dy, axis=0, keepdims=True)


def _layer_norm_bwd(name, x, g, dy, carry=None):
    d = x.shape[1]

    def fn(i, o, r):
        dx, dg, db = _ln_bwd_tile(i[0][...], i[1][...], i[2][...])
        o[0][...] = dx
        o[1][...] = dx.astype(BF16)
        r[0][...] += dg
        r[1][...] += db

    return _rowwise(name, fn, [_rows(x), _full(g), _rows(dy)], [(d, F32), (d, BF16)], [(1, d), (1, d)], carry=carry)


def _conv_fwd(name, glu, dw, dw_b, ts=512, carry=None):
    s, c = glu.shape
    tc = dw.shape[2]
    per = ts // CONV_HALO
    back = CONV_HALO - (CONV_WIDTH - 1)

    def body(cur_ref, prev_ref, w_ref, b_ref, out_ref, buf):
        i = pl.program_id(1)
        buf[pl.ds(0, CONV_HALO), :] = jnp.where(i > 0, prev_ref[...], 0.0)
        buf[pl.ds(CONV_HALO, ts), :] = cur_ref[...]
        for r0 in range(0, ts, CONV_ROWS):
            acc = jnp.broadcast_to(b_ref[...], (CONV_ROWS, tc))
            for j in range(CONV_WIDTH):
                acc = acc + w_ref[j:j + 1, :] * buf[pl.ds(r0 + back + j, CONV_ROWS), :]
            out_ref[pl.ds(r0, CONV_ROWS), :] = acc

    (out,) = _call(
        name, body, [glu, glu, dw, dw_b], grid=(c // tc, s // ts),
        in_specs=[pl.BlockSpec((ts, tc), lambda j, i: (i, j)),
                  pl.BlockSpec((CONV_HALO, tc), lambda j, i: (jnp.maximum(i * per - 1, 0), j)),
                  pl.BlockSpec((None, CONV_WIDTH, tc), lambda j, i: (j, 0, 0)),
                  pl.BlockSpec((1, tc), lambda j, i: (0, j))],
        out_specs=[pl.BlockSpec((ts, tc), lambda j, i: (i, j))],
        out_shape=[jax.ShapeDtypeStruct((s, c), F32)],
        scratch_shapes=[pltpu.VMEM((ts + CONV_HALO, tc), F32)],
        sem=("parallel", "parallel"), carry=carry)
    return out


def _conv_bwd(name, glu, dc, dw, ts=512, carry=None):
    s, c = glu.shape
    tc = dw.shape[2]
    per = ts // CONV_HALO
    back = CONV_HALO - (CONV_WIDTH - 1)
    last = s // ts - 1

    def body(g_ref, gprev_ref, dc_ref, dcnext_ref, w_ref, dglu_ref, ddw_ref, ddb_ref, gbuf, dbuf):
        i = pl.program_id(1)

        @pl.when(i == 0)
        def _():
            ddw_ref[...] = jnp.zeros(ddw_ref.shape, F32)
            ddb_ref[...] = jnp.zeros(ddb_ref.shape, F32)

        gbuf[pl.ds(0, CONV_HALO), :] = jnp.where(i > 0, gprev_ref[...], 0.0)
        gbuf[pl.ds(CONV_HALO, ts), :] = g_ref[...]
        dbuf[pl.ds(0, ts), :] = dc_ref[...]
        dbuf[pl.ds(ts, CONV_HALO), :] = jnp.where(i < last, dcnext_ref[...], 0.0)
        taps = [jnp.zeros((1, tc), F32)] * CONV_WIDTH
        for r0 in range(0, ts, CONV_ROWS):
            d_here = dbuf[pl.ds(r0, CONV_ROWS), :]
            acc = jnp.zeros((CONV_ROWS, tc), F32)
            for j in range(CONV_WIDTH):
                acc = acc + w_ref[j:j + 1, :] * dbuf[pl.ds(r0 + (CONV_WIDTH - 1) - j, CONV_ROWS), :]
                taps[j] = taps[j] + jnp.sum(d_here * gbuf[pl.ds(r0 + back + j, CONV_ROWS), :], axis=0, keepdims=True)
            dglu_ref[pl.ds(r0, CONV_ROWS), :] = acc
        for j in range(CONV_WIDTH):
            ddw_ref[j:j + 1, :] += taps[j]
        ddb_ref[...] += jnp.sum(dc_ref[...], axis=0, keepdims=True)

    return _call(
        name, body, [glu, glu, dc, dc, dw], grid=(c // tc, s // ts),
        in_specs=[pl.BlockSpec((ts, tc), lambda j, i: (i, j)),
                  pl.BlockSpec((CONV_HALO, tc), lambda j, i: (jnp.maximum(i * per - 1, 0), j)),
                  pl.BlockSpec((ts, tc), lambda j, i: (i, j)),
                  pl.BlockSpec((CONV_HALO, tc), lambda j, i: (jnp.minimum((i + 1) * per, (last + 1) * per - 1), j)),
                  pl.BlockSpec((None, CONV_WIDTH, tc), lambda j, i: (j, 0, 0))],
        out_specs=[pl.BlockSpec((ts, tc), lambda j, i: (i, j)),
                   pl.BlockSpec((CONV_WIDTH, tc), lambda j, i: (0, j)),
                   pl.BlockSpec((1, tc), lambda j, i: (0, j))],
        out_shape=[jax.ShapeDtypeStruct((s, c), F32), jax.ShapeDtypeStruct((CONV_WIDTH, c), F32),
                   jax.ShapeDtypeStruct((1, c), F32)],
        scratch_shapes=[pltpu.VMEM((ts + CONV_HALO, tc), F32), pltpu.VMEM((ts + CONV_HALO, tc), F32)],
        sem=("parallel", "arbitrary"), carry=carry)


def _rope_tables(positions):
    half = HEAD_DIM // 2
    inv = (np.float32(ROPE_THETA) ** (-np.arange(half, dtype=np.float32) * np.float32(2.0 / HEAD_DIM))).astype(np.float32)
    inv_freq = jnp.asarray(np.concatenate([inv, inv])[None, :])
    sign = jnp.asarray(np.concatenate([-np.ones(half, np.float32), np.ones(half, np.float32)])[None, :])

    def fn(i, o, r):
        ang = i[0][...].astype(F32) * i[1][...]
        o[0][...] = jnp.cos(ang)
        o[1][...] = jnp.sin(ang) * i[2][...]

    return _rowwise("rope_tables", fn, [_rows(positions), _full(inv_freq), _full(sign)], [(HEAD_DIM, F32), (HEAD_DIM, F32)], ts=512)


def _rot(x, cos, sin):
    return x * cos + pltpu.roll(x, HEAD_DIM // 2, 1) * sin


def _unrot(x, cos, sin):
    return x * cos - pltpu.roll(x, HEAD_DIM // 2, 1) * sin


def _split_rows(scr, value, d):
    if d == 1:
        return [value]
    scr[...] = value
    return [scr[pl.ds(r, scr.shape[0] // d, stride=d), :] for r in range(d)]


def _join_rows(scr, planes):
    d = len(planes)
    if d == 1:
        return planes[0]
    for r, plane in enumerate(planes):
        scr[pl.ds(r, scr.shape[0] // d, stride=d), :] = plane
    return scr[...]


def _lane(h, shape):
    return lax.broadcasted_iota(jnp.int32, shape, 1) == h


def _attn_specs(width):
    cur = pl.BlockSpec((None, ATTN_BLOCK, width), lambda r, n: (r, n, 0))
    prev = pl.BlockSpec((None, ATTN_BLOCK, width), lambda r, n: (r, jnp.maximum(n - 1, 0), 0))
    return cur, prev


def _masks(n):
    row = lax.broadcasted_iota(jnp.int32, (ATTN_BLOCK, ATTN_BLOCK), 0)
    col = lax.broadcasted_iota(jnp.int32, (ATTN_BLOCK, ATTN_BLOCK), 1)
    return col <= row, jnp.logical_and(col >= row, n > 0)


_NT = (((1,), (1,)), ((), ()))
_TN = (((0,), (0,)), ((), ()))
_NN = (((1,), (0,)), ((), ()))


def _attn_fwd(name, q, k, v, carry=None):
    dil, ln, d = k.shape
    nh = d // HEAD_DIM
    nb = ln // ATTN_BLOCK
    scale = HEAD_DIM ** -0.5

    def body(q_ref, kc_ref, kp_ref, vc_ref, vp_ref, o_ref, l_ref):
        mask_c, mask_p = _masks(pl.program_id(1))
        mask = jnp.concatenate([mask_p, mask_c], axis=1)
        stats = jnp.zeros((ATTN_BLOCK, LANE), F32)
        for h in range(nh):
            hs = slice(h * HEAD_DIM, (h + 1) * HEAD_DIM)
            keys = jnp.concatenate([kp_ref[:, hs], kc_ref[:, hs]], axis=0)
            vals = jnp.concatenate([vp_ref[:, hs], vc_ref[:, hs]], axis=0)
            sc = jnp.where(mask, lax.dot_general(q_ref[:, hs], keys, _NT, preferred_element_type=F32) * scale, NEG)
            m = jnp.max(sc, axis=1, keepdims=True)
            p = jnp.exp(sc - m)
            l = jnp.sum(p, axis=1, keepdims=True)
            o_ref[:, hs] = lax.dot_general(p.astype(BF16), vals, _NN, preferred_element_type=F32) / l
            stats = jnp.where(_lane(h, stats.shape), m + jnp.log(l), stats)
        l_ref[...] = stats

    (cur, prev), (stat, _) = _attn_specs(d), _attn_specs(LANE)
    return _call(name, body, [q, k, k, v, v], grid=(dil, nb), in_specs=[cur, cur, prev, cur, prev], out_specs=[cur, stat],
                 out_shape=[jax.ShapeDtypeStruct((dil, ln, d), F32), jax.ShapeDtypeStruct((dil, ln, LANE), F32)],
                 sem=("parallel", "parallel"), carry=carry)


def _attn_dq(name, q, k, v, do, lse, dsum, carry=None):
    dil, ln, d = k.shape
    nh = d // HEAD_DIM
    nb = ln // ATTN_BLOCK
    scale = HEAD_DIM ** -0.5

    def body(q_ref, kc_ref, kp_ref, vc_ref, vp_ref, do_ref, l_ref, d_ref, dq_ref):
        mask_c, mask_p = _masks(pl.program_id(1))
        mask = jnp.concatenate([mask_p, mask_c], axis=1)
        for h in range(nh):
            hs = slice(h * HEAD_DIM, (h + 1) * HEAD_DIM)
            keys = jnp.concatenate([kp_ref[:, hs], kc_ref[:, hs]], axis=0)
            vals = jnp.concatenate([vp_ref[:, hs], vc_ref[:, hs]], axis=0)
            sc = lax.dot_general(q_ref[:, hs], keys, _NT, preferred_element_type=F32) * scale
            p = jnp.where(mask, jnp.exp(jnp.where(mask, sc, NEG) - l_ref[:, h:h + 1]), 0.0)
            dp = lax.dot_general(do_ref[:, hs], vals, _NT, preferred_element_type=F32)
            ds = p * (dp - d_ref[:, h:h + 1])
            dq_ref[:, hs] = lax.dot_general(ds.astype(BF16), keys, _NN, preferred_element_type=F32) * scale

    (cur, prev), (stat, _) = _attn_specs(d), _attn_specs(LANE)
    (dq,) = _call(name, body, [q, k, k, v, v, do, lse, dsum], grid=(dil, nb),
                  in_specs=[cur, cur, prev, cur, prev, cur, stat, stat], out_specs=[cur],
                  out_shape=[jax.ShapeDtypeStruct((dil, ln, d), F32)], sem=("parallel", "parallel"), carry=carry)
    return dq


def _attn_dkv(name, q, k, v, do, lse, dsum, carry=None):
    dil, ln, d = k.shape
    nh = d // HEAD_DIM
    nb = ln // ATTN_BLOCK
    scale = HEAD_DIM ** -0.5

    def body(k_ref, v_ref, qc_ref, qn_ref, doc_ref, don_ref, lc_ref, lnx_ref, dc_ref, dn_ref, dk_ref, dv_ref):
        n = pl.program_id(1)
        row = lax.broadcasted_iota(jnp.int32, (ATTN_BLOCK, ATTN_BLOCK), 0)
        col = lax.broadcasted_iota(jnp.int32, (ATTN_BLOCK, ATTN_BLOCK), 1)
        mask = jnp.concatenate([row <= col, jnp.logical_and(row >= col, n < nb - 1)], axis=1)
        lse_t = jnp.concatenate([lc_ref[...].T, lnx_ref[...].T], axis=1)
        dsum_t = jnp.concatenate([dc_ref[...].T, dn_ref[...].T], axis=1)
        for h in range(nh):
            hs = slice(h * HEAD_DIM, (h + 1) * HEAD_DIM)
            qs = jnp.concatenate([qc_ref[:, hs], qn_ref[:, hs]], axis=0)
            douts = jnp.concatenate([doc_ref[:, hs], don_ref[:, hs]], axis=0)
            sc = lax.dot_general(k_ref[:, hs], qs, _NT, preferred_element_type=F32) * scale
            p = jnp.where(mask, jnp.exp(jnp.where(mask, sc, NEG) - lse_t[h:h + 1, :]), 0.0)
            dp = lax.dot_general(v_ref[:, hs], douts, _NT, preferred_element_type=F32)
            ds = p * (dp - dsum_t[h:h + 1, :])
            dv_ref[:, hs] = lax.dot_general(p.astype(BF16), douts, _NN, preferred_element_type=F32)
            dk_ref[:, hs] = lax.dot_general(ds.astype(BF16), qs, _NN, preferred_element_type=F32) * scale

    def specs(width):
        cur = pl.BlockSpec((None, ATTN_BLOCK, width), lambda r, n: (r, n, 0))
        nxt = pl.BlockSpec((None, ATTN_BLOCK, width), lambda r, n: (r, jnp.minimum(n + 1, nb - 1), 0))
        return cur, nxt

    (cur, nxt), (stat, stat_next) = specs(d), specs(LANE)
    return _call(name, body, [k, v, q, q, do, do, lse, lse, dsum, dsum], grid=(dil, nb),
                 in_specs=[cur, cur, cur, nxt, cur, nxt, stat, stat_next, stat, stat_next], out_specs=[cur, cur],
                 out_shape=[jax.ShapeDtypeStruct((dil, ln, d), F32)] * 2, sem=("parallel", "parallel"), carry=carry)


def _adamw_tile(w, g, m, v):
    m = ADAM_B1 * m + (1.0 - ADAM_B1) * g
    v = ADAM_B2 * v + (1.0 - ADAM_B2) * (g * g)
    m_hat = m / (1.0 - ADAM_B1 ** ADAM_STEP)
    v_hat = v / (1.0 - ADAM_B2 ** ADAM_STEP)
    delta = -ADAM_LR * (m_hat / (jnp.sqrt(v_hat) + ADAM_EPS) + ADAM_WD * w)
    return delta, m, v


def _adamw_big(name, parts, w, m, v, carry=None):
    layers, r, c = w.shape
    assert len(parts) == layers and all(sum(ch.shape[1] for ch in per_layer) == r for per_layer in parts)
    every = [ch for per_layer in parts for ch in per_layer]
    per_row = 2 * c * (len(every) * N_DEV * every[0].dtype.itemsize + 7 * 4)
    tr = 16
    while tr * 2 <= min(min(ch.shape[1] for ch in every), V7X_VMEM_LIMIT // 2 // per_row) and all(ch.shape[1] % (tr * 2) == 0 for ch in every):
        tr *= 2
    pieces = []
    for ly, per_layer in enumerate(parts):
        at = 0
        for ch in per_layer:
            pieces.append((ly, at, ch.shape[1] // tr, ch))
            at += ch.shape[1] // tr

    def within(layer, i, ly, first, tiles):
        return jnp.logical_and(layer == ly, jnp.logical_and(i >= first, i < first + tiles))

    def body(*refs):
        part_refs = refs[:len(pieces)]
        w_ref, m_ref, v_ref, g_out, d_out, m_out, v_out = refs[len(pieces):]
        layer, i = pl.program_id(0), pl.program_id(1)
        for (ly, first, tiles, _), part_ref in zip(pieces, part_refs):
            @pl.when(within(layer, i, ly, first, tiles))
            def _(part_ref=part_ref):
                g = part_ref[0].astype(F32)
                for dev in range(1, N_DEV):
                    g = g + part_ref[dev].astype(F32)
                delta, mn, vn = _adamw_tile(w_ref[...], g, m_ref[...], v_ref[...])
                g_out[...] = g
                d_out[...] = delta
                m_out[...] = mn
                v_out[...] = vn

    def part_index(layer, i, ly, first, tiles):
        return (0, jnp.where(within(layer, i, ly, first, tiles), i - first, 0), 0)

    own = pl.BlockSpec((None, tr, c), lambda ly, i: (ly, i, 0))
    part_specs = [pl.BlockSpec((N_DEV, tr, c), functools.partial(part_index, ly=ly, first=first, tiles=tiles))
                  for ly, first, tiles, _ in pieces]
    return _call(name, body, [ch for _, _, _, ch in pieces] + [w, m, v], grid=(layers, r // tr), in_specs=part_specs + [own] * 3,
                 out_specs=[own] * 4, out_shape=[jax.ShapeDtypeStruct(w.shape, F32)] * 4, sem=("parallel", "parallel"),
                 carry=carry)


def _sum_slots(name, slots):
    _, r, c = slots.shape

    def body(s_ref, o_ref):
        g = s_ref[0]
        for j in range(1, N_DEV):
            g = g + s_ref[j]
        o_ref[...] = g

    return pl.pallas_call(body, name=name, out_shape=jax.ShapeDtypeStruct((r, c), F32),
                          compiler_params=_params())(slots)


def _adamw_small(name, w, g, m, v):
    def body(w_ref, g_ref, m_ref, v_ref, d_out, m_out, v_out):
        delta, mn, vn = _adamw_tile(w_ref[...], g_ref[...], m_ref[...], v_ref[...])
        d_out[...] = delta
        m_out[...] = mn
        v_out[...] = vn

    return pl.pallas_call(body, name=name, out_shape=[jax.ShapeDtypeStruct(w.shape, F32)] * 3,
                          compiler_params=_params())(w, g, m, v)


class Weights:
    def __init__(self, shards):
        self.shards, self.full, self.parts = shards, {}, {}

    @staticmethod
    def by_columns(key):
        return key.rstrip("01") in COLUMN_SHARDED

    def gather(self, *keys):
        return Exchange(gathers=[(self.shards[k], self.by_columns(k)) for k in keys], keys=keys)

    def landed(self, ex):
        for key, full in zip(ex.keys, ex.gathered):
            cols = self.by_columns(key)
            self.full[key] = W(full if cols else full.reshape(-1, full.shape[-1]), cols)

    def scatter(self, grads, gathers=()):
        if not grads and not gathers:
            return None
        return Exchange(gathers=gathers, scatters=[(g, self.by_columns(k), part) for k, (g, part) in grads.items()], keys=list(grads))

    def received(self, ex):
        for key, (first, _, of), part in zip(ex.keys, ex.s_parts, ex.parts) if ex is not None else ():
            self.parts.setdefault(key, {})[first / of] = part

    def chunks(self, key):
        return [self.parts[key][j] for j in sorted(self.parts[key])]

    def __getitem__(self, key):
        return self.full[key]


def _mlp_ple_fwd(tag, z1, p_i, ln1_g, ln1_b, ln2_g, ln2_b, wt, carries, target=None):
    h1, h1b = _layer_norm(f"ln1_{tag}", z1, ln1_g, ln1_b, (F32, BF16))
    up, act = _mm_nn(f"mlp_up_{tag}", h1b, wt["mlp_up" + tag], (F32, BF16),
                     epilogue=lambda acc: (acc, jnp.square(jnp.maximum(acc, 0.0))), carry=carries.get("mlp_up"))
    if "mlp_up" in carries:
        wt.landed(carries["mlp_up"])
    (z2,) = _mm_nn(f"mlp_down_{tag}", act, wt["mlp_down" + tag], (F32,), epilogue=lambda acc, h: (ALPHA * h + acc,),
                   extras=[(h1, "tile")], carry=carries.get("mlp_down"))
    if "mlp_down" in carries:
        wt.landed(carries["mlp_down"])
    h2, h2b = _layer_norm(f"ln2_{tag}", z2, ln2_g, ln2_b, (F32, BF16))
    (pe,) = _mm_nn(f"ple_proj_{tag}", p_i, wt["ple_proj" + tag], (F32,))

    saved = dict(z1=z1, h1b=h1b, up=up, act=act, z2=z2, h2b=h2b, p=p_i)
    if target is None:
        def gate(acc, h, e):
            out = h + e * _sigmoid(acc)
            return acc, out, out

        gp, out, outb = _mm_nn(f"ple_gate_{tag}", h2b, wt["ple_gate" + tag], (F32, F32, BF16), epilogue=gate,
                               extras=[(h2, "tile"), (pe, "tile")], carry=carries.get("ple_gate"))
        if "ple_gate" in carries:
            wt.landed(carries["ple_gate"])
        saved.update(pe=pe, gp=gp)
        return out, outb, saved

    width = z1.shape[1]

    def gate_and_loss(acc, h, e, goal):
        sg = _sigmoid(acc)
        diff = h + e * sg - goal
        d_y = diff * (1.0 / width)
        return d_y, d_y * sg, d_y * e * sg * (1.0 - sg), jnp.broadcast_to(jnp.sum(diff * diff), (1, LANE))

    d_y, d_pe, d_gp, sq = _mm_nn(f"ple_gate_{tag}", h2b, wt["ple_gate" + tag], (F32, BF16, BF16), epilogue=gate_and_loss,
                                 extras=[(h2, "tile"), (pe, "tile"), (target, "tile")], sums=1)
    saved.update(d_pe=d_pe, d_gp=d_gp)
    return d_y, 0.5 * sq[0, 0] / width, saved


WHOLE = (0, 1, 1)


def _mlp_ple_bwd(tag, d_out, sv, ln1_g, ln2_g, wt, plan, waiting):
    made = {}

    def riders(kernel, extra=()):
        items = {name + tag: (made[name], part) for name, part in plan[kernel]}
        return wt.scatter({**items, **dict(extra)})

    d_pe, d_gp = sv["d_pe"], sv["d_gp"]
    made["ple_proj"] = _mm_tn(f"g_ple_proj_{tag}", sv["p"], d_pe, wt["ple_proj" + tag])
    made["ple_gate"] = _mm_tn(f"g_ple_gate_{tag}", sv["h2b"], d_gp, wt["ple_gate" + tag])
    (d_h2,) = _mm_nt(f"d_ple_gate_{tag}", d_gp, wt["ple_gate" + tag], (F32,), epilogue=lambda acc, dy: (dy + acc,),
                     extras=[(d_out, "tile")])
    d_z2, d_z2b, g_ln2_g, g_ln2_b = _layer_norm_bwd(f"ln2_bwd_{tag}", sv["z2"], ln2_g, d_h2)
    ex = riders("g_mlp_down", waiting.items())
    made["mlp_down"] = _mm_tn(f"g_mlp_down_{tag}", sv["act"], d_z2b, wt["mlp_down" + tag], carry=ex)
    wt.received(ex)
    ex = riders("d_mlp_down")
    (d_up,) = _mm_nt(f"d_mlp_down_{tag}", d_z2b, wt["mlp_down" + tag], (BF16,),
                     epilogue=lambda acc, u: (acc * (2.0 * jnp.maximum(u, 0.0)),), extras=[(sv["up"], "tile")], carry=ex)
    wt.received(ex)
    ex = riders("g_mlp_up")
    made["mlp_up"] = _mm_tn(f"g_mlp_up_{tag}", sv["h1b"], d_up, wt["mlp_up" + tag], carry=ex)
    wt.received(ex)
    ex = riders("d_mlp_up")
    (d_h1,) = _mm_nt(f"d_mlp_up_{tag}", d_up, wt["mlp_up" + tag], (F32,), epilogue=lambda acc, dz: (ALPHA * dz + acc,),
                     extras=[(d_z2, "tile")], carry=ex)
    wt.received(ex)
    ex = riders("ln1_bwd")
    d_z1, d_z1b, g_ln1_g, g_ln1_b = _layer_norm_bwd(f"ln1_bwd_{tag}", sv["z1"], ln1_g, d_h1, carry=ex)
    wt.received(ex)
    return d_z1, d_z1b, dict(ln1_g=g_ln1_g, ln1_b=g_ln1_b, ln2_g=g_ln2_g, ln2_b=g_ln2_b), made["mlp_down"], made["mlp_up"]


def _local_step(x, p, positions, target, wt, small):
    s, d = x.shape
    nh = d // HEAD_DIM
    xb, p = x.astype(BF16), p.astype(BF16)

    ex = wt.gather("conv_w_out")
    (u,) = _mm_nn("conv_in", xb, wt["conv_w_in"], (F32,), epilogue=lambda acc, b: (acc + b,), extras=[(small["conv_b_in"], "row")],
                  carry=ex)
    wt.landed(ex)

    def glu_fn(i, o, r):
        o[0][...] = i[0][...] * _sigmoid(i[1][...])

    (glu,) = _rowwise("glu", glu_fn, [_rows(u, 0, d), _rows(u, 1, d)], [(d, F32)])
    ex = wt.gather("mlp_up0")
    c = _conv_fwd("dwconv", glu, small["conv_dw"], small["conv_dw_b"], carry=ex)
    wt.landed(ex)

    def ln_silu(i, o, r):
        xhat, _ = _ln_stats(i[0][...])
        n = xhat * i[1][...] + i[2][...]
        o[0][...] = (n * _sigmoid(n)).astype(BF16)

    (sb,) = _rowwise("conv_ln_silu", ln_silu, [_rows(c), _full(small["conv_ln_g"]), _full(small["conv_ln_b"])], [(d, BF16)])
    ex = wt.gather("ple_proj0", "ple_gate0")
    (z1,) = _mm_nn("conv_out", sb, wt["conv_w_out"], (F32,), epilogue=lambda acc, xt: (ALPHA * xt + acc,), extras=[(x, "tile")],
                   carry=ex)
    wt.landed(ex)
    x1, x1b, sv0 = _mlp_ple_fwd("0", z1, p[0], small["ln1_g"][0:1], small["ln1_b"][0:1], small["ln2_g"][0:1], small["ln2_b"][0:1], wt,
                                dict(mlp_up=wt.gather("mlp_down0"), mlp_down=wt.gather("mlp_up1"), ple_gate=wt.gather("w_kv")))

    (kvn,) = _layer_norm("kv_ln", x1, small["kv_ln_g"], small["kv_ln_b"], (BF16,))
    ex = wt.gather("attn_w_q")
    (kv,) = _mm_nn("kv_proj", kvn, wt["w_kv"], (F32,), carry=ex)
    wt.landed(ex)
    (q,) = _mm_nn("q_proj", x1b, wt["attn_w_q"], (F32,))
    cos, sin = _rope_tables(positions)
    row_scratch = [pltpu.VMEM((ROW_TILE, LANE), F32)]

    def rot_kv(i, o, r, scr):
        cs, sn = i[2][...], i[3][...]
        for h in range(nh):
            hs = slice(h * HEAD_DIM, (h + 1) * HEAD_DIM)
            for base, val in ((0, _rot(i[0][:, hs], cs, sn)), (N_GROUPS, i[1][:, hs])):
                for g, dil in enumerate(GROUP_DILATIONS):
                    for res, plane in enumerate(_split_rows(scr, val, dil)):
                        o[base + g][res, :, hs] = plane.astype(BF16)

    by_group = [(d, BF16, dil) for dil in GROUP_DILATIONS]
    ex = wt.gather("attn_w_o")
    kv_groups = _rowwise("rotary_kv", rot_kv, [_rows(kv, 0, d), _rows(kv, 1, d), _rows(cos), _rows(sin)], by_group * 2,
                         ts=ROW_TILE, scratch=row_scratch, carry=ex)
    wt.landed(ex)
    kg, vg = kv_groups[:N_GROUPS], kv_groups[N_GROUPS:]

    def rot_q(i, o, r, scr):
        cs, sn = i[1][...], i[2][...]
        for g, dil in enumerate(GROUP_DILATIONS):
            for h in range(nh):
                hs = slice(h * HEAD_DIM, (h + 1) * HEAD_DIM)
                val = _rot(i[0][:, g * d + h * HEAD_DIM:g * d + (h + 1) * HEAD_DIM], cs, sn)
                for res, plane in enumerate(_split_rows(scr, val, dil)):
                    o[g][res, :, hs] = plane.astype(BF16)

    qg = _rowwise("rotary_q", rot_q, [_rows(q), _rows(cos), _rows(sin)], by_group, ts=ROW_TILE, scratch=row_scratch)

    og, lg = zip(*[_attn_fwd(f"attn_fwd_{g}", qg[g], kg[g], vg[g]) for g in range(N_GROUPS)])

    def merge(i, o, r, scr):
        lses = []
        for g, dil in enumerate(GROUP_DILATIONS):
            lses.append(_join_rows(scr, [i[N_GROUPS + g][res] for res in range(dil)]))
        top = functools.reduce(jnp.maximum, lses)
        es = [jnp.exp(l - top) for l in lses]
        den = functools.reduce(lambda a, b: a + b, es)
        total = top + jnp.log(den)
        for g, dil in enumerate(GROUP_DILATIONS):
            for res, plane in enumerate(_split_rows(scr, total, dil)):
                o[2 + g][res] = plane
        ws = [e / den for e in es]
        for h in range(nh):
            hs = slice(h * HEAD_DIM, (h + 1) * HEAD_DIM)
            out = jnp.zeros((ROW_TILE, HEAD_DIM), F32)
            for g, dil in enumerate(GROUP_DILATIONS):
                og_h = _join_rows(scr, [i[g][res, :, hs] for res in range(dil)])
                out = out + ws[g][:, h:h + 1] * og_h
            o[0][:, hs] = out
            o[1][:, hs] = out.astype(BF16)

    merged = _rowwise("attn_merge", merge, [_by_residue(t) for t in og + lg],
                      [(d, F32), (d, BF16)] + [(LANE, F32, dil) for dil in GROUP_DILATIONS], ts=ROW_TILE, scratch=row_scratch)
    o, ob, lse_g = merged[0], merged[1], merged[2:]
    (z1b,) = _mm_nn("attn_out", ob, wt["attn_w_o"], (F32,), epilogue=lambda acc, xt: (ALPHA * xt + acc,), extras=[(x1, "tile")])
    d_y, loss, sv1 = _mlp_ple_fwd("1", z1b, p[1], small["ln1_g"][1:2], small["ln1_b"][1:2], small["ln2_g"][1:2], small["ln2_b"][1:2],
                                  wt, dict(mlp_up=wt.gather("mlp_down1"), mlp_down=wt.gather("ple_proj1", "ple_gate1")), target=target)

    plan = dict(g_mlp_down=[("ple_proj", WHOLE), ("ple_gate", WHOLE)], d_mlp_down=[("mlp_down", (0, 3, 8))],
                g_mlp_up=[("mlp_down", (3, 6, 8))], d_mlp_up=[("mlp_down", (6, 8, 8))], ln1_bwd=[])
    d_z1, d_z1b, g1, _, g_up1 = _mlp_ple_bwd("1", d_y, sv1, small["ln1_g"][1:2], small["ln2_g"][1:2], wt, plan, {})
    g_wo = _mm_tn("g_attn_out", ob, d_z1b, wt["attn_w_o"])
    (d_o,) = _mm_nt("d_attn_out", d_z1b, wt["attn_w_o"], (F32,))

    def dsum_fn(i, o, r, scr):
        stats = jnp.zeros((ROW_TILE, LANE), F32)
        for h in range(nh):
            hs = slice(h * HEAD_DIM, (h + 1) * HEAD_DIM)
            dout = i[0][:, hs]
            stats = jnp.where(_lane(h, stats.shape), jnp.sum(dout * i[1][:, hs], axis=1, keepdims=True), stats)
            for g, dil in enumerate(GROUP_DILATIONS):
                for res, plane in enumerate(_split_rows(scr, dout, dil)):
                    o[g][res, :, hs] = plane.astype(BF16)
        for g, dil in enumerate(GROUP_DILATIONS):
            for res, plane in enumerate(_split_rows(scr, stats, dil)):
                o[N_GROUPS + g][res] = plane

    res_ = _rowwise("attn_dsum", dsum_fn, [_rows(d_o), _rows(o)], by_group + [(LANE, F32, dil) for dil in GROUP_DILATIONS],
                    ts=ROW_TILE, scratch=row_scratch)
    dog, dsum_g = res_[:N_GROUPS], res_[N_GROUPS:]
    dqs, dks, dvs = [], [], []
    riders = [wt.scatter({"mlp_up1": (g_up1, (q, q + 1, 4))}) for q in range(4)] + [wt.scatter({"attn_w_o": (g_wo, WHOLE)}), None]
    for g in range(N_GROUPS):
        dqs.append(_attn_dq(f"attn_dq_{g}", qg[g], kg[g], vg[g], dog[g], lse_g[g], dsum_g[g], carry=riders[2 * g]))
        dk, dv = _attn_dkv(f"attn_dkv_{g}", qg[g], kg[g], vg[g], dog[g], lse_g[g], dsum_g[g], carry=riders[2 * g + 1])
        dks.append(dk)
        dvs.append(dv)
    for ex in riders:
        wt.received(ex)

    def unrot_q(i, o, r, scr):
        cs, sn = i[N_GROUPS][...], i[N_GROUPS + 1][...]
        for g, dil in enumerate(GROUP_DILATIONS):
            for h in range(nh):
                hs = slice(h * HEAD_DIM, (h + 1) * HEAD_DIM)
                dq = _join_rows(scr, [i[g][res, :, hs] for res in range(dil)])
                o[0][:, g * d + h * HEAD_DIM:g * d + (h + 1) * HEAD_DIM] = _unrot(dq, cs, sn).astype(BF16)

    (d_q,) = _rowwise("rotary_q_bwd", unrot_q, [_by_residue(t) for t in dqs] + [_rows(cos), _rows(sin)], [(N_GROUPS * d, BF16)],
                      ts=ROW_TILE, scratch=row_scratch)

    def unrot_kv(i, o, r, scr):
        cs, sn = i[2 * N_GROUPS][...], i[2 * N_GROUPS + 1][...]
        for h in range(nh):
            hs = slice(h * HEAD_DIM, (h + 1) * HEAD_DIM)
            for base in (0, N_GROUPS):
                tot = jnp.zeros((ROW_TILE, HEAD_DIM), F32)
                for g, dil in enumerate(GROUP_DILATIONS):
                    tot = tot + _join_rows(scr, [i[base + g][res, :, hs] for res in range(dil)])
                if base == 0:
                    o[0][:, hs] = _unrot(tot, cs, sn).astype(BF16)
                else:
                    o[0][:, d + h * HEAD_DIM:d + (h + 1) * HEAD_DIM] = tot.astype(BF16)

    (d_kv,) = _rowwise("rotary_kv_bwd", unrot_kv, [_by_residue(t) for t in dks + dvs] + [_rows(cos), _rows(sin)], [(2 * d, BF16)],
                       ts=ROW_TILE, scratch=row_scratch)
    g_wq = _mm_tn("g_q_proj", x1b, d_q, wt["attn_w_q"])
    ex = wt.scatter({"attn_w_q": (g_wq, (0, 3, 8))})
    g_wkv = _mm_tn("g_kv_proj", kvn, d_kv, wt["w_kv"], carry=ex)
    wt.received(ex)
    ex = wt.scatter({"attn_w_q": (g_wq, (3, 6, 8))})
    (d_x1a,) = _mm_nt("d_q_proj", d_q, wt["attn_w_q"], (F32,), epilogue=lambda acc, dz: (ALPHA * dz + acc,), extras=[(d_z1, "tile")],
                      carry=ex)
    wt.received(ex)
    ex = wt.scatter({"attn_w_q": (g_wq, (6, 8, 8)), "w_kv": (g_wkv, (0, 1, 8))})
    (d_kvn,) = _mm_nt("d_kv_proj", d_kv, wt["w_kv"], (F32,), carry=ex)
    wt.received(ex)

    def kv_ln_bwd(i, o, r):
        dx, dg, db = _ln_bwd_tile(i[0][...], i[1][...], i[2][...])
        dx = dx + i[3][...]
        sg = _sigmoid(i[5][...])
        o[0][...] = dx
        o[1][...] = (dx * sg).astype(BF16)
        o[2][...] = (dx * i[4][...] * sg * (1.0 - sg)).astype(BF16)
        r[0][...] += dg
        r[1][...] += db

    ex = wt.scatter({"w_kv": (g_wkv, (1, 4, 8))})
    d_x1, sv0["d_pe"], sv0["d_gp"], g_kv_ln_g, g_kv_ln_b = _rowwise(
        "kv_ln_bwd", kv_ln_bwd, [_rows(x1), _full(small["kv_ln_g"]), _rows(d_kvn), _rows(d_x1a), _rows(sv0["pe"]), _rows(sv0["gp"])],
        [(d, F32), (d, BF16), (d, BF16)], [(1, d), (1, d)], ts=128, carry=ex)
    wt.received(ex)

    plan = dict(g_mlp_down=[("ple_proj", WHOLE)], d_mlp_down=[("ple_gate", WHOLE), ("mlp_down", (0, 1, 8))],
                g_mlp_up=[("mlp_down", (1, 4, 8))], d_mlp_up=[("mlp_down", (4, 7, 8))], ln1_bwd=[("mlp_up", (0, 1, 8))])
    d_z1, d_z1b, g0, g_down0, g_up0 = _mlp_ple_bwd("0", d_x1, sv0, small["ln1_g"][0:1], small["ln2_g"][0:1], wt, plan,
                                                   {"w_kv": (g_wkv, (4, 8, 8))})
    ex = wt.scatter({"mlp_up0": (g_up0, (1, 2, 8))})
    g_wout = _mm_tn("g_conv_out", sb, d_z1b, wt["conv_w_out"], carry=ex)
    wt.received(ex)
    ex = wt.scatter({"mlp_up0": (g_up0, (2, 3, 8))})
    (d_s,) = _mm_nt("d_conv_out", d_z1b, wt["conv_w_out"], (F32,), carry=ex)
    wt.received(ex)

    def ln_silu_bwd(i, o, r):
        cx, gn, bn, ds_ = i[0][...], i[1][...], i[2][...], i[3][...]
        xhat, _ = _ln_stats(cx)
        n = xhat * gn + bn
        sg = _sigmoid(n)
        dn = ds_ * (sg * (1.0 + n * (1.0 - sg)))
        dx, dg, db = _ln_bwd_tile(cx, gn, dn)
        o[0][...] = dx
        r[0][...] += dg
        r[1][...] += db

    ex = wt.scatter({"mlp_up0": (g_up0, (3, 4, 8))})
    d_c, g_cln_g, g_cln_b = _rowwise("conv_ln_silu_bwd", ln_silu_bwd,
                                     [_rows(c), _full(small["conv_ln_g"]), _full(small["conv_ln_b"]), _rows(d_s)],
                                     [(d, F32)], [(1, d), (1, d)], carry=ex)
    wt.received(ex)
    ex = wt.scatter({"mlp_down0": (g_down0, (7, 8, 8)), "mlp_up0": (g_up0, (4, 8, 8))})
    d_glu, g_dw, g_dwb = _conv_bwd("dwconv_bwd", glu, d_c, small["conv_dw"], carry=ex)
    wt.received(ex)

    def glu_bwd(i, o, r):
        a, gt, dg_ = i[0][...], i[1][...], i[2][...]
        sg = _sigmoid(gt)
        da = dg_ * sg
        dgate = dg_ * a * sg * (1.0 - sg)
        o[0][:, 0:d] = da.astype(BF16)
        o[0][:, d:2 * d] = dgate.astype(BF16)
        r[0][:, 0:d] += jnp.sum(da, axis=0, keepdims=True)
        r[0][:, d:2 * d] += jnp.sum(dgate, axis=0, keepdims=True)

    ex = wt.scatter({"conv_w_out": (g_wout, (0, 1, 2))})
    d_u, g_bin = _rowwise("glu_bwd", glu_bwd, [_rows(u, 0, d), _rows(u, 1, d), _rows(d_glu)], [(2 * d, BF16)], [(1, 2 * d)],
                          carry=ex)
    wt.received(ex)
    ex = wt.scatter({"conv_w_out": (g_wout, (1, 2, 2))})
    g_win = _mm_tn("g_conv_in", xb, d_u, wt["conv_w_in"], carry=ex)
    wt.received(ex)
    rows = [g_bin.reshape(2, d), g_dw, g_dwb, g_cln_g, g_cln_b, g_kv_ln_g, g_kv_ln_b]
    rows += [jnp.concatenate([g0[n], g1[n]], axis=0) for n in ("ln1_g", "ln1_b", "ln2_g", "ln2_b")]
    rows, offsets = _stack_rows(rows)
    ex = wt.scatter({"conv_w_in": (g_win, (0, 4, 8))}, gathers=[(rows, False)])
    (grad_x,) = _mm_nt("d_conv_in", d_u, wt["conv_w_in"], (F32,), epilogue=lambda acc, dz: (ALPHA * dz + acc,), extras=[(d_z1, "tile")],
                       carry=ex)
    wt.received(ex)
    return loss, grad_x, ex.gathered[0], offsets, g_win


BIG = ("conv_w_in", "conv_w_out", "w_kv", "attn_w_q", "attn_w_o", "mlp_up", "mlp_down", "ple_proj", "ple_gate")
COLUMN_SHARDED = ("conv_w_in", "w_kv", "attn_w_q", "mlp_up", "ple_proj")
WEIGHTS = ("conv_w_in", "conv_b_in", "conv_dw", "conv_dw_b", "conv_ln_g", "conv_ln_b", "conv_w_out", "kv_ln_g", "kv_ln_b",
           "w_kv", "attn_w_q", "attn_w_o", "ln1_g", "ln1_b", "mlp_up", "mlp_down", "ln2_g", "ln2_b", "ple_proj", "ple_gate")


def kernel(x, p, positions, conv_w_in, conv_b_in, conv_dw, conv_dw_b, conv_ln_g, conv_ln_b, conv_w_out, kv_ln_g, kv_ln_b, w_kv, attn_w_q, attn_w_o, ln1_g, ln1_b, mlp_up, mlp_down, ln2_g, ln2_b, ple_proj, ple_gate, loss_target, m_conv_w_in, m_conv_b_in, m_conv_dw, m_conv_dw_b, m_conv_ln_g, m_conv_ln_b, m_conv_w_out, m_kv_ln_g, m_kv_ln_b, m_w_kv, m_attn_w_q, m_attn_w_o, m_ln1_g, m_ln1_b, m_mlp_up, m_mlp_down, m_ln2_g, m_ln2_b, m_ple_proj, m_ple_gate, v_conv_w_in, v_conv_b_in, v_conv_dw, v_conv_dw_b, v_conv_ln_g, v_conv_ln_b, v_conv_w_out, v_kv_ln_g, v_kv_ln_b, v_w_kv, v_attn_w_q, v_attn_w_o, v_ln1_g, v_ln1_b, v_mlp_up, v_mlp_down, v_ln2_g, v_ln2_b, v_ple_proj, v_ple_gate):
    given = dict(locals())
    wts = {n: given[n] for n in WEIGHTS}
    moms = {n: given["m_" + n] for n in WEIGHTS}
    vels = {n: given["v_" + n] for n in WEIGHTS}
    s, d = x.shape[1], x.shape[2]
    shard = d // N_DEV
    me = 4 * lax.axis_index("x") + 2 * lax.axis_index("y") + lax.axis_index("c")

    def layers_of(a):
        return a.reshape((-1,) + a.shape[-2:])

    shards = {}
    for n in BIG:
        w3 = layers_of(wts[n])
        for ly in range(w3.shape[0]):
            shards[n + str(ly) if w3.shape[0] > 1 else n] = w3[ly].astype(BF16)
    wt = Weights(shards)
    pack, at = _stack_rows([wts["conv_b_in"].reshape(2, shard), wts["conv_dw"].reshape(CONV_WIDTH, shard),
                            wts["conv_dw_b"], wts["conv_ln_g"], wts["conv_ln_b"]])
    ex = Exchange(gathers=[(shards["conv_w_in"], True), (pack, False)], keys=["conv_w_in"])
    _exchange_alone("gather_first", ex)
    wt.landed(ex)
    packed = ex.gathered[1]
    small = dict(conv_b_in=packed[:, at[0]:at[0] + 2].reshape(1, 2 * d), conv_dw=packed[:, at[1]:at[1] + CONV_WIDTH],
                 conv_dw_b=packed[:, at[2]].reshape(1, d), conv_ln_g=packed[:, at[3]].reshape(1, d),
                 conv_ln_b=packed[:, at[4]].reshape(1, d), kv_ln_g=kv_ln_g.reshape(1, d), kv_ln_b=kv_ln_b.reshape(1, d),
                 ln1_g=ln1_g, ln1_b=ln1_b, ln2_g=ln2_g, ln2_b=ln2_b)

    loss, grad_x, all_rows, at, g_win = _local_step(x[0], p[:, 0], positions.reshape(s, 1), loss_target[0], wt, small)
    loss = lax.psum(loss, ("x", "y", "c"))

    riding = dict(mlp_down={"conv_w_in": (g_win, (4, 8, 8))})
    out = {}
    for n in list(riding) + [n for n in BIG if n not in riding]:
        w3 = layers_of(wts[n])
        keys = [n + str(ly) if w3.shape[0] > 1 else n for ly in range(w3.shape[0])]
        ex = wt.scatter(riding.get(n, {}))
        res = _adamw_big("adamw_" + n, [wt.chunks(k) for k in keys], w3, layers_of(moms[n]), layers_of(vels[n]), carry=ex)
        wt.received(ex)
        out[n] = [r.reshape(wts[n].shape) for r in res]
    tot = _sum_slots("sum_small_grads", all_rows)
    mine = lax.dynamic_slice_in_dim(tot, me * shard, shard, axis=1)
    b_in = lax.dynamic_slice_in_dim(tot[at[0]:at[0] + 2].reshape(1, 2 * d), me * 2 * shard, 2 * shard, axis=1)
    g_small = dict(conv_b_in=b_in, conv_dw=mine[at[1]:at[1] + CONV_WIDTH].reshape(conv_dw.shape), conv_dw_b=mine[at[2]:at[2] + 1],
                   conv_ln_g=mine[at[3]:at[3] + 1], conv_ln_b=mine[at[4]:at[4] + 1], kv_ln_g=tot[at[5]], kv_ln_b=tot[at[6]])
    for j, n in enumerate(("ln1_g", "ln1_b", "ln2_g", "ln2_b")):
        g_small[n] = tot[at[7 + j]:at[7 + j] + DEPTH]
    order = [n for n in WEIGHTS if n not in BIG]

    def flat(t):
        return _stack_rows([t[n].reshape(-1, shard) for n in order])

    (w_s, at), (g_s, _), (m_s, _), (v_s, _) = flat(wts), flat(g_small), flat(moms), flat(vels)
    d_s, m_s, v_s = _adamw_small("adamw_small", w_s, g_s, m_s, v_s)
    for n, a in zip(order, at):
        nrow = wts[n].size // shard
        out[n] = [g_small[n].reshape(wts[n].shape)] + [t[a:a + nrow].reshape(wts[n].shape) for t in (d_s, m_s, v_s)]
    return (loss, grad_x[None], *[out[n][0] for n in WEIGHTS], *[out[n][1] for n in WEIGHTS],
            *[out[n][2] for n in WEIGHTS], *[out[n][3] for n in WEIGHTS])
```

```python
import functools

import numpy as np
import jax
import jax.numpy as jnp
from jax import lax
from jax.experimental import pallas as pl
from jax.experimental.pallas import tpu as pltpu

F32, BF16 = jnp.float32, jnp.bfloat16

N_DEV = 8
HEAD_DIM = 128
ATTN_BLOCK = 128
GROUP_DILATIONS = (1, 4, 16)
N_GROUPS = len(GROUP_DILATIONS)
CONV_WIDTH = 31
CONV_HALO = 32
CONV_ROWS = 64
ROPE_THETA = 10000.0
LN_EPS = 1e-5
DEPTH = 2
ALPHA = (2 * DEPTH) ** 0.25
ADAM_LR, ADAM_B1, ADAM_B2, ADAM_EPS, ADAM_WD, ADAM_STEP = 0.001, 0.9, 0.999, 1e-08, 0.01, 10
NEG = -1e30
V7X_VMEM_LIMIT = 56 * 2 ** 20
LANE = 128
SUBLANES = 8
ROW_TILE = 256
GRAD_DTYPE = BF16

MESH = pl.DeviceIdType.MESH
ANY = pl.BlockSpec(memory_space=pl.ANY)


def _params(*sem):
    return pltpu.CompilerParams(dimension_semantics=sem or None, vmem_limit_bytes=V7X_VMEM_LIMIT)


def _sigmoid(x):
    return 1.0 / (1.0 + jnp.exp(-x))


def _divisor(n, most):
    best = None
    for t in range(LANE, min(n, most) + 1, LANE):
        if n % t == 0:
            best = t
    assert best is not None, (n, most)
    return best


def _stack_rows(parts):
    out, offsets, at = [], [], 0
    for a in parts:
        pad = -a.shape[0] % SUBLANES
        offsets.append(at)
        out.append(a)
        if pad:
            out.append(jnp.zeros((pad, a.shape[1]), a.dtype))
        at += a.shape[0] + pad
    return jnp.concatenate(out, axis=0), offsets


class Exchange:
    OTHER_CHIPS = (4, 2, 6)

    def __init__(self, gathers=(), scatters=(), keys=()):
        self.gathers, self.g_cols = [a for a, _ in gathers], [c for _, c in gathers]
        self.scatters, self.s_cols, self.s_parts = [s[0] for s in scatters], [s[1] for s in scatters], [s[2] for s in scatters]
        self.keys = list(keys)
        self.n_g, self.n_s = len(self.gathers), len(self.scatters)
        self.n = self.n_g + self.n_s
        self.operands = self.gathers + self.scatters
        self.gathered = self.parts = None

    def rows(self, t):
        a = self.scatters[t]
        first, last, of = self.s_parts[t]
        per = (a.shape[0] if self.s_cols[t] else a.shape[1]) // of
        return first * per, (last - first) * per

    def out_shape(self):
        outs = []
        for a, cols in zip(self.gathers, self.g_cols):
            outs.append(jax.ShapeDtypeStruct((a.shape[0], N_DEV * a.shape[1]) if cols else (N_DEV,) + a.shape, a.dtype))
        for t, (a, cols) in enumerate(zip(self.scatters, self.s_cols)):
            outs.append(jax.ShapeDtypeStruct((N_DEV, self.rows(t)[1], a.shape[1] // N_DEV if cols else a.shape[2]), a.dtype))
        return outs

    def scratch(self):
        dma = pltpu.SemaphoreType.DMA
        return [dma((max(self.n_g, 1) * 7,)), dma((max(self.n_g, 1) * 7,)), dma((max(self.n_s, 1) * 7,)),
                dma((max(self.n_s, 1) * 7,)), dma((self.n,))]

    def take(self, results):
        self.gathered, self.parts = list(results[:self.n_g]), list(results[self.n_g:])

    def _copies(self, ins, outs, sems):
        n_g, n_s = self.n_g, self.n_s
        g_in, s_in, g_out, s_out = ins[:n_g], ins[n_g:], outs[:n_g], outs[n_g:]
        g_send, g_recv, s_send, s_recv, local_sem = sems
        x, y, c = lax.axis_index("x"), lax.axis_index("y"), lax.axis_index("c")

        def peer(k):
            return (1 - x if k & 4 else x, 1 - y if k & 2 else y, 1 - c if k & 1 else c)

        def number(p):
            return 4 * p[0] + 2 * p[1] + p[2]

        me = number((x, y, c))

        def slot(t, j):
            first, count = self.rows(t)
            if self.s_cols[t]:
                width = self.scatters[t].shape[1] // N_DEV
                return s_in[t].at[pl.ds(first, count), pl.ds(pl.multiple_of(j * width, LANE), width)]
            return s_in[t].at[j, pl.ds(first, count)]

        def place(t, j):
            if self.g_cols[t]:
                width = self.gathers[t].shape[1]
                return g_out[t].at[:, pl.ds(pl.multiple_of(j * width, LANE), width)]
            return g_out[t].at[j]

        def local():
            cps = [pltpu.make_async_copy(g_in[t], place(t, me), local_sem.at[t]) for t in range(n_g)]
            return cps + [pltpu.make_async_copy(slot(t, me), s_out[t].at[me], local_sem.at[n_g + t]) for t in range(n_s)]

        def scatter(t, k):
            p = peer(k)
            return pltpu.make_async_remote_copy(
                src_ref=slot(t, number(p)), dst_ref=s_out[t].at[me], send_sem=s_send.at[t * 7 + k - 1],
                recv_sem=s_recv.at[t * 7 + k - 1], device_id=p, device_id_type=MESH)

        def landed(t, k):
            p = peer(k)
            return pltpu.make_async_remote_copy(
                src_ref=slot(t, me), dst_ref=s_out[t].at[number(p)], send_sem=s_send.at[t * 7 + k - 1],
                recv_sem=s_recv.at[t * 7 + k - 1], device_id=p, device_id_type=MESH)

        def gather(t, pair, block, to, src=None):
            slot = place(t, number(block))
            return pltpu.make_async_remote_copy(
                src_ref=slot if src is None else src, dst_ref=slot, send_sem=g_send.at[t * 7 + pair],
                recv_sem=g_recv.at[t * 7 + pair], device_id=to, device_id_type=MESH)

        def first_sends():
            cps = []
            for t in range(n_g):
                cps.append(gather(t, 0, peer(0), peer(1), src=g_in[t]))
                cps += [gather(t, 1 + j, peer(0), peer(k), src=g_in[t]) for j, k in enumerate(self.OTHER_CHIPS)]
            for t in range(n_s):
                cps += [scatter(t, k) for k in range(1, N_DEV)]
            return cps

        return peer, local, landed, gather, first_sends

    def start(self, ins, outs, sems):
        _, local, _, _, first_sends = self._copies(ins, outs, sems)
        for cp in local() + first_sends():
            cp.start()

    def finish(self, ins, outs, sems):
        peer, local, landed, gather, first_sends = self._copies(ins, outs, sems)
        mine, sibling = peer(0), peer(1)
        passed = []
        for j, k in enumerate(self.OTHER_CHIPS):
            for t in range(self.n_g):
                gather(t, 1 + j, peer(k), mine).wait_recv()
                passed.append(gather(t, 4 + j, peer(k), sibling))
                passed[-1].start()
        for t in range(self.n_g):
            gather(t, 0, sibling, mine).wait_recv()
            for j, k in enumerate(self.OTHER_CHIPS):
                gather(t, 4 + j, peer(k ^ 1), mine).wait_recv()
        for t in range(self.n_s):
            for k in range(1, N_DEV):
                landed(t, k).wait_recv()
        for cp in first_sends() + passed:
            cp.wait_send()
        for cp in local():
            cp.wait()


def _exchange_alone(name, ex):
    def body(*refs):
        ins, outs, sems = refs[:ex.n], refs[ex.n:2 * ex.n], refs[2 * ex.n:]
        ex.start(ins, outs, sems)
        ex.finish(ins, outs, sems)

    ex.take(pl.pallas_call(body, name=name, in_specs=[ANY] * ex.n, out_specs=[ANY] * ex.n, out_shape=ex.out_shape(),
                           scratch_shapes=ex.scratch())(*ex.operands))


def _call(name, body, args, *, grid, in_specs, out_specs, out_shape, scratch_shapes=(), sem=(), carry=None):
    if carry is None:
        return pl.pallas_call(body, name=name, grid=grid, in_specs=in_specs, out_specs=out_specs, out_shape=out_shape,
                              scratch_shapes=list(scratch_shapes), compiler_params=_params(*sem))(*args)
    ex = carry
    n_in, n_out, n_scr = len(args), len(out_shape), len(scratch_shapes)

    def carried(*refs):
        ins, cin = refs[:n_in], refs[n_in:n_in + ex.n]
        at = n_in + ex.n
        outs, cout = refs[at:at + n_out], refs[at + n_out:at + n_out + ex.n]
        at += n_out + ex.n
        scr, sems = refs[at:at + n_scr], refs[at + n_scr:]
        ids = [pl.program_id(a) for a in range(len(grid))]
        first = functools.reduce(jnp.logical_and, [i == 0 for i in ids])
        last = functools.reduce(jnp.logical_and, [i == g - 1 for i, g in zip(ids, grid)])

        @pl.when(first)
        def _():
            ex.start(cin, cout, sems)

        body(*ins, *outs, *scr)

        @pl.when(last)
        def _():
            ex.finish(cin, cout, sems)

    res = pl.pallas_call(
        carried, name=name, grid=grid, in_specs=list(in_specs) + [ANY] * ex.n, out_specs=list(out_specs) + [ANY] * ex.n,
        out_shape=list(out_shape) + ex.out_shape(), scratch_shapes=list(scratch_shapes) + ex.scratch(),
        compiler_params=_params(*("arbitrary",) * len(grid)),
    )(*args, *ex.operands)
    ex.take(res[n_out:])
    return res[:n_out]


class W:
    def __init__(self, arr, cols):
        self.arr, self.cols = arr, cols
        self.k, self.n = arr.shape
        self.shard_cols = self.n // N_DEV if cols else self.n


def _matmul(name, grid, operands, specs, dims, tile, extras, outs, out_specs, epilogue, carry=None, sums=0):
    assert grid[2] == 1
    n_ex, n_out = len(extras), len(outs)

    def body(*refs):
        a_ref, b_ref = refs[0], refs[1]
        ex_refs = refs[2:2 + n_ex]
        out_refs = refs[2 + n_ex:2 + n_ex + n_out]
        sum_refs = refs[2 + n_ex + n_out:]
        if sums:
            @pl.when(jnp.logical_and(pl.program_id(0) == 0, pl.program_id(1) == 0))
            def _():
                for r in sum_refs:
                    r[...] = jnp.zeros(r.shape, F32)
        acc = lax.dot_general(a_ref[...].astype(BF16), b_ref[...].astype(BF16), (dims, ((), ())),
                              preferred_element_type=F32)
        res = epilogue(acc, *[r[...] for r in ex_refs]) if epilogue else (acc,) * n_out
        for r, v in zip(out_refs, res[:n_out]):
            r[...] = v.astype(r.dtype)
        for r, v in zip(sum_refs, res[n_out:]):
            r[...] += v

    total = pl.BlockSpec((1, LANE), lambda i, j, c: (0, 0))
    return _call(name, body, list(operands) + [a for a, _ in extras], grid=grid,
                 in_specs=list(specs) + [s for _, s in extras], out_specs=list(out_specs) + [total] * sums,
                 out_shape=list(outs) + [jax.ShapeDtypeStruct((1, LANE), F32)] * sums,
                 sem=("arbitrary",) * 3 if sums else ("parallel", "parallel", "arbitrary"), carry=carry)


def _extra_specs(extras, tm, tn):
    out = []
    for arr, kind in extras:
        if kind == "tile":
            out.append((arr, pl.BlockSpec((tm, tn), lambda i, j, c: (i, j))))
        else:
            out.append((arr, pl.BlockSpec((1, tn), lambda i, j, c: (0, j))))
    return out


def _tile_cols(contraction, streams):
    left = V7X_VMEM_LIMIT - V7X_VMEM_LIMIT // 8 - 2 * 1024 * contraction * 2
    for cols in (1024, 512, 256):
        if 2 * cols * (1024 * 4 * streams + contraction * 2) <= left:
            return cols
    return LANE


def _mm_nn(name, a, w, out_dtypes, epilogue=None, extras=(), carry=None, sums=0):
    m, k = a.shape
    assert k == w.k
    tm = 1024 if a.dtype == BF16 else 512
    tn = _divisor(w.n, _tile_cols(k, len(out_dtypes) + sum(kind == "tile" for _, kind in extras)))
    assert tm * k * a.dtype.itemsize <= 16 * 2 ** 20, (name, tm, k)
    grid = (m // tm, w.n // tn, 1)
    specs = [pl.BlockSpec((tm, k), lambda i, j, c: (i, 0)), pl.BlockSpec((k, tn), lambda i, j, c: (0, j))]
    outs = [jax.ShapeDtypeStruct((m, w.n), d) for d in out_dtypes]
    out_specs = [pl.BlockSpec((tm, tn), lambda i, j, c: (i, j)) for _ in outs]
    return _matmul(name, grid, (a, w.arr), specs, ((1,), (0,)), (tm, tn), _extra_specs(extras, tm, tn), outs, out_specs,
                   epilogue, carry, sums)


def _mm_nt(name, dy, w, out_dtypes, epilogue=None, extras=(), carry=None):
    m, n = dy.shape
    assert n == w.n and dy.dtype == BF16
    tm = 1024
    to = _divisor(w.k, _tile_cols(n, len(out_dtypes) + sum(kind == "tile" for _, kind in extras)))
    assert tm * n * dy.dtype.itemsize <= 16 * 2 ** 20, (name, tm, n)
    grid = (m // tm, w.k // to, 1)
    specs = [pl.BlockSpec((tm, n), lambda i, j, c: (i, 0)), pl.BlockSpec((to, n), lambda i, j, c: (j, 0))]
    outs = [jax.ShapeDtypeStruct((m, w.k), d) for d in out_dtypes]
    out_specs = [pl.BlockSpec((tm, to), lambda i, j, c: (i, j)) for _ in outs]
    return _matmul(name, grid, (dy, w.arr), specs, ((1,), (1,)), (tm, to), _extra_specs(extras, tm, to), outs, out_specs,
                   epilogue, carry)


def _mm_tn(name, a, dy, like, carry=None):
    m, k = a.shape
    n = dy.shape[1]
    assert (k, n) == (like.k, like.n)
    tk = _divisor(k, 1024 if a.dtype == BF16 else 512)
    tn = _divisor(n, 1024)
    grid = (k // tk, n // tn, 1)
    specs = [pl.BlockSpec((m, tk), lambda i, j, c: (0, i)), pl.BlockSpec((m, tn), lambda i, j, c: (0, j))]
    out_specs = [pl.BlockSpec((tk, tn), lambda i, j, c: (i, j))]
    (g,) = _matmul(name, grid, (a, dy), specs, ((0,), (0,)), (tk, tn), [], [jax.ShapeDtypeStruct((k, n), GRAD_DTYPE)],
                   out_specs, None, carry)
    return g if like.cols else g.reshape(N_DEV, k // N_DEV, n)


def _rows(arr, blk=0, width=None):
    return ("rows", arr, blk, width or arr.shape[1])


def _full(arr):
    return ("full", arr)


def _by_residue(arr):
    return ("residue", arr)


def _rowwise(name, fn, ins, outs, reds=(), ts=256, carry=None, scratch=()):
    s = next(i[1].shape[0] if i[0] == "rows" else i[1].shape[0] * i[1].shape[1] for i in ins if i[0] != "full")
    n_in, n_out, n_red = len(ins), len(outs), len(reds)
    in_specs = []
    for i in ins:
        if i[0] == "rows":
            in_specs.append(pl.BlockSpec((ts, i[3]), functools.partial(lambda t, blk: (t, blk), blk=i[2])))
        elif i[0] == "residue":
            d, _, w = i[1].shape
            in_specs.append(pl.BlockSpec((d, ts // d, w), lambda t: (0, t, 0)))
        else:
            in_specs.append(pl.BlockSpec(i[1].shape, functools.partial(lambda t, nd: (0,) * nd, nd=i[1].ndim)))
    out_shape, out_specs = [], []
    for o in outs:
        if len(o) == 2:
            out_shape.append(jax.ShapeDtypeStruct((s, o[0]), o[1]))
            out_specs.append(pl.BlockSpec((ts, o[0]), lambda t: (t, 0)))
        else:
            out_shape.append(jax.ShapeDtypeStruct((o[2], s // o[2], o[0]), o[1]))
            out_specs.append(pl.BlockSpec((o[2], ts // o[2], o[0]), lambda t: (0, t, 0)))
    out_shape += [jax.ShapeDtypeStruct(r, F32) for r in reds]
    out_specs += [pl.BlockSpec(r, lambda t: (0, 0)) for r in reds]

    def body(*refs):
        red_refs = refs[n_in + n_out:n_in + n_out + n_red]
        if red_refs:
            @pl.when(pl.program_id(0) == 0)
            def _():
                for r in red_refs:
                    r[...] = jnp.zeros(r.shape, F32)
        fn(refs[:n_in], refs[n_in:n_in + n_out], red_refs, *refs[n_in + n_out + n_red:])

    return _call(name, body, [i[1] for i in ins], grid=(s // ts,), in_specs=in_specs, out_specs=out_specs,
                 out_shape=out_shape, scratch_shapes=list(scratch), sem=("arbitrary" if reds else "parallel",), carry=carry)


def _ln_stats(x):
    mu = jnp.mean(x, axis=-1, keepdims=True)
    xc = x - mu
    var = jnp.mean(xc * xc, axis=-1, keepdims=True)
    return xc * lax.rsqrt(var + LN_EPS), lax.rsqrt(var + LN_EPS)


def _layer_norm(name, x, g, b, out_dtypes):
    def fn(i, o, r):
        xhat, _ = _ln_stats(i[0][...])
        y = xhat * i[1][...] + i[2][...]
        for ref in o:
            ref[...] = y.astype(ref.dtype)

    return _rowwise(name, fn, [_rows(x), _full(g), _full(b)], [(x.shape[1], d) for d in out_dtypes])


def _ln_bwd_tile(x, g, dy):
    xhat, rstd = _ln_stats(x)
    dyg = dy * g
    m1 = jnp.mean(dyg, axis=-1, keepdims=True)
    m2 = jnp.mean(dyg * xhat, axis=-1, keepdims=True)
    dx = rstd * (dyg - m1 - xhat * m2)
    return dx, jnp.sum(dy * xhat, axis=0, keepdims=True), jnp.sum(dy, axis=0, keepdims=True)


def _layer_norm_bwd(name, x, g, dy, carry=None):
    d = x.shape[1]

    def fn(i, o, r):
        dx, dg, db = _ln_bwd_tile(i[0][...], i[1][...], i[2][...])
        o[0][...] = dx
        o[1][...] = dx.astype(BF16)
        r[0][...] += dg
        r[1][...] += db

    return _rowwise(name, fn, [_rows(x), _full(g), _rows(dy)], [(d, F32), (d, BF16)], [(1, d), (1, d)], carry=carry)


def _shifted_copies(buf, shifted, ts):
    rows = ts + CONV_HALO - SUBLANES
    for s in range(1, SUBLANES):
        shifted[s - 1] = buf[pl.ds(s, rows), :]


def _rows_from(buf, shifted, start):
    s = start % SUBLANES
    if s == 0:
        return buf[pl.ds(start, CONV_ROWS), :]
    return shifted[s - 1, pl.ds(start - s, CONV_ROWS), :]


def _conv_fwd(name, glu, dw, dw_b, ts=512, carry=None):
    s, c = glu.shape
    tc = dw.shape[2]
    per = ts // CONV_HALO
    back = CONV_HALO - (CONV_WIDTH - 1)

    def body(cur_ref, prev_ref, w_ref, b_ref, out_ref, buf):
        i = pl.program_id(1)
        buf[pl.ds(0, CONV_HALO), :] = jnp.where(i > 0, prev_ref[...], 0.0)
        buf[pl.ds(CONV_HALO, ts), :] = cur_ref[...]
        for r0 in range(0, ts, CONV_ROWS):
            acc = jnp.broadcast_to(b_ref[...], (CONV_ROWS, tc))
            for j in range(CONV_WIDTH):
                acc = acc + w_ref[j:j + 1, :] * buf[pl.ds(r0 + back + j, CONV_ROWS), :]
            out_ref[pl.ds(r0, CONV_ROWS), :] = acc

    (out,) = _call(
        name, body, [glu, glu, dw, dw_b], grid=(c // tc, s // ts),
        in_specs=[pl.BlockSpec((ts, tc), lambda j, i: (i, j)),
                  pl.BlockSpec((CONV_HALO, tc), lambda j, i: (jnp.maximum(i * per - 1, 0), j)),
                  pl.BlockSpec((None, CONV_WIDTH, tc), lambda j, i: (j, 0, 0)),
                  pl.BlockSpec((1, tc), lambda j, i: (0, j))],
        out_specs=[pl.BlockSpec((ts, tc), lambda j, i: (i, j))],
        out_shape=[jax.ShapeDtypeStruct((s, c), F32)],
        scratch_shapes=[pltpu.VMEM((ts + CONV_HALO, tc), F32)],
        sem=("parallel", "parallel"), carry=carry)
    return out


def _conv_bwd(name, glu, dc, dw, ts=512, carry=None):
    s, c = glu.shape
    tc = dw.shape[2]
    per = ts // CONV_HALO
    back = CONV_HALO - (CONV_WIDTH - 1)
    last = s // ts - 1

    def body(g_ref, gprev_ref, dc_ref, dcnext_ref, w_ref, dglu_ref, ddw_ref, ddb_ref, gbuf, dbuf, gshift, dshift):
        i = pl.program_id(1)

        @pl.when(i == 0)
        def _():
            ddw_ref[...] = jnp.zeros(ddw_ref.shape, F32)
            ddb_ref[...] = jnp.zeros(ddb_ref.shape, F32)

        gbuf[pl.ds(0, CONV_HALO), :] = jnp.where(i > 0, gprev_ref[...], 0.0)
        gbuf[pl.ds(CONV_HALO, ts), :] = g_ref[...]
        dbuf[pl.ds(0, ts), :] = dc_ref[...]
        dbuf[pl.ds(ts, CONV_HALO), :] = jnp.where(i < last, dcnext_ref[...], 0.0)
        _shifted_copies(gbuf, gshift, ts)
        _shifted_copies(dbuf, dshift, ts)
        taps = [jnp.zeros((1, tc), F32)] * CONV_WIDTH
        for r0 in range(0, ts, CONV_ROWS):
            d_here = dbuf[pl.ds(r0, CONV_ROWS), :]
            acc = jnp.zeros((CONV_ROWS, tc), F32)
            for j in range(CONV_WIDTH):
                acc = acc + w_ref[j:j + 1, :] * _rows_from(dbuf, dshift, r0 + (CONV_WIDTH - 1) - j)
                taps[j] = taps[j] + jnp.sum(d_here * _rows_from(gbuf, gshift, r0 + back + j), axis=0, keepdims=True)
            dglu_ref[pl.ds(r0, CONV_ROWS), :] = acc
        for j in range(CONV_WIDTH):
            ddw_ref[j:j + 1, :] += taps[j]
        ddb_ref[...] += jnp.sum(dc_ref[...], axis=0, keepdims=True)

    return _call(
        name, body, [glu, glu, dc, dc, dw], grid=(c // tc, s // ts),
        in_specs=[pl.BlockSpec((ts, tc), lambda j, i: (i, j)),
                  pl.BlockSpec((CONV_HALO, tc), lambda j, i: (jnp.maximum(i * per - 1, 0), j)),
                  pl.BlockSpec((ts, tc), lambda j, i: (i, j)),
                  pl.BlockSpec((CONV_HALO, tc), lambda j, i: (jnp.minimum((i + 1) * per, (last + 1) * per - 1), j)),
                  pl.BlockSpec((None, CONV_WIDTH, tc), lambda j, i: (j, 0, 0))],
        out_specs=[pl.BlockSpec((ts, tc), lambda j, i: (i, j)),
                   pl.BlockSpec((CONV_WIDTH, tc), lambda j, i: (0, j)),
                   pl.BlockSpec((1, tc), lambda j, i: (0, j))],
        out_shape=[jax.ShapeDtypeStruct((s, c), F32), jax.ShapeDtypeStruct((CONV_WIDTH, c), F32),
                   jax.ShapeDtypeStruct((1, c), F32)],
        scratch_shapes=[pltpu.VMEM((ts + CONV_HALO, tc), F32), pltpu.VMEM((ts + CONV_HALO, tc), F32),
                        pltpu.VMEM((SUBLANES - 1, ts + CONV_HALO - SUBLANES, tc), F32),
                        pltpu.VMEM((SUBLANES - 1, ts + CONV_HALO - SUBLANES, tc), F32)],
        sem=("parallel", "arbitrary"), carry=carry)


def _rope_tables(positions):
    half = HEAD_DIM // 2
    inv = (np.float32(ROPE_THETA) ** (-np.arange(half, dtype=np.float32) * np.float32(2.0 / HEAD_DIM))).astype(np.float32)
    inv_freq = jnp.asarray(np.concatenate([inv, inv])[None, :])
    sign = jnp.asarray(np.concatenate([-np.ones(half, np.float32), np.ones(half, np.float32)])[None, :])

    def fn(i, o, r):
        ang = i[0][...].astype(F32) * i[1][...]
        o[0][...] = jnp.cos(ang)
        o[1][...] = jnp.sin(ang) * i[2][...]

    return _rowwise("rope_tables", fn, [_rows(positions), _full(inv_freq), _full(sign)], [(HEAD_DIM, F32), (HEAD_DIM, F32)], ts=512)


def _rot(x, cos, sin):
    return x * cos + pltpu.roll(x, HEAD_DIM // 2, 1) * sin


def _unrot(x, cos, sin):
    return x * cos - pltpu.roll(x, HEAD_DIM // 2, 1) * sin


def _split_rows(scr, value, d):
    if d == 1:
        return [value]
    scr[...] = value
    return [scr[pl.ds(r, scr.shape[0] // d, stride=d), :] for r in range(d)]


def _join_rows(scr, planes):
    d = len(planes)
    if d == 1:
        return planes[0]
    for r, plane in enumerate(planes):
        scr[pl.ds(r, scr.shape[0] // d, stride=d), :] = plane
    return scr[...]


def _lane(h, shape):
    return lax.broadcasted_iota(jnp.int32, shape, 1) == h


def _attn_specs(width):
    cur = pl.BlockSpec((None, ATTN_BLOCK, width), lambda r, n: (r, n, 0))
    prev = pl.BlockSpec((None, ATTN_BLOCK, width), lambda r, n: (r, jnp.maximum(n - 1, 0), 0))
    return cur, prev


def _masks(n):
    row = lax.broadcasted_iota(jnp.int32, (ATTN_BLOCK, ATTN_BLOCK), 0)
    col = lax.broadcasted_iota(jnp.int32, (ATTN_BLOCK, ATTN_BLOCK), 1)
    return col <= row, jnp.logical_and(col >= row, n > 0)


_NT = (((1,), (1,)), ((), ()))
_TN = (((0,), (0,)), ((), ()))
_NN = (((1,), (0,)), ((), ()))


def _attn_fwd(name, q, k, v, carry=None):
    dil, ln, d = k.shape
    nh = d // HEAD_DIM
    nb = ln // ATTN_BLOCK
    scale = HEAD_DIM ** -0.5

    def body(q_ref, kc_ref, kp_ref, vc_ref, vp_ref, o_ref, l_ref):
        mask_c, mask_p = _masks(pl.program_id(1))
        mask = jnp.concatenate([mask_p, mask_c], axis=1)
        stats = jnp.zeros((ATTN_BLOCK, LANE), F32)
        for h in range(nh):
            hs = slice(h * HEAD_DIM, (h + 1) * HEAD_DIM)
            keys = jnp.concatenate([kp_ref[:, hs], kc_ref[:, hs]], axis=0)
            vals = jnp.concatenate([vp_ref[:, hs], vc_ref[:, hs]], axis=0)
            sc = jnp.where(mask, lax.dot_general(q_ref[:, hs], keys, _NT, preferred_element_type=F32) * scale, NEG)
            m = jnp.max(sc, axis=1, keepdims=True)
            p = jnp.exp(sc - m)
            l = jnp.sum(p, axis=1, keepdims=True)
            o_ref[:, hs] = lax.dot_general(p.astype(BF16), vals, _NN, preferred_element_type=F32) / l
            stats = jnp.where(_lane(h, stats.shape), m + jnp.log(l), stats)
        l_ref[...] = stats

    (cur, prev), (stat, _) = _attn_specs(d), _attn_specs(LANE)
    return _call(name, body, [q, k, k, v, v], grid=(dil, nb), in_specs=[cur, cur, prev, cur, prev], out_specs=[cur, stat],
                 out_shape=[jax.ShapeDtypeStruct((dil, ln, d), F32), jax.ShapeDtypeStruct((dil, ln, LANE), F32)],
                 sem=("parallel", "parallel"), carry=carry)


def _attn_dq(name, q, k, v, do, lse, dsum, carry=None):
    dil, ln, d = k.shape
    nh = d // HEAD_DIM
    nb = ln // ATTN_BLOCK
    scale = HEAD_DIM ** -0.5

    def body(q_ref, kc_ref, kp_ref, vc_ref, vp_ref, do_ref, l_ref, d_ref, dq_ref):
        mask_c, mask_p = _masks(pl.program_id(1))
        mask = jnp.concatenate([mask_p, mask_c], axis=1)
        for h in range(nh):
            hs = slice(h * HEAD_DIM, (h + 1) * HEAD_DIM)
            keys = jnp.concatenate([kp_ref[:, hs], kc_ref[:, hs]], axis=0)
            vals = jnp.concatenate([vp_ref[:, hs], vc_ref[:, hs]], axis=0)
            sc = lax.dot_general(q_ref[:, hs], keys, _NT, preferred_element_type=F32) * scale
            p = jnp.where(mask, jnp.exp(jnp.where(mask, sc, NEG) - l_ref[:, h:h + 1]), 0.0)
            dp = lax.dot_general(do_ref[:, hs], vals, _NT, preferred_element_type=F32)
            ds = p * (dp - d_ref[:, h:h + 1])
            dq_ref[:, hs] = lax.dot_general(ds.astype(BF16), keys, _NN, preferred_element_type=F32) * scale

    (cur, prev), (stat, _) = _attn_specs(d), _attn_specs(LANE)
    (dq,) = _call(name, body, [q, k, k, v, v, do, lse, dsum], grid=(dil, nb),
                  in_specs=[cur, cur, prev, cur, prev, cur, stat, stat], out_specs=[cur],
                  out_shape=[jax.ShapeDtypeStruct((dil, ln, d), F32)], sem=("parallel", "parallel"), carry=carry)
    return dq


def _attn_dkv(name, q, k, v, do, lse, dsum, carry=None):
    dil, ln, d = k.shape
    nh = d // HEAD_DIM
    nb = ln // ATTN_BLOCK
    scale = HEAD_DIM ** -0.5

    def body(k_ref, v_ref, qc_ref, qn_ref, doc_ref, don_ref, lc_ref, lnx_ref, dc_ref, dn_ref, dk_ref, dv_ref):
        n = pl.program_id(1)
        row = lax.broadcasted_iota(jnp.int32, (ATTN_BLOCK, ATTN_BLOCK), 0)
        col = lax.broadcasted_iota(jnp.int32, (ATTN_BLOCK, ATTN_BLOCK), 1)
        mask = jnp.concatenate([row <= col, jnp.logical_and(row >= col, n < nb - 1)], axis=1)
        lse_t = jnp.concatenate([lc_ref[...].T, lnx_ref[...].T], axis=1)
        dsum_t = jnp.concatenate([dc_ref[...].T, dn_ref[...].T], axis=1)
        for h in range(nh):
            hs = slice(h * HEAD_DIM, (h + 1) * HEAD_DIM)
            qs = jnp.concatenate([qc_ref[:, hs], qn_ref[:, hs]], axis=0)
            douts = jnp.concatenate([doc_ref[:, hs], don_ref[:, hs]], axis=0)
            sc = lax.dot_general(k_ref[:, hs], qs, _NT, preferred_element_type=F32) * scale
            p = jnp.where(mask, jnp.exp(jnp.where(mask, sc, NEG) - lse_t[h:h + 1, :]), 0.0)
            dp = lax.dot_general(v_ref[:, hs], douts, _NT, preferred_element_type=F32)
            ds = p * (dp - dsum_t[h:h + 1, :])
            dv_ref[:, hs] = lax.dot_general(p.astype(BF16), douts, _NN, preferred_element_type=F32)
            dk_ref[:, hs] = lax.dot_general(ds.astype(BF16), qs, _NN, preferred_element_type=F32) * scale

    def specs(width):
        cur = pl.BlockSpec((None, ATTN_BLOCK, width), lambda r, n: (r, n, 0))
        nxt = pl.BlockSpec((None, ATTN_BLOCK, width), lambda r, n: (r, jnp.minimum(n + 1, nb - 1), 0))
        return cur, nxt

    (cur, nxt), (stat, stat_next) = specs(d), specs(LANE)
    return _call(name, body, [k, v, q, q, do, do, lse, lse, dsum, dsum], grid=(dil, nb),
                 in_specs=[cur, cur, cur, nxt, cur, nxt, stat, stat_next, stat, stat_next], out_specs=[cur, cur],
                 out_shape=[jax.ShapeDtypeStruct((dil, ln, d), F32)] * 2, sem=("parallel", "parallel"), carry=carry)


def _adamw_tile(w, g, m, v):
    m = ADAM_B1 * m + (1.0 - ADAM_B1) * g
    v = ADAM_B2 * v + (1.0 - ADAM_B2) * (g * g)
    m_hat = m / (1.0 - ADAM_B1 ** ADAM_STEP)
    v_hat = v / (1.0 - ADAM_B2 ** ADAM_STEP)
    delta = -ADAM_LR * (m_hat / (jnp.sqrt(v_hat) + ADAM_EPS) + ADAM_WD * w)
    return delta, m, v


def _adamw_big(name, parts, w, m, v, carry=None):
    layers, r, c = w.shape
    assert len(parts) == layers and all(sum(ch.shape[1] for ch in per_layer) == r for per_layer in parts)
    every = [ch for per_layer in parts for ch in per_layer]
    per_row = 2 * c * (len(every) * N_DEV * every[0].dtype.itemsize + 7 * 4)
    tr = 16
    while tr * 2 <= min(min(ch.shape[1] for ch in every), V7X_VMEM_LIMIT // 2 // per_row) and all(ch.shape[1] % (tr * 2) == 0 for ch in every):
        tr *= 2
    pieces = []
    for ly, per_layer in enumerate(parts):
        at = 0
        for ch in per_layer:
            pieces.append((ly, at, ch.shape[1] // tr, ch))
            at += ch.shape[1] // tr

    def within(layer, i, ly, first, tiles):
        return jnp.logical_and(layer == ly, jnp.logical_and(i >= first, i < first + tiles))

    def body(*refs):
        part_refs = refs[:len(pieces)]
        w_ref, m_ref, v_ref, g_out, d_out, m_out, v_out = refs[len(pieces):]
        layer, i = pl.program_id(0), pl.program_id(1)
        for (ly, first, tiles, _), part_ref in zip(pieces, part_refs):
            @pl.when(within(layer, i, ly, first, tiles))
            def _(part_ref=part_ref):
                g = part_ref[0].astype(F32)
                for dev in range(1, N_DEV):
                    g = g + part_ref[dev].astype(F32)
                delta, mn, vn = _adamw_tile(w_ref[...], g, m_ref[...], v_ref[...])
                g_out[...] = g
                d_out[...] = delta
                m_out[...] = mn
                v_out[...] = vn

    def part_index(layer, i, ly, first, tiles):
        return (0, jnp.where(within(layer, i, ly, first, tiles), i - first, 0), 0)

    own = pl.BlockSpec((None, tr, c), lambda ly, i: (ly, i, 0))
    part_specs = [pl.BlockSpec((N_DEV, tr, c), functools.partial(part_index, ly=ly, first=first, tiles=tiles))
                  for ly, first, tiles, _ in pieces]
    return _call(name, body, [ch for _, _, _, ch in pieces] + [w, m, v], grid=(layers, r // tr), in_specs=part_specs + [own] * 3,
                 out_specs=[own] * 4, out_shape=[jax.ShapeDtypeStruct(w.shape, F32)] * 4, sem=("parallel", "parallel"),
                 carry=carry)


def _sum_slots(name, slots):
    _, r, c = slots.shape

    def body(s_ref, o_ref):
        g = s_ref[0]
        for j in range(1, N_DEV):
            g = g + s_ref[j]
        o_ref[...] = g

    return pl.pallas_call(body, name=name, out_shape=jax.ShapeDtypeStruct((r, c), F32),
                          compiler_params=_params())(slots)


def _adamw_small(name, w, g, m, v):
    def body(w_ref, g_ref, m_ref, v_ref, d_out, m_out, v_out):
        delta, mn, vn = _adamw_tile(w_ref[...], g_ref[...], m_ref[...], v_ref[...])
        d_out[...] = delta
        m_out[...] = mn
        v_out[...] = vn

    return pl.pallas_call(body, name=name, out_shape=[jax.ShapeDtypeStruct(w.shape, F32)] * 3,
                          compiler_params=_params())(w, g, m, v)


class Weights:
    def __init__(self, shards):
        self.shards, self.full, self.parts = shards, {}, {}

    @staticmethod
    def by_columns(key):
        return key.rstrip("01") in COLUMN_SHARDED

    def gather(self, *keys):
        return Exchange(gathers=[(self.shards[k], self.by_columns(k)) for k in keys], keys=keys)

    def landed(self, ex):
        for key, full in zip(ex.keys, ex.gathered):
            cols = self.by_columns(key)
            self.full[key] = W(full if cols else full.reshape(-1, full.shape[-1]), cols)

    def scatter(self, grads, gathers=()):
        if not grads and not gathers:
            return None
        return Exchange(gathers=gathers, scatters=[(g, self.by_columns(k), part) for k, (g, part) in grads.items()], keys=list(grads))

    def received(self, ex):
        for key, (first, _, of), part in zip(ex.keys, ex.s_parts, ex.parts) if ex is not None else ():
            self.parts.setdefault(key, {})[first / of] = part

    def chunks(self, key):
        return [self.parts[key][j] for j in sorted(self.parts[key])]

    def __getitem__(self, key):
        return self.full[key]


def _mlp_ple_fwd(tag, z1, p_i, ln1_g, ln1_b, ln2_g, ln2_b, wt, carries, target=None):
    h1, h1b = _layer_norm(f"ln1_{tag}", z1, ln1_g, ln1_b, (F32, BF16))
    up, act = _mm_nn(f"mlp_up_{tag}", h1b, wt["mlp_up" + tag], (F32, BF16),
                     epilogue=lambda acc: (acc, jnp.square(jnp.maximum(acc, 0.0))), carry=carries.get("mlp_up"))
    if "mlp_up" in carries:
        wt.landed(carries["mlp_up"])
    (z2,) = _mm_nn(f"mlp_down_{tag}", act, wt["mlp_down" + tag], (F32,), epilogue=lambda acc, h: (ALPHA * h + acc,),
                   extras=[(h1, "tile")], carry=carries.get("mlp_down"))
    if "mlp_down" in carries:
        wt.landed(carries["mlp_down"])
    h2, h2b = _layer_norm(f"ln2_{tag}", z2, ln2_g, ln2_b, (F32, BF16))
    (pe,) = _mm_nn(f"ple_proj_{tag}", p_i, wt["ple_proj" + tag], (F32,))

    saved = dict(z1=z1, h1b=h1b, up=up, act=act, z2=z2, h2b=h2b, p=p_i)
    if target is None:
        def gate(acc, h, e):
            out = h + e * _sigmoid(acc)
            return acc, out, out

        gp, out, outb = _mm_nn(f"ple_gate_{tag}", h2b, wt["ple_gate" + tag], (F32, F32, BF16), epilogue=gate,
                               extras=[(h2, "tile"), (pe, "tile")], carry=carries.get("ple_gate"))
        if "ple_gate" in carries:
            wt.landed(carries["ple_gate"])
        saved.update(pe=pe, gp=gp)
        return out, outb, saved

    width = z1.shape[1]

    def gate_and_loss(acc, h, e, goal):
        sg = _sigmoid(acc)
        diff = h + e * sg - goal
        d_y = diff * (1.0 / width)
        return d_y, d_y * sg, d_y * e * sg * (1.0 - sg), jnp.broadcast_to(jnp.sum(diff * diff), (1, LANE))

    d_y, d_pe, d_gp, sq = _mm_nn(f"ple_gate_{tag}", h2b, wt["ple_gate" + tag], (F32, BF16, BF16), epilogue=gate_and_loss,
                                 extras=[(h2, "tile"), (pe, "tile"), (target, "tile")], sums=1)
    saved.update(d_pe=d_pe, d_gp=d_gp)
    return d_y, 0.5 * sq[0, 0] / width, saved


WHOLE = (0, 1, 1)


def _mlp_ple_bwd(tag, d_out, sv, ln1_g, ln2_g, wt, plan, waiting):
    made = {}

    def riders(kernel, extra=()):
        items = {name + tag: (made[name], part) for name, part in plan[kernel]}
        return wt.scatter({**items, **dict(extra)})

    d_pe, d_gp = sv["d_pe"], sv["d_gp"]
    made["ple_proj"] = _mm_tn(f"g_ple_proj_{tag}", sv["p"], d_pe, wt["ple_proj" + tag])
    made["ple_gate"] = _mm_tn(f"g_ple_gate_{tag}", sv["h2b"], d_gp, wt["ple_gate" + tag])
    (d_h2,) = _mm_nt(f"d_ple_gate_{tag}", d_gp, wt["ple_gate" + tag], (F32,), epilogue=lambda acc, dy: (dy + acc,),
                     extras=[(d_out, "tile")])
    d_z2, d_z2b, g_ln2_g, g_ln2_b = _layer_norm_bwd(f"ln2_bwd_{tag}", sv["z2"], ln2_g, d_h2)
    ex = riders("g_mlp_down", waiting.items())
    made["mlp_down"] = _mm_tn(f"g_mlp_down_{tag}", sv["act"], d_z2b, wt["mlp_down" + tag], carry=ex)
    wt.received(ex)
    ex = riders("d_mlp_down")
    (d_up,) = _mm_nt(f"d_mlp_down_{tag}", d_z2b, wt["mlp_down" + tag], (BF16,),
                     epilogue=lambda acc, u: (acc * (2.0 * jnp.maximum(u, 0.0)),), extras=[(sv["up"], "tile")], carry=ex)
    wt.received(ex)
    ex = riders("g_mlp_up")
    made["mlp_up"] = _mm_tn(f"g_mlp_up_{tag}", sv["h1b"], d_up, wt["mlp_up" + tag], carry=ex)
    wt.received(ex)
    ex = riders("d_mlp_up")
    (d_h1,) = _mm_nt(f"d_mlp_up_{tag}", d_up, wt["mlp_up" + tag], (F32,), epilogue=lambda acc, dz: (ALPHA * dz + acc,),
                     extras=[(d_z2, "tile")], carry=ex)
    wt.received(ex)
    ex = riders("ln1_bwd")
    d_z1, d_z1b, g_ln1_g, g_ln1_b = _layer_norm_bwd(f"ln1_bwd_{tag}", sv["z1"], ln1_g, d_h1, carry=ex)
    wt.received(ex)
    return d_z1, d_z1b, dict(ln1_g=g_ln1_g, ln1_b=g_ln1_b, ln2_g=g_ln2_g, ln2_b=g_ln2_b), made["mlp_down"], made["mlp_up"]


def _local_step(x, p, positions, target, wt, small):
    s, d = x.shape
    nh = d // HEAD_DIM
    xb, p = x.astype(BF16), p.astype(BF16)

    ex = wt.gather("conv_w_out", "ple_proj0", "ple_gate0")
    (u,) = _mm_nn("conv_in", xb, wt["conv_w_in"], (F32,), epilogue=lambda acc, b: (acc + b,), extras=[(small["conv_b_in"], "row")],
                  carry=ex)
    wt.landed(ex)

    def glu_fn(i, o, r):
        o[0][...] = i[0][...] * _sigmoid(i[1][...])

    (glu,) = _rowwise("glu", glu_fn, [_rows(u, 0, d), _rows(u, 1, d)], [(d, F32)])
    ex = wt.gather("mlp_up0")
    c = _conv_fwd("dwconv", glu, small["conv_dw"], small["conv_dw_b"], carry=ex)
    wt.landed(ex)

    def ln_silu(i, o, r):
        xhat, _ = _ln_stats(i[0][...])
        n = xhat * i[1][...] + i[2][...]
        o[0][...] = (n * _sigmoid(n)).astype(BF16)

    (sb,) = _rowwise("conv_ln_silu", ln_silu, [_rows(c), _full(small["conv_ln_g"]), _full(small["conv_ln_b"])], [(d, BF16)])
    ex = wt.gather("attn_w_o")
    (z1,) = _mm_nn("conv_out", sb, wt["conv_w_out"], (F32,), epilogue=lambda acc, xt: (ALPHA * xt + acc,), extras=[(x, "tile")],
                   carry=ex)
    wt.landed(ex)
    x1, x1b, sv0 = _mlp_ple_fwd("0", z1, p[0], small["ln1_g"][0:1], small["ln1_b"][0:1], small["ln2_g"][0:1], small["ln2_b"][0:1], wt,
                                dict(mlp_up=wt.gather("mlp_down0"), mlp_down=wt.gather("attn_w_q"), ple_gate=wt.gather("w_kv")))

    (kvn,) = _layer_norm("kv_ln", x1, small["kv_ln_g"], small["kv_ln_b"], (BF16,))
    (kv,) = _mm_nn("kv_proj", kvn, wt["w_kv"], (F32,))
    ex = wt.gather("mlp_up1")
    (q,) = _mm_nn("q_proj", x1b, wt["attn_w_q"], (F32,), carry=ex)
    wt.landed(ex)
    cos, sin = _rope_tables(positions)
    row_scratch = [pltpu.VMEM((ROW_TILE, LANE), F32)]

    def rot_kv(i, o, r, scr):
        cs, sn = i[2][...], i[3][...]
        for h in range(nh):
            hs = slice(h * HEAD_DIM, (h + 1) * HEAD_DIM)
            for base, val in ((0, _rot(i[0][:, hs], cs, sn)), (N_GROUPS, i[1][:, hs])):
                for g, dil in enumerate(GROUP_DILATIONS):
                    for res, plane in enumerate(_split_rows(scr, val, dil)):
                        o[base + g][res, :, hs] = plane.astype(BF16)

    by_group = [(d, BF16, dil) for dil in GROUP_DILATIONS]
    kv_groups = _rowwise("rotary_kv", rot_kv, [_rows(kv, 0, d), _rows(kv, 1, d), _rows(cos), _rows(sin)], by_group * 2,
                         ts=ROW_TILE, scratch=row_scratch)
    kg, vg = kv_groups[:N_GROUPS], kv_groups[N_GROUPS:]

    def rot_q(i, o, r, scr):
        cs, sn = i[1][...], i[2][...]
        for g, dil in enumerate(GROUP_DILATIONS):
            for h in range(nh):
                hs = slice(h * HEAD_DIM, (h + 1) * HEAD_DIM)
                val = _rot(i[0][:, g * d + h * HEAD_DIM:g * d + (h + 1) * HEAD_DIM], cs, sn)
                for res, plane in enumerate(_split_rows(scr, val, dil)):
                    o[g][res, :, hs] = plane.astype(BF16)

    qg = _rowwise("rotary_q", rot_q, [_rows(q), _rows(cos), _rows(sin)], by_group, ts=ROW_TILE, scratch=row_scratch)

    og, lg = zip(*[_attn_fwd(f"attn_fwd_{g}", qg[g], kg[g], vg[g]) for g in range(N_GROUPS)])

    def merge(i, o, r, scr):
        lses = []
        for g, dil in enumerate(GROUP_DILATIONS):
            lses.append(_join_rows(scr, [i[N_GROUPS + g][res] for res in range(dil)]))
        top = functools.reduce(jnp.maximum, lses)
        es = [jnp.exp(l - top) for l in lses]
        den = functools.reduce(lambda a, b: a + b, es)
        total = top + jnp.log(den)
        for g, dil in enumerate(GROUP_DILATIONS):
            for res, plane in enumerate(_split_rows(scr, total, dil)):
                o[2 + g][res] = plane
        ws = [e / den for e in es]
        for h in range(nh):
            hs = slice(h * HEAD_DIM, (h + 1) * HEAD_DIM)
            out = jnp.zeros((ROW_TILE, HEAD_DIM), F32)
            for g, dil in enumerate(GROUP_DILATIONS):
                og_h = _join_rows(scr, [i[g][res, :, hs] for res in range(dil)])
                out = out + ws[g][:, h:h + 1] * og_h
            o[0][:, hs] = out
            o[1][:, hs] = out.astype(BF16)

    merged = _rowwise("attn_merge", merge, [_by_residue(t) for t in og + lg],
                      [(d, F32), (d, BF16)] + [(LANE, F32, dil) for dil in GROUP_DILATIONS], ts=ROW_TILE, scratch=row_scratch)
    o, ob, lse_g = merged[0], merged[1], merged[2:]
    (z1b,) = _mm_nn("attn_out", ob, wt["attn_w_o"], (F32,), epilogue=lambda acc, xt: (ALPHA * xt + acc,), extras=[(x1, "tile")])
    d_y, loss, sv1 = _mlp_ple_fwd("1", z1b, p[1], small["ln1_g"][1:2], small["ln1_b"][1:2], small["ln2_g"][1:2], small["ln2_b"][1:2],
                                  wt, dict(mlp_up=wt.gather("mlp_down1"), mlp_down=wt.gather("ple_proj1", "ple_gate1")), target=target)

    plan = dict(g_mlp_down=[("ple_proj", WHOLE), ("ple_gate", WHOLE)], d_mlp_down=[("mlp_down", (0, 3, 8))],
                g_mlp_up=[("mlp_down", (3, 6, 8))], d_mlp_up=[("mlp_down", (6, 8, 8))], ln1_bwd=[])
    d_z1, d_z1b, g1, _, g_up1 = _mlp_ple_bwd("1", d_y, sv1, small["ln1_g"][1:2], small["ln2_g"][1:2], wt, plan, {})
    g_wo = _mm_tn("g_attn_out", ob, d_z1b, wt["attn_w_o"])
    (d_o,) = _mm_nt("d_attn_out", d_z1b, wt["attn_w_o"], (F32,))

    def dsum_fn(i, o, r, scr):
        stats = jnp.zeros((ROW_TILE, LANE), F32)
        for h in range(nh):
            hs = slice(h * HEAD_DIM, (h + 1) * HEAD_DIM)
            dout = i[0][:, hs]
            stats = jnp.where(_lane(h, stats.shape), jnp.sum(dout * i[1][:, hs], axis=1, keepdims=True), stats)
            for g, dil in enumerate(GROUP_DILATIONS):
                for res, plane in enumerate(_split_rows(scr, dout, dil)):
                    o[g][res, :, hs] = plane.astype(BF16)
        for g, dil in enumerate(GROUP_DILATIONS):
            for res, plane in enumerate(_split_rows(scr, stats, dil)):
                o[N_GROUPS + g][res] = plane

    res_ = _rowwise("attn_dsum", dsum_fn, [_rows(d_o), _rows(o)], by_group + [(LANE, F32, dil) for dil in GROUP_DILATIONS],
                    ts=ROW_TILE, scratch=row_scratch)
    dog, dsum_g = res_[:N_GROUPS], res_[N_GROUPS:]
    dqs, dks, dvs = [], [], []
    riders = [wt.scatter({"mlp_up1": (g_up1, (q, q + 1, 4))}) for q in range(4)] + [wt.scatter({"attn_w_o": (g_wo, WHOLE)}), None]
    for g in range(N_GROUPS):
        dqs.append(_attn_dq(f"attn_dq_{g}", qg[g], kg[g], vg[g], dog[g], lse_g[g], dsum_g[g], carry=riders[2 * g]))
        dk, dv = _attn_dkv(f"attn_dkv_{g}", qg[g], kg[g], vg[g], dog[g], lse_g[g], dsum_g[g], carry=riders[2 * g + 1])
        dks.append(dk)
        dvs.append(dv)
    for ex in riders:
        wt.received(ex)

    def unrot_q(i, o, r, scr):
        cs, sn = i[N_GROUPS][...], i[N_GROUPS + 1][...]
        for g, dil in enumerate(GROUP_DILATIONS):
            for h in range(nh):
                hs = slice(h * HEAD_DIM, (h + 1) * HEAD_DIM)
                dq = _join_rows(scr, [i[g][res, :, hs] for res in range(dil)])
                o[0][:, g * d + h * HEAD_DIM:g * d + (h + 1) * HEAD_DIM] = _unrot(dq, cs, sn).astype(BF16)

    (d_q,) = _rowwise("rotary_q_bwd", unrot_q, [_by_residue(t) for t in dqs] + [_rows(cos), _rows(sin)], [(N_GROUPS * d, BF16)],
                      ts=ROW_TILE, scratch=row_scratch)

    def unrot_kv(i, o, r, scr):
        cs, sn = i[2 * N_GROUPS][...], i[2 * N_GROUPS + 1][...]
        for h in range(nh):
            hs = slice(h * HEAD_DIM, (h + 1) * HEAD_DIM)
            for base in (0, N_GROUPS):
                tot = jnp.zeros((ROW_TILE, HEAD_DIM), F32)
                for g, dil in enumerate(GROUP_DILATIONS):
                    tot = tot + _join_rows(scr, [i[base + g][res, :, hs] for res in range(dil)])
                if base == 0:
                    o[0][:, hs] = _unrot(tot, cs, sn).astype(BF16)
                else:
                    o[0][:, d + h * HEAD_DIM:d + (h + 1) * HEAD_DIM] = tot.astype(BF16)

    (d_kv,) = _rowwise("rotary_kv_bwd", unrot_kv, [_by_residue(t) for t in dks + dvs] + [_rows(cos), _rows(sin)], [(2 * d, BF16)],
                       ts=ROW_TILE, scratch=row_scratch)
    g_wq = _mm_tn("g_q_proj", x1b, d_q, wt["attn_w_q"])
    ex = wt.scatter({"attn_w_q": (g_wq, (0, 3, 8))})
    g_wkv = _mm_tn("g_kv_proj", kvn, d_kv, wt["w_kv"], carry=ex)
    wt.received(ex)
    ex = wt.scatter({"attn_w_q": (g_wq, (3, 6, 8))})
    (d_x1a,) = _mm_nt("d_q_proj", d_q, wt["attn_w_q"], (F32,), epilogue=lambda acc, dz: (ALPHA * dz + acc,), extras=[(d_z1, "tile")],
                      carry=ex)
    wt.received(ex)
    ex = wt.scatter({"attn_w_q": (g_wq, (6, 8, 8)), "w_kv": (g_wkv, (0, 1, 8))})
    (d_kvn,) = _mm_nt("d_kv_proj", d_kv, wt["w_kv"], (F32,), carry=ex)
    wt.received(ex)

    def kv_ln_bwd(i, o, r):
        dx, dg, db = _ln_bwd_tile(i[0][...], i[1][...], i[2][...])
        dx = dx + i[3][...]
        sg = _sigmoid(i[5][...])
        o[0][...] = dx
        o[1][...] = (dx * sg).astype(BF16)
        o[2][...] = (dx * i[4][...] * sg * (1.0 - sg)).astype(BF16)
        r[0][...] += dg
        r[1][...] += db

    ex = wt.scatter({"w_kv": (g_wkv, (1, 4, 8))})
    d_x1, sv0["d_pe"], sv0["d_gp"], g_kv_ln_g, g_kv_ln_b = _rowwise(
        "kv_ln_bwd", kv_ln_bwd, [_rows(x1), _full(small["kv_ln_g"]), _rows(d_kvn), _rows(d_x1a), _rows(sv0["pe"]), _rows(sv0["gp"])],
        [(d, F32), (d, BF16), (d, BF16)], [(1, d), (1, d)], ts=128, carry=ex)
    wt.received(ex)

    plan = dict(g_mlp_down=[("ple_proj", WHOLE)], d_mlp_down=[("ple_gate", WHOLE), ("mlp_down", (0, 1, 8))],
                g_mlp_up=[("mlp_down", (1, 4, 8))], d_mlp_up=[("mlp_down", (4, 7, 8))], ln1_bwd=[("mlp_up", (0, 1, 8))])
    d_z1, d_z1b, g0, g_down0, g_up0 = _mlp_ple_bwd("0", d_x1, sv0, small["ln1_g"][0:1], small["ln2_g"][0:1], wt, plan,
                                                   {"w_kv": (g_wkv, (4, 8, 8))})
    ex = wt.scatter({"mlp_up0": (g_up0, (1, 2, 8))})
    g_wout = _mm_tn("g_conv_out", sb, d_z1b, wt["conv_w_out"], carry=ex)
    wt.received(ex)
    ex = wt.scatter({"mlp_up0": (g_up0, (2, 3, 8))})
    (d_s,) = _mm_nt("d_conv_out", d_z1b, wt["conv_w_out"], (F32,), carry=ex)
    wt.received(ex)

    def ln_silu_bwd(i, o, r):
        cx, gn, bn, ds_ = i[0][...], i[1][...], i[2][...], i[3][...]
        xhat, _ = _ln_stats(cx)
        n = xhat * gn + bn
        sg = _sigmoid(n)
        dn = ds_ * (sg * (1.0 + n * (1.0 - sg)))
        dx, dg, db = _ln_bwd_tile(cx, gn, dn)
        o[0][...] = dx
        r[0][...] += dg
        r[1][...] += db

    ex = wt.scatter({"mlp_up0": (g_up0, (3, 4, 8))})
    d_c, g_cln_g, g_cln_b = _rowwise("conv_ln_silu_bwd", ln_silu_bwd,
                                     [_rows(c), _full(small["conv_ln_g"]), _full(small["conv_ln_b"]), _rows(d_s)],
                                     [(d, F32)], [(1, d), (1, d)], carry=ex)
    wt.received(ex)
    ex = wt.scatter({"mlp_down0": (g_down0, (7, 8, 8)), "mlp_up0": (g_up0, (4, 8, 8))})
    d_glu, g_dw, g_dwb = _conv_bwd("dwconv_bwd", glu, d_c, small["conv_dw"], carry=ex)
    wt.received(ex)

    def glu_bwd(i, o, r):
        a, gt, dg_ = i[0][...], i[1][...], i[2][...]
        sg = _sigmoid(gt)
        da = dg_ * sg
        dgate = dg_ * a * sg * (1.0 - sg)
        o[0][:, 0:d] = da.astype(BF16)
        o[0][:, d:2 * d] = dgate.astype(BF16)
        r[0][:, 0:d] += jnp.sum(da, axis=0, keepdims=True)
        r[0][:, d:2 * d] += jnp.sum(dgate, axis=0, keepdims=True)

    ex = wt.scatter({"conv_w_out": (g_wout, (0, 1, 2))})
    d_u, g_bin = _rowwise("glu_bwd", glu_bwd, [_rows(u, 0, d), _rows(u, 1, d), _rows(d_glu)], [(2 * d, BF16)], [(1, 2 * d)],
                          carry=ex)
    wt.received(ex)
    ex = wt.scatter({"conv_w_out": (g_wout, (1, 2, 2))})
    g_win = _mm_tn("g_conv_in", xb, d_u, wt["conv_w_in"], carry=ex)
    wt.received(ex)
    rows = [g_bin.reshape(2, d), g_dw, g_dwb, g_cln_g, g_cln_b, g_kv_ln_g, g_kv_ln_b]
    rows += [jnp.concatenate([g0[n], g1[n]], axis=0) for n in ("ln1_g", "ln1_b", "ln2_g", "ln2_b")]
    rows, offsets = _stack_rows(rows)
    ex = wt.scatter({"conv_w_in": (g_win, (0, 4, 8))}, gathers=[(rows, False)])
    (grad_x,) = _mm_nt("d_conv_in", d_u, wt["conv_w_in"], (F32,), epilogue=lambda acc, dz: (ALPHA * dz + acc,), extras=[(d_z1, "tile")],
                       carry=ex)
    wt.received(ex)
    return loss, grad_x, ex.gathered[0], offsets, g_win


BIG = ("conv_w_in", "conv_w_out", "w_kv", "attn_w_q", "attn_w_o", "mlp_up", "mlp_down", "ple_proj", "ple_gate")
COLUMN_SHARDED = ("conv_w_in", "w_kv", "attn_w_q", "mlp_up", "ple_proj")
WEIGHTS = ("conv_w_in", "conv_b_in", "conv_dw", "conv_dw_b", "conv_ln_g", "conv_ln_b", "conv_w_out", "kv_ln_g", "kv_ln_b",
           "w_kv", "attn_w_q", "attn_w_o", "ln1_g", "ln1_b", "mlp_up", "mlp_down", "ln2_g", "ln2_b", "ple_proj", "ple_gate")


def kernel(x, p, positions, conv_w_in, conv_b_in, conv_dw, conv_dw_b, conv_ln_g, conv_ln_b, conv_w_out, kv_ln_g, kv_ln_b, w_kv, attn_w_q, attn_w_o, ln1_g, ln1_b, mlp_up, mlp_down, ln2_g, ln2_b, ple_proj, ple_gate, loss_target, m_conv_w_in, m_conv_b_in, m_conv_dw, m_conv_dw_b, m_conv_ln_g, m_conv_ln_b, m_conv_w_out, m_kv_ln_g, m_kv_ln_b, m_w_kv, m_attn_w_q, m_attn_w_o, m_ln1_g, m_ln1_b, m_mlp_up, m_mlp_down, m_ln2_g, m_ln2_b, m_ple_proj, m_ple_gate, v_conv_w_in, v_conv_b_in, v_conv_dw, v_conv_dw_b, v_conv_ln_g, v_conv_ln_b, v_conv_w_out, v_kv_ln_g, v_kv_ln_b, v_w_kv, v_attn_w_q, v_attn_w_o, v_ln1_g, v_ln1_b, v_mlp_up, v_mlp_down, v_ln2_g, v_ln2_b, v_ple_proj, v_ple_gate):
    given = dict(locals())
    wts = {n: given[n] for n in WEIGHTS}
    moms = {n: given["m_" + n] for n in WEIGHTS}
    vels = {n: given["v_" + n] for n in WEIGHTS}
    s, d = x.shape[1], x.shape[2]
    shard = d // N_DEV
    me = 4 * lax.axis_index("x") + 2 * lax.axis_index("y") + lax.axis_index("c")

    def layers_of(a):
        return a.reshape((-1,) + a.shape[-2:])

    shards = {}
    for n in BIG:
        w3 = layers_of(wts[n])
        for ly in range(w3.shape[0]):
            shards[n + str(ly) if w3.shape[0] > 1 else n] = w3[ly].astype(BF16)
    wt = Weights(shards)
    pack, at = _stack_rows([wts["conv_b_in"].reshape(2, shard), wts["conv_dw"].reshape(CONV_WIDTH, shard),
                            wts["conv_dw_b"], wts["conv_ln_g"], wts["conv_ln_b"]])
    ex = Exchange(gathers=[(shards["conv_w_in"], True), (pack, False)], keys=["conv_w_in"])
    _exchange_alone("gather_first", ex)
    wt.landed(ex)
    packed = ex.gathered[1]
    small = dict(conv_b_in=packed[:, at[0]:at[0] + 2].reshape(1, 2 * d), conv_dw=packed[:, at[1]:at[1] + CONV_WIDTH],
                 conv_dw_b=packed[:, at[2]].reshape(1, d), conv_ln_g=packed[:, at[3]].reshape(1, d),
                 conv_ln_b=packed[:, at[4]].reshape(1, d), kv_ln_g=kv_ln_g.reshape(1, d), kv_ln_b=kv_ln_b.reshape(1, d),
                 ln1_g=ln1_g, ln1_b=ln1_b, ln2_g=ln2_g, ln2_b=ln2_b)

    loss, grad_x, all_rows, at, g_win = _local_step(x[0], p[:, 0], positions.reshape(s, 1), loss_target[0], wt, small)
    loss = lax.psum(loss, ("x", "y", "c"))

    riding = dict(mlp_down={"conv_w_in": (g_win, (4, 8, 8))})
    out = {}
    for n in list(riding) + [n for n in BIG if n not in riding]:
        w3 = layers_of(wts[n])
        keys = [n + str(ly) if w3.shape[0] > 1 else n for ly in range(w3.shape[0])]
        ex = wt.scatter(riding.get(n, {}))
        res = _adamw_big("adamw_" + n, [wt.chunks(k) for k in keys], w3, layers_of(moms[n]), layers_of(vels[n]), carry=ex)
        wt.received(ex)
        out[n] = [r.reshape(wts[n].shape) for r in res]
    tot = _sum_slots("sum_small_grads", all_rows)
    mine = lax.dynamic_slice_in_dim(tot, me * shard, shard, axis=1)
    b_in = lax.dynamic_slice_in_dim(tot[at[0]:at[0] + 2].reshape(1, 2 * d), me * 2 * shard, 2 * shard, axis=1)
    g_small = dict(conv_b_in=b_in, conv_dw=mine[at[1]:at[1] + CONV_WIDTH].reshape(conv_dw.shape), conv_dw_b=mine[at[2]:at[2] + 1],
                   conv_ln_g=mine[at[3]:at[3] + 1], conv_ln_b=mine[at[4]:at[4] + 1], kv_ln_g=tot[at[5]], kv_ln_b=tot[at[6]])
    for j, n in enumerate(("ln1_g", "ln1_b", "ln2_g", "ln2_b")):
        g_small[n] = tot[at[7 + j]:at[7 + j] + DEPTH]
    order = [n for n in WEIGHTS if n not in BIG]

    def flat(t):
        return _stack_rows([t[n].reshape(-1, shard) for n in order])

    (w_s, at), (g_s, _), (m_s, _), (v_s, _) = flat(wts), flat(g_small), flat(moms), flat(vels)
    d_s, m_s, v_s = _adamw_small("adamw_small", w_s, g_s, m_s, v_s)
    for n, a in zip(order, at):
        nrow = wts[n].size // shard
        out[n] = [g_small[n].reshape(wts[n].shape)] + [t[a:a + nrow].reshape(wts[n].shape) for t in (d_s, m_s, v_s)]
    return (loss, grad_x[None], *[out[n][0] for n in WEIGHTS], *[out[n][1] for n in WEIGHTS],
            *[out[n][2] for n in WEIGHTS], *[out[n][3] for n in WEIGHTS])
```

```python
import functools

import numpy as np
import jax
import jax.numpy as jnp
from jax import lax
from jax.experimental import pallas as pl
from jax.experimental.pallas import tpu as pltpu

F32, BF16 = jnp.float32, jnp.bfloat16

N_DEV = 8
HEAD_DIM = 128
ATTN_BLOCK = 128
GROUP_DILATIONS = (1, 4, 16)
N_GROUPS = len(GROUP_DILATIONS)
CONV_WIDTH = 31
CONV_HALO = 32
CONV_ROWS = 64
ROPE_THETA = 10000.0
LN_EPS = 1e-5
DEPTH = 2
ALPHA = (2 * DEPTH) ** 0.25
ADAM_LR, ADAM_B1, ADAM_B2, ADAM_EPS, ADAM_WD, ADAM_STEP = 0.001, 0.9, 0.999, 1e-08, 0.01, 10
NEG = -1e30
V7X_VMEM_LIMIT = 56 * 2 ** 20
LANE = 128
SUBLANES = 8
ROW_TILE = 256
GRAD_DTYPE = BF16

MESH = pl.DeviceIdType.MESH
ANY = pl.BlockSpec(memory_space=pl.ANY)


def _params(*sem):
    return pltpu.CompilerParams(dimension_semantics=sem or None, vmem_limit_bytes=V7X_VMEM_LIMIT)


def _sigmoid(x):
    return 1.0 / (1.0 + jnp.exp(-x))


def _divisor(n, most):
    best = None
    for t in range(LANE, min(n, most) + 1, LANE):
        if n % t == 0:
            best = t
    assert best is not None, (n, most)
    return best


def _stack_rows(parts):
    out, offsets, at = [], [], 0
    for a in parts:
        pad = -a.shape[0] % SUBLANES
        offsets.append(at)
        out.append(a)
        if pad:
            out.append(jnp.zeros((pad, a.shape[1]), a.dtype))
        at += a.shape[0] + pad
    return jnp.concatenate(out, axis=0), offsets


class Exchange:
    OTHER_CHIPS = (4, 2, 6)

    def __init__(self, gathers=(), scatters=(), keys=(), forward_at=None):
        self.forward_at = forward_at
        self.gathers, self.g_cols = [a for a, _ in gathers], [c for _, c in gathers]
        self.scatters, self.s_cols, self.s_parts = [s[0] for s in scatters], [s[1] for s in scatters], [s[2] for s in scatters]
        self.keys = list(keys)
        self.n_g, self.n_s = len(self.gathers), len(self.scatters)
        self.n = self.n_g + self.n_s
        self.operands = self.gathers + self.scatters
        self.gathered = self.parts = None

    def rows(self, t):
        a = self.scatters[t]
        first, last, of = self.s_parts[t]
        per = (a.shape[0] if self.s_cols[t] else a.shape[1]) // of
        return first * per, (last - first) * per

    def out_shape(self):
        outs = []
        for a, cols in zip(self.gathers, self.g_cols):
            outs.append(jax.ShapeDtypeStruct((a.shape[0], N_DEV * a.shape[1]) if cols else (N_DEV,) + a.shape, a.dtype))
        for t, (a, cols) in enumerate(zip(self.scatters, self.s_cols)):
            outs.append(jax.ShapeDtypeStruct((N_DEV, self.rows(t)[1], a.shape[1] // N_DEV if cols else a.shape[2]), a.dtype))
        return outs

    def scratch(self):
        dma = pltpu.SemaphoreType.DMA
        return [dma((max(self.n_g, 1) * 7,)), dma((max(self.n_g, 1) * 7,)), dma((max(self.n_s, 1) * 7,)),
                dma((max(self.n_s, 1) * 7,)), dma((self.n,))]

    def take(self, results):
        self.gathered, self.parts = list(results[:self.n_g]), list(results[self.n_g:])

    def _copies(self, ins, outs, sems):
        n_g, n_s = self.n_g, self.n_s
        g_in, s_in, g_out, s_out = ins[:n_g], ins[n_g:], outs[:n_g], outs[n_g:]
        g_send, g_recv, s_send, s_recv, local_sem = sems
        x, y, c = lax.axis_index("x"), lax.axis_index("y"), lax.axis_index("c")

        def peer(k):
            return (1 - x if k & 4 else x, 1 - y if k & 2 else y, 1 - c if k & 1 else c)

        def number(p):
            return 4 * p[0] + 2 * p[1] + p[2]

        me = number((x, y, c))

        def slot(t, j):
            first, count = self.rows(t)
            if self.s_cols[t]:
                width = self.scatters[t].shape[1] // N_DEV
                return s_in[t].at[pl.ds(first, count), pl.ds(pl.multiple_of(j * width, LANE), width)]
            return s_in[t].at[j, pl.ds(first, count)]

        def place(t, j):
            if self.g_cols[t]:
                width = self.gathers[t].shape[1]
                return g_out[t].at[:, pl.ds(pl.multiple_of(j * width, LANE), width)]
            return g_out[t].at[j]

        def local():
            cps = [pltpu.make_async_copy(g_in[t], place(t, me), local_sem.at[t]) for t in range(n_g)]
            return cps + [pltpu.make_async_copy(slot(t, me), s_out[t].at[me], local_sem.at[n_g + t]) for t in range(n_s)]

        def scatter(t, k):
            p = peer(k)
            return pltpu.make_async_remote_copy(
                src_ref=slot(t, number(p)), dst_ref=s_out[t].at[me], send_sem=s_send.at[t * 7 + k - 1],
                recv_sem=s_recv.at[t * 7 + k - 1], device_id=p, device_id_type=MESH)

        def landed(t, k):
            p = peer(k)
            return pltpu.make_async_remote_copy(
                src_ref=slot(t, me), dst_ref=s_out[t].at[number(p)], send_sem=s_send.at[t * 7 + k - 1],
                recv_sem=s_recv.at[t * 7 + k - 1], device_id=p, device_id_type=MESH)

        def gather(t, pair, block, to, src=None):
            slot = place(t, number(block))
            return pltpu.make_async_remote_copy(
                src_ref=slot if src is None else src, dst_ref=slot, send_sem=g_send.at[t * 7 + pair],
                recv_sem=g_recv.at[t * 7 + pair], device_id=to, device_id_type=MESH)

        def first_sends():
            cps = []
            for t in range(n_g):
                cps.append(gather(t, 0, peer(0), peer(1), src=g_in[t]))
                cps += [gather(t, 1 + j, peer(0), peer(k), src=g_in[t]) for j, k in enumerate(self.OTHER_CHIPS)]
            for t in range(n_s):
                cps += [scatter(t, k) for k in range(1, N_DEV)]
            return cps

        return peer, local, landed, gather, first_sends

    def start(self, ins, outs, sems):
        _, local, _, _, first_sends = self._copies(ins, outs, sems)
        for cp in local() + first_sends():
            cp.start()

    def forward(self, ins, outs, sems):
        peer, _, _, gather, _ = self._copies(ins, outs, sems)
        mine, sibling = peer(0), peer(1)
        for j, k in enumerate(self.OTHER_CHIPS):
            for t in range(self.n_g):
                gather(t, 1 + j, peer(k), mine).wait_recv()
                gather(t, 4 + j, peer(k), sibling).start()

    def finish(self, ins, outs, sems, forwarded=False):
        peer, local, landed, gather, first_sends = self._copies(ins, outs, sems)
        mine, sibling = peer(0), peer(1)
        passed = []
        for j, k in enumerate(self.OTHER_CHIPS):
            for t in range(self.n_g):
                if not forwarded:
                    gather(t, 1 + j, peer(k), mine).wait_recv()
                passed.append(gather(t, 4 + j, peer(k), sibling))
                if not forwarded:
                    passed[-1].start()
        for t in range(self.n_g):
            gather(t, 0, sibling, mine).wait_recv()
            for j, k in enumerate(self.OTHER_CHIPS):
                gather(t, 4 + j, peer(k ^ 1), mine).wait_recv()
        for t in range(self.n_s):
            for k in range(1, N_DEV):
                landed(t, k).wait_recv()
        for cp in first_sends() + passed:
            cp.wait_send()
        for cp in local():
            cp.wait()


def _exchange_alone(name, ex):
    def body(*refs):
        ins, outs, sems = refs[:ex.n], refs[ex.n:2 * ex.n], refs[2 * ex.n:]
        ex.start(ins, outs, sems)
        ex.finish(ins, outs, sems)

    ex.take(pl.pallas_call(body, name=name, in_specs=[ANY] * ex.n, out_specs=[ANY] * ex.n, out_shape=ex.out_shape(),
                           scratch_shapes=ex.scratch())(*ex.operands))


def _call(name, body, args, *, grid, in_specs, out_specs, out_shape, scratch_shapes=(), sem=(), carry=None):
    if carry is None:
        return pl.pallas_call(body, name=name, grid=grid, in_specs=in_specs, out_specs=out_specs, out_shape=out_shape,
                              scratch_shapes=list(scratch_shapes), compiler_params=_params(*sem))(*args)
    ex = carry
    n_in, n_out, n_scr = len(args), len(out_shape), len(scratch_shapes)
    steps = int(np.prod(grid))
    forward_step = int(steps * ex.forward_at) if ex.forward_at else None
    if forward_step is not None and not 0 < forward_step < steps - 1:
        forward_step = None

    def carried(*refs):
        ins, cin = refs[:n_in], refs[n_in:n_in + ex.n]
        at = n_in + ex.n
        outs, cout = refs[at:at + n_out], refs[at + n_out:at + n_out + ex.n]
        at += n_out + ex.n
        scr, sems = refs[at:at + n_scr], refs[at + n_scr:]
        ids = [pl.program_id(a) for a in range(len(grid))]
        first = functools.reduce(jnp.logical_and, [i == 0 for i in ids])
        last = functools.reduce(jnp.logical_and, [i == g - 1 for i, g in zip(ids, grid)])
        step = functools.reduce(lambda acc, ig: acc * ig[1] + ig[0], zip(ids, grid), 0)

        @pl.when(first)
        def _():
            ex.start(cin, cout, sems)

        body(*ins, *outs, *scr)

        if forward_step is not None:
            @pl.when(step == forward_step)
            def _():
                ex.forward(cin, cout, sems)

        @pl.when(last)
        def _():
            ex.finish(cin, cout, sems, forwarded=forward_step is not None)

    res = pl.pallas_call(
        carried, name=name, grid=grid, in_specs=list(in_specs) + [ANY] * ex.n, out_specs=list(out_specs) + [ANY] * ex.n,
        out_shape=list(out_shape) + ex.out_shape(), scratch_shapes=list(scratch_shapes) + ex.scratch(),
        compiler_params=_params(*("arbitrary",) * len(grid)),
    )(*args, *ex.operands)
    ex.take(res[n_out:])
    return res[:n_out]


class W:
    def __init__(self, arr, cols):
        self.arr, self.cols = arr, cols
        self.k, self.n = arr.shape
        self.shard_cols = self.n // N_DEV if cols else self.n


def _matmul(name, grid, operands, specs, dims, tile, extras, outs, out_specs, epilogue, carry=None, sums=0):
    assert grid[2] == 1
    n_ex, n_out = len(extras), len(outs)

    def body(*refs):
        a_ref, b_ref = refs[0], refs[1]
        ex_refs = refs[2:2 + n_ex]
        out_refs = refs[2 + n_ex:2 + n_ex + n_out]
        sum_refs = refs[2 + n_ex + n_out:]
        if sums:
            @pl.when(jnp.logical_and(pl.program_id(0) == 0, pl.program_id(1) == 0))
            def _():
                for r in sum_refs:
                    r[...] = jnp.zeros(r.shape, F32)
        acc = lax.dot_general(a_ref[...].astype(BF16), b_ref[...].astype(BF16), (dims, ((), ())),
                              preferred_element_type=F32)
        res = epilogue(acc, *[r[...] for r in ex_refs]) if epilogue else (acc,) * n_out
        for r, v in zip(out_refs, res[:n_out]):
            r[...] = v.astype(r.dtype)
        for r, v in zip(sum_refs, res[n_out:]):
            r[...] += v

    total = pl.BlockSpec((1, LANE), lambda i, j, c: (0, 0))
    return _call(name, body, list(operands) + [a for a, _ in extras], grid=grid,
                 in_specs=list(specs) + [s for _, s in extras], out_specs=list(out_specs) + [total] * sums,
                 out_shape=list(outs) + [jax.ShapeDtypeStruct((1, LANE), F32)] * sums,
                 sem=("arbitrary",) * 3 if sums else ("parallel", "parallel", "arbitrary"), carry=carry)


def _extra_specs(extras, tm, tn):
    out = []
    for arr, kind in extras:
        if kind == "tile":
            out.append((arr, pl.BlockSpec((tm, tn), lambda i, j, c: (i, j))))
        else:
            out.append((arr, pl.BlockSpec((1, tn), lambda i, j, c: (0, j))))
    return out


def _tile_cols(contraction, streams):
    left = V7X_VMEM_LIMIT - V7X_VMEM_LIMIT // 8 - 2 * 1024 * contraction * 2
    for cols in (1024, 512, 256):
        if 2 * cols * (1024 * 4 * streams + contraction * 2) <= left:
            return cols
    return LANE


def _mm_nn(name, a, w, out_dtypes, epilogue=None, extras=(), carry=None, sums=0):
    m, k = a.shape
    assert k == w.k
    tm = 1024 if a.dtype == BF16 else 512
    tn = _divisor(w.n, _tile_cols(k, len(out_dtypes) + sum(kind == "tile" for _, kind in extras)))
    assert tm * k * a.dtype.itemsize <= 16 * 2 ** 20, (name, tm, k)
    grid = (m // tm, w.n // tn, 1)
    specs = [pl.BlockSpec((tm, k), lambda i, j, c: (i, 0)), pl.BlockSpec((k, tn), lambda i, j, c: (0, j))]
    outs = [jax.ShapeDtypeStruct((m, w.n), d) for d in out_dtypes]
    out_specs = [pl.BlockSpec((tm, tn), lambda i, j, c: (i, j)) for _ in outs]
    return _matmul(name, grid, (a, w.arr), specs, ((1,), (0,)), (tm, tn), _extra_specs(extras, tm, tn), outs, out_specs,
                   epilogue, carry, sums)


def _mm_nt(name, dy, w, out_dtypes, epilogue=None, extras=(), carry=None):
    m, n = dy.shape
    assert n == w.n and dy.dtype == BF16
    tm = 1024
    to = _divisor(w.k, _tile_cols(n, len(out_dtypes) + sum(kind == "tile" for _, kind in extras)))
    assert tm * n * dy.dtype.itemsize <= 16 * 2 ** 20, (name, tm, n)
    grid = (m // tm, w.k // to, 1)
    specs = [pl.BlockSpec((tm, n), lambda i, j, c: (i, 0)), pl.BlockSpec((to, n), lambda i, j, c: (j, 0))]
    outs = [jax.ShapeDtypeStruct((m, w.k), d) for d in out_dtypes]
    out_specs = [pl.BlockSpec((tm, to), lambda i, j, c: (i, j)) for _ in outs]
    return _matmul(name, grid, (dy, w.arr), specs, ((1,), (1,)), (tm, to), _extra_specs(extras, tm, to), outs, out_specs,
                   epilogue, carry)


def _mm_tn(name, a, dy, like, carry=None):
    m, k = a.shape
    n = dy.shape[1]
    assert (k, n) == (like.k, like.n)
    tk = _divisor(k, 1024 if a.dtype == BF16 else 512)
    tn = _divisor(n, 1024)
    grid = (k // tk, n // tn, 1)
    specs = [pl.BlockSpec((m, tk), lambda i, j, c: (0, i)), pl.BlockSpec((m, tn), lambda i, j, c: (0, j))]
    out_specs = [pl.BlockSpec((tk, tn), lambda i, j, c: (i, j))]
    (g,) = _matmul(name, grid, (a, dy), specs, ((0,), (0,)), (tk, tn), [], [jax.ShapeDtypeStruct((k, n), GRAD_DTYPE)],
                   out_specs, None, carry)
    return g if like.cols else g.reshape(N_DEV, k // N_DEV, n)


def _rows(arr, blk=0, width=None):
    return ("rows", arr, blk, width or arr.shape[1])


def _full(arr):
    return ("full", arr)


def _by_residue(arr):
    return ("residue", arr)


def _rowwise(name, fn, ins, outs, reds=(), ts=256, carry=None, scratch=()):
    s = next(i[1].shape[0] if i[0] == "rows" else i[1].shape[0] * i[1].shape[1] for i in ins if i[0] != "full")
    n_in, n_out, n_red = len(ins), len(outs), len(reds)
    in_specs = []
    for i in ins:
        if i[0] == "rows":
            in_specs.append(pl.BlockSpec((ts, i[3]), functools.partial(lambda t, blk: (t, blk), blk=i[2])))
        elif i[0] == "residue":
            d, _, w = i[1].shape
            in_specs.append(pl.BlockSpec((d, ts // d, w), lambda t: (0, t, 0)))
        else:
            in_specs.append(pl.BlockSpec(i[1].shape, functools.partial(lambda t, nd: (0,) * nd, nd=i[1].ndim)))
    out_shape, out_specs = [], []
    for o in outs:
        if len(o) == 2:
            out_shape.append(jax.ShapeDtypeStruct((s, o[0]), o[1]))
            out_specs.append(pl.BlockSpec((ts, o[0]), lambda t: (t, 0)))
        else:
            out_shape.append(jax.ShapeDtypeStruct((o[2], s // o[2], o[0]), o[1]))
            out_specs.append(pl.BlockSpec((o[2], ts // o[2], o[0]), lambda t: (0, t, 0)))
    out_shape += [jax.ShapeDtypeStruct(r, F32) for r in reds]
    out_specs += [pl.BlockSpec(r, lambda t: (0, 0)) for r in reds]

    def body(*refs):
        red_refs = refs[n_in + n_out:n_in + n_out + n_red]
        if red_refs:
            @pl.when(pl.program_id(0) == 0)
            def _():
                for r in red_refs:
                    r[...] = jnp.zeros(r.shape, F32)
        fn(refs[:n_in], refs[n_in:n_in + n_out], red_refs, *refs[n_in + n_out + n_red:])

    return _call(name, body, [i[1] for i in ins], grid=(s // ts,), in_specs=in_specs, out_specs=out_specs,
                 out_shape=out_shape, scratch_shapes=list(scratch), sem=("arbitrary" if reds else "parallel",), carry=carry)


def _ln_stats(x):
    mu = jnp.mean(x, axis=-1, keepdims=True)
    xc = x - mu
    var = jnp.mean(xc * xc, axis=-1, keepdims=True)
    return xc * lax.rsqrt(var + LN_EPS), lax.rsqrt(var + LN_EPS)


def _layer_norm(name, x, g, b, out_dtypes):
    def fn(i, o, r):
        xhat, _ = _ln_stats(i[0][...])
        y = xhat * i[1][...] + i[2][...]
        for ref in o:
            ref[...] = y.astype(ref.dtype)

    return _rowwise(name, fn, [_rows(x), _full(g), _full(b)], [(x.shape[1], d) for d in out_dtypes])


def _ln_bwd_tile(x, g, dy):
    xhat, rstd = _ln_stats(x)
    dyg = dy * g
    m1 = jnp.mean(dyg, axis=-1, keepdims=True)
    m2 = jnp.mean(dyg * xhat, axis=-1, keepdims=True)
    dx = rstd * (dyg - m1 - xhat * m2)
    return dx, jnp.sum(dy * xhat, axis=0, keepdims=True), jnp.sum(dy, axis=0, keepdims=True)


def _layer_norm_bwd(name, x, g, dy, carry=None):
    d = x.shape[1]

    def fn(i, o, r):
        dx, dg, db = _ln_bwd_tile(i[0][...], i[1][...], i[2][...])
        o[0][...] = dx
        o[1][...] = dx.astype(BF16)
        r[0][...] += dg
        r[1][...] += db

    return _rowwise(name, fn, [_rows(x), _full(g), _rows(dy)], [(d, F32), (d, BF16)], [(1, d), (1, d)], carry=carry)


def _shifted_copies(buf, shifted, ts):
    rows = ts + CONV_HALO - SUBLANES
    for s in range(1, SUBLANES):
        shifted[s - 1] = buf[pl.ds(s, rows), :]


def _rows_from(buf, shifted, start):
    s = start % SUBLANES
    if s == 0:
        return buf[pl.ds(start, CONV_ROWS), :]
    return shifted[s - 1, pl.ds(start - s, CONV_ROWS), :]


def _conv_fwd(name, glu, dw, dw_b, ts=512, carry=None):
    s, c = glu.shape
    tc = dw.shape[2]
    per = ts // CONV_HALO
    back = CONV_HALO - (CONV_WIDTH - 1)

    def body(cur_ref, prev_ref, w_ref, b_ref, out_ref, buf):
        i = pl.program_id(1)
        buf[pl.ds(0, CONV_HALO), :] = jnp.where(i > 0, prev_ref[...], 0.0)
        buf[pl.ds(CONV_HALO, ts), :] = cur_ref[...]
        for r0 in range(0, ts, CONV_ROWS):
            acc = jnp.broadcast_to(b_ref[...], (CONV_ROWS, tc))
            for j in range(CONV_WIDTH):
                acc = acc + w_ref[j:j + 1, :] * buf[pl.ds(r0 + back + j, CONV_ROWS), :]
            out_ref[pl.ds(r0, CONV_ROWS), :] = acc

    (out,) = _call(
        name, body, [glu, glu, dw, dw_b], grid=(c // tc, s // ts),
        in_specs=[pl.BlockSpec((ts, tc), lambda j, i: (i, j)),
                  pl.BlockSpec((CONV_HALO, tc), lambda j, i: (jnp.maximum(i * per - 1, 0), j)),
                  pl.BlockSpec((None, CONV_WIDTH, tc), lambda j, i: (j, 0, 0)),
                  pl.BlockSpec((1, tc), lambda j, i: (0, j))],
        out_specs=[pl.BlockSpec((ts, tc), lambda j, i: (i, j))],
        out_shape=[jax.ShapeDtypeStruct((s, c), F32)],
        scratch_shapes=[pltpu.VMEM((ts + CONV_HALO, tc), F32)],
        sem=("parallel", "parallel"), carry=carry)
    return out


def _conv_bwd(name, glu, dc, dw, ts=512, carry=None):
    s, c = glu.shape
    tc = dw.shape[2]
    per = ts // CONV_HALO
    back = CONV_HALO - (CONV_WIDTH - 1)
    last = s // ts - 1

    def body(g_ref, gprev_ref, dc_ref, dcnext_ref, w_ref, dglu_ref, ddw_ref, ddb_ref, gbuf, dbuf, gshift, dshift):
        i = pl.program_id(1)

        @pl.when(i == 0)
        def _():
            ddw_ref[...] = jnp.zeros(ddw_ref.shape, F32)
            ddb_ref[...] = jnp.zeros(ddb_ref.shape, F32)

        gbuf[pl.ds(0, CONV_HALO), :] = jnp.where(i > 0, gprev_ref[...], 0.0)
        gbuf[pl.ds(CONV_HALO, ts), :] = g_ref[...]
        dbuf[pl.ds(0, ts), :] = dc_ref[...]
        dbuf[pl.ds(ts, CONV_HALO), :] = jnp.where(i < last, dcnext_ref[...], 0.0)
        _shifted_copies(gbuf, gshift, ts)
        _shifted_copies(dbuf, dshift, ts)
        taps = [jnp.zeros((1, tc), F32)] * CONV_WIDTH
        for r0 in range(0, ts, CONV_ROWS):
            d_here = dbuf[pl.ds(r0, CONV_ROWS), :]
            acc = jnp.zeros((CONV_ROWS, tc), F32)
            for j in range(CONV_WIDTH):
                acc = acc + w_ref[j:j + 1, :] * _rows_from(dbuf, dshift, r0 + (CONV_WIDTH - 1) - j)
                taps[j] = taps[j] + jnp.sum(d_here * _rows_from(gbuf, gshift, r0 + back + j), axis=0, keepdims=True)
            dglu_ref[pl.ds(r0, CONV_ROWS), :] = acc
        for j in range(CONV_WIDTH):
            ddw_ref[j:j + 1, :] += taps[j]
        ddb_ref[...] += jnp.sum(dc_ref[...], axis=0, keepdims=True)

    return _call(
        name, body, [glu, glu, dc, dc, dw], grid=(c // tc, s // ts),
        in_specs=[pl.BlockSpec((ts, tc), lambda j, i: (i, j)),
                  pl.BlockSpec((CONV_HALO, tc), lambda j, i: (jnp.maximum(i * per - 1, 0), j)),
                  pl.BlockSpec((ts, tc), lambda j, i: (i, j)),
                  pl.BlockSpec((CONV_HALO, tc), lambda j, i: (jnp.minimum((i + 1) * per, (last + 1) * per - 1), j)),
                  pl.BlockSpec((None, CONV_WIDTH, tc), lambda j, i: (j, 0, 0))],
        out_specs=[pl.BlockSpec((ts, tc), lambda j, i: (i, j)),
                   pl.BlockSpec((CONV_WIDTH, tc), lambda j, i: (0, j)),
                   pl.BlockSpec((1, tc), lambda j, i: (0, j))],
        out_shape=[jax.ShapeDtypeStruct((s, c), F32), jax.ShapeDtypeStruct((CONV_WIDTH, c), F32),
                   jax.ShapeDtypeStruct((1, c), F32)],
        scratch_shapes=[pltpu.VMEM((ts + CONV_HALO, tc), F32), pltpu.VMEM((ts + CONV_HALO, tc), F32),
                        pltpu.VMEM((SUBLANES - 1, ts + CONV_HALO - SUBLANES, tc), F32),
                        pltpu.VMEM((SUBLANES - 1, ts + CONV_HALO - SUBLANES, tc), F32)],
        sem=("parallel", "arbitrary"), carry=carry)


def _rope_tables(positions):
    half = HEAD_DIM // 2
    inv = (np.float32(ROPE_THETA) ** (-np.arange(half, dtype=np.float32) * np.float32(2.0 / HEAD_DIM))).astype(np.float32)
    inv_freq = jnp.asarray(np.concatenate([inv, inv])[None, :])
    sign = jnp.asarray(np.concatenate([-np.ones(half, np.float32), np.ones(half, np.float32)])[None, :])

    def fn(i, o, r):
        ang = i[0][...].astype(F32) * i[1][...]
        o[0][...] = jnp.cos(ang)
        o[1][...] = jnp.sin(ang) * i[2][...]

    return _rowwise("rope_tables", fn, [_rows(positions), _full(inv_freq), _full(sign)], [(HEAD_DIM, F32), (HEAD_DIM, F32)], ts=512)


def _rot(x, cos, sin):
    return x * cos + pltpu.roll(x, HEAD_DIM // 2, 1) * sin


def _unrot(x, cos, sin):
    return x * cos - pltpu.roll(x, HEAD_DIM // 2, 1) * sin


def _split_rows(scr, value, d):
    if d == 1:
        return [value]
    scr[...] = value
    return [scr[pl.ds(r, scr.shape[0] // d, stride=d), :] for r in range(d)]


def _join_rows(scr, planes):
    d = len(planes)
    if d == 1:
        return planes[0]
    for r, plane in enumerate(planes):
        scr[pl.ds(r, scr.shape[0] // d, stride=d), :] = plane
    return scr[...]


def _lane(h, shape):
    return lax.broadcasted_iota(jnp.int32, shape, 1) == h


def _attn_specs(width):
    cur = pl.BlockSpec((None, ATTN_BLOCK, width), lambda r, n: (r, n, 0))
    prev = pl.BlockSpec((None, ATTN_BLOCK, width), lambda r, n: (r, jnp.maximum(n - 1, 0), 0))
    return cur, prev


def _masks(n):
    row = lax.broadcasted_iota(jnp.int32, (ATTN_BLOCK, ATTN_BLOCK), 0)
    col = lax.broadcasted_iota(jnp.int32, (ATTN_BLOCK, ATTN_BLOCK), 1)
    return col <= row, jnp.logical_and(col >= row, n > 0)


_NT = (((1,), (1,)), ((), ()))
_TN = (((0,), (0,)), ((), ()))
_NN = (((1,), (0,)), ((), ()))


def _attn_fwd(name, q, k, v, carry=None):
    dil, ln, d = k.shape
    nh = d // HEAD_DIM
    nb = ln // ATTN_BLOCK
    scale = HEAD_DIM ** -0.5

    def body(q_ref, kc_ref, kp_ref, vc_ref, vp_ref, o_ref, l_ref):
        mask_c, mask_p = _masks(pl.program_id(1))
        mask = jnp.concatenate([mask_p, mask_c], axis=1)
        stats = jnp.zeros((ATTN_BLOCK, LANE), F32)
        for h in range(nh):
            hs = slice(h * HEAD_DIM, (h + 1) * HEAD_DIM)
            keys = jnp.concatenate([kp_ref[:, hs], kc_ref[:, hs]], axis=0)
            vals = jnp.concatenate([vp_ref[:, hs], vc_ref[:, hs]], axis=0)
            sc = jnp.where(mask, lax.dot_general(q_ref[:, hs], keys, _NT, preferred_element_type=F32) * scale, NEG)
            m = jnp.max(sc, axis=1, keepdims=True)
            p = jnp.exp(sc - m)
            l = jnp.sum(p, axis=1, keepdims=True)
            o_ref[:, hs] = lax.dot_general(p.astype(BF16), vals, _NN, preferred_element_type=F32) / l
            stats = jnp.where(_lane(h, stats.shape), m + jnp.log(l), stats)
        l_ref[...] = stats

    (cur, prev), (stat, _) = _attn_specs(d), _attn_specs(LANE)
    return _call(name, body, [q, k, k, v, v], grid=(dil, nb), in_specs=[cur, cur, prev, cur, prev], out_specs=[cur, stat],
                 out_shape=[jax.ShapeDtypeStruct((dil, ln, d), F32), jax.ShapeDtypeStruct((dil, ln, LANE), F32)],
                 sem=("parallel", "parallel"), carry=carry)


def _attn_dq(name, q, k, v, do, lse, dsum, carry=None):
    dil, ln, d = k.shape
    nh = d // HEAD_DIM
    nb = ln // ATTN_BLOCK
    scale = HEAD_DIM ** -0.5

    def body(q_ref, kc_ref, kp_ref, vc_ref, vp_ref, do_ref, l_ref, d_ref, dq_ref):
        mask_c, mask_p = _masks(pl.program_id(1))
        mask = jnp.concatenate([mask_p, mask_c], axis=1)
        for h in range(nh):
            hs = slice(h * HEAD_DIM, (h + 1) * HEAD_DIM)
            keys = jnp.concatenate([kp_ref[:, hs], kc_ref[:, hs]], axis=0)
            vals = jnp.concatenate([vp_ref[:, hs], vc_ref[:, hs]], axis=0)
            sc = lax.dot_general(q_ref[:, hs], keys, _NT, preferred_element_type=F32) * scale
            p = jnp.where(mask, jnp.exp(jnp.where(mask, sc, NEG) - l_ref[:, h:h + 1]), 0.0)
            dp = lax.dot_general(do_ref[:, hs], vals, _NT, preferred_element_type=F32)
            ds = p * (dp - d_ref[:, h:h + 1])
            dq_ref[:, hs] = lax.dot_general(ds.astype(BF16), keys, _NN, preferred_element_type=F32) * scale

    (cur, prev), (stat, _) = _attn_specs(d), _attn_specs(LANE)
    (dq,) = _call(name, body, [q, k, k, v, v, do, lse, dsum], grid=(dil, nb),
                  in_specs=[cur, cur, prev, cur, prev, cur, stat, stat], out_specs=[cur],
                  out_shape=[jax.ShapeDtypeStruct((dil, ln, d), F32)], sem=("parallel", "parallel"), carry=carry)
    return dq


def _attn_dkv(name, q, k, v, do, lse, dsum, carry=None):
    dil, ln, d = k.shape
    nh = d // HEAD_DIM
    nb = ln // ATTN_BLOCK
    scale = HEAD_DIM ** -0.5

    def body(k_ref, v_ref, qc_ref, qn_ref, doc_ref, don_ref, lc_ref, lnx_ref, dc_ref, dn_ref, dk_ref, dv_ref):
        n = pl.program_id(1)
        row = lax.broadcasted_iota(jnp.int32, (ATTN_BLOCK, ATTN_BLOCK), 0)
        col = lax.broadcasted_iota(jnp.int32, (ATTN_BLOCK, ATTN_BLOCK), 1)
        mask = jnp.concatenate([row <= col, jnp.logical_and(row >= col, n < nb - 1)], axis=1)
        lse_t = jnp.concatenate([lc_ref[...].T, lnx_ref[...].T], axis=1)
        dsum_t = jnp.concatenate([dc_ref[...].T, dn_ref[...].T], axis=1)
        for h in range(nh):
            hs = slice(h * HEAD_DIM, (h + 1) * HEAD_DIM)
            qs = jnp.concatenate([qc_ref[:, hs], qn_ref[:, hs]], axis=0)
            douts = jnp.concatenate([doc_ref[:, hs], don_ref[:, hs]], axis=0)
            sc = lax.dot_general(k_ref[:, hs], qs, _NT, preferred_element_type=F32) * scale
            p = jnp.where(mask, jnp.exp(jnp.where(mask, sc, NEG) - lse_t[h:h + 1, :]), 0.0)
            dp = lax.dot_general(v_ref[:, hs], douts, _NT, preferred_element_type=F32)
            ds = p * (dp - dsum_t[h:h + 1, :])
            dv_ref[:, hs] = lax.dot_general(p.astype(BF16), douts, _NN, preferred_element_type=F32)
            dk_ref[:, hs] = lax.dot_general(ds.astype(BF16), qs, _NN, preferred_element_type=F32) * scale

    def specs(width):
        cur = pl.BlockSpec((None, ATTN_BLOCK, width), lambda r, n: (r, n, 0))
        nxt = pl.BlockSpec((None, ATTN_BLOCK, width), lambda r, n: (r, jnp.minimum(n + 1, nb - 1), 0))
        return cur, nxt

    (cur, nxt), (stat, stat_next) = specs(d), specs(LANE)
    return _call(name, body, [k, v, q, q, do, do, lse, lse, dsum, dsum], grid=(dil, nb),
                 in_specs=[cur, cur, cur, nxt, cur, nxt, stat, stat_next, stat, stat_next], out_specs=[cur, cur],
                 out_shape=[jax.ShapeDtypeStruct((dil, ln, d), F32)] * 2, sem=("parallel", "parallel"), carry=carry)


def _adamw_tile(w, g, m, v):
    m = ADAM_B1 * m + (1.0 - ADAM_B1) * g
    v = ADAM_B2 * v + (1.0 - ADAM_B2) * (g * g)
    m_hat = m / (1.0 - ADAM_B1 ** ADAM_STEP)
    v_hat = v / (1.0 - ADAM_B2 ** ADAM_STEP)
    delta = -ADAM_LR * (m_hat / (jnp.sqrt(v_hat) + ADAM_EPS) + ADAM_WD * w)
    return delta, m, v


def _adamw_big(name, parts, w, m, v, carry=None):
    layers, r, c = w.shape
    assert len(parts) == layers and all(sum(ch.shape[1] for ch in per_layer) == r for per_layer in parts)
    every = [ch for per_layer in parts for ch in per_layer]
    per_row = 2 * c * (len(every) * N_DEV * every[0].dtype.itemsize + 7 * 4)
    tr = 16
    while tr * 2 <= min(min(ch.shape[1] for ch in every), V7X_VMEM_LIMIT // 2 // per_row) and all(ch.shape[1] % (tr * 2) == 0 for ch in every):
        tr *= 2
    pieces = []
    for ly, per_layer in enumerate(parts):
        at = 0
        for ch in per_layer:
            pieces.append((ly, at, ch.shape[1] // tr, ch))
            at += ch.shape[1] // tr

    def within(layer, i, ly, first, tiles):
        return jnp.logical_and(layer == ly, jnp.logical_and(i >= first, i < first + tiles))

    def body(*refs):
        part_refs = refs[:len(pieces)]
        w_ref, m_ref, v_ref, g_out, d_out, m_out, v_out = refs[len(pieces):]
        layer, i = pl.program_id(0), pl.program_id(1)
        for (ly, first, tiles, _), part_ref in zip(pieces, part_refs):
            @pl.when(within(layer, i, ly, first, tiles))
            def _(part_ref=part_ref):
                g = part_ref[0].astype(F32)
                for dev in range(1, N_DEV):
                    g = g + part_ref[dev].astype(F32)
                delta, mn, vn = _adamw_tile(w_ref[...], g, m_ref[...], v_ref[...])
                g_out[...] = g
                d_out[...] = delta
                m_out[...] = mn
                v_out[...] = vn

    def part_index(layer, i, ly, first, tiles):
        return (0, jnp.where(within(layer, i, ly, first, tiles), i - first, 0), 0)

    own = pl.BlockSpec((None, tr, c), lambda ly, i: (ly, i, 0))
    part_specs = [pl.BlockSpec((N_DEV, tr, c), functools.partial(part_index, ly=ly, first=first, tiles=tiles))
                  for ly, first, tiles, _ in pieces]
    return _call(name, body, [ch for _, _, _, ch in pieces] + [w, m, v], grid=(layers, r // tr), in_specs=part_specs + [own] * 3,
                 out_specs=[own] * 4, out_shape=[jax.ShapeDtypeStruct(w.shape, F32)] * 4, sem=("parallel", "parallel"),
                 carry=carry)


def _sum_slots(name, slots):
    _, r, c = slots.shape

    def body(s_ref, o_ref):
        g = s_ref[0]
        for j in range(1, N_DEV):
            g = g + s_ref[j]
        o_ref[...] = g

    return pl.pallas_call(body, name=name, out_shape=jax.ShapeDtypeStruct((r, c), F32),
                          compiler_params=_params())(slots)


def _adamw_small(name, w, g, m, v):
    def body(w_ref, g_ref, m_ref, v_ref, d_out, m_out, v_out):
        delta, mn, vn = _adamw_tile(w_ref[...], g_ref[...], m_ref[...], v_ref[...])
        d_out[...] = delta
        m_out[...] = mn
        v_out[...] = vn

    return pl.pallas_call(body, name=name, out_shape=[jax.ShapeDtypeStruct(w.shape, F32)] * 3,
                          compiler_params=_params())(w, g, m, v)


class Weights:
    def __init__(self, shards):
        self.shards, self.full, self.parts = shards, {}, {}

    @staticmethod
    def by_columns(key):
        return key.rstrip("01") in COLUMN_SHARDED

    def gather(self, *keys, forward_at=None):
        return Exchange(gathers=[(self.shards[k], self.by_columns(k)) for k in keys], keys=keys, forward_at=forward_at)

    def landed(self, ex):
        for key, full in zip(ex.keys, ex.gathered):
            cols = self.by_columns(key)
            self.full[key] = W(full if cols else full.reshape(-1, full.shape[-1]), cols)

    def scatter(self, grads, gathers=()):
        if not grads and not gathers:
            return None
        return Exchange(gathers=gathers, scatters=[(g, self.by_columns(k), part) for k, (g, part) in grads.items()], keys=list(grads))

    def received(self, ex):
        for key, (first, _, of), part in zip(ex.keys, ex.s_parts, ex.parts) if ex is not None else ():
            self.parts.setdefault(key, {})[first / of] = part

    def chunks(self, key):
        return [self.parts[key][j] for j in sorted(self.parts[key])]

    def __getitem__(self, key):
        return self.full[key]


def _mlp_ple_fwd(tag, z1, p_i, ln1_g, ln1_b, ln2_g, ln2_b, wt, carries, target=None):
    h1, h1b = _layer_norm(f"ln1_{tag}", z1, ln1_g, ln1_b, (F32, BF16))
    up, act = _mm_nn(f"mlp_up_{tag}", h1b, wt["mlp_up" + tag], (F32, BF16),
                     epilogue=lambda acc: (acc, jnp.square(jnp.maximum(acc, 0.0))), carry=carries.get("mlp_up"))
    if "mlp_up" in carries:
        wt.landed(carries["mlp_up"])
    (z2,) = _mm_nn(f"mlp_down_{tag}", act, wt["mlp_down" + tag], (F32,), epilogue=lambda acc, h: (ALPHA * h + acc,),
                   extras=[(h1, "tile")], carry=carries.get("mlp_down"))
    if "mlp_down" in carries:
        wt.landed(carries["mlp_down"])
    h2, h2b = _layer_norm(f"ln2_{tag}", z2, ln2_g, ln2_b, (F32, BF16))
    (pe,) = _mm_nn(f"ple_proj_{tag}", p_i, wt["ple_proj" + tag], (F32,))

    saved = dict(z1=z1, h1b=h1b, up=up, act=act, z2=z2, h2b=h2b, p=p_i)
    if target is None:
        def gate(acc, h, e):
            out = h + e * _sigmoid(acc)
            return acc, out, out

        gp, out, outb = _mm_nn(f"ple_gate_{tag}", h2b, wt["ple_gate" + tag], (F32, F32, BF16), epilogue=gate,
                               extras=[(h2, "tile"), (pe, "tile")], carry=carries.get("ple_gate"))
        if "ple_gate" in carries:
            wt.landed(carries["ple_gate"])
        saved.update(pe=pe, gp=gp)
        return out, outb, saved

    width = z1.shape[1]

    def gate_and_loss(acc, h, e, goal):
        sg = _sigmoid(acc)
        diff = h + e * sg - goal
        d_y = diff * (1.0 / width)
        return d_y, d_y * sg, d_y * e * sg * (1.0 - sg), jnp.broadcast_to(jnp.sum(diff * diff), (1, LANE))

    d_y, d_pe, d_gp, sq = _mm_nn(f"ple_gate_{tag}", h2b, wt["ple_gate" + tag], (F32, BF16, BF16), epilogue=gate_and_loss,
                                 extras=[(h2, "tile"), (pe, "tile"), (target, "tile")], sums=1)
    saved.update(d_pe=d_pe, d_gp=d_gp)
    return d_y, 0.5 * sq[0, 0] / width, saved


WHOLE = (0, 1, 1)


def _mlp_ple_bwd(tag, d_out, sv, ln1_g, ln2_g, wt, plan, waiting):
    made = {}

    def riders(kernel, extra=()):
        items = {name + tag: (made[name], part) for name, part in plan[kernel]}
        return wt.scatter({**items, **dict(extra)})

    d_pe, d_gp = sv["d_pe"], sv["d_gp"]
    made["ple_proj"] = _mm_tn(f"g_ple_proj_{tag}", sv["p"], d_pe, wt["ple_proj" + tag])
    made["ple_gate"] = _mm_tn(f"g_ple_gate_{tag}", sv["h2b"], d_gp, wt["ple_gate" + tag])
    (d_h2,) = _mm_nt(f"d_ple_gate_{tag}", d_gp, wt["ple_gate" + tag], (F32,), epilogue=lambda acc, dy: (dy + acc,),
                     extras=[(d_out, "tile")])
    d_z2, d_z2b, g_ln2_g, g_ln2_b = _layer_norm_bwd(f"ln2_bwd_{tag}", sv["z2"], ln2_g, d_h2)
    ex = riders("g_mlp_down", waiting.items())
    made["mlp_down"] = _mm_tn(f"g_mlp_down_{tag}", sv["act"], d_z2b, wt["mlp_down" + tag], carry=ex)
    wt.received(ex)
    ex = riders("d_mlp_down")
    (d_up,) = _mm_nt(f"d_mlp_down_{tag}", d_z2b, wt["mlp_down" + tag], (BF16,),
                     epilogue=lambda acc, u: (acc * (2.0 * jnp.maximum(u, 0.0)),), extras=[(sv["up"], "tile")], carry=ex)
    wt.received(ex)
    ex = riders("g_mlp_up")
    made["mlp_up"] = _mm_tn(f"g_mlp_up_{tag}", sv["h1b"], d_up, wt["mlp_up" + tag], carry=ex)
    wt.received(ex)
    ex = riders("d_mlp_up")
    (d_h1,) = _mm_nt(f"d_mlp_up_{tag}", d_up, wt["mlp_up" + tag], (F32,), epilogue=lambda acc, dz: (ALPHA * dz + acc,),
                     extras=[(d_z2, "tile")], carry=ex)
    wt.received(ex)
    ex = riders("ln1_bwd")
    d_z1, d_z1b, g_ln1_g, g_ln1_b = _layer_norm_bwd(f"ln1_bwd_{tag}", sv["z1"], ln1_g, d_h1, carry=ex)
    wt.received(ex)
    return d_z1, d_z1b, dict(ln1_g=g_ln1_g, ln1_b=g_ln1_b, ln2_g=g_ln2_g, ln2_b=g_ln2_b), made["mlp_down"], made["mlp_up"]


def _local_step(x, p, positions, target, wt, small):
    s, d = x.shape
    nh = d // HEAD_DIM
    xb, p = x.astype(BF16), p.astype(BF16)

    ex = wt.gather("conv_w_out", "ple_proj0", "ple_gate0")
    (u,) = _mm_nn("conv_in", xb, wt["conv_w_in"], (F32,), epilogue=lambda acc, b: (acc + b,), extras=[(small["conv_b_in"], "row")],
                  carry=ex)
    wt.landed(ex)

    def glu_fn(i, o, r):
        o[0][...] = i[0][...] * _sigmoid(i[1][...])

    (glu,) = _rowwise("glu", glu_fn, [_rows(u, 0, d), _rows(u, 1, d)], [(d, F32)])
    ex = wt.gather("mlp_up0")
    c = _conv_fwd("dwconv", glu, small["conv_dw"], small["conv_dw_b"], carry=ex)
    wt.landed(ex)

    def ln_silu(i, o, r):
        xhat, _ = _ln_stats(i[0][...])
        n = xhat * i[1][...] + i[2][...]
        o[0][...] = (n * _sigmoid(n)).astype(BF16)

    (sb,) = _rowwise("conv_ln_silu", ln_silu, [_rows(c), _full(small["conv_ln_g"]), _full(small["conv_ln_b"])], [(d, BF16)])
    ex = wt.gather("attn_w_o")
    (z1,) = _mm_nn("conv_out", sb, wt["conv_w_out"], (F32,), epilogue=lambda acc, xt: (ALPHA * xt + acc,), extras=[(x, "tile")],
                   carry=ex)
    wt.landed(ex)
    x1, x1b, sv0 = _mlp_ple_fwd("0", z1, p[0], small["ln1_g"][0:1], small["ln1_b"][0:1], small["ln2_g"][0:1], small["ln2_b"][0:1], wt,
                                dict(mlp_up=wt.gather("mlp_down0"), mlp_down=wt.gather("attn_w_q", forward_at=0.9),
                                     ple_gate=wt.gather("w_kv")))

    (kvn,) = _layer_norm("kv_ln", x1, small["kv_ln_g"], small["kv_ln_b"], (BF16,))
    (kv,) = _mm_nn("kv_proj", kvn, wt["w_kv"], (F32,))
    ex = wt.gather("mlp_up1")
    (q,) = _mm_nn("q_proj", x1b, wt["attn_w_q"], (F32,), carry=ex)
    wt.landed(ex)
    cos, sin = _rope_tables(positions)
    row_scratch = [pltpu.VMEM((ROW_TILE, LANE), F32)]

    def rot_kv(i, o, r, scr):
        cs, sn = i[2][...], i[3][...]
        for h in range(nh):
            hs = slice(h * HEAD_DIM, (h + 1) * HEAD_DIM)
            for base, val in ((0, _rot(i[0][:, hs], cs, sn)), (N_GROUPS, i[1][:, hs])):
                for g, dil in enumerate(GROUP_DILATIONS):
                    for res, plane in enumerate(_split_rows(scr, val, dil)):
                        o[base + g][res, :, hs] = plane.astype(BF16)

    by_group = [(d, BF16, dil) for dil in GROUP_DILATIONS]
    kv_groups = _rowwise("rotary_kv", rot_kv, [_rows(kv, 0, d), _rows(kv, 1, d), _rows(cos), _rows(sin)], by_group * 2,
                         ts=ROW_TILE, scratch=row_scratch)
    kg, vg = kv_groups[:N_GROUPS], kv_groups[N_GROUPS:]

    def rot_q(i, o, r, scr):
        cs, sn = i[1][...], i[2][...]
        for g, dil in enumerate(GROUP_DILATIONS):
            for h in range(nh):
                hs = slice(h * HEAD_DIM, (h + 1) * HEAD_DIM)
                val = _rot(i[0][:, g * d + h * HEAD_DIM:g * d + (h + 1) * HEAD_DIM], cs, sn)
                for res, plane in enumerate(_split_rows(scr, val, dil)):
                    o[g][res, :, hs] = plane.astype(BF16)

    qg = _rowwise("rotary_q", rot_q, [_rows(q), _rows(cos), _rows(sin)], by_group, ts=ROW_TILE, scratch=row_scratch)

    og, lg = zip(*[_attn_fwd(f"attn_fwd_{g}", qg[g], kg[g], vg[g]) for g in range(N_GROUPS)])

    def merge(i, o, r, scr):
        lses = []
        for g, dil in enumerate(GROUP_DILATIONS):
            lses.append(_join_rows(scr, [i[N_GROUPS + g][res] for res in range(dil)]))
        top = functools.reduce(jnp.maximum, lses)
        es = [jnp.exp(l - top) for l in lses]
        den = functools.reduce(lambda a, b: a + b, es)
        total = top + jnp.log(den)
        for g, dil in enumerate(GROUP_DILATIONS):
            for res, plane in enumerate(_split_rows(scr, total, dil)):
                o[2 + g][res] = plane
        ws = [e / den for e in es]
        for h in range(nh):
            hs = slice(h * HEAD_DIM, (h + 1) * HEAD_DIM)
            out = jnp.zeros((ROW_TILE, HEAD_DIM), F32)
            for g, dil in enumerate(GROUP_DILATIONS):
                og_h = _join_rows(scr, [i[g][res, :, hs] for res in range(dil)])
                out = out + ws[g][:, h:h + 1] * og_h
            o[0][:, hs] = out
            o[1][:, hs] = out.astype(BF16)

    merged = _rowwise("attn_merge", merge, [_by_residue(t) for t in og + lg],
                      [(d, F32), (d, BF16)] + [(LANE, F32, dil) for dil in GROUP_DILATIONS], ts=ROW_TILE, scratch=row_scratch)
    o, ob, lse_g = merged[0], merged[1], merged[2:]
    (z1b,) = _mm_nn("attn_out", ob, wt["attn_w_o"], (F32,), epilogue=lambda acc, xt: (ALPHA * xt + acc,), extras=[(x1, "tile")])
    d_y, loss, sv1 = _mlp_ple_fwd("1", z1b, p[1], small["ln1_g"][1:2], small["ln1_b"][1:2], small["ln2_g"][1:2], small["ln2_b"][1:2],
                                  wt, dict(mlp_up=wt.gather("mlp_down1"),
                                           mlp_down=wt.gather("ple_proj1", "ple_gate1", forward_at=0.5)), target=target)

    plan = dict(g_mlp_down=[("ple_proj", WHOLE), ("ple_gate", WHOLE)], d_mlp_down=[("mlp_down", (0, 3, 8))],
                g_mlp_up=[("mlp_down", (3, 6, 8))], d_mlp_up=[("mlp_down", (6, 8, 8))], ln1_bwd=[])
    d_z1, d_z1b, g1, _, g_up1 = _mlp_ple_bwd("1", d_y, sv1, small["ln1_g"][1:2], small["ln2_g"][1:2], wt, plan, {})
    g_wo = _mm_tn("g_attn_out", ob, d_z1b, wt["attn_w_o"])
    (d_o,) = _mm_nt("d_attn_out", d_z1b, wt["attn_w_o"], (F32,))

    def dsum_fn(i, o, r, scr):
        stats = jnp.zeros((ROW_TILE, LANE), F32)
        for h in range(nh):
            hs = slice(h * HEAD_DIM, (h + 1) * HEAD_DIM)
            dout = i[0][:, hs]
            stats = jnp.where(_lane(h, stats.shape), jnp.sum(dout * i[1][:, hs], axis=1, keepdims=True), stats)
            for g, dil in enumerate(GROUP_DILATIONS):
                for res, plane in enumerate(_split_rows(scr, dout, dil)):
                    o[g][res, :, hs] = plane.astype(BF16)
        for g, dil in enumerate(GROUP_DILATIONS):
            for res, plane in enumerate(_split_rows(scr, stats, dil)):
                o[N_GROUPS + g][res] = plane

    res_ = _rowwise("attn_dsum", dsum_fn, [_rows(d_o), _rows(o)], by_group + [(LANE, F32, dil) for dil in GROUP_DILATIONS],
                    ts=ROW_TILE, scratch=row_scratch)
    dog, dsum_g = res_[:N_GROUPS], res_[N_GROUPS:]
    dqs, dks, dvs = [], [], []
    riders = [wt.scatter({"mlp_up1": (g_up1, (q, q + 1, 4))}) for q in range(4)] + [wt.scatter({"attn_w_o": (g_wo, WHOLE)}), None]
    for g in range(N_GROUPS):
        dqs.append(_attn_dq(f"attn_dq_{g}", qg[g], kg[g], vg[g], dog[g], lse_g[g], dsum_g[g], carry=riders[2 * g]))
        dk, dv = _attn_dkv(f"attn_dkv_{g}", qg[g], kg[g], vg[g], dog[g], lse_g[g], dsum_g[g], carry=riders[2 * g + 1])
        dks.append(dk)
        dvs.append(dv)
    for ex in riders:
        wt.received(ex)

    def unrot_q(i, o, r, scr):
        cs, sn = i[N_GROUPS][...], i[N_GROUPS + 1][...]
        for g, dil in enumerate(GROUP_DILATIONS):
            for h in range(nh):
                hs = slice(h * HEAD_DIM, (h + 1) * HEAD_DIM)
                dq = _join_rows(scr, [i[g][res, :, hs] for res in range(dil)])
                o[0][:, g * d + h * HEAD_DIM:g * d + (h + 1) * HEAD_DIM] = _unrot(dq, cs, sn).astype(BF16)

    (d_q,) = _rowwise("rotary_q_bwd", unrot_q, [_by_residue(t) for t in dqs] + [_rows(cos), _rows(sin)], [(N_GROUPS * d, BF16)],
                      ts=ROW_TILE, scratch=row_scratch)

    def unrot_kv(i, o, r, scr):
        cs, sn = i[2 * N_GROUPS][...], i[2 * N_GROUPS + 1][...]
        for h in range(nh):
            hs = slice(h * HEAD_DIM, (h + 1) * HEAD_DIM)
            for base in (0, N_GROUPS):
                tot = jnp.zeros((ROW_TILE, HEAD_DIM), F32)
                for g, dil in enumerate(GROUP_DILATIONS):
                    tot = tot + _join_rows(scr, [i[base + g][res, :, hs] for res in range(dil)])
                if base == 0:
                    o[0][:, hs] = _unrot(tot, cs, sn).astype(BF16)
                else:
                    o[0][:, d + h * HEAD_DIM:d + (h + 1) * HEAD_DIM] = tot.astype(BF16)

    (d_kv,) = _rowwise("rotary_kv_bwd", unrot_kv, [_by_residue(t) for t in dks + dvs] + [_rows(cos), _rows(sin)], [(2 * d, BF16)],
                       ts=ROW_TILE, scratch=row_scratch)
    g_wq = _mm_tn("g_q_proj", x1b, d_q, wt["attn_w_q"])
    ex = wt.scatter({"attn_w_q": (g_wq, (0, 3, 8))})
    g_wkv = _mm_tn("g_kv_proj", kvn, d_kv, wt["w_kv"], carry=ex)
    wt.received(ex)
    ex = wt.scatter({"attn_w_q": (g_wq, (3, 6, 8))})
    (d_x1a,) = _mm_nt("d_q_proj", d_q, wt["attn_w_q"], (F32,), epilogue=lambda acc, dz: (ALPHA * dz + acc,), extras=[(d_z1, "tile")],
                      carry=ex)
    wt.received(ex)
    ex = wt.scatter({"attn_w_q": (g_wq, (6, 8, 8)), "w_kv": (g_wkv, (0, 1, 8))})
    (d_kvn,) = _mm_nt("d_kv_proj", d_kv, wt["w_kv"], (F32,), carry=ex)
    wt.received(ex)

    def kv_ln_bwd(i, o, r):
        dx, dg, db = _ln_bwd_tile(i[0][...], i[1][...], i[2][...])
        dx = dx + i[3][...]
        sg = _sigmoid(i[5][...])
        o[0][...] = dx
        o[1][...] = (dx * sg).astype(BF16)
        o[2][...] = (dx * i[4][...] * sg * (1.0 - sg)).astype(BF16)
        r[0][...] += dg
        r[1][...] += db

    ex = wt.scatter({"w_kv": (g_wkv, (1, 4, 8))})
    d_x1, sv0["d_pe"], sv0["d_gp"], g_kv_ln_g, g_kv_ln_b = _rowwise(
        "kv_ln_bwd", kv_ln_bwd, [_rows(x1), _full(small["kv_ln_g"]), _rows(d_kvn), _rows(d_x1a), _rows(sv0["pe"]), _rows(sv0["gp"])],
        [(d, F32), (d, BF16), (d, BF16)], [(1, d), (1, d)], ts=128, carry=ex)
    wt.received(ex)

    plan = dict(g_mlp_down=[("ple_proj", WHOLE)], d_mlp_down=[("ple_gate", WHOLE), ("mlp_down", (0, 1, 8))],
                g_mlp_up=[("mlp_down", (1, 4, 8))], d_mlp_up=[("mlp_down", (4, 7, 8))], ln1_bwd=[("mlp_up", (0, 1, 8))])
    d_z1, d_z1b, g0, g_down0, g_up0 = _mlp_ple_bwd("0", d_x1, sv0, small["ln1_g"][0:1], small["ln2_g"][0:1], wt, plan,
                                                   {"w_kv": (g_wkv, (4, 8, 8))})
    ex = wt.scatter({"mlp_up0": (g_up0, (1, 2, 8))})
    g_wout = _mm_tn("g_conv_out", sb, d_z1b, wt["conv_w_out"], carry=ex)
    wt.received(ex)
    ex = wt.scatter({"mlp_up0": (g_up0, (2, 3, 8))})
    (d_s,) = _mm_nt("d_conv_out", d_z1b, wt["conv_w_out"], (F32,), carry=ex)
    wt.received(ex)

    def ln_silu_bwd(i, o, r):
        cx, gn, bn, ds_ = i[0][...], i[1][...], i[2][...], i[3][...]
        xhat, _ = _ln_stats(cx)
        n = xhat * gn + bn
        sg = _sigmoid(n)
        dn = ds_ * (sg * (1.0 + n * (1.0 - sg)))
        dx, dg, db = _ln_bwd_tile(cx, gn, dn)
        o[0][...] = dx
        r[0][...] += dg
        r[1][...] += db

    ex = wt.scatter({"mlp_up0": (g_up0, (3, 4, 8))})
    d_c, g_cln_g, g_cln_b = _rowwise("conv_ln_silu_bwd", ln_silu_bwd,
                                     [_rows(c), _full(small["conv_ln_g"]), _full(small["conv_ln_b"]), _rows(d_s)],
                                     [(d, F32)], [(1, d), (1, d)], carry=ex)
    wt.received(ex)
    ex = wt.scatter({"mlp_down0": (g_down0, (7, 8, 8)), "mlp_up0": (g_up0, (4, 8, 8))})
    d_glu, g_dw, g_dwb = _conv_bwd("dwconv_bwd", glu, d_c, small["conv_dw"], carry=ex)
    wt.received(ex)

    def glu_bwd(i, o, r):
        a, gt, dg_ = i[0][...], i[1][...], i[2][...]
        sg = _sigmoid(gt)
        da = dg_ * sg
        dgate = dg_ * a * sg * (1.0 - sg)
        o[0][:, 0:d] = da.astype(BF16)
        o[0][:, d:2 * d] = dgate.astype(BF16)
        r[0][:, 0:d] += jnp.sum(da, axis=0, keepdims=True)
        r[0][:, d:2 * d] += jnp.sum(dgate, axis=0, keepdims=True)

    ex = wt.scatter({"conv_w_out": (g_wout, (0, 1, 2))})
    d_u, g_bin = _rowwise("glu_bwd", glu_bwd, [_rows(u, 0, d), _rows(u, 1, d), _rows(d_glu)], [(2 * d, BF16)], [(1, 2 * d)],
                          carry=ex)
    wt.received(ex)
    ex = wt.scatter({"conv_w_out": (g_wout, (1, 2, 2))})
    g_win = _mm_tn("g_conv_in", xb, d_u, wt["conv_w_in"], carry=ex)
    wt.received(ex)
    rows = [g_bin.reshape(2, d), g_dw, g_dwb, g_cln_g, g_cln_b, g_kv_ln_g, g_kv_ln_b]
    rows += [jnp.concatenate([g0[n], g1[n]], axis=0) for n in ("ln1_g", "ln1_b", "ln2_g", "ln2_b")]
    rows, offsets = _stack_rows(rows)
    ex = wt.scatter({"conv_w_in": (g_win, (0, 4, 8))}, gathers=[(rows, False)])
    (grad_x,) = _mm_nt("d_conv_in", d_u, wt["conv_w_in"], (F32,), epilogue=lambda acc, dz: (ALPHA * dz + acc,), extras=[(d_z1, "tile")],
                       carry=ex)
    wt.received(ex)
    return loss, grad_x, ex.gathered[0], offsets, g_win


BIG = ("conv_w_in", "conv_w_out", "w_kv", "attn_w_q", "attn_w_o", "mlp_up", "mlp_down", "ple_proj", "ple_gate")
COLUMN_SHARDED = ("conv_w_in", "w_kv", "attn_w_q", "mlp_up", "ple_proj")
WEIGHTS = ("conv_w_in", "conv_b_in", "conv_dw", "conv_dw_b", "conv_ln_g", "conv_ln_b", "conv_w_out", "kv_ln_g", "kv_ln_b",
           "w_kv", "attn_w_q", "attn_w_o", "ln1_g", "ln1_b", "mlp_up", "mlp_down", "ln2_g", "ln2_b", "ple_proj", "ple_gate")


def kernel(x, p, positions, conv_w_in, conv_b_in, conv_dw, conv_dw_b, conv_ln_g, conv_ln_b, conv_w_out, kv_ln_g, kv_ln_b, w_kv, attn_w_q, attn_w_o, ln1_g, ln1_b, mlp_up, mlp_down, ln2_g, ln2_b, ple_proj, ple_gate, loss_target, m_conv_w_in, m_conv_b_in, m_conv_dw, m_conv_dw_b, m_conv_ln_g, m_conv_ln_b, m_conv_w_out, m_kv_ln_g, m_kv_ln_b, m_w_kv, m_attn_w_q, m_attn_w_o, m_ln1_g, m_ln1_b, m_mlp_up, m_mlp_down, m_ln2_g, m_ln2_b, m_ple_proj, m_ple_gate, v_conv_w_in, v_conv_b_in, v_conv_dw, v_conv_dw_b, v_conv_ln_g, v_conv_ln_b, v_conv_w_out, v_kv_ln_g, v_kv_ln_b, v_w_kv, v_attn_w_q, v_attn_w_o, v_ln1_g, v_ln1_b, v_mlp_up, v_mlp_down, v_ln2_g, v_ln2_b, v_ple_proj, v_ple_gate):
    given = dict(locals())
    wts = {n: given[n] for n in WEIGHTS}
    moms = {n: given["m_" + n] for n in WEIGHTS}
    vels = {n: given["v_" + n] for n in WEIGHTS}
    s, d = x.shape[1], x.shape[2]
    shard = d // N_DEV
    me = 4 * lax.axis_index("x") + 2 * lax.axis_index("y") + lax.axis_index("c")

    def layers_of(a):
        return a.reshape((-1,) + a.shape[-2:])

    shards = {}
    for n in BIG:
        w3 = layers_of(wts[n])
        for ly in range(w3.shape[0]):
            shards[n + str(ly) if w3.shape[0] > 1 else n] = w3[ly].astype(BF16)
    wt = Weights(shards)
    pack, at = _stack_rows([wts["conv_b_in"].reshape(2, shard), wts["conv_dw"].reshape(CONV_WIDTH, shard),
                            wts["conv_dw_b"], wts["conv_ln_g"], wts["conv_ln_b"]])
    ex = Exchange(gathers=[(shards["conv_w_in"], True), (pack, False)], keys=["conv_w_in"])
    _exchange_alone("gather_first", ex)
    wt.landed(ex)
    packed = ex.gathered[1]
    small = dict(conv_b_in=packed[:, at[0]:at[0] + 2].reshape(1, 2 * d), conv_dw=packed[:, at[1]:at[1] + CONV_WIDTH],
                 conv_dw_b=packed[:, at[2]].reshape(1, d), conv_ln_g=packed[:, at[3]].reshape(1, d),
                 conv_ln_b=packed[:, at[4]].reshape(1, d), kv_ln_g=kv_ln_g.reshape(1, d), kv_ln_b=kv_ln_b.reshape(1, d),
                 ln1_g=ln1_g, ln1_b=ln1_b, ln2_g=ln2_g, ln2_b=ln2_b)

    loss, grad_x, all_rows, at, g_win = _local_step(x[0], p[:, 0], positions.reshape(s, 1), loss_target[0], wt, small)
    loss = lax.psum(loss, ("x", "y", "c"))

    riding = dict(mlp_down={"conv_w_in": (g_win, (4, 8, 8))})
    out = {}
    for n in list(riding) + [n for n in BIG if n not in riding]:
        w3 = layers_of(wts[n])
        keys = [n + str(ly) if w3.shape[0] > 1 else n for ly in range(w3.shape[0])]
        ex = wt.scatter(riding.get(n, {}))
        res = _adamw_big("adamw_" + n, [wt.chunks(k) for k in keys], w3, layers_of(moms[n]), layers_of(vels[n]), carry=ex)
        wt.received(ex)
        out[n] = [r.reshape(wts[n].shape) for r in res]
    tot = _sum_slots("sum_small_grads", all_rows)
    mine = lax.dynamic_slice_in_dim(tot, me * shard, shard, axis=1)
    b_in = lax.dynamic_slice_in_dim(tot[at[0]:at[0] + 2].reshape(1, 2 * d), me * 2 * shard, 2 * shard, axis=1)
    g_small = dict(conv_b_in=b_in, conv_dw=mine[at[1]:at[1] + CONV_WIDTH].reshape(conv_dw.shape), conv_dw_b=mine[at[2]:at[2] + 1],
                   conv_ln_g=mine[at[3]:at[3] + 1], conv_ln_b=mine[at[4]:at[4] + 1], kv_ln_g=tot[at[5]], kv_ln_b=tot[at[6]])
    for j, n in enumerate(("ln1_g", "ln1_b", "ln2_g", "ln2_b")):
        g_small[n] = tot[at[7 + j]:at[7 + j] + DEPTH]
    order = [n for n in WEIGHTS if n not in BIG]

    def flat(t):
        return _stack_rows([t[n].reshape(-1, shard) for n in order])

    (w_s, at), (g_s, _), (m_s, _), (v_s, _) = flat(wts), flat(g_small), flat(moms), flat(vels)
    d_s, m_s, v_s = _adamw_small("adamw_small", w_s, g_s, m_s, v_s)
    for n, a in zip(order, at):
        nrow = wts[n].size // shard
        out[n] = [g_small[n].reshape(wts[n].shape)] + [t[a:a + nrow].reshape(wts[n].shape) for t in (d_s, m_s, v_s)]
    return (loss, grad_x[None], *[out[n][0] for n in WEIGHTS], *[out[n][1] for n in WEIGHTS],
            *[out[n][2] for n in WEIGHTS], *[out[n][3] for n in WEIGHTS])
```

```python
import functools

import numpy as np
import jax
import jax.numpy as jnp
from jax import lax
from jax.experimental import pallas as pl
from jax.experimental.pallas import tpu as pltpu

F32, BF16 = jnp.float32, jnp.bfloat16

N_DEV = 8
HEAD_DIM = 128
ATTN_BLOCK = 128
GROUP_DILATIONS = (1, 4, 16)
N_GROUPS = len(GROUP_DILATIONS)
CONV_WIDTH = 31
CONV_HALO = 32
CONV_ROWS = 64
ROPE_THETA = 10000.0
LN_EPS = 1e-5
DEPTH = 2
ALPHA = (2 * DEPTH) ** 0.25
ADAM_LR, ADAM_B1, ADAM_B2, ADAM_EPS, ADAM_WD, ADAM_STEP = 0.001, 0.9, 0.999, 1e-08, 0.01, 10
NEG = -1e30
V7X_VMEM_LIMIT = 56 * 2 ** 20
LANE = 128
SUBLANES = 8
ROW_TILE = 256
GRAD_DTYPE = BF16

MESH = pl.DeviceIdType.MESH
ANY = pl.BlockSpec(memory_space=pl.ANY)


def _params(*sem):
    return pltpu.CompilerParams(dimension_semantics=sem or None, vmem_limit_bytes=V7X_VMEM_LIMIT)


def _sigmoid(x):
    return 1.0 / (1.0 + jnp.exp(-x))


def _divisor(n, most):
    best = None
    for t in range(LANE, min(n, most) + 1, LANE):
        if n % t == 0:
            best = t
    assert best is not None, (n, most)
    return best


def _stack_rows(parts):
    out, offsets, at = [], [], 0
    for a in parts:
        pad = -a.shape[0] % SUBLANES
        offsets.append(at)
        out.append(a)
        if pad:
            out.append(jnp.zeros((pad, a.shape[1]), a.dtype))
        at += a.shape[0] + pad
    return jnp.concatenate(out, axis=0), offsets


class Exchange:
    OTHER_CHIPS = (4, 2, 6)

    def __init__(self, gathers=(), scatters=(), keys=(), forward_at=None):
        self.forward_at = forward_at
        self.gathers, self.g_cols = [a for a, _ in gathers], [c for _, c in gathers]
        self.scatters, self.s_cols, self.s_parts = [s[0] for s in scatters], [s[1] for s in scatters], [s[2] for s in scatters]
        self.keys = list(keys)
        self.n_g, self.n_s = len(self.gathers), len(self.scatters)
        self.n = self.n_g + self.n_s
        self.operands = self.gathers + self.scatters
        self.gathered = self.parts = None

    def rows(self, t):
        a = self.scatters[t]
        first, last, of = self.s_parts[t]
        per = (a.shape[0] if self.s_cols[t] else a.shape[1]) // of
        return first * per, (last - first) * per

    def out_shape(self):
        outs = []
        for a, cols in zip(self.gathers, self.g_cols):
            outs.append(jax.ShapeDtypeStruct((a.shape[0], N_DEV * a.shape[1]) if cols else (N_DEV,) + a.shape, a.dtype))
        for t, (a, cols) in enumerate(zip(self.scatters, self.s_cols)):
            outs.append(jax.ShapeDtypeStruct((N_DEV, self.rows(t)[1], a.shape[1] // N_DEV if cols else a.shape[2]), a.dtype))
        return outs

    def scratch(self):
        dma = pltpu.SemaphoreType.DMA
        return [dma((max(self.n_g, 1) * 7,)), dma((max(self.n_g, 1) * 7,)), dma((max(self.n_s, 1) * 7,)),
                dma((max(self.n_s, 1) * 7,)), dma((self.n,))]

    def take(self, results):
        self.gathered, self.parts = list(results[:self.n_g]), list(results[self.n_g:])

    def _copies(self, ins, outs, sems):
        n_g, n_s = self.n_g, self.n_s
        g_in, s_in, g_out, s_out = ins[:n_g], ins[n_g:], outs[:n_g], outs[n_g:]
        g_send, g_recv, s_send, s_recv, local_sem = sems
        x, y, c = lax.axis_index("x"), lax.axis_index("y"), lax.axis_index("c")

        def peer(k):
            return (1 - x if k & 4 else x, 1 - y if k & 2 else y, 1 - c if k & 1 else c)

        def number(p):
            return 4 * p[0] + 2 * p[1] + p[2]

        me = number((x, y, c))

        def slot(t, j):
            first, count = self.rows(t)
            if self.s_cols[t]:
                width = self.scatters[t].shape[1] // N_DEV
                return s_in[t].at[pl.ds(first, count), pl.ds(pl.multiple_of(j * width, LANE), width)]
            return s_in[t].at[j, pl.ds(first, count)]

        def place(t, j):
            if self.g_cols[t]:
                width = self.gathers[t].shape[1]
                return g_out[t].at[:, pl.ds(pl.multiple_of(j * width, LANE), width)]
            return g_out[t].at[j]

        def local():
            cps = [pltpu.make_async_copy(g_in[t], place(t, me), local_sem.at[t]) for t in range(n_g)]
            return cps + [pltpu.make_async_copy(slot(t, me), s_out[t].at[me], local_sem.at[n_g + t]) for t in range(n_s)]

        def scatter(t, k):
            p = peer(k)
            return pltpu.make_async_remote_copy(
                src_ref=slot(t, number(p)), dst_ref=s_out[t].at[me], send_sem=s_send.at[t * 7 + k - 1],
                recv_sem=s_recv.at[t * 7 + k - 1], device_id=p, device_id_type=MESH)

        def landed(t, k):
            p = peer(k)
            return pltpu.make_async_remote_copy(
                src_ref=slot(t, me), dst_ref=s_out[t].at[number(p)], send_sem=s_send.at[t * 7 + k - 1],
                recv_sem=s_recv.at[t * 7 + k - 1], device_id=p, device_id_type=MESH)

        def gather(t, pair, block, to, src=None):
            slot = place(t, number(block))
            return pltpu.make_async_remote_copy(
                src_ref=slot if src is None else src, dst_ref=slot, send_sem=g_send.at[t * 7 + pair],
                recv_sem=g_recv.at[t * 7 + pair], device_id=to, device_id_type=MESH)

        def first_sends():
            cps = []
            for t in range(n_g):
                cps.append(gather(t, 0, peer(0), peer(1), src=g_in[t]))
                cps += [gather(t, 1 + j, peer(0), peer(k), src=g_in[t]) for j, k in enumerate(self.OTHER_CHIPS)]
            for t in range(n_s):
                cps += [scatter(t, k) for k in range(1, N_DEV)]
            return cps

        return peer, local, landed, gather, first_sends

    def start(self, ins, outs, sems):
        _, local, _, _, first_sends = self._copies(ins, outs, sems)
        for cp in local() + first_sends():
            cp.start()

    def forward(self, ins, outs, sems):
        peer, _, _, gather, _ = self._copies(ins, outs, sems)
        mine, sibling = peer(0), peer(1)
        for j, k in enumerate(self.OTHER_CHIPS):
            for t in range(self.n_g):
                gather(t, 1 + j, peer(k), mine).wait_recv()
                gather(t, 4 + j, peer(k), sibling).start()

    def finish(self, ins, outs, sems, forwarded=False):
        peer, local, landed, gather, first_sends = self._copies(ins, outs, sems)
        mine, sibling = peer(0), peer(1)
        passed = []
        for j, k in enumerate(self.OTHER_CHIPS):
            for t in range(self.n_g):
                if not forwarded:
                    gather(t, 1 + j, peer(k), mine).wait_recv()
                passed.append(gather(t, 4 + j, peer(k), sibling))
                if not forwarded:
                    passed[-1].start()
        for t in range(self.n_g):
            gather(t, 0, sibling, mine).wait_recv()
            for j, k in enumerate(self.OTHER_CHIPS):
                gather(t, 4 + j, peer(k ^ 1), mine).wait_recv()
        for t in range(self.n_s):
            for k in range(1, N_DEV):
                landed(t, k).wait_recv()
        for cp in first_sends() + passed:
            cp.wait_send()
        for cp in local():
            cp.wait()


def _exchange_alone(name, ex):
    def body(*refs):
        ins, outs, sems = refs[:ex.n], refs[ex.n:2 * ex.n], refs[2 * ex.n:]
        ex.start(ins, outs, sems)
        ex.finish(ins, outs, sems)

    ex.take(pl.pallas_call(body, name=name, in_specs=[ANY] * ex.n, out_specs=[ANY] * ex.n, out_shape=ex.out_shape(),
                           scratch_shapes=ex.scratch())(*ex.operands))


def _call(name, body, args, *, grid, in_specs, out_specs, out_shape, scratch_shapes=(), sem=(), carry=None):
    if carry is None:
        return pl.pallas_call(body, name=name, grid=grid, in_specs=in_specs, out_specs=out_specs, out_shape=out_shape,
                              scratch_shapes=list(scratch_shapes), compiler_params=_params(*sem))(*args)
    ex = carry
    n_in, n_out, n_scr = len(args), len(out_shape), len(scratch_shapes)
    steps = int(np.prod(grid))
    forward_step = int(steps * ex.forward_at) if ex.forward_at else None
    if forward_step is not None and not 0 < forward_step < steps - 1:
        forward_step = None

    def carried(*refs):
        ins, cin = refs[:n_in], refs[n_in:n_in + ex.n]
        at = n_in + ex.n
        outs, cout = refs[at:at + n_out], refs[at + n_out:at + n_out + ex.n]
        at += n_out + ex.n
        scr, sems = refs[at:at + n_scr], refs[at + n_scr:]
        ids = [pl.program_id(a) for a in range(len(grid))]
        first = functools.reduce(jnp.logical_and, [i == 0 for i in ids])
        last = functools.reduce(jnp.logical_and, [i == g - 1 for i, g in zip(ids, grid)])
        step = functools.reduce(lambda acc, ig: acc * ig[1] + ig[0], zip(ids, grid), 0)

        @pl.when(first)
        def _():
            ex.start(cin, cout, sems)

        body(*ins, *outs, *scr)

        if forward_step is not None:
            @pl.when(step == forward_step)
            def _():
                ex.forward(cin, cout, sems)

        @pl.when(last)
        def _():
            ex.finish(cin, cout, sems, forwarded=forward_step is not None)

    res = pl.pallas_call(
        carried, name=name, grid=grid, in_specs=list(in_specs) + [ANY] * ex.n, out_specs=list(out_specs) + [ANY] * ex.n,
        out_shape=list(out_shape) + ex.out_shape(), scratch_shapes=list(scratch_shapes) + ex.scratch(),
        compiler_params=_params(*("arbitrary",) * len(grid)),
    )(*args, *ex.operands)
    ex.take(res[n_out:])
    return res[:n_out]


class W:
    def __init__(self, arr, cols):
        self.arr, self.cols = arr, cols
        self.k, self.n = arr.shape
        self.shard_cols = self.n // N_DEV if cols else self.n


def _matmul(name, grid, operands, specs, dims, tile, extras, outs, out_specs, epilogue, carry=None, sums=0):
    assert grid[2] == 1
    n_ex, n_out = len(extras), len(outs)

    def body(*refs):
        a_ref, b_ref = refs[0], refs[1]
        ex_refs = refs[2:2 + n_ex]
        out_refs = refs[2 + n_ex:2 + n_ex + n_out]
        sum_refs = refs[2 + n_ex + n_out:]
        if sums:
            @pl.when(jnp.logical_and(pl.program_id(0) == 0, pl.program_id(1) == 0))
            def _():
                for r in sum_refs:
                    r[...] = jnp.zeros(r.shape, F32)
        acc = lax.dot_general(a_ref[...].astype(BF16), b_ref[...].astype(BF16), (dims, ((), ())),
                              preferred_element_type=F32)
        res = epilogue(acc, *[r[...] for r in ex_refs]) if epilogue else (acc,) * n_out
        for r, v in zip(out_refs, res[:n_out]):
            r[...] = v.astype(r.dtype)
        for r, v in zip(sum_refs, res[n_out:]):
            r[...] += v

    total = pl.BlockSpec((1, LANE), lambda i, j, c: (0, 0))
    return _call(name, body, list(operands) + [a for a, _ in extras], grid=grid,
                 in_specs=list(specs) + [s for _, s in extras], out_specs=list(out_specs) + [total] * sums,
                 out_shape=list(outs) + [jax.ShapeDtypeStruct((1, LANE), F32)] * sums,
                 sem=("arbitrary",) * 3 if sums else ("parallel", "parallel", "arbitrary"), carry=carry)


def _extra_specs(extras, tm, tn):
    out = []
    for arr, kind in extras:
        if kind == "tile":
            out.append((arr, pl.BlockSpec((tm, tn), lambda i, j, c: (i, j))))
        else:
            out.append((arr, pl.BlockSpec((1, tn), lambda i, j, c: (0, j))))
    return out


def _tile_cols(contraction, streams):
    left = V7X_VMEM_LIMIT - V7X_VMEM_LIMIT // 8 - 2 * 1024 * contraction * 2
    for cols in (1024, 512, 256):
        if 2 * cols * (1024 * 4 * streams + contraction * 2) <= left:
            return cols
    return LANE


def _mm_nn(name, a, w, out_dtypes, epilogue=None, extras=(), carry=None, sums=0):
    m, k = a.shape
    assert k == w.k
    tm = 1024 if a.dtype == BF16 else 512
    tn = _divisor(w.n, _tile_cols(k, len(out_dtypes) + sum(kind == "tile" for _, kind in extras)))
    assert tm * k * a.dtype.itemsize <= 16 * 2 ** 20, (name, tm, k)
    grid = (m // tm, w.n // tn, 1)
    specs = [pl.BlockSpec((tm, k), lambda i, j, c: (i, 0)), pl.BlockSpec((k, tn), lambda i, j, c: (0, j))]
    outs = [jax.ShapeDtypeStruct((m, w.n), d) for d in out_dtypes]
    out_specs = [pl.BlockSpec((tm, tn), lambda i, j, c: (i, j)) for _ in outs]
    return _matmul(name, grid, (a, w.arr), specs, ((1,), (0,)), (tm, tn), _extra_specs(extras, tm, tn), outs, out_specs,
                   epilogue, carry, sums)


def _mm_nt(name, dy, w, out_dtypes, epilogue=None, extras=(), carry=None):
    m, n = dy.shape
    assert n == w.n and dy.dtype == BF16
    tm = 1024
    to = _divisor(w.k, _tile_cols(n, len(out_dtypes) + sum(kind == "tile" for _, kind in extras)))
    assert tm * n * dy.dtype.itemsize <= 16 * 2 ** 20, (name, tm, n)
    grid = (m // tm, w.k // to, 1)
    specs = [pl.BlockSpec((tm, n), lambda i, j, c: (i, 0)), pl.BlockSpec((to, n), lambda i, j, c: (j, 0))]
    outs = [jax.ShapeDtypeStruct((m, w.k), d) for d in out_dtypes]
    out_specs = [pl.BlockSpec((tm, to), lambda i, j, c: (i, j)) for _ in outs]
    return _matmul(name, grid, (dy, w.arr), specs, ((1,), (1,)), (tm, to), _extra_specs(extras, tm, to), outs, out_specs,
                   epilogue, carry)


def _mm_tn(name, a, dy, like, carry=None):
    m, k = a.shape
    n = dy.shape[1]
    assert (k, n) == (like.k, like.n)
    tk = _divisor(k, 1024 if a.dtype == BF16 else 512)
    tn = _divisor(n, 1024)
    grid = (k // tk, n // tn, 1)
    specs = [pl.BlockSpec((m, tk), lambda i, j, c: (0, i)), pl.BlockSpec((m, tn), lambda i, j, c: (0, j))]
    out_specs = [pl.BlockSpec((tk, tn), lambda i, j, c: (i, j))]
    (g,) = _matmul(name, grid, (a, dy), specs, ((0,), (0,)), (tk, tn), [], [jax.ShapeDtypeStruct((k, n), GRAD_DTYPE)],
                   out_specs, None, carry)
    return g if like.cols else g.reshape(N_DEV, k // N_DEV, n)


def _rows(arr, blk=0, width=None):
    return ("rows", arr, blk, width or arr.shape[1])


def _full(arr):
    return ("full", arr)


def _by_residue(arr):
    return ("residue", arr)


def _rowwise(name, fn, ins, outs, reds=(), ts=256, carry=None, scratch=()):
    s = next(i[1].shape[0] if i[0] == "rows" else i[1].shape[0] * i[1].shape[1] for i in ins if i[0] != "full")
    n_in, n_out, n_red = len(ins), len(outs), len(reds)
    in_specs = []
    for i in ins:
        if i[0] == "rows":
            in_specs.append(pl.BlockSpec((ts, i[3]), functools.partial(lambda t, blk: (t, blk), blk=i[2])))
        elif i[0] == "residue":
            d, _, w = i[1].shape
            in_specs.append(pl.BlockSpec((d, ts // d, w), lambda t: (0, t, 0)))
        else:
            in_specs.append(pl.BlockSpec(i[1].shape, functools.partial(lambda t, nd: (0,) * nd, nd=i[1].ndim)))
    out_shape, out_specs = [], []
    for o in outs:
        if len(o) == 2:
            out_shape.append(jax.ShapeDtypeStruct((s, o[0]), o[1]))
            out_specs.append(pl.BlockSpec((ts, o[0]), lambda t: (t, 0)))
        else:
            out_shape.append(jax.ShapeDtypeStruct((o[2], s // o[2], o[0]), o[1]))
            out_specs.append(pl.BlockSpec((o[2], ts // o[2], o[0]), lambda t: (0, t, 0)))
    out_shape += [jax.ShapeDtypeStruct(r, F32) for r in reds]
    out_specs += [pl.BlockSpec(r, lambda t: (0, 0)) for r in reds]

    def body(*refs):
        red_refs = refs[n_in + n_out:n_in + n_out + n_red]
        if red_refs:
            @pl.when(pl.program_id(0) == 0)
            def _():
                for r in red_refs:
                    r[...] = jnp.zeros(r.shape, F32)
        fn(refs[:n_in], refs[n_in:n_in + n_out], red_refs, *refs[n_in + n_out + n_red:])

    return _call(name, body, [i[1] for i in ins], grid=(s // ts,), in_specs=in_specs, out_specs=out_specs,
                 out_shape=out_shape, scratch_shapes=list(scratch), sem=("arbitrary" if reds else "parallel",), carry=carry)


def _ln_stats(x):
    mu = jnp.mean(x, axis=-1, keepdims=True)
    xc = x - mu
    var = jnp.mean(xc * xc, axis=-1, keepdims=True)
    return xc * lax.rsqrt(var + LN_EPS), lax.rsqrt(var + LN_EPS)


def _layer_norm(name, x, g, b, out_dtypes):
    def fn(i, o, r):
        xhat, _ = _ln_stats(i[0][...])
        y = xhat * i[1][...] + i[2][...]
        for ref in o:
            ref[...] = y.astype(ref.dtype)

    return _rowwise(name, fn, [_rows(x), _full(g), _full(b)], [(x.shape[1], d) for d in out_dtypes])


def _ln_bwd_tile(x, g, dy):
    xhat, rstd = _ln_stats(x)
    dyg = dy * g
    m1 = jnp.mean(dyg, axis=-1, keepdims=True)
    m2 = jnp.mean(dyg * xhat, axis=-1, keepdims=True)
    dx = rstd * (dyg - m1 - xhat * m2)
    return dx, jnp.sum(dy * xhat, axis=0, keepdims=True), jnp.sum(dy, axis=0, keepdims=True)


def _layer_norm_bwd(name, x, g, dy, carry=None):
    d = x.shape[1]

    def fn(i, o, r):
        dx, dg, db = _ln_bwd_tile(i[0][...], i[1][...], i[2][...])
        o[0][...] = dx
        o[1][...] = dx.astype(BF16)
        r[0][...] += dg
        r[1][...] += db

    return _rowwise(name, fn, [_rows(x), _full(g), _rows(dy)], [(d, F32), (d, BF16)], [(1, d), (1, d)], carry=carry)


def _shifted_copies(buf, shifted, ts):
    rows = ts + CONV_HALO - SUBLANES
    for s in range(1, SUBLANES):
        shifted[s - 1] = buf[pl.ds(s, rows), :]


def _rows_from(buf, shifted, start):
    s = start % SUBLANES
    if s == 0:
        return buf[pl.ds(start, CONV_ROWS), :]
    return shifted[s - 1, pl.ds(start - s, CONV_ROWS), :]


def _conv_fwd(name, glu, dw, dw_b, ts=512, carry=None):
    s, c = glu.shape
    tc = dw.shape[2]
    per = ts // CONV_HALO
    back = CONV_HALO - (CONV_WIDTH - 1)

    def body(cur_ref, prev_ref, w_ref, b_ref, out_ref, buf):
        i = pl.program_id(1)
        buf[pl.ds(0, CONV_HALO), :] = jnp.where(i > 0, prev_ref[...], 0.0)
        buf[pl.ds(CONV_HALO, ts), :] = cur_ref[...]
        for r0 in range(0, ts, CONV_ROWS):
            acc = jnp.broadcast_to(b_ref[...], (CONV_ROWS, tc))
            for j in range(CONV_WIDTH):
                acc = acc + w_ref[j:j + 1, :] * buf[pl.ds(r0 + back + j, CONV_ROWS), :]
            out_ref[pl.ds(r0, CONV_ROWS), :] = acc

    (out,) = _call(
        name, body, [glu, glu, dw, dw_b], grid=(c // tc, s // ts),
        in_specs=[pl.BlockSpec((ts, tc), lambda j, i: (i, j)),
                  pl.BlockSpec((CONV_HALO, tc), lambda j, i: (jnp.maximum(i * per - 1, 0), j)),
                  pl.BlockSpec((None, CONV_WIDTH, tc), lambda j, i: (j, 0, 0)),
                  pl.BlockSpec((1, tc), lambda j, i: (0, j))],
        out_specs=[pl.BlockSpec((ts, tc), lambda j, i: (i, j))],
        out_shape=[jax.ShapeDtypeStruct((s, c), F32)],
        scratch_shapes=[pltpu.VMEM((ts + CONV_HALO, tc), F32)],
        sem=("parallel", "parallel"), carry=carry)
    return out


def _conv_bwd(name, glu, dc, dw, ts=512, carry=None):
    s, c = glu.shape
    tc = dw.shape[2]
    per = ts // CONV_HALO
    back = CONV_HALO - (CONV_WIDTH - 1)
    last = s // ts - 1

    def body(g_ref, gprev_ref, dc_ref, dcnext_ref, w_ref, dglu_ref, ddw_ref, ddb_ref, gbuf, dbuf, gshift, dshift):
        i = pl.program_id(1)

        @pl.when(i == 0)
        def _():
            ddw_ref[...] = jnp.zeros(ddw_ref.shape, F32)
            ddb_ref[...] = jnp.zeros(ddb_ref.shape, F32)

        gbuf[pl.ds(0, CONV_HALO), :] = jnp.where(i > 0, gprev_ref[...], 0.0)
        gbuf[pl.ds(CONV_HALO, ts), :] = g_ref[...]
        dbuf[pl.ds(0, ts), :] = dc_ref[...]
        dbuf[pl.ds(ts, CONV_HALO), :] = jnp.where(i < last, dcnext_ref[...], 0.0)
        _shifted_copies(gbuf, gshift, ts)
        _shifted_copies(dbuf, dshift, ts)
        taps = [jnp.zeros((SUBLANES, tc), F32)] * CONV_WIDTH
        for r0 in range(0, ts, CONV_ROWS):
            d_here = dbuf[pl.ds(r0, CONV_ROWS), :]
            acc = jnp.zeros((CONV_ROWS, tc), F32)
            for j in range(CONV_WIDTH):
                acc = acc + w_ref[j:j + 1, :] * _rows_from(dbuf, dshift, r0 + (CONV_WIDTH - 1) - j)
                prod = d_here * _rows_from(gbuf, gshift, r0 + back + j)
                taps[j] = taps[j] + jnp.sum(prod.reshape(CONV_ROWS // SUBLANES, SUBLANES, tc), axis=0)
            dglu_ref[pl.ds(r0, CONV_ROWS), :] = acc
        for j in range(CONV_WIDTH):
            ddw_ref[j:j + 1, :] += jnp.sum(taps[j], axis=0, keepdims=True)
        ddb_ref[...] += jnp.sum(dc_ref[...], axis=0, keepdims=True)

    return _call(
        name, body, [glu, glu, dc, dc, dw], grid=(c // tc, s // ts),
        in_specs=[pl.BlockSpec((ts, tc), lambda j, i: (i, j)),
                  pl.BlockSpec((CONV_HALO, tc), lambda j, i: (jnp.maximum(i * per - 1, 0), j)),
                  pl.BlockSpec((ts, tc), lambda j, i: (i, j)),
                  pl.BlockSpec((CONV_HALO, tc), lambda j, i: (jnp.minimum((i + 1) * per, (last + 1) * per - 1), j)),
                  pl.BlockSpec((None, CONV_WIDTH, tc), lambda j, i: (j, 0, 0))],
        out_specs=[pl.BlockSpec((ts, tc), lambda j, i: (i, j)),
                   pl.BlockSpec((CONV_WIDTH, tc), lambda j, i: (0, j)),
                   pl.BlockSpec((1, tc), lambda j, i: (0, j))],
        out_shape=[jax.ShapeDtypeStruct((s, c), F32), jax.ShapeDtypeStruct((CONV_WIDTH, c), F32),
                   jax.ShapeDtypeStruct((1, c), F32)],
        scratch_shapes=[pltpu.VMEM((ts + CONV_HALO, tc), F32), pltpu.VMEM((ts + CONV_HALO, tc), F32),
                        pltpu.VMEM((SUBLANES - 1, ts + CONV_HALO - SUBLANES, tc), F32),
                        pltpu.VMEM((SUBLANES - 1, ts + CONV_HALO - SUBLANES, tc), F32)],
        sem=("parallel", "arbitrary"), carry=carry)


def _rope_tables(positions):
    half = HEAD_DIM // 2
    inv = (np.float32(ROPE_THETA) ** (-np.arange(half, dtype=np.float32) * np.float32(2.0 / HEAD_DIM))).astype(np.float32)
    inv_freq = jnp.asarray(np.concatenate([inv, inv])[None, :])
    sign = jnp.asarray(np.concatenate([-np.ones(half, np.float32), np.ones(half, np.float32)])[None, :])

    def fn(i, o, r):
        ang = i[0][...].astype(F32) * i[1][...]
        o[0][...] = jnp.cos(ang)
        o[1][...] = jnp.sin(ang) * i[2][...]

    return _rowwise("rope_tables", fn, [_rows(positions), _full(inv_freq), _full(sign)], [(HEAD_DIM, F32), (HEAD_DIM, F32)], ts=512)


def _rot(x, cos, sin):
    return x * cos + pltpu.roll(x, HEAD_DIM // 2, 1) * sin


def _unrot(x, cos, sin):
    return x * cos - pltpu.roll(x, HEAD_DIM // 2, 1) * sin


def _split_rows(scr, value, d):
    if d == 1:
        return [value]
    scr[...] = value
    return [scr[pl.ds(r, scr.shape[0] // d, stride=d), :] for r in range(d)]


def _join_rows(scr, planes):
    d = len(planes)
    if d == 1:
        return planes[0]
    for r, plane in enumerate(planes):
        scr[pl.ds(r, scr.shape[0] // d, stride=d), :] = plane
    return scr[...]


def _lane(h, shape):
    return lax.broadcasted_iota(jnp.int32, shape, 1) == h


def _attn_specs(width):
    cur = pl.BlockSpec((None, ATTN_BLOCK, width), lambda r, n: (r, n, 0))
    prev = pl.BlockSpec((None, ATTN_BLOCK, width), lambda r, n: (r, jnp.maximum(n - 1, 0), 0))
    return cur, prev


def _masks(n):
    row = lax.broadcasted_iota(jnp.int32, (ATTN_BLOCK, ATTN_BLOCK), 0)
    col = lax.broadcasted_iota(jnp.int32, (ATTN_BLOCK, ATTN_BLOCK), 1)
    return col <= row, jnp.logical_and(col >= row, n > 0)


_NT = (((1,), (1,)), ((), ()))
_TN = (((0,), (0,)), ((), ()))
_NN = (((1,), (0,)), ((), ()))


def _attn_fwd(name, q, k, v, carry=None):
    dil, ln, d = k.shape
    nh = d // HEAD_DIM
    nb = ln // ATTN_BLOCK
    scale = HEAD_DIM ** -0.5

    def body(q_ref, kc_ref, kp_ref, vc_ref, vp_ref, o_ref, l_ref):
        mask_c, mask_p = _masks(pl.program_id(1))
        mask = jnp.concatenate([mask_p, mask_c], axis=1)
        stats = jnp.zeros((ATTN_BLOCK, LANE), F32)
        for h in range(nh):
            hs = slice(h * HEAD_DIM, (h + 1) * HEAD_DIM)
            keys = jnp.concatenate([kp_ref[:, hs], kc_ref[:, hs]], axis=0)
            vals = jnp.concatenate([vp_ref[:, hs], vc_ref[:, hs]], axis=0)
            sc = jnp.where(mask, lax.dot_general(q_ref[:, hs], keys, _NT, preferred_element_type=F32) * scale, NEG)
            m = jnp.max(sc, axis=1, keepdims=True)
            p = jnp.exp(sc - m)
            l = jnp.sum(p, axis=1, keepdims=True)
            o_ref[:, hs] = lax.dot_general(p.astype(BF16), vals, _NN, preferred_element_type=F32) / l
            stats = jnp.where(_lane(h, stats.shape), m + jnp.log(l), stats)
        l_ref[...] = stats

    (cur, prev), (stat, _) = _attn_specs(d), _attn_specs(LANE)
    return _call(name, body, [q, k, k, v, v], grid=(dil, nb), in_specs=[cur, cur, prev, cur, prev], out_specs=[cur, stat],
                 out_shape=[jax.ShapeDtypeStruct((dil, ln, d), F32), jax.ShapeDtypeStruct((dil, ln, LANE), F32)],
                 sem=("parallel", "parallel"), carry=carry)


def _attn_dq(name, q, k, v, do, lse, dsum, carry=None):
    dil, ln, d = k.shape
    nh = d // HEAD_DIM
    nb = ln // ATTN_BLOCK
    scale = HEAD_DIM ** -0.5

    def body(q_ref, kc_ref, kp_ref, vc_ref, vp_ref, do_ref, l_ref, d_ref, dq_ref):
        mask_c, mask_p = _masks(pl.program_id(1))
        mask = jnp.concatenate([mask_p, mask_c], axis=1)
        for h in range(nh):
            hs = slice(h * HEAD_DIM, (h + 1) * HEAD_DIM)
            keys = jnp.concatenate([kp_ref[:, hs], kc_ref[:, hs]], axis=0)
            vals = jnp.concatenate([vp_ref[:, hs], vc_ref[:, hs]], axis=0)
            sc = lax.dot_general(q_ref[:, hs], keys, _NT, preferred_element_type=F32) * scale
            p = jnp.where(mask, jnp.exp(jnp.where(mask, sc, NEG) - l_ref[:, h:h + 1]), 0.0)
            dp = lax.dot_general(do_ref[:, hs], vals, _NT, preferred_element_type=F32)
            ds = p * (dp - d_ref[:, h:h + 1])
            dq_ref[:, hs] = lax.dot_general(ds.astype(BF16), keys, _NN, preferred_element_type=F32) * scale

    (cur, prev), (stat, _) = _attn_specs(d), _attn_specs(LANE)
    (dq,) = _call(name, body, [q, k, k, v, v, do, lse, dsum], grid=(dil, nb),
                  in_specs=[cur, cur, prev, cur, prev, cur, stat, stat], out_specs=[cur],
                  out_shape=[jax.ShapeDtypeStruct((dil, ln, d), F32)], sem=("parallel", "parallel"), carry=carry)
    return dq


def _attn_dkv(name, q, k, v, do, lse, dsum, carry=None):
    dil, ln, d = k.shape
    nh = d // HEAD_DIM
    nb = ln // ATTN_BLOCK
    scale = HEAD_DIM ** -0.5

    def body(k_ref, v_ref, qc_ref, qn_ref, doc_ref, don_ref, lc_ref, lnx_ref, dc_ref, dn_ref, dk_ref, dv_ref):
        n = pl.program_id(1)
        row = lax.broadcasted_iota(jnp.int32, (ATTN_BLOCK, ATTN_BLOCK), 0)
        col = lax.broadcasted_iota(jnp.int32, (ATTN_BLOCK, ATTN_BLOCK), 1)
        mask = jnp.concatenate([row <= col, jnp.logical_and(row >= col, n < nb - 1)], axis=1)
        lse_t = jnp.concatenate([lc_ref[...].T, lnx_ref[...].T], axis=1)
        dsum_t = jnp.concatenate([dc_ref[...].T, dn_ref[...].T], axis=1)
        for h in range(nh):
            hs = slice(h * HEAD_DIM, (h + 1) * HEAD_DIM)
            qs = jnp.concatenate([qc_ref[:, hs], qn_ref[:, hs]], axis=0)
            douts = jnp.concatenate([doc_ref[:, hs], don_ref[:, hs]], axis=0)
            sc = lax.dot_general(k_ref[:, hs], qs, _NT, preferred_element_type=F32) * scale
            p = jnp.where(mask, jnp.exp(jnp.where(mask, sc, NEG) - lse_t[h:h + 1, :]), 0.0)
            dp = lax.dot_general(v_ref[:, hs], douts, _NT, preferred_element_type=F32)
            ds = p * (dp - dsum_t[h:h + 1, :])
            dv_ref[:, hs] = lax.dot_general(p.astype(BF16), douts, _NN, preferred_element_type=F32)
            dk_ref[:, hs] = lax.dot_general(ds.astype(BF16), qs, _NN, preferred_element_type=F32) * scale

    def specs(width):
        cur = pl.BlockSpec((None, ATTN_BLOCK, width), lambda r, n: (r, n, 0))
        nxt = pl.BlockSpec((None, ATTN_BLOCK, width), lambda r, n: (r, jnp.minimum(n + 1, nb - 1), 0))
        return cur, nxt

    (cur, nxt), (stat, stat_next) = specs(d), specs(LANE)
    return _call(name, body, [k, v, q, q, do, do, lse, lse, dsum, dsum], grid=(dil, nb),
                 in_specs=[cur, cur, cur, nxt, cur, nxt, stat, stat_next, stat, stat_next], out_specs=[cur, cur],
                 out_shape=[jax.ShapeDtypeStruct((dil, ln, d), F32)] * 2, sem=("parallel", "parallel"), carry=carry)


def _adamw_tile(w, g, m, v):
    m = ADAM_B1 * m + (1.0 - ADAM_B1) * g
    v = ADAM_B2 * v + (1.0 - ADAM_B2) * (g * g)
    m_hat = m / (1.0 - ADAM_B1 ** ADAM_STEP)
    v_hat = v / (1.0 - ADAM_B2 ** ADAM_STEP)
    delta = -ADAM_LR * (m_hat / (jnp.sqrt(v_hat) + ADAM_EPS) + ADAM_WD * w)
    return delta, m, v


def _adamw_big(name, parts, w, m, v, carry=None):
    layers, r, c = w.shape
    assert len(parts) == layers and all(sum(ch.shape[1] for ch in per_layer) == r for per_layer in parts)
    every = [ch for per_layer in parts for ch in per_layer]
    per_row = 2 * c * (len(every) * N_DEV * every[0].dtype.itemsize + 7 * 4)
    tr = 16
    while tr * 2 <= min(min(ch.shape[1] for ch in every), V7X_VMEM_LIMIT // 2 // per_row) and all(ch.shape[1] % (tr * 2) == 0 for ch in every):
        tr *= 2
    pieces = []
    for ly, per_layer in enumerate(parts):
        at = 0
        for ch in per_layer:
            pieces.append((ly, at, ch.shape[1] // tr, ch))
            at += ch.shape[1] // tr

    def within(layer, i, ly, first, tiles):
        return jnp.logical_and(layer == ly, jnp.logical_and(i >= first, i < first + tiles))

    def body(*refs):
        part_refs = refs[:len(pieces)]
        w_ref, m_ref, v_ref, g_out, d_out, m_out, v_out = refs[len(pieces):]
        layer, i = pl.program_id(0), pl.program_id(1)
        for (ly, first, tiles, _), part_ref in zip(pieces, part_refs):
            @pl.when(within(layer, i, ly, first, tiles))
            def _(part_ref=part_ref):
                g = part_ref[0].astype(F32)
                for dev in range(1, N_DEV):
                    g = g + part_ref[dev].astype(F32)
                delta, mn, vn = _adamw_tile(w_ref[...], g, m_ref[...], v_ref[...])
                g_out[...] = g
                d_out[...] = delta
                m_out[...] = mn
                v_out[...] = vn

    def part_index(layer, i, ly, first, tiles):
        return (0, jnp.where(within(layer, i, ly, first, tiles), i - first, 0), 0)

    own = pl.BlockSpec((None, tr, c), lambda ly, i: (ly, i, 0))
    part_specs = [pl.BlockSpec((N_DEV, tr, c), functools.partial(part_index, ly=ly, first=first, tiles=tiles))
                  for ly, first, tiles, _ in pieces]
    return _call(name, body, [ch for _, _, _, ch in pieces] + [w, m, v], grid=(layers, r // tr), in_specs=part_specs + [own] * 3,
                 out_specs=[own] * 4, out_shape=[jax.ShapeDtypeStruct(w.shape, F32)] * 4, sem=("parallel", "parallel"),
                 carry=carry)


def _sum_slots(name, slots):
    _, r, c = slots.shape

    def body(s_ref, o_ref):
        g = s_ref[0]
        for j in range(1, N_DEV):
            g = g + s_ref[j]
        o_ref[...] = g

    return pl.pallas_call(body, name=name, out_shape=jax.ShapeDtypeStruct((r, c), F32),
                          compiler_params=_params())(slots)


def _adamw_small(name, w, g, m, v):
    def body(w_ref, g_ref, m_ref, v_ref, d_out, m_out, v_out):
        delta, mn, vn = _adamw_tile(w_ref[...], g_ref[...], m_ref[...], v_ref[...])
        d_out[...] = delta
        m_out[...] = mn
        v_out[...] = vn

    return pl.pallas_call(body, name=name, out_shape=[jax.ShapeDtypeStruct(w.shape, F32)] * 3,
                          compiler_params=_params())(w, g, m, v)


class Weights:
    def __init__(self, shards):
        self.shards, self.full, self.parts = shards, {}, {}

    @staticmethod
    def by_columns(key):
        return key.rstrip("01") in COLUMN_SHARDED

    def gather(self, *keys, forward_at=None):
        return Exchange(gathers=[(self.shards[k], self.by_columns(k)) for k in keys], keys=keys, forward_at=forward_at)

    def landed(self, ex):
        for key, full in zip(ex.keys, ex.gathered):
            cols = self.by_columns(key)
            self.full[key] = W(full if cols else full.reshape(-1, full.shape[-1]), cols)

    def scatter(self, grads, gathers=()):
        if not grads and not gathers:
            return None
        return Exchange(gathers=gathers, scatters=[(g, self.by_columns(k), part) for k, (g, part) in grads.items()], keys=list(grads))

    def received(self, ex):
        for key, (first, _, of), part in zip(ex.keys, ex.s_parts, ex.parts) if ex is not None else ():
            self.parts.setdefault(key, {})[first / of] = part

    def chunks(self, key):
        return [self.parts[key][j] for j in sorted(self.parts[key])]

    def __getitem__(self, key):
        return self.full[key]


def _mlp_ple_fwd(tag, z1, p_i, ln1_g, ln1_b, ln2_g, ln2_b, wt, carries, target=None):
    h1, h1b = _layer_norm(f"ln1_{tag}", z1, ln1_g, ln1_b, (F32, BF16))
    up, act = _mm_nn(f"mlp_up_{tag}", h1b, wt["mlp_up" + tag], (F32, BF16),
                     epilogue=lambda acc: (acc, jnp.square(jnp.maximum(acc, 0.0))), carry=carries.get("mlp_up"))
    if "mlp_up" in carries:
        wt.landed(carries["mlp_up"])
    (z2,) = _mm_nn(f"mlp_down_{tag}", act, wt["mlp_down" + tag], (F32,), epilogue=lambda acc, h: (ALPHA * h + acc,),
                   extras=[(h1, "tile")], carry=carries.get("mlp_down"))
    if "mlp_down" in carries:
        wt.landed(carries["mlp_down"])
    h2, h2b = _layer_norm(f"ln2_{tag}", z2, ln2_g, ln2_b, (F32, BF16))
    (pe,) = _mm_nn(f"ple_proj_{tag}", p_i, wt["ple_proj" + tag], (F32,))

    saved = dict(z1=z1, h1b=h1b, up=up, act=act, z2=z2, h2b=h2b, p=p_i)
    if target is None:
        def gate(acc, h, e):
            out = h + e * _sigmoid(acc)
            return acc, out, out

        gp, out, outb = _mm_nn(f"ple_gate_{tag}", h2b, wt["ple_gate" + tag], (F32, F32, BF16), epilogue=gate,
                               extras=[(h2, "tile"), (pe, "tile")], carry=carries.get("ple_gate"))
        if "ple_gate" in carries:
            wt.landed(carries["ple_gate"])
        saved.update(pe=pe, gp=gp)
        return out, outb, saved

    width = z1.shape[1]

    def gate_and_loss(acc, h, e, goal):
        sg = _sigmoid(acc)
        diff = h + e * sg - goal
        d_y = diff * (1.0 / width)
        return d_y, d_y * sg, d_y * e * sg * (1.0 - sg), jnp.broadcast_to(jnp.sum(diff * diff), (1, LANE))

    d_y, d_pe, d_gp, sq = _mm_nn(f"ple_gate_{tag}", h2b, wt["ple_gate" + tag], (F32, BF16, BF16), epilogue=gate_and_loss,
                                 extras=[(h2, "tile"), (pe, "tile"), (target, "tile")], sums=1)
    saved.update(d_pe=d_pe, d_gp=d_gp)
    return d_y, 0.5 * sq[0, 0] / width, saved


WHOLE = (0, 1, 1)


def _mlp_ple_bwd(tag, d_out, sv, ln1_g, ln2_g, wt, plan, waiting):
    made = {}

    def riders(kernel, extra=()):
        items = {name + tag: (made[name], part) for name, part in plan[kernel]}
        return wt.scatter({**items, **dict(extra)})

    d_pe, d_gp = sv["d_pe"], sv["d_gp"]
    made["ple_proj"] = _mm_tn(f"g_ple_proj_{tag}", sv["p"], d_pe, wt["ple_proj" + tag])
    made["ple_gate"] = _mm_tn(f"g_ple_gate_{tag}", sv["h2b"], d_gp, wt["ple_gate" + tag])
    (d_h2,) = _mm_nt(f"d_ple_gate_{tag}", d_gp, wt["ple_gate" + tag], (F32,), epilogue=lambda acc, dy: (dy + acc,),
                     extras=[(d_out, "tile")])
    d_z2, d_z2b, g_ln2_g, g_ln2_b = _layer_norm_bwd(f"ln2_bwd_{tag}", sv["z2"], ln2_g, d_h2)
    ex = riders("g_mlp_down", waiting.items())
    made["mlp_down"] = _mm_tn(f"g_mlp_down_{tag}", sv["act"], d_z2b, wt["mlp_down" + tag], carry=ex)
    wt.received(ex)
    ex = riders("d_mlp_down")
    (d_up,) = _mm_nt(f"d_mlp_down_{tag}", d_z2b, wt["mlp_down" + tag], (BF16,),
                     epilogue=lambda acc, u: (acc * (2.0 * jnp.maximum(u, 0.0)),), extras=[(sv["up"], "tile")], carry=ex)
    wt.received(ex)
    ex = riders("g_mlp_up")
    made["mlp_up"] = _mm_tn(f"g_mlp_up_{tag}", sv["h1b"], d_up, wt["mlp_up" + tag], carry=ex)
    wt.received(ex)
    ex = riders("d_mlp_up")
    (d_h1,) = _mm_nt(f"d_mlp_up_{tag}", d_up, wt["mlp_up" + tag], (F32,), epilogue=lambda acc, dz: (ALPHA * dz + acc,),
                     extras=[(d_z2, "tile")], carry=ex)
    wt.received(ex)
    ex = riders("ln1_bwd")
    d_z1, d_z1b, g_ln1_g, g_ln1_b = _layer_norm_bwd(f"ln1_bwd_{tag}", sv["z1"], ln1_g, d_h1, carry=ex)
    wt.received(ex)
    return d_z1, d_z1b, dict(ln1_g=g_ln1_g, ln1_b=g_ln1_b, ln2_g=g_ln2_g, ln2_b=g_ln2_b), made["mlp_down"], made["mlp_up"]


def _local_step(x, p, positions, target, wt, small):
    s, d = x.shape
    nh = d // HEAD_DIM
    xb, p = x.astype(BF16), p.astype(BF16)

    ex = wt.gather("conv_w_out", "ple_proj0", "ple_gate0")
    (u,) = _mm_nn("conv_in", xb, wt["conv_w_in"], (F32,), epilogue=lambda acc, b: (acc + b,), extras=[(small["conv_b_in"], "row")],
                  carry=ex)
    wt.landed(ex)

    def glu_fn(i, o, r):
        o[0][...] = i[0][...] * _sigmoid(i[1][...])

    (glu,) = _rowwise("glu", glu_fn, [_rows(u, 0, d), _rows(u, 1, d)], [(d, F32)])
    ex = wt.gather("mlp_up0")
    c = _conv_fwd("dwconv", glu, small["conv_dw"], small["conv_dw_b"], carry=ex)
    wt.landed(ex)

    def ln_silu(i, o, r):
        xhat, _ = _ln_stats(i[0][...])
        n = xhat * i[1][...] + i[2][...]
        o[0][...] = (n * _sigmoid(n)).astype(BF16)

    (sb,) = _rowwise("conv_ln_silu", ln_silu, [_rows(c), _full(small["conv_ln_g"]), _full(small["conv_ln_b"])], [(d, BF16)])
    ex = wt.gather("attn_w_o", forward_at=0.8)
    (z1,) = _mm_nn("conv_out", sb, wt["conv_w_out"], (F32,), epilogue=lambda acc, xt: (ALPHA * xt + acc,), extras=[(x, "tile")],
                   carry=ex)
    wt.landed(ex)
    x1, x1b, sv0 = _mlp_ple_fwd("0", z1, p[0], small["ln1_g"][0:1], small["ln1_b"][0:1], small["ln2_g"][0:1], small["ln2_b"][0:1], wt,
                                dict(mlp_up=wt.gather("mlp_down0"), mlp_down=wt.gather("attn_w_q", forward_at=0.9),
                                     ple_gate=wt.gather("w_kv")))

    (kvn,) = _layer_norm("kv_ln", x1, small["kv_ln_g"], small["kv_ln_b"], (BF16,))
    (kv,) = _mm_nn("kv_proj", kvn, wt["w_kv"], (F32,))
    ex = wt.gather("mlp_up1")
    (q,) = _mm_nn("q_proj", x1b, wt["attn_w_q"], (F32,), carry=ex)
    wt.landed(ex)
    cos, sin = _rope_tables(positions)
    row_scratch = [pltpu.VMEM((ROW_TILE, LANE), F32)]

    def rot_kv(i, o, r, scr):
        cs, sn = i[2][...], i[3][...]
        for h in range(nh):
            hs = slice(h * HEAD_DIM, (h + 1) * HEAD_DIM)
            for base, val in ((0, _rot(i[0][:, hs], cs, sn)), (N_GROUPS, i[1][:, hs])):
                for g, dil in enumerate(GROUP_DILATIONS):
                    for res, plane in enumerate(_split_rows(scr, val, dil)):
                        o[base + g][res, :, hs] = plane.astype(BF16)

    by_group = [(d, BF16, dil) for dil in GROUP_DILATIONS]
    kv_groups = _rowwise("rotary_kv", rot_kv, [_rows(kv, 0, d), _rows(kv, 1, d), _rows(cos), _rows(sin)], by_group * 2,
                         ts=ROW_TILE, scratch=row_scratch)
    kg, vg = kv_groups[:N_GROUPS], kv_groups[N_GROUPS:]

    def rot_q(i, o, r, scr):
        cs, sn = i[1][...], i[2][...]
        for g, dil in enumerate(GROUP_DILATIONS):
            for h in range(nh):
                hs = slice(h * HEAD_DIM, (h + 1) * HEAD_DIM)
                val = _rot(i[0][:, g * d + h * HEAD_DIM:g * d + (h + 1) * HEAD_DIM], cs, sn)
                for res, plane in enumerate(_split_rows(scr, val, dil)):
                    o[g][res, :, hs] = plane.astype(BF16)

    qg = _rowwise("rotary_q", rot_q, [_rows(q), _rows(cos), _rows(sin)], by_group, ts=ROW_TILE, scratch=row_scratch)

    og, lg = zip(*[_attn_fwd(f"attn_fwd_{g}", qg[g], kg[g], vg[g]) for g in range(N_GROUPS)])

    def merge(i, o, r, scr):
        lses = []
        for g, dil in enumerate(GROUP_DILATIONS):
            lses.append(_join_rows(scr, [i[N_GROUPS + g][res] for res in range(dil)]))
        top = functools.reduce(jnp.maximum, lses)
        es = [jnp.exp(l - top) for l in lses]
        den = functools.reduce(lambda a, b: a + b, es)
        total = top + jnp.log(den)
        for g, dil in enumerate(GROUP_DILATIONS):
            for res, plane in enumerate(_split_rows(scr, total, dil)):
                o[2 + g][res] = plane
        ws = [e / den for e in es]
        for h in range(nh):
            hs = slice(h * HEAD_DIM, (h + 1) * HEAD_DIM)
            out = jnp.zeros((ROW_TILE, HEAD_DIM), F32)
            for g, dil in enumerate(GROUP_DILATIONS):
                og_h = _join_rows(scr, [i[g][res, :, hs] for res in range(dil)])
                out = out + ws[g][:, h:h + 1] * og_h
            o[0][:, hs] = out
            o[1][:, hs] = out.astype(BF16)

    merged = _rowwise("attn_merge", merge, [_by_residue(t) for t in og + lg],
                      [(d, F32), (d, BF16)] + [(LANE, F32, dil) for dil in GROUP_DILATIONS], ts=ROW_TILE, scratch=row_scratch)
    o, ob, lse_g = merged[0], merged[1], merged[2:]
    (z1b,) = _mm_nn("attn_out", ob, wt["attn_w_o"], (F32,), epilogue=lambda acc, xt: (ALPHA * xt + acc,), extras=[(x1, "tile")])
    d_y, loss, sv1 = _mlp_ple_fwd("1", z1b, p[1], small["ln1_g"][1:2], small["ln1_b"][1:2], small["ln2_g"][1:2], small["ln2_b"][1:2],
                                  wt, dict(mlp_up=wt.gather("mlp_down1"),
                                           mlp_down=wt.gather("ple_proj1", "ple_gate1", forward_at=0.5)), target=target)

    plan = dict(g_mlp_down=[("ple_proj", WHOLE), ("ple_gate", WHOLE)], d_mlp_down=[("mlp_down", (0, 3, 8))],
                g_mlp_up=[("mlp_down", (3, 6, 8))], d_mlp_up=[("mlp_down", (6, 8, 8))], ln1_bwd=[])
    d_z1, d_z1b, g1, _, g_up1 = _mlp_ple_bwd("1", d_y, sv1, small["ln1_g"][1:2], small["ln2_g"][1:2], wt, plan, {})
    g_wo = _mm_tn("g_attn_out", ob, d_z1b, wt["attn_w_o"])
    (d_o,) = _mm_nt("d_attn_out", d_z1b, wt["attn_w_o"], (F32,))

    def dsum_fn(i, o, r, scr):
        stats = jnp.zeros((ROW_TILE, LANE), F32)
        for h in range(nh):
            hs = slice(h * HEAD_DIM, (h + 1) * HEAD_DIM)
            dout = i[0][:, hs]
            stats = jnp.where(_lane(h, stats.shape), jnp.sum(dout * i[1][:, hs], axis=1, keepdims=True), stats)
            for g, dil in enumerate(GROUP_DILATIONS):
                for res, plane in enumerate(_split_rows(scr, dout, dil)):
                    o[g][res, :, hs] = plane.astype(BF16)
        for g, dil in enumerate(GROUP_DILATIONS):
            for res, plane in enumerate(_split_rows(scr, stats, dil)):
                o[N_GROUPS + g][res] = plane

    res_ = _rowwise("attn_dsum", dsum_fn, [_rows(d_o), _rows(o)], by_group + [(LANE, F32, dil) for dil in GROUP_DILATIONS],
                    ts=ROW_TILE, scratch=row_scratch)
    dog, dsum_g = res_[:N_GROUPS], res_[N_GROUPS:]
    dqs, dks, dvs = [], [], []
    riders = [wt.scatter({"mlp_up1": (g_up1, (q, q + 1, 4))}) for q in range(4)] + [wt.scatter({"attn_w_o": (g_wo, WHOLE)}), None]
    for g in range(N_GROUPS):
        dqs.append(_attn_dq(f"attn_dq_{g}", qg[g], kg[g], vg[g], dog[g], lse_g[g], dsum_g[g], carry=riders[2 * g]))
        dk, dv = _attn_dkv(f"attn_dkv_{g}", qg[g], kg[g], vg[g], dog[g], lse_g[g], dsum_g[g], carry=riders[2 * g + 1])
        dks.append(dk)
        dvs.append(dv)
    for ex in riders:
        wt.received(ex)

    def unrot_q(i, o, r, scr):
        cs, sn = i[N_GROUPS][...], i[N_GROUPS + 1][...]
        for g, dil in enumerate(GROUP_DILATIONS):
            for h in range(nh):
                hs = slice(h * HEAD_DIM, (h + 1) * HEAD_DIM)
                dq = _join_rows(scr, [i[g][res, :, hs] for res in range(dil)])
                o[0][:, g * d + h * HEAD_DIM:g * d + (h + 1) * HEAD_DIM] = _unrot(dq, cs, sn).astype(BF16)

    (d_q,) = _rowwise("rotary_q_bwd", unrot_q, [_by_residue(t) for t in dqs] + [_rows(cos), _rows(sin)], [(N_GROUPS * d, BF16)],
                      ts=ROW_TILE, scratch=row_scratch)

    def unrot_kv(i, o, r, scr):
        cs, sn = i[2 * N_GROUPS][...], i[2 * N_GROUPS + 1][...]
        for h in range(nh):
            hs = slice(h * HEAD_DIM, (h + 1) * HEAD_DIM)
            for base in (0, N_GROUPS):
                tot = jnp.zeros((ROW_TILE, HEAD_DIM), F32)
                for g, dil in enumerate(GROUP_DILATIONS):
                    tot = tot + _join_rows(scr, [i[base + g][res, :, hs] for res in range(dil)])
                if base == 0:
                    o[0][:, hs] = _unrot(tot, cs, sn).astype(BF16)
                else:
                    o[0][:, d + h * HEAD_DIM:d + (h + 1) * HEAD_DIM] = tot.astype(BF16)

    (d_kv,) = _rowwise("rotary_kv_bwd", unrot_kv, [_by_residue(t) for t in dks + dvs] + [_rows(cos), _rows(sin)], [(2 * d, BF16)],
                       ts=ROW_TILE, scratch=row_scratch)
    g_wq = _mm_tn("g_q_proj", x1b, d_q, wt["attn_w_q"])
    ex = wt.scatter({"attn_w_q": (g_wq, (0, 3, 8))})
    g_wkv = _mm_tn("g_kv_proj", kvn, d_kv, wt["w_kv"], carry=ex)
    wt.received(ex)
    ex = wt.scatter({"attn_w_q": (g_wq, (3, 6, 8))})
    (d_x1a,) = _mm_nt("d_q_proj", d_q, wt["attn_w_q"], (F32,), epilogue=lambda acc, dz: (ALPHA * dz + acc,), extras=[(d_z1, "tile")],
                      carry=ex)
    wt.received(ex)
    ex = wt.scatter({"attn_w_q": (g_wq, (6, 8, 8)), "w_kv": (g_wkv, (0, 1, 8))})
    (d_kvn,) = _mm_nt("d_kv_proj", d_kv, wt["w_kv"], (F32,), carry=ex)
    wt.received(ex)

    def kv_ln_bwd(i, o, r):
        dx, dg, db = _ln_bwd_tile(i[0][...], i[1][...], i[2][...])
        dx = dx + i[3][...]
        sg = _sigmoid(i[5][...])
        o[0][...] = dx
        o[1][...] = (dx * sg).astype(BF16)
        o[2][...] = (dx * i[4][...] * sg * (1.0 - sg)).astype(BF16)
        r[0][...] += dg
        r[1][...] += db

    ex = wt.scatter({"w_kv": (g_wkv, (1, 4, 8))})
    d_x1, sv0["d_pe"], sv0["d_gp"], g_kv_ln_g, g_kv_ln_b = _rowwise(
        "kv_ln_bwd", kv_ln_bwd, [_rows(x1), _full(small["kv_ln_g"]), _rows(d_kvn), _rows(d_x1a), _rows(sv0["pe"]), _rows(sv0["gp"])],
        [(d, F32), (d, BF16), (d, BF16)], [(1, d), (1, d)], ts=128, carry=ex)
    wt.received(ex)

    plan = dict(g_mlp_down=[("ple_proj", WHOLE)], d_mlp_down=[("ple_gate", WHOLE), ("mlp_down", (0, 1, 8))],
                g_mlp_up=[("mlp_down", (1, 4, 8))], d_mlp_up=[("mlp_down", (4, 7, 8))], ln1_bwd=[("mlp_up", (0, 1, 8))])
    d_z1, d_z1b, g0, g_down0, g_up0 = _mlp_ple_bwd("0", d_x1, sv0, small["ln1_g"][0:1], small["ln2_g"][0:1], wt, plan,
                                                   {"w_kv": (g_wkv, (4, 8, 8))})
    ex = wt.scatter({"mlp_up0": (g_up0, (1, 2, 8))})
    g_wout = _mm_tn("g_conv_out", sb, d_z1b, wt["conv_w_out"], carry=ex)
    wt.received(ex)
    ex = wt.scatter({"mlp_up0": (g_up0, (2, 3, 8))})
    (d_s,) = _mm_nt("d_conv_out", d_z1b, wt["conv_w_out"], (F32,), carry=ex)
    wt.received(ex)

    def ln_silu_bwd(i, o, r):
        cx, gn, bn, ds_ = i[0][...], i[1][...], i[2][...], i[3][...]
        xhat, _ = _ln_stats(cx)
        n = xhat * gn + bn
        sg = _sigmoid(n)
        dn = ds_ * (sg * (1.0 + n * (1.0 - sg)))
        dx, dg, db = _ln_bwd_tile(cx, gn, dn)
        o[0][...] = dx
        r[0][...] += dg
        r[1][...] += db

    ex = wt.scatter({"mlp_up0": (g_up0, (3, 4, 8))})
    d_c, g_cln_g, g_cln_b = _rowwise("conv_ln_silu_bwd", ln_silu_bwd,
                                     [_rows(c), _full(small["conv_ln_g"]), _full(small["conv_ln_b"]), _rows(d_s)],
                                     [(d, F32)], [(1, d), (1, d)], carry=ex)
    wt.received(ex)
    ex = wt.scatter({"mlp_down0": (g_down0, (7, 8, 8)), "mlp_up0": (g_up0, (4, 8, 8))})
    d_glu, g_dw, g_dwb = _conv_bwd("dwconv_bwd", glu, d_c, small["conv_dw"], carry=ex)
    wt.received(ex)

    def glu_bwd(i, o, r):
        a, gt, dg_ = i[0][...], i[1][...], i[2][...]
        sg = _sigmoid(gt)
        da = dg_ * sg
        dgate = dg_ * a * sg * (1.0 - sg)
        o[0][:, 0:d] = da.astype(BF16)
        o[0][:, d:2 * d] = dgate.astype(BF16)
        r[0][:, 0:d] += jnp.sum(da, axis=0, keepdims=True)
        r[0][:, d:2 * d] += jnp.sum(dgate, axis=0, keepdims=True)

    ex = wt.scatter({"conv_w_out": (g_wout, (0, 1, 2))})
    d_u, g_bin = _rowwise("glu_bwd", glu_bwd, [_rows(u, 0, d), _rows(u, 1, d), _rows(d_glu)], [(2 * d, BF16)], [(1, 2 * d)],
                          carry=ex)
    wt.received(ex)
    ex = wt.scatter({"conv_w_out": (g_wout, (1, 2, 2))})
    g_win = _mm_tn("g_conv_in", xb, d_u, wt["conv_w_in"], carry=ex)
    wt.received(ex)
    rows = [g_bin.reshape(2, d), g_dw, g_dwb, g_cln_g, g_cln_b, g_kv_ln_g, g_kv_ln_b]
    rows += [jnp.concatenate([g0[n], g1[n]], axis=0) for n in ("ln1_g", "ln1_b", "ln2_g", "ln2_b")]
    rows, offsets = _stack_rows(rows)
    ex = wt.scatter({"conv_w_in": (g_win, (0, 4, 8))}, gathers=[(rows, False)])
    (grad_x,) = _mm_nt("d_conv_in", d_u, wt["conv_w_in"], (F32,), epilogue=lambda acc, dz: (ALPHA * dz + acc,), extras=[(d_z1, "tile")],
                       carry=ex)
    wt.received(ex)
    return loss, grad_x, ex.gathered[0], offsets, g_win


BIG = ("conv_w_in", "conv_w_out", "w_kv", "attn_w_q", "attn_w_o", "mlp_up", "mlp_down", "ple_proj", "ple_gate")
COLUMN_SHARDED = ("conv_w_in", "w_kv", "attn_w_q", "mlp_up", "ple_proj")
WEIGHTS = ("conv_w_in", "conv_b_in", "conv_dw", "conv_dw_b", "conv_ln_g", "conv_ln_b", "conv_w_out", "kv_ln_g", "kv_ln_b",
           "w_kv", "attn_w_q", "attn_w_o", "ln1_g", "ln1_b", "mlp_up", "mlp_down", "ln2_g", "ln2_b", "ple_proj", "ple_gate")


def kernel(x, p, positions, conv_w_in, conv_b_in, conv_dw, conv_dw_b, conv_ln_g, conv_ln_b, conv_w_out, kv_ln_g, kv_ln_b, w_kv, attn_w_q, attn_w_o, ln1_g, ln1_b, mlp_up, mlp_down, ln2_g, ln2_b, ple_proj, ple_gate, loss_target, m_conv_w_in, m_conv_b_in, m_conv_dw, m_conv_dw_b, m_conv_ln_g, m_conv_ln_b, m_conv_w_out, m_kv_ln_g, m_kv_ln_b, m_w_kv, m_attn_w_q, m_attn_w_o, m_ln1_g, m_ln1_b, m_mlp_up, m_mlp_down, m_ln2_g, m_ln2_b, m_ple_proj, m_ple_gate, v_conv_w_in, v_conv_b_in, v_conv_dw, v_conv_dw_b, v_conv_ln_g, v_conv_ln_b, v_conv_w_out, v_kv_ln_g, v_kv_ln_b, v_w_kv, v_attn_w_q, v_attn_w_o, v_ln1_g, v_ln1_b, v_mlp_up, v_mlp_down, v_ln2_g, v_ln2_b, v_ple_proj, v_ple_gate):
    given = dict(locals())
    wts = {n: given[n] for n in WEIGHTS}
    moms = {n: given["m_" + n] for n in WEIGHTS}
    vels = {n: given["v_" + n] for n in WEIGHTS}
    s, d = x.shape[1], x.shape[2]
    shard = d // N_DEV
    me = 4 * lax.axis_index("x") + 2 * lax.axis_index("y") + lax.axis_index("c")

    def layers_of(a):
        return a.reshape((-1,) + a.shape[-2:])

    shards = {}
    for n in BIG:
        w3 = layers_of(wts[n])
        for ly in range(w3.shape[0]):
            shards[n + str(ly) if w3.shape[0] > 1 else n] = w3[ly].astype(BF16)
    wt = Weights(shards)
    pack, at = _stack_rows([wts["conv_b_in"].reshape(2, shard), wts["conv_dw"].reshape(CONV_WIDTH, shard),
                            wts["conv_dw_b"], wts["conv_ln_g"], wts["conv_ln_b"]])
    ex = Exchange(gathers=[(shards["conv_w_in"], True), (pack, False)], keys=["conv_w_in"])
    _exchange_alone("gather_first", ex)
    wt.landed(ex)
    packed = ex.gathered[1]
    small = dict(conv_b_in=packed[:, at[0]:at[0] + 2].reshape(1, 2 * d), conv_dw=packed[:, at[1]:at[1] + CONV_WIDTH],
                 conv_dw_b=packed[:, at[2]].reshape(1, d), conv_ln_g=packed[:, at[3]].reshape(1, d),
                 conv_ln_b=packed[:, at[4]].reshape(1, d), kv_ln_g=kv_ln_g.reshape(1, d), kv_ln_b=kv_ln_b.reshape(1, d),
                 ln1_g=ln1_g, ln1_b=ln1_b, ln2_g=ln2_g, ln2_b=ln2_b)

    loss, grad_x, all_rows, at, g_win = _local_step(x[0], p[:, 0], positions.reshape(s, 1), loss_target[0], wt, small)
    loss = lax.psum(loss, ("x", "y", "c"))

    riding = dict(mlp_down={"conv_w_in": (g_win, (4, 8, 8))})
    out = {}
    for n in list(riding) + [n for n in BIG if n not in riding]:
        w3 = layers_of(wts[n])
        keys = [n + str(ly) if w3.shape[0] > 1 else n for ly in range(w3.shape[0])]
        ex = wt.scatter(riding.get(n, {}))
        res = _adamw_big("adamw_" + n, [wt.chunks(k) for k in keys], w3, layers_of(moms[n]), layers_of(vels[n]), carry=ex)
        wt.received(ex)
        out[n] = [r.reshape(wts[n].shape) for r in res]
    tot = _sum_slots("sum_small_grads", all_rows)
    mine = lax.dynamic_slice_in_dim(tot, me * shard, shard, axis=1)
    b_in = lax.dynamic_slice_in_dim(tot[at[0]:at[0] + 2].reshape(1, 2 * d), me * 2 * shard, 2 * shard, axis=1)
    g_small = dict(conv_b_in=b_in, conv_dw=mine[at[1]:at[1] + CONV_WIDTH].reshape(conv_dw.shape), conv_dw_b=mine[at[2]:at[2] + 1],
                   conv_ln_g=mine[at[3]:at[3] + 1], conv_ln_b=mine[at[4]:at[4] + 1], kv_ln_g=tot[at[5]], kv_ln_b=tot[at[6]])
    for j, n in enumerate(("ln1_g", "ln1_b", "ln2_g", "ln2_b")):
        g_small[n] = tot[at[7 + j]:at[7 + j] + DEPTH]
    order = [n for n in WEIGHTS if n not in BIG]

    def flat(t):
        return _stack_rows([t[n].reshape(-1, shard) for n in order])

    (w_s, at), (g_s, _), (m_s, _), (v_s, _) = flat(wts), flat(g_small), flat(moms), flat(vels)
    d_s, m_s, v_s = _adamw_small("adamw_small", w_s, g_s, m_s, v_s)
    for n, a in zip(order, at):
        nrow = wts[n].size // shard
        out[n] = [g_small[n].reshape(wts[n].shape)] + [t[a:a + nrow].reshape(wts[n].shape) for t in (d_s, m_s, v_s)]
    return (loss, grad_x[None], *[out[n][0] for n in WEIGHTS], *[out[n][1] for n in WEIGHTS],
            *[out[n][2] for n in WEIGHTS], *[out[n][3] for n in WEIGHTS])
```
